```python
import math
import jax, jax.numpy as jnp
from jax import lax
import numpy as np


D_MODEL = 1024
BATCH = 16
SEQ = 2048
DEPTH = 2
DEC_BATCH = 8
DEC_SEQ = 8192
PAST_LEN = 128

GRID_W = 64
N_BRANCH = 4
BRANCH_W = 256
HEAD_DIM = 64
ATT_Q_HEADS = 4
ATT_KV_HEADS = 2
ATT_GROUP = ATT_Q_HEADS // ATT_KV_HEADS
Q_BLOCK = 128
ROPE_BASE = 10000.0
ROPE_FREQS = HEAD_DIM // 4
HY_WIDTH = BRANCH_W
HY_EMB = 33
HY_BANDS = (HY_EMB - 1) // 2
HY_FILTER_HIDDEN = 64
HY_FAST_DECAY = 0.3
HY_SLOW_DECAY = 1.5
HY_TARGET = 1e-2
RET_HEADS = 4
RET_W = RET_HEADS * HEAD_DIM
RET_CHUNK = 128
SC_WIDTH = BRANCH_W
D_FF = ((-(-8 * D_MODEL // 3)) + 255) // 256 * 256
NORM_EPS = 1e-6

ATT_Q_W = ATT_Q_HEADS * HEAD_DIM
ATT_KV_W = ATT_KV_HEADS * HEAD_DIM
A_K_OFF = ATT_Q_W
A_V_OFF = A_K_OFF + ATT_KV_W
HY_OFF = A_V_OFF + ATT_KV_W
RET_OFF = HY_OFF + 3 * HY_WIDTH
SC_OFF = RET_OFF + 4 * RET_W
GATE_OFF = SC_OFF + 3 * SC_WIDTH
IN_COLS = GATE_OFF + N_BRANCH * D_MODEL

kernel_name = 'hybrid_gated_parallel_encoder'

F32 = jnp.float32


def rms_norm(x, gain=None):
    x32 = x.astype(F32)
    y = x32 * lax.rsqrt(jnp.mean(x32 * x32, axis=-1, keepdims=True) + NORM_EPS)
    if gain is not None:
        y = y * gain.astype(F32)
    return y.astype(x.dtype)


def axial_rope_tables(L):
    rows = L // GRID_W
    r = jnp.repeat(jnp.arange(rows, dtype=F32), GRID_W)
    c = jnp.tile(jnp.arange(GRID_W, dtype=F32), rows)
    inv = ROPE_BASE ** (-jnp.arange(ROPE_FREQS, dtype=F32) / ROPE_FREQS)
    ang = jnp.stack([r[:, None] * inv, c[:, None] * inv], axis=1)
    return jnp.cos(ang), jnp.sin(ang)


def apply_rope(x, cos, sin):
    B, L, H, _ = x.shape
    xr = x.astype(F32).reshape(B, L, H, 2, 2, ROPE_FREQS)
    a = xr[..., 0, :]
    b = xr[..., 1, :]
    c = cos[None, :, None]
    s = sin[None, :, None]
    out = jnp.stack([a * c - b * s, b * c + a * s], axis=-2)
    return out.reshape(B, L, H, HEAD_DIM).astype(x.dtype)


def conv3(x, w, b=None):
    xp = jnp.pad(x, ((0, 0), (1, 1), (0, 0)))
    y = xp[:, :-2] * w[0] + xp[:, 1:-1] * w[1] + xp[:, 2:] * w[2]
    if b is not None:
        y = y + b
    return y


def block_attention(q, k, v):
    B, L = q.shape[0], q.shape[1]
    nb = L // Q_BLOCK
    qg = q.reshape(B, nb, Q_BLOCK, ATT_KV_HEADS, ATT_GROUP, HEAD_DIM).transpose(1, 0, 2, 3, 4, 5)
    scale = HEAD_DIM ** -0.5

    def one_block(qb):
        s = jnp.einsum('bqkgd,bskd->bkgqs', qb, k).astype(F32) * scale
        p = jax.nn.softmax(s, axis=-1).astype(v.dtype)
        return jnp.einsum('bkgqs,bskd->bqkgd', p, v)

    o = lax.map(one_block, qg)
    return o.transpose(1, 0, 2, 3, 4, 5).reshape(B, L, ATT_Q_W)


def hyena_filter(L, w1, b1, w2, b2, w3, freq):
    t = jnp.linspace(0.0, 1.0, L, dtype=F32)[:, None]
    f = jnp.linspace(1e-4, HY_BANDS - 1, HY_BANDS, dtype=F32)
    ang = (2.0 * math.pi / L) * jnp.arange(L, dtype=F32)[:, None] * f[None, :]
    z = jnp.concatenate([t, jnp.cos(ang), -jnp.sin(ang)], axis=-1)
    h = jnp.sin(freq[0].astype(F32) * (z @ w1.astype(F32) + b1.astype(F32)))
    h = jnp.sin(freq[1].astype(F32) * (h @ w2.astype(F32) + b2.astype(F32)))
    h = (h @ w3.astype(F32)).reshape(L, 2, HY_WIDTH)
    deltas = jnp.abs(jnp.linspace(math.log(HY_TARGET) / HY_SLOW_DECAY,
                                  math.log(HY_TARGET) / HY_FAST_DECAY, HY_WIDTH, dtype=F32))
    h = h * jnp.exp(-t * deltas)[:, None, :]
    k2 = jnp.concatenate([h[:, 0], jnp.zeros((1, HY_WIDTH), F32), h[:0:-1, 1]], axis=0)
    return k2 * lax.rsqrt(jnp.sum(k2 * k2, axis=0, keepdims=True) + NORM_EPS)


def long_conv(u, k2):
    L = u.shape[1]
    n = 2 * L
    uf = jnp.fft.rfft(u, n=n, axis=1)
    kf = jnp.fft.rfft(k2, n=n, axis=0)
    return jnp.fft.irfft(uf * kf[None], n=n, axis=1)[:, :L]


def retention_scan(q, k, v, log_g, inclusive):
    B, L, H, D = q.shape
    C = RET_CHUNK
    nc = L // C
    qc = q.reshape(B, nc, C, H, D)
    kc = k.reshape(B, nc, C, H, D)
    vc = v.reshape(B, nc, C, H, D)
    i = jnp.arange(C, dtype=F32)
    diff = i[:, None] - i[None, :]
    mask = (diff >= 0) if inclusive else (diff > 0)
    decay = jnp.where(mask[None], jnp.exp(jnp.where(mask, diff, 0.0)[None] * log_g[:, None, None]), 0.0)
    s = jnp.einsum('bnihd,bnjhd->bnhij', qc, kc) * decay[None, None]
    o_intra = jnp.einsum('bnhij,bnjhd->bnihd', s, vc)
    w_k = jnp.exp((C - 1 - i)[None, :] * log_g[:, None])
    kv = jnp.einsum('bnjhd,bnjhe,hj->nbhde', kc, vc, w_k)
    g_chunk = jnp.exp(C * log_g)[None, :, None, None]

    def step(S, kv_n):
        return g_chunk * S + kv_n, S

    _, states = lax.scan(step, jnp.zeros((B, H, D, D), F32), kv)
    w_q = jnp.exp((i + 1)[None, :] * log_g[:, None])
    o_cross = jnp.einsum('bnihd,nbhde,hi->bnihe', qc, states, w_q)
    return (o_intra + o_cross).reshape(B, L, H, D)


def layer(x, cos, sin, k2, ng, w_in, qkn, hcw, hcb, hbias, rde, scw, wb, wo, wfi, wfo):
    B, L, _ = x.shape
    dt = x.dtype
    xn = rms_norm(x, ng[0])
    p = xn @ w_in

    q = p[..., 0:ATT_Q_W].reshape(B, L, ATT_Q_HEADS, HEAD_DIM)
    k = p[..., A_K_OFF:A_V_OFF].reshape(B, L, ATT_KV_HEADS, HEAD_DIM)
    v = p[..., A_V_OFF:HY_OFF].reshape(B, L, ATT_KV_HEADS, HEAD_DIM)
    q = apply_rope(rms_norm(q, qkn[0]), cos, sin)
    k = apply_rope(rms_norm(k, qkn[1]), cos, sin)
    out_a = block_attention(q, k, v)

    u = conv3(p[..., HY_OFF:RET_OFF], hcw, hcb)
    x0 = u[..., 0:HY_WIDTH]
    x1 = u[..., HY_WIDTH:2 * HY_WIDTH]
    hv = u[..., 2 * HY_WIDTH:3 * HY_WIDTH]
    z = (hv * x1).astype(F32)
    z = long_conv(z, k2) + z * hbias.astype(F32)
    out_b = (x0.astype(F32) * z).astype(dt)

    rq = p[..., RET_OFF:RET_OFF + RET_W].reshape(B, L, RET_HEADS, HEAD_DIM)
    rk = p[..., RET_OFF + RET_W:RET_OFF + 2 * RET_W].reshape(B, L, RET_HEADS, HEAD_DIM)
    rv = p[..., RET_OFF + 2 * RET_W:RET_OFF + 3 * RET_W].reshape(B, L, RET_HEADS, HEAD_DIM)
    rg = p[..., RET_OFF + 3 * RET_W:SC_OFF]
    rq = apply_rope(rq, cos, sin).astype(F32)
    rk = apply_rope(rk, cos, sin).astype(F32) * (HEAD_DIM ** -0.5)
    rv = rv.astype(F32)
    log_g = jnp.log1p(-jnp.exp2(-rde.astype(F32)))
    fwd = retention_scan(rq, rk, rv, log_g[0], True)
    bwd = jnp.flip(retention_scan(jnp.flip(rq, 1), jnp.flip(rk, 1), jnp.flip(rv, 1), log_g[1], False), 1)
    ret = rms_norm(fwd + bwd).reshape(B, L, RET_W).astype(dt)
    out_c = ret * jax.nn.silu(rg)

    sb = p[..., SC_OFF:SC_OFF + SC_WIDTH]
    sc = p[..., SC_OFF + SC_WIDTH:SC_OFF + 2 * SC_WIDTH]
    sh = p[..., SC_OFF + 2 * SC_WIDTH:GATE_OFF]
    out_d = sb * conv3(sc * sh, scw)

    merged = None
    for n, br in enumerate((out_a, out_b, out_c, out_d)):
        gate = jax.nn.sigmoid(p[..., GATE_OFF + n * D_MODEL:GATE_OFF + (n + 1) * D_MODEL])
        term = gate * (br @ wb[n])
        merged = term if merged is None else merged + term
    h = x + rms_norm(merged @ wo, ng[1])

    gu = rms_norm(h, ng[2]) @ wfi
    f = (jax.nn.silu(gu[..., :D_FF]) * gu[..., D_FF:]) @ wfo
    return h + rms_norm(f, ng[3])


def trunk(x, norm_gains, w_in, qk_norm, hy_conv_w, hy_conv_b, hy_w1, hy_b1, hy_w2, hy_b2, hy_w3,
          hy_freq, hy_bias, ret_decay_exp, sc_conv_w, w_branch, w_out, w_ffn_in, w_ffn_out):
    L = x.shape[1]
    cos, sin = axial_rope_tables(L)
    for l in range(DEPTH):
        k2 = hyena_filter(L, hy_w1[l], hy_b1[l], hy_w2[l], hy_b2[l], hy_w3[l], hy_freq[l])
        x = layer(x, cos, sin, k2, norm_gains[l], w_in[l], qk_norm[l], hy_conv_w[l], hy_conv_b[l],
                  hy_bias[l], ret_decay_exp[l], sc_conv_w[l], w_branch[l], w_out[l], w_ffn_in[l], w_ffn_out[l])
    return x


def setup_inputs(seed: int = 0) -> dict:
    key = jax.random.key(seed)
    ks = jax.random.split(key, 22)
    nrm = lambda k, shape, s: jax.random.normal(k, shape, F32) * s
    return {
        'x_prompt': nrm(ks[0], (BATCH, SEQ, D_MODEL), 1.0),
        'x_sample': nrm(ks[1], (DEC_BATCH, DEC_SEQ, D_MODEL), 1.0),
        'norm_gains': 1.0 + nrm(ks[2], (DEPTH, 4, D_MODEL), 0.01),
        'w_in': nrm(ks[3], (DEPTH, D_MODEL, IN_COLS), D_MODEL ** -0.5),
        'qk_norm': 1.0 + nrm(ks[4], (DEPTH, 2, HEAD_DIM), 0.01),
        'hy_conv_w': nrm(ks[5], (DEPTH, 3, 3 * HY_WIDTH), 3 ** -0.5),
        'hy_conv_b': nrm(ks[6], (DEPTH, 3 * HY_WIDTH), 0.01),
        'hy_w1': nrm(ks[7], (DEPTH, HY_EMB, HY_FILTER_HIDDEN), HY_EMB ** -0.5),
        'hy_b1': nrm(ks[8], (DEPTH, HY_FILTER_HIDDEN), 0.01),
        'hy_w2': nrm(ks[9], (DEPTH, HY_FILTER_HIDDEN, HY_FILTER_HIDDEN), HY_FILTER_HIDDEN ** -0.5),
        'hy_b2': nrm(ks[10], (DEPTH, HY_FILTER_HIDDEN), 0.01),
        'hy_w3': nrm(ks[11], (DEPTH, HY_FILTER_HIDDEN, 2 * HY_WIDTH), HY_FILTER_HIDDEN ** -0.5),
        'hy_freq': 1.0 + nrm(ks[12], (DEPTH, 2, HY_FILTER_HIDDEN), 0.01),
        'hy_bias': nrm(ks[13], (DEPTH, HY_WIDTH), 0.1),
        'ret_decay_exp': 5.0 + jnp.arange(RET_HEADS, dtype=F32)[None, None, :] + nrm(ks[14], (DEPTH, 2, RET_HEADS), 0.1),
        'sc_conv_w': nrm(ks[15], (DEPTH, 3, SC_WIDTH), 3 ** -0.5),
        'w_branch': nrm(ks[16], (DEPTH, N_BRANCH, BRANCH_W, D_MODEL), BRANCH_W ** -0.5),
        'w_out': nrm(ks[17], (DEPTH, D_MODEL, D_MODEL), D_MODEL ** -0.5),
        'w_ffn_in': nrm(ks[18], (DEPTH, D_MODEL, 2 * D_FF), D_MODEL ** -0.5),
        'w_ffn_out': nrm(ks[19], (DEPTH, D_FF, D_MODEL), D_FF ** -0.5),
    }


def reference(x_prompt, x_sample, norm_gains, w_in, qk_norm, hy_conv_w, hy_conv_b, hy_w1, hy_b1, hy_w2, hy_b2,
              hy_w3, hy_freq, hy_bias, ret_decay_exp, sc_conv_w, w_branch, w_out, w_ffn_in, w_ffn_out):
    y_prompt = trunk(x_prompt, norm_gains, w_in, qk_norm, hy_conv_w, hy_conv_b, hy_w1, hy_b1, hy_w2, hy_b2,
                     hy_w3, hy_freq, hy_bias, ret_decay_exp, sc_conv_w, w_branch, w_out, w_ffn_in, w_ffn_out)
    y_sample = trunk(x_sample, norm_gains, w_in, qk_norm, hy_conv_w, hy_conv_b, hy_w1, hy_b1, hy_w2, hy_b2,
                     hy_w3, hy_freq, hy_bias, ret_decay_exp, sc_conv_w, w_branch, w_out, w_ffn_in, w_ffn_out)
    return (y_prompt, y_sample)
```

```python
import functools
import math

import numpy as np
import jax
import jax.numpy as jnp
from jax import lax
from jax.experimental import pallas as pl
from jax.experimental.pallas import tpu as pltpu

F32 = jnp.float32
BF16 = jnp.bfloat16

D_MODEL = 1024
GRID_W = 64
N_BRANCH = 4
BRANCH_W = 256
HEAD_DIM = 64
ATT_Q_HEADS = 4
ATT_KV_HEADS = 2
ROPE_BASE = 10000.0
ROPE_FREQS = HEAD_DIM // 4
HY_WIDTH = BRANCH_W
HY_EMB = 33
HY_BANDS = (HY_EMB - 1) // 2
HY_FILTER_HIDDEN = 64
HY_FAST_DECAY = 0.3
HY_SLOW_DECAY = 1.5
HY_TARGET = 1e-2
RET_HEADS = 4
RET_W = RET_HEADS * HEAD_DIM
RET_CHUNK = 128
SC_WIDTH = BRANCH_W
D_FF = 2816
NORM_EPS = 1e-6

ATT_Q_W = ATT_Q_HEADS * HEAD_DIM
ATT_KV_W = ATT_KV_HEADS * HEAD_DIM
A_K_OFF = ATT_Q_W
A_V_OFF = A_K_OFF + ATT_KV_W
HY_OFF = A_V_OFF + ATT_KV_W
RET_OFF = HY_OFF + 3 * HY_WIDTH
SC_OFF = RET_OFF + 4 * RET_W
GATE_OFF = SC_OFF + 3 * SC_WIDTH
IN_COLS = GATE_OFF + N_BRANCH * D_MODEL

LANES = 128
SUBLANES = 8
HY_BLK = HY_OFF // LANES
RET_BLK = RET_OFF // LANES
SC_BLK = SC_OFF // LANES
GATE_BLK = GATE_OFF // D_MODEL
FFT_PASSES = 3
MIB = 1 << 20


def _cparams(sem, vmem_mib):
    return pltpu.CompilerParams(dimension_semantics=sem, vmem_limit_bytes=vmem_mib * MIB)


def _sigmoid(x):
    return 1.0 / (1.0 + jnp.exp(-x))


def _split(x):
    hi = x.astype(BF16)
    lo = (x - hi.astype(F32)).astype(BF16)
    return hi, lo


def _np_split(a64):
    a32 = np.asarray(a64, np.float32)
    hi = a32.astype(BF16)
    lo = (a32 - hi.astype(np.float32)).astype(BF16)
    return jnp.asarray(hi), jnp.asarray(lo)


def _mm_const(a_hi, a_lo, x, passes):
    xh, xl = _split(x)
    out = jnp.dot(a_hi, xh, preferred_element_type=F32)
    if passes >= 3:
        out = out + jnp.dot(a_lo, xh, preferred_element_type=F32)
        out = out + jnp.dot(a_hi, xl, preferred_element_type=F32)
    return out


def _mm3(a, b):
    ah, al = _split(a)
    bh, bl = _split(b)
    out = jnp.dot(ah, bh, preferred_element_type=F32)
    out = out + jnp.dot(al, bh, preferred_element_type=F32)
    return out + jnp.dot(ah, bl, preferred_element_type=F32)


def _head_mean_sq(x, j_bf):
    hi, lo = _split(x * x)
    return (jnp.dot(hi, j_bf, preferred_element_type=F32)
            + jnp.dot(lo, j_bf, preferred_element_type=F32))


def _rope(x, c, s):
    lane = lax.broadcasted_iota(jnp.int32, (1, LANES), 1)
    is_b = (lane & 16) != 0
    partner = jnp.where(is_b, pltpu.roll(x, 16, 1), pltpu.roll(x, LANES - 16, 1))
    return x * c + partner * s


def _in_proj_kernel(x_ref, g_ref, w_ref, o_ref, xn_ref):
    @pl.when(pl.program_id(1) == 0)
    def _():
        x = x_ref[...]
        ms = jnp.mean(x * x, axis=-1, keepdims=True)
        xn_ref[...] = (x * lax.rsqrt(ms + NORM_EPS) * g_ref[...]).astype(BF16)

    o_ref[...] = jnp.dot(xn_ref[...], w_ref[...], preferred_element_type=F32)


def _in_proj(x2, g, w_bf):
    T = x2.shape[0]
    tm = min(1024, T)
    tn = 1024
    return pl.pallas_call(
        _in_proj_kernel,
        grid=(T // tm, IN_COLS // tn),
        in_specs=[pl.BlockSpec((tm, D_MODEL), lambda i, j: (i, 0)),
                  pl.BlockSpec((1, D_MODEL), lambda i, j: (0, 0)),
                  pl.BlockSpec((D_MODEL, tn), lambda i, j: (0, j))],
        out_specs=pl.BlockSpec((tm, tn), lambda i, j: (i, j)),
        out_shape=jax.ShapeDtypeStruct((T, IN_COLS), F32),
        scratch_shapes=[pltpu.VMEM((tm, D_MODEL), BF16)],
        compiler_params=_cparams(("parallel", "arbitrary"), 40),
        name="in_proj",
    )(x2, g, w_bf)


def _attn_prep_kernel(p_ref, c_ref, s_ref, qg_ref, kg_ref, j_ref,
                      qb_ref, k_ref, ks_ref, v_ref, vs_ref):
    c = c_ref[...]
    s = s_ref[...]
    j_bf = j_ref[...]
    lane = lax.broadcasted_iota(jnp.int32, (1, LANES), 1)
    lo64 = lane < HEAD_DIM
    for blk in range(2):
        q = p_ref[:, LANES * blk:LANES * (blk + 1)]
        qn = q * lax.rsqrt(_head_mean_sq(q, j_bf) + NORM_EPS) * qg_ref[...]
        qr = (_rope(qn, c, s) * (HEAD_DIM ** -0.5)).astype(BF16)
        zero = jnp.zeros_like(qr)
        qb_ref[:, 2 * LANES * blk:2 * LANES * blk + LANES] = jnp.where(lo64, qr, zero)
        qb_ref[:, 2 * LANES * blk + LANES:2 * LANES * (blk + 1)] = jnp.where(lo64, zero, qr)
    k = p_ref[:, A_K_OFF:A_K_OFF + LANES]
    kn = k * lax.rsqrt(_head_mean_sq(k, j_bf) + NORM_EPS) * kg_ref[...]
    kr = _rope(kn, c, s)
    k_ref[...] = kr.astype(BF16)
    ks_ref[...] = pltpu.roll(kr, HEAD_DIM, 1).astype(BF16)
    v = p_ref[:, A_V_OFF:A_V_OFF + LANES]
    v_ref[...] = v.astype(BF16)
    vs_ref[...] = pltpu.roll(v, HEAD_DIM, 1).astype(BF16)


def _attn_prep(p3, cos_t, sin_t, qg, kg, j_bf):
    B, L, _ = p3.shape
    tm = min(1024, L)
    seq = lambda w: pl.BlockSpec((None, tm, w), lambda b, i: (b, i, 0))
    tab = pl.BlockSpec((tm, LANES), lambda b, i: (i, 0))
    vec = pl.BlockSpec((1, LANES), lambda b, i: (0, 0))
    kv_shape = jax.ShapeDtypeStruct((B, L, LANES), BF16)
    return pl.pallas_call(
        _attn_prep_kernel,
        grid=(B, L // tm),
        in_specs=[seq(HY_OFF), tab, tab, vec, vec,
                  pl.BlockSpec((LANES, LANES), lambda b, i: (0, 0))],
        out_specs=[seq(4 * LANES), seq(LANES), seq(LANES), seq(LANES), seq(LANES)],
        out_shape=[jax.ShapeDtypeStruct((B, L, 4 * LANES), BF16),
                   kv_shape, kv_shape, kv_shape, kv_shape],
        compiler_params=_cparams(("parallel", "parallel"), 32),
        name="attn_prep",
    )(p3, cos_t, sin_t, qg, kg, j_bf)


def _flash_kernel(q_ref, k_ref, ks_ref, v_ref, vs_ref, o_ref, *, tk, nk):
    tq = q_ref.shape[0]
    lane = lax.broadcasted_iota(jnp.int32, (1, LANES), 1)
    lo64 = lane < HEAD_DIM
    outs = []
    for h in range(ATT_Q_HEADS):
        qh = q_ref[:, LANES * h:LANES * (h + 1)]
        kr = k_ref if h in (0, 3) else ks_ref
        vr = v_ref if h in (0, 3) else vs_ref

        def body(c, carry, qh=qh, kr=kr, vr=vr):
            m, l, acc = carry
            st = pl.multiple_of(c * tk, tk)
            kc = kr[pl.ds(st, tk), :]
            vc = vr[pl.ds(st, tk), :]
            s = lax.dot_general(qh, kc, (((1,), (1,)), ((), ())), preferred_element_type=F32)
            m_new = jnp.maximum(m, jnp.max(s, axis=-1, keepdims=True))
            alpha = jnp.exp(m - m_new)
            p = jnp.exp(s - m_new)
            l = alpha * l + jnp.sum(p, axis=-1, keepdims=True)
            acc = alpha * acc + jnp.dot(p.astype(BF16), vc, preferred_element_type=F32)
            return m_new, l, acc

        m0 = jnp.full((tq, 1), -jnp.inf, F32)
        l0 = jnp.zeros((tq, 1), F32)
        a0 = jnp.zeros((tq, LANES), F32)
        _, l, acc = lax.fori_loop(0, nk, body, (m0, l0, a0))
        outs.append(acc / l)
    o_ref[:, 0:LANES] = jnp.where(lo64, outs[0], outs[1]).astype(o_ref.dtype)
    o_ref[:, LANES:2 * LANES] = jnp.where(lo64, outs[2], outs[3]).astype(o_ref.dtype)


def _flash(qb, k, ks, v, vs):
    B, L, _ = qb.shape
    tq = min(512, L)
    tk = min(512, L)
    full = pl.BlockSpec((None, L, LANES), lambda b, i: (b, 0, 0))
    return pl.pallas_call(
        functools.partial(_flash_kernel, tk=tk, nk=L // tk),
        grid=(B, L // tq),
        in_specs=[pl.BlockSpec((None, tq, 4 * LANES), lambda b, i: (b, i, 0)),
                  full, full, full, full],
        out_specs=pl.BlockSpec((None, tq, ATT_Q_W), lambda b, i: (b, i, 0)),
        out_shape=jax.ShapeDtypeStruct((B, L, ATT_Q_W), BF16),
        compiler_params=_cparams(("parallel", "arbitrary"), 40),
        name="flash_attn",
    )(qb, k, ks, v, vs)


def _conv3_tile(main, prev8, next8, w, first, last):
    tb = main.shape[0]
    row = lax.broadcasted_iota(jnp.int32, (tb, 1), 0)
    before = jnp.where(first, 0.0, prev8[SUBLANES - 1:SUBLANES, :])
    after = jnp.where(last, 0.0, next8[0:1, :])
    up = jnp.where(row == 0, before, pltpu.roll(main, 1, 0))
    dn = jnp.where(row == tb - 1, after, pltpu.roll(main, tb - 1, 0))
    return up * w[0:1, :] + main * w[1:2, :] + dn * w[2:3, :]


def _halo_specs(tb, L, col_fn):
    nb8 = L // SUBLANES
    r8 = tb // SUBLANES
    main = pl.BlockSpec((None, tb, LANES), lambda b, w, s: (b, s, col_fn(w)))
    prev = pl.BlockSpec((None, SUBLANES, LANES),
                        lambda b, w, s: (b, jnp.maximum(s * r8 - 1, 0), col_fn(w)))
    nxt = pl.BlockSpec((None, SUBLANES, LANES),
                       lambda b, w, s: (b, jnp.minimum((s + 1) * r8, nb8 - 1), col_fn(w)))
    return [main, prev, nxt]


def _hy_pro_kernel(x0_ref, x0p_ref, x0n_ref, x1_ref, x1p_ref, x1n_ref,
                   hv_ref, hvp_ref, hvn_ref, w0_ref, w1_ref, w2_ref,
                   b0_ref, b1_ref, b2_ref, z_ref, x0u_ref):
    first = pl.program_id(2) == 0
    last = pl.program_id(2) == pl.num_programs(2) - 1
    x0 = _conv3_tile(x0_ref[...], x0p_ref[...], x0n_ref[...], w0_ref[...], first, last) + b0_ref[...]
    x1 = _conv3_tile(x1_ref[...], x1p_ref[...], x1n_ref[...], w1_ref[...], first, last) + b1_ref[...]
    hv = _conv3_tile(hv_ref[...], hvp_ref[...], hvn_ref[...], w2_ref[...], first, last) + b2_ref[...]
    z_ref[...] = hv * x1
    x0u_ref[...] = x0


def _hy_prologue(p3, hcw, hcb):
    B, L, _ = p3.shape
    tb = min(1024, L)
    nw = HY_WIDTH // LANES
    in_specs = []
    for piece in range(3):
        in_specs += _halo_specs(tb, L, lambda w, piece=piece: HY_BLK + nw * piece + w)
    for piece in range(3):
        in_specs.append(pl.BlockSpec((3, LANES), lambda b, w, s, piece=piece: (0, nw * piece + w)))
    for piece in range(3):
        in_specs.append(pl.BlockSpec((1, LANES), lambda b, w, s, piece=piece: (0, nw * piece + w)))
    out_spec = pl.BlockSpec((None, tb, LANES), lambda b, w, s: (b, s, w))
    out_sh = jax.ShapeDtypeStruct((B, L, HY_WIDTH), F32)
    args = [p3] * 9 + [hcw] * 3 + [hcb] * 3
    return pl.pallas_call(
        _hy_pro_kernel,
        grid=(B, nw, L // tb),
        in_specs=in_specs,
        out_specs=[out_spec, out_spec],
        out_shape=[out_sh, out_sh],
        compiler_params=_cparams(("parallel", "parallel", "parallel"), 32),
        name="hyena_prologue",
    )(*args)


def _sconv_kernel(sb_ref, sc_ref, scp_ref, scn_ref, sh_ref, shp_ref, shn_ref, w_ref, o_ref):
    first = pl.program_id(2) == 0
    last = pl.program_id(2) == pl.num_programs(2) - 1
    y = _conv3_tile(sc_ref[...] * sh_ref[...], scp_ref[...] * shp_ref[...],
                    scn_ref[...] * shn_ref[...], w_ref[...], first, last)
    o_ref[...] = (sb_ref[...] * y).astype(o_ref.dtype)


def _sconv(p3, scw):
    B, L, _ = p3.shape
    tb = min(1024, L)
    nw = SC_WIDTH // LANES
    in_specs = [pl.BlockSpec((None, tb, LANES), lambda b, w, s: (b, s, SC_BLK + w))]
    in_specs += _halo_specs(tb, L, lambda w: SC_BLK + nw + w)
    in_specs += _halo_specs(tb, L, lambda w: SC_BLK + 2 * nw + w)
    in_specs.append(pl.BlockSpec((3, LANES), lambda b, w, s: (0, w)))
    return pl.pallas_call(
        _sconv_kernel,
        grid=(B, nw, L // tb),
        in_specs=in_specs,
        out_specs=pl.BlockSpec((None, tb, LANES), lambda b, w, s: (b, s, w)),
        out_shape=jax.ShapeDtypeStruct((B, L, SC_WIDTH), BF16),
        compiler_params=_cparams(("parallel", "parallel", "parallel"), 32),
        name="short_conv",
    )(*([p3] * 7 + [scw]))


def _hy_filter_kernel(z_ref, w1_ref, b1_ref, w2_ref, b2_ref, w3_ref, fr_ref, dl_ref,
                      k_ref, ss_ref, *, L):
    step = pl.program_id(0)
    tr = z_ref.shape[0]
    z = z_ref[...]
    h = jnp.sin(fr_ref[0:1, :] * (_mm3(z, w1_ref[...]) + b1_ref[...]))
    h = jnp.sin(fr_ref[1:2, :] * (_mm3(h, w2_ref[...]) + b2_ref[...]))
    h3 = _mm3(h, w3_ref[...])
    row = step * tr + lax.broadcasted_iota(jnp.int32, (tr, 1), 0)
    val = jnp.where(row < L, h3[:, :HY_WIDTH], h3[:, HY_WIDTH:])
    val = val * jnp.exp(-z[:, 0:1] * dl_ref[...])
    val = jnp.where(row == L, 0.0, val)
    k_ref[...] = val

    @pl.when(step == 0)
    def _():
        ss_ref[...] = jnp.zeros_like(ss_ref)

    ss_ref[...] += jnp.sum(val * val, axis=0, keepdims=True)


def _hy_filter_raw(z2, w1p, b1, w2, b2, w3, freq, deltas, L):
    n = 2 * L
    tr = min(512, n)
    H = HY_FILTER_HIDDEN
    full = lambda shape: pl.BlockSpec(shape, lambda i: (0, 0))
    return pl.pallas_call(
        functools.partial(_hy_filter_kernel, L=L),
        grid=(n // tr,),
        in_specs=[pl.BlockSpec((tr, H), lambda i: (i, 0)),
                  full((H, H)), full((1, H)), full((H, H)), full((1, H)),
                  full((H, 2 * HY_WIDTH)), full((2, H)), full((1, HY_WIDTH))],
        out_specs=[pl.BlockSpec((tr, HY_WIDTH), lambda i: (i, 0)), full((1, HY_WIDTH))],
        out_shape=[jax.ShapeDtypeStruct((n, HY_WIDTH), F32),
                   jax.ShapeDtypeStruct((1, HY_WIDTH), F32)],
        compiler_params=_cparams(("arbitrary",), 32),
        name="hyena_filter",
    )(z2, w1p, b1, w2, b2, w3, freq, deltas)


def _fft_dims(L):
    n = 2 * L
    n2 = 128 if n >= 16384 else 64
    return n // n2, n2


def _fft_consts(n1, n2, r):
    n = n1 * n2
    a1 = 2.0 * np.pi * np.outer(np.arange(n1), np.arange(n1)) / n1
    c1, s1 = np.cos(a1), np.sin(a1)
    a2 = 2.0 * np.pi * np.outer(np.arange(n2), np.arange(n2)) / n2
    c2, s2 = np.cos(a2), np.sin(a2)
    f1 = np.concatenate([c1[:, :r], -s1[:, :r]], axis=0)
    f1_full = np.concatenate([c1, -s1], axis=0)
    m2 = np.block([[c2, s2], [-s2, c2]])
    m2i = np.block([[c2, -s2], [s2, c2]])
    fi = np.concatenate([c1[:r, :], -s1[:r, :]], axis=1)
    aw = 2.0 * np.pi * np.arange(n1) / n
    w_re = np.broadcast_to(np.cos(aw)[:, None], (n1, LANES)).astype(np.float32)
    w_im = np.broadcast_to(-np.sin(aw)[:, None], (n1, LANES)).astype(np.float32)
    return dict(f1=_np_split(f1), f1_full=_np_split(f1_full), m2=_np_split(m2),
                m2i=_np_split(m2i), fi=_np_split(fi),
                w_re=jnp.asarray(w_re), w_im=jnp.asarray(w_im))


def _fft_stage1(src_ref, a_ref, f_hi, f_lo, w_re, w_im, n1, n2, r):
    def body(i, carry):
        t_re, t_im = carry
        xs = src_ref[pl.ds(i, r, stride=n2), :]
        a = _mm_const(f_hi, f_lo, xs, FFT_PASSES)
        ar, ai = a[:n1], a[n1:]
        base = pl.multiple_of(i * 2 * n1, 2 * n1)
        a_ref[pl.ds(base, n1), :] = ar * t_re - ai * t_im
        a_ref[pl.ds(base + n1, n1), :] = ar * t_im + ai * t_re
        return t_re * w_re - t_im * w_im, t_re * w_im + t_im * w_re

    ones = jnp.ones((n1, LANES), F32)
    lax.fori_loop(0, n2, body, (ones, jnp.zeros((n1, LANES), F32)))


def _filter_fft_kernel(k_ref, ss_ref, f_hi_ref, f_lo_ref, m_hi_ref, m_lo_ref, wre_ref, wim_ref,
                       kr_ref, ki_ref, a_ref, *, n1, n2, kc):
    c = pl.program_id(1)

    @pl.when(c == 0)
    def _():
        _fft_stage1(k_ref, a_ref, f_hi_ref[...], f_lo_ref[...], wre_ref[...], wim_ref[...],
                    n1, n2, n1)

    scale = lax.rsqrt(ss_ref[...] + NORM_EPS) * (1.0 / (n1 * n2))

    def body(kk, _):
        k1 = c * kc + kk
        ar = a_ref[pl.ds(k1, n2, stride=2 * n1), :]
        ai = a_ref[pl.ds(n1 + k1, n2, stride=2 * n1), :]
        x = _mm_const(m_hi_ref[...], m_lo_ref[...], jnp.concatenate([ar, ai], axis=0), FFT_PASSES)
        kr_ref[kk] = x[:n2] * scale
        ki_ref[kk] = x[n2:] * scale
        return 0

    lax.fori_loop(0, kc, body, 0)


def _filter_fft(k2raw, sumsq, cst, n1, n2):
    n = n1 * n2
    kc = min(16, n1)
    nw = HY_WIDTH // LANES
    f_hi, f_lo = cst["f1_full"]
    m_hi, m_lo = cst["m2"]
    full = lambda a: pl.BlockSpec(a.shape, lambda w, c: (0,) * a.ndim)
    out_spec = pl.BlockSpec((kc, n2, LANES), lambda w, c: (c, 0, w))
    out_sh = jax.ShapeDtypeStruct((n1, n2, HY_WIDTH), F32)
    return pl.pallas_call(
        functools.partial(_filter_fft_kernel, n1=n1, n2=n2, kc=kc),
        grid=(nw, n1 // kc),
        in_specs=[pl.BlockSpec((n, LANES), lambda w, c: (0, w)),
                  pl.BlockSpec((1, LANES), lambda w, c: (0, w)),
                  full(f_hi), full(f_lo), full(m_hi), full(m_lo),
                  full(cst["w_re"]), full(cst["w_im"])],
        out_specs=[out_spec, out_spec],
        out_shape=[out_sh, out_sh],
        scratch_shapes=[pltpu.VMEM((2 * n, LANES), F32)],
        compiler_params=_cparams(("parallel", "arbitrary"), 48),
        name="hyena_filter_fft",
    )(k2raw, sumsq, f_hi, f_lo, m_hi, m_lo, cst["w_re"], cst["w_im"])


def _hy_conv_kernel(z_ref, x0_ref, hb_ref, kr_ref, ki_ref, f_hi_ref, f_lo_ref,
                    m_hi_ref, m_lo_ref, mi_hi_ref, mi_lo_ref, fi_hi_ref, fi_lo_ref,
                    wre_ref, wim_ref, o_ref, a_ref, *, n1, n2, kc):
    c = pl.program_id(2)
    r = n1 // 2

    @pl.when(c == 0)
    def _():
        _fft_stage1(z_ref, a_ref, f_hi_ref[...], f_lo_ref[...], wre_ref[...], wim_ref[...],
                    n1, n2, r)

    def mid(kk, _):
        k1 = c * kc + kk
        ar = a_ref[pl.ds(k1, n2, stride=2 * n1), :]
        ai = a_ref[pl.ds(n1 + k1, n2, stride=2 * n1), :]
        x = _mm_const(m_hi_ref[...], m_lo_ref[...], jnp.concatenate([ar, ai], axis=0), FFT_PASSES)
        xr, xi = x[:n2], x[n2:]
        kr = kr_ref[kk]
        ki = ki_ref[kk]
        y = jnp.concatenate([xr * kr - xi * ki, xr * ki + xi * kr], axis=0)
        bq = _mm_const(mi_hi_ref[...], mi_lo_ref[...], y, FFT_PASSES)
        a_ref[pl.ds(k1, n2, stride=2 * n1), :] = bq[:n2]
        a_ref[pl.ds(n1 + k1, n2, stride=2 * n1), :] = bq[n2:]
        return 0

    lax.fori_loop(0, kc, mid, 0)

    @pl.when(c == pl.num_programs(2) - 1)
    def _():
        w_re = wre_ref[...]
        w_im = wim_ref[...]
        hb = hb_ref[...]

        def body(i, carry):
            t_re, t_im = carry
            base = pl.multiple_of(i * 2 * n1, 2 * n1)
            br = a_ref[pl.ds(base, n1), :]
            bi = a_ref[pl.ds(base + n1, n1), :]
            bt = jnp.concatenate([br * t_re + bi * t_im, bi * t_re - br * t_im], axis=0)
            y = _mm_const(fi_hi_ref[...], fi_lo_ref[...], bt, FFT_PASSES)
            zs = z_ref[pl.ds(i, r, stride=n2), :]
            xs = x0_ref[pl.ds(i, r, stride=n2), :]
            o_ref[pl.ds(i, r, stride=n2), :] = (xs * (y + zs * hb)).astype(o_ref.dtype)
            return t_re * w_re - t_im * w_im, t_re * w_im + t_im * w_re

        ones = jnp.ones((n1, LANES), F32)
        lax.fori_loop(0, n2, body, (ones, jnp.zeros((n1, LANES), F32)))


def _hy_conv(z, x0u, hbias, kf_re, kf_im, cst, n1, n2):
    B, L, _ = z.shape
    kc = min(16, n1)
    nw = HY_WIDTH // LANES
    seq = pl.BlockSpec((None, L, LANES), lambda b, w, c: (b, 0, w))
    kf_spec = pl.BlockSpec((kc, n2, LANES), lambda b, w, c: (c, 0, w))
    full = lambda a: pl.BlockSpec(a.shape, lambda b, w, c: (0,) * a.ndim)
    consts = [*cst["f1"], *cst["m2"], *cst["m2i"], *cst["fi"], cst["w_re"], cst["w_im"]]
    return pl.pallas_call(
        functools.partial(_hy_conv_kernel, n1=n1, n2=n2, kc=kc),
        grid=(B, nw, n1 // kc),
        in_specs=[seq, seq, pl.BlockSpec((1, LANES), lambda b, w, c: (0, w)), kf_spec, kf_spec]
                 + [full(a) for a in consts],
        out_specs=seq,
        out_shape=jax.ShapeDtypeStruct((B, L, HY_WIDTH), F32),
        scratch_shapes=[pltpu.VMEM((2 * n1 * n2, LANES), F32)],
        compiler_params=_cparams(("parallel", "parallel", "arbitrary"), 56),
        name="hyena_conv",
    )(z, x0u, hbias, kf_re, kf_im, *consts)


T_D0, T_D1, T_WQF, T_WQB, T_WKF, T_WKB, T_GCF, T_GCB, T_BD = range(9)


def _ret_tables_kernel(rl_ref, rh_ref, t_ref):
    C = RET_CHUNK
    i = lax.broadcasted_iota(jnp.int32, (C, LANES), 0).astype(F32)
    jn = lax.broadcasted_iota(jnp.int32, (C, LANES), 1).astype(F32)
    log_g = lambda e: jnp.log1p(-jnp.exp2(-e))
    diff = i - jn
    for hp in range(2):
        lgf = log_g(rh_ref[hp, 0:1, :])
        lgb = log_g(rh_ref[hp, 1:2, :])
        fwd = jnp.exp(jnp.where(diff >= 0, diff, 0.0) * lgf)
        bwd = jnp.exp(jnp.where(diff < 0, -diff, 0.0) * lgb)
        t_ref[T_D0 + hp] = jnp.where(diff >= 0, fwd, bwd)
    lf = log_g(rl_ref[0:1, :])
    lb = log_g(rl_ref[1:2, :])
    t_ref[T_WQF] = jnp.exp((i + 1.0) * lf)
    t_ref[T_WQB] = jnp.exp((C - i) * lb)
    t_ref[T_WKF] = jnp.exp((C - 1.0 - i) * lf)
    t_ref[T_WKB] = jnp.exp(i * lb)
    bd = jnp.where((i < HEAD_DIM) == (jn < HEAD_DIM), 1.0, 0.0)
    t_ref[T_GCF] = jnp.exp(C * lf) * bd
    t_ref[T_GCB] = jnp.exp(C * lb) * bd
    t_ref[T_BD] = bd


def _ret_tables(rde):
    per_lane = jnp.repeat(rde.reshape(2, 2, 2), HEAD_DIM, axis=2)
    per_lane = per_lane.transpose(1, 0, 2)
    per_head = jnp.broadcast_to(rde.reshape(2, 2, 2, 1), (2, 2, 2, LANES))
    per_head = per_head.transpose(1, 2, 0, 3)
    return pl.pallas_call(
        _ret_tables_kernel,
        grid=(2,),
        in_specs=[pl.BlockSpec((None, 2, LANES), lambda j: (j, 0, 0)),
                  pl.BlockSpec((None, 2, 2, LANES), lambda j: (j, 0, 0, 0))],
        out_specs=pl.BlockSpec((None, 9, RET_CHUNK, LANES), lambda j: (j, 0, 0, 0)),
        out_shape=jax.ShapeDtypeStruct((2, 9, RET_CHUNK, LANES), F32),
        compiler_params=_cparams(("parallel",), 32),
        name="retention_tables",
    )(per_lane, per_head)


def _ret_state_kernel(rk_ref, rv_ref, c_ref, s_ref, t_ref, st_ref, r_ref):
    C = RET_CHUNK
    nch = rk_ref.shape[0] // C

    @pl.when(pl.program_id(2) == 0)
    def _():
        r_ref[...] = jnp.zeros_like(r_ref)

    wkb = t_ref[T_WKB]
    gcb = t_ref[T_GCB]
    bd = t_ref[T_BD]
    state = r_ref[...]
    for n in range(nch - 1, -1, -1):
        st_ref[n] = state
        sl = slice(n * C, (n + 1) * C)
        k = _rope(rk_ref[sl, :], c_ref[sl, :], s_ref[sl, :]) * (HEAD_DIM ** -0.5)
        kw = (k * wkb).astype(BF16)
        v = rv_ref[sl, :].astype(BF16)
        kv = lax.dot_general(kw, v, (((0,), (0,)), ((), ())), preferred_element_type=F32)
        state = gcb * state + bd * kv
    r_ref[...] = state


def _ret_main_kernel(rq_ref, rk_ref, rv_ref, rg_ref, c_ref, s_ref, t_ref, st_ref, j_ref,
                     o_ref, sf_ref):
    C = RET_CHUNK
    nch = rq_ref.shape[0] // C

    @pl.when(pl.program_id(2) == 0)
    def _():
        sf_ref[...] = jnp.zeros_like(sf_ref)

    lane = lax.broadcasted_iota(jnp.int32, (1, LANES), 1)
    lo64 = lane < HEAD_DIM
    j_bf = j_ref[...]
    state = sf_ref[...]
    for n in range(nch):
        sl = slice(n * C, (n + 1) * C)
        cs, sn = c_ref[sl, :], s_ref[sl, :]
        q = _rope(rq_ref[sl, :], cs, sn)
        k = _rope(rk_ref[sl, :], cs, sn) * (HEAD_DIM ** -0.5)
        vb = rv_ref[sl, :].astype(BF16)
        qb = q.astype(BF16)
        kb = k.astype(BF16)
        zero = jnp.zeros_like(qb)
        parts = []
        for hp in range(2):
            qm = jnp.where(lo64, qb, zero) if hp == 0 else jnp.where(lo64, zero, qb)
            sc = lax.dot_general(qm, kb, (((1,), (1,)), ((), ())), preferred_element_type=F32)
            sd = (sc * t_ref[T_D0 + hp]).astype(BF16)
            parts.append(jnp.dot(sd, vb, preferred_element_type=F32))
        o = jnp.where(lo64, parts[0], parts[1])
        o = o + jnp.dot((q * t_ref[T_WQF]).astype(BF16), state.astype(BF16),
                        preferred_element_type=F32)
        o = o + jnp.dot((q * t_ref[T_WQB]).astype(BF16), st_ref[n].astype(BF16),
                        preferred_element_type=F32)
        kw = (k * t_ref[T_WKF]).astype(BF16)
        kv = lax.dot_general(kw, vb, (((0,), (0,)), ((), ())), preferred_element_type=F32)
        state = t_ref[T_GCF] * state + t_ref[T_BD] * kv
        ret = o * lax.rsqrt(_head_mean_sq(o, j_bf) + NORM_EPS)
        g = rg_ref[sl, :]
        o_ref[sl, :] = (ret * (g * _sigmoid(g))).astype(o_ref.dtype)
    sf_ref[...] = state


def _retention(p3, cos_t, sin_t, tables, j_bf):
    B, L, _ = p3.shape
    C = RET_CHUNK
    tb = min(1024, L)
    nblk = L // tb
    nch = tb // C
    nw = RET_W // LANES
    col = lambda piece: (lambda b, j, s: (b, s, RET_BLK + nw * piece + j))
    colr = lambda piece: (lambda b, j, s: (b, nblk - 1 - s, RET_BLK + nw * piece + j))
    seq = lambda fn: pl.BlockSpec((None, tb, LANES), fn)
    tab_spec = pl.BlockSpec((None, 9, C, LANES), lambda b, j, s: (j, 0, 0, 0))
    states = pl.pallas_call(
        _ret_state_kernel,
        grid=(B, nw, nblk),
        in_specs=[seq(colr(1)), seq(colr(2)),
                  pl.BlockSpec((tb, LANES), lambda b, j, s: (nblk - 1 - s, 0)),
                  pl.BlockSpec((tb, LANES), lambda b, j, s: (nblk - 1 - s, 0)),
                  tab_spec],
        out_specs=pl.BlockSpec((None, None, nch, LANES, LANES),
                               lambda b, j, s: (b, j, nblk - 1 - s, 0, 0)),
        out_shape=jax.ShapeDtypeStruct((B, nw, L // C, LANES, LANES), F32),
        scratch_shapes=[pltpu.VMEM((LANES, LANES), F32)],
        compiler_params=_cparams(("parallel", "parallel", "arbitrary"), 32),
        name="retention_state",
    )(p3, p3, cos_t, sin_t, tables)
    return pl.pallas_call(
        _ret_main_kernel,
        grid=(B, nw, nblk),
        in_specs=[seq(col(0)), seq(col(1)), seq(col(2)), seq(col(3)),
                  pl.BlockSpec((tb, LANES), lambda b, j, s: (s, 0)),
                  pl.BlockSpec((tb, LANES), lambda b, j, s: (s, 0)),
                  tab_spec,
                  pl.BlockSpec((None, None, nch, LANES, LANES), lambda b, j, s: (b, j, s, 0, 0)),
                  pl.BlockSpec((LANES, LANES), lambda b, j, s: (0, 0))],
        out_specs=pl.BlockSpec((None, tb, LANES), lambda b, j, s: (b, s, j)),
        out_shape=jax.ShapeDtypeStruct((B, L, RET_W), BF16),
        scratch_shapes=[pltpu.VMEM((LANES, LANES), F32)],
        compiler_params=_cparams(("parallel", "parallel", "arbitrary"), 32),
        name="retention_main",
    )(p3, p3, p3, p3, cos_t, sin_t, tables, states, j_bf)


def _merge_kernel(oa_ref, ob_ref, oc_ref, od_ref, g0_ref, g1_ref, g2_ref, g3_ref,
                  wb_ref, wo_ref, x_ref, gn_ref, h_ref):
    merged = None
    for n, (br, gr) in enumerate(((oa_ref, g0_ref), (ob_ref, g1_ref),
                                  (oc_ref, g2_ref), (od_ref, g3_ref))):
        term = _sigmoid(gr[...]) * jnp.dot(br[...].astype(BF16), wb_ref[n],
                                                preferred_element_type=F32)
        merged = term if merged is None else merged + term
    y = jnp.dot(merged.astype(BF16), wo_ref[...], preferred_element_type=F32)
    ms = jnp.mean(y * y, axis=-1, keepdims=True)
    h_ref[...] = x_ref[...] + y * lax.rsqrt(ms + NORM_EPS) * gn_ref[...]


def _merge(oa, ob, oc, od, p2, wb_bf, wo_bf, x2, gn):
    T = x2.shape[0]
    tm = min(512, T)
    br = pl.BlockSpec((tm, BRANCH_W), lambda i: (i, 0))
    gate = lambda n: pl.BlockSpec((tm, D_MODEL), lambda i, n=n: (i, GATE_BLK + n))
    row = pl.BlockSpec((tm, D_MODEL), lambda i: (i, 0))
    return pl.pallas_call(
        _merge_kernel,
        grid=(T // tm,),
        in_specs=[br, br, br, br, gate(0), gate(1), gate(2), gate(3),
                  pl.BlockSpec((N_BRANCH, BRANCH_W, D_MODEL), lambda i: (0, 0, 0)),
                  pl.BlockSpec((D_MODEL, D_MODEL), lambda i: (0, 0)),
                  row, pl.BlockSpec((1, D_MODEL), lambda i: (0, 0))],
        out_specs=row,
        out_shape=jax.ShapeDtypeStruct((T, D_MODEL), F32),
        compiler_params=_cparams(("parallel",), 48),
        name="merge_out_proj",
    )(oa, ob, oc, od, p2, p2, p2, p2, wb_bf, wo_bf, x2, gn)


def _ffn_kernel(h_ref, g2_ref, wg_ref, wu_ref, wd_ref, g3_ref, o_ref, hn_ref, acc_ref):
    j = pl.program_id(1)

    @pl.when(j == 0)
    def _():
        h = h_ref[...]
        ms = jnp.mean(h * h, axis=-1, keepdims=True)
        hn_ref[...] = (h * lax.rsqrt(ms + NORM_EPS) * g2_ref[...]).astype(BF16)
        acc_ref[...] = jnp.zeros_like(acc_ref)

    hn = hn_ref[...]
    g = jnp.dot(hn, wg_ref[...], preferred_element_type=F32)
    u = jnp.dot(hn, wu_ref[...], preferred_element_type=F32)
    a = (g * _sigmoid(g) * u).astype(BF16)
    acc_ref[...] += jnp.dot(a, wd_ref[...], preferred_element_type=F32)

    @pl.when(j == pl.num_programs(1) - 1)
    def _():
        f = acc_ref[...]
        ms = jnp.mean(f * f, axis=-1, keepdims=True)
        o_ref[...] = h_ref[...] + f * lax.rsqrt(ms + NORM_EPS) * g3_ref[...]


def _ffn(h2, g2, wg_bf, wu_bf, wd_bf, g3):
    T = h2.shape[0]
    tm = min(1024, T)
    tf = 256
    row = pl.BlockSpec((tm, D_MODEL), lambda i, j: (i, 0))
    vec = pl.BlockSpec((1, D_MODEL), lambda i, j: (0, 0))
    return pl.pallas_call(
        _ffn_kernel,
        grid=(T // tm, D_FF // tf),
        in_specs=[row, vec,
                  pl.BlockSpec((D_MODEL, tf), lambda i, j: (0, j)),
                  pl.BlockSpec((D_MODEL, tf), lambda i, j: (0, j)),
                  pl.BlockSpec((tf, D_MODEL), lambda i, j: (j, 0)),
                  vec],
        out_specs=row,
        out_shape=jax.ShapeDtypeStruct((T, D_MODEL), F32),
        scratch_shapes=[pltpu.VMEM((tm, D_MODEL), BF16), pltpu.VMEM((tm, D_MODEL), F32)],
        compiler_params=_cparams(("parallel", "arbitrary"), 40),
        name="ffn",
    )(h2, g2, wg_bf, wu_bf, wd_bf, g3)


def _rope_tables(L):
    rows = L // GRID_W
    r = jnp.repeat(jnp.arange(rows, dtype=F32), GRID_W)
    c = jnp.tile(jnp.arange(GRID_W, dtype=F32), rows)
    inv = ROPE_BASE ** (-jnp.arange(ROPE_FREQS, dtype=F32) / ROPE_FREQS)
    ar = r[:, None] * inv
    ac = c[:, None] * inv
    cos64 = jnp.concatenate([jnp.cos(ar), jnp.cos(ar), jnp.cos(ac), jnp.cos(ac)], axis=1)
    sin64 = jnp.concatenate([-jnp.sin(ar), jnp.sin(ar), -jnp.sin(ac), jnp.sin(ac)], axis=1)
    return jnp.tile(cos64, (1, 2)), jnp.tile(sin64, (1, 2))


def _filter_features(L):
    t = jnp.linspace(0.0, 1.0, L, dtype=F32)[:, None]
    f = jnp.linspace(1e-4, HY_BANDS - 1, HY_BANDS, dtype=F32)
    ang = (2.0 * math.pi / L) * jnp.arange(L, dtype=F32)[:, None] * f[None, :]
    z = jnp.concatenate([t, jnp.cos(ang), -jnp.sin(ang)], axis=-1)
    z = jnp.pad(z, ((0, 0), (0, HY_FILTER_HIDDEN - HY_EMB)))
    idx = np.concatenate([np.arange(L), [0], np.arange(L - 1, 0, -1)])
    return z[idx]


def _head_mean_matrix():
    i = np.arange(LANES)
    j = ((i[:, None] // HEAD_DIM) == (i[None, :] // HEAD_DIM)).astype(np.float32) / HEAD_DIM
    return jnp.asarray(j.astype(BF16))


def _trunk(x, wts, fft_dims=None):
    B, L, _ = x.shape
    T = B * L
    n1, n2 = fft_dims or _fft_dims(L)
    cst = _fft_consts(n1, n2, n1 // 2)
    cos_t, sin_t = _rope_tables(L)
    z2 = _filter_features(L)
    deltas = jnp.abs(jnp.linspace(math.log(HY_TARGET) / HY_SLOW_DECAY,
                                  math.log(HY_TARGET) / HY_FAST_DECAY, HY_WIDTH, dtype=F32))[None, :]
    j_bf = _head_mean_matrix()
    depth = wts["w_in"].shape[0]
    x2 = x.reshape(T, D_MODEL)
    for l in range(depth):
        ng = wts["norm_gains"][l]
        w1p = jnp.pad(wts["hy_w1"][l], ((0, HY_FILTER_HIDDEN - HY_EMB), (0, 0)))
        k2raw, sumsq = _hy_filter_raw(z2, w1p, wts["hy_b1"][l][None, :], wts["hy_w2"][l],
                                      wts["hy_b2"][l][None, :], wts["hy_w3"][l],
                                      wts["hy_freq"][l], deltas, L)
        kf_re, kf_im = _filter_fft(k2raw, sumsq, cst, n1, n2)

        p2 = _in_proj(x2, ng[0][None, :], wts["w_in_bf"][l])
        p3 = p2.reshape(B, L, IN_COLS)

        qg = jnp.tile(wts["qk_norm"][l, 0], 2)[None, :]
        kg = jnp.tile(wts["qk_norm"][l, 1], 2)[None, :]
        qb, k, ks, v, vs = _attn_prep(p3, cos_t, sin_t, qg, kg, j_bf)
        out_a = _flash(qb, k, ks, v, vs)

        z, x0u = _hy_prologue(p3, wts["hy_conv_w"][l], wts["hy_conv_b"][l][None, :])
        out_b = _hy_conv(z, x0u, wts["hy_bias"][l][None, :], kf_re, kf_im, cst, n1, n2)

        tables = _ret_tables(wts["ret_decay_exp"][l])
        out_c = _retention(p3, cos_t, sin_t, tables, j_bf)

        out_d = _sconv(p3, wts["sc_conv_w"][l])

        h2 = _merge(out_a.reshape(T, BRANCH_W), out_b.reshape(T, BRANCH_W),
                    out_c.reshape(T, BRANCH_W), out_d.reshape(T, BRANCH_W),
                    p2, wts["w_branch_bf"][l], wts["w_out_bf"][l], x2, ng[1][None, :])
        x2 = _ffn(h2, ng[2][None, :], wts["w_gate_bf"][l], wts["w_up_bf"][l],
                  wts["w_ffn_out_bf"][l], ng[3][None, :])
    return x2.reshape(B, L, D_MODEL)


def kernel(x_prompt, x_sample, norm_gains, w_in, qk_norm, hy_conv_w, hy_conv_b, hy_w1, hy_b1, hy_w2,
           hy_b2, hy_w3, hy_freq, hy_bias, ret_decay_exp, sc_conv_w, w_branch, w_out, w_ffn_in,
           w_ffn_out):
    wts = dict(norm_gains=norm_gains, w_in=w_in, qk_norm=qk_norm, hy_conv_w=hy_conv_w,
               hy_conv_b=hy_conv_b, hy_w1=hy_w1, hy_b1=hy_b1, hy_w2=hy_w2, hy_b2=hy_b2,
               hy_w3=hy_w3, hy_freq=hy_freq, hy_bias=hy_bias, ret_decay_exp=ret_decay_exp,
               sc_conv_w=sc_conv_w,
               w_in_bf=w_in.astype(BF16), w_branch_bf=w_branch.astype(BF16),
               w_out_bf=w_out.astype(BF16), w_gate_bf=w_ffn_in[..., :D_FF].astype(BF16),
               w_up_bf=w_ffn_in[..., D_FF:].astype(BF16), w_ffn_out_bf=w_ffn_out.astype(BF16))
    return _trunk(x_prompt, wts), _trunk(x_sample, wts)
```

```python
import functools
import math

import numpy as np
import jax
import jax.numpy as jnp
from jax import lax
from jax.experimental import pallas as pl
from jax.experimental.pallas import tpu as pltpu

F32 = jnp.float32
BF16 = jnp.bfloat16

D_MODEL = 1024
GRID_W = 64
N_BRANCH = 4
BRANCH_W = 256
HEAD_DIM = 64
ATT_Q_HEADS = 4
ATT_KV_HEADS = 2
ROPE_BASE = 10000.0
ROPE_FREQS = HEAD_DIM // 4
HY_WIDTH = BRANCH_W
HY_EMB = 33
HY_BANDS = (HY_EMB - 1) // 2
HY_FILTER_HIDDEN = 64
HY_FAST_DECAY = 0.3
HY_SLOW_DECAY = 1.5
HY_TARGET = 1e-2
RET_HEADS = 4
RET_W = RET_HEADS * HEAD_DIM
RET_CHUNK = 128
SC_WIDTH = BRANCH_W
D_FF = 2816
NORM_EPS = 1e-6

ATT_Q_W = ATT_Q_HEADS * HEAD_DIM
ATT_KV_W = ATT_KV_HEADS * HEAD_DIM
A_K_OFF = ATT_Q_W
A_V_OFF = A_K_OFF + ATT_KV_W
HY_OFF = A_V_OFF + ATT_KV_W
RET_OFF = HY_OFF + 3 * HY_WIDTH
SC_OFF = RET_OFF + 4 * RET_W
GATE_OFF = SC_OFF + 3 * SC_WIDTH
IN_COLS = GATE_OFF + N_BRANCH * D_MODEL

LANES = 128
SUBLANES = 8
HY_BLK = HY_OFF // LANES
RET_BLK = RET_OFF // LANES
SC_BLK = SC_OFF // LANES
GATE_BLK = GATE_OFF // D_MODEL
ATT_KB = 512
Q_SCALE = HEAD_DIM ** -0.5 * math.log2(math.e)
FFT_PASSES_FILTER = 3
FFT_PASSES_DATA = 1
MIB = 1 << 20


def _cparams(sem, vmem_mib):
    return pltpu.CompilerParams(dimension_semantics=sem, vmem_limit_bytes=vmem_mib * MIB)


def _sigmoid(x):
    return 1.0 / (1.0 + jnp.exp(-x))


def _split(x):
    hi = x.astype(BF16)
    lo = (x - hi.astype(F32)).astype(BF16)
    return hi, lo


def _np_split(a64):
    a32 = np.asarray(a64, np.float32)
    hi = a32.astype(BF16)
    lo = (a32 - hi.astype(np.float32)).astype(BF16)
    return jnp.asarray(hi), jnp.asarray(lo)


def _mm_const(a_hi, a_lo, x, passes):
    if passes == 1:
        return jnp.dot(a_hi, x.astype(BF16), preferred_element_type=F32)
    xh, xl = _split(x)
    out = jnp.dot(a_hi, xh, preferred_element_type=F32)
    if passes >= 3:
        out = out + jnp.dot(a_lo, xh, preferred_element_type=F32)
        out = out + jnp.dot(a_hi, xl, preferred_element_type=F32)
    return out


def _mm3(a, b):
    ah, al = _split(a)
    bh, bl = _split(b)
    out = jnp.dot(ah, bh, preferred_element_type=F32)
    out = out + jnp.dot(al, bh, preferred_element_type=F32)
    return out + jnp.dot(ah, bl, preferred_element_type=F32)


def _head_mean_sq(x, j_bf):
    hi, lo = _split(x * x)
    return (jnp.dot(hi, j_bf, preferred_element_type=F32)
            + jnp.dot(lo, j_bf, preferred_element_type=F32))


def _rope(x, c, s):
    lane = lax.broadcasted_iota(jnp.int32, (1, LANES), 1)
    is_b = (lane & 16) != 0
    partner = jnp.where(is_b, pltpu.roll(x, 16, 1), pltpu.roll(x, LANES - 16, 1))
    return x * c + partner * s


def _in_proj_kernel(x_ref, g_ref, w_ref, o_ref, xn_ref):
    @pl.when(pl.program_id(1) == 0)
    def _():
        x = x_ref[...]
        ms = jnp.mean(x * x, axis=-1, keepdims=True)
        xn_ref[...] = (x * lax.rsqrt(ms + NORM_EPS) * g_ref[...]).astype(BF16)

    o_ref[...] = jnp.dot(xn_ref[...], w_ref[...], preferred_element_type=F32)


def _in_proj(x2, g, w_bf):
    T = x2.shape[0]
    tm = min(1024, T)
    tn = 1024
    return pl.pallas_call(
        _in_proj_kernel,
        grid=(T // tm, IN_COLS // tn),
        in_specs=[pl.BlockSpec((tm, D_MODEL), lambda i, j: (i, 0)),
                  pl.BlockSpec((1, D_MODEL), lambda i, j: (0, 0)),
                  pl.BlockSpec((D_MODEL, tn), lambda i, j: (0, j))],
        out_specs=pl.BlockSpec((tm, tn), lambda i, j: (i, j)),
        out_shape=jax.ShapeDtypeStruct((T, IN_COLS), F32),
        scratch_shapes=[pltpu.VMEM((tm, D_MODEL), BF16)],
        compiler_params=_cparams(("parallel", "arbitrary"), 40),
        name="in_proj",
    )(x2, g, w_bf)


def _attn_prep_kernel(p_ref, c_ref, s_ref, qg_ref, kg_ref, j_ref, qt_ref, k_ref, vt_ref):
    c = c_ref[...]
    s = s_ref[...]
    j_bf = j_ref[...]
    tm = p_ref.shape[0]
    zeros = jnp.zeros((HEAD_DIM, tm), BF16)
    for blk in range(2):
        q = p_ref[:, LANES * blk:LANES * (blk + 1)]
        qn = q * lax.rsqrt(_head_mean_sq(q, j_bf) + NORM_EPS) * qg_ref[...]
        qt = (_rope(qn, c, s) * Q_SCALE).T.astype(BF16)
        lo, hi = qt[:HEAD_DIM], qt[HEAD_DIM:]
        if blk == 0:
            qt_ref[0] = jnp.concatenate([lo, zeros], axis=0)
            qt_ref[1] = jnp.concatenate([hi, zeros], axis=0)
        else:
            qt_ref[2] = jnp.concatenate([zeros, lo], axis=0)
            qt_ref[3] = jnp.concatenate([zeros, hi], axis=0)
    k = p_ref[:, A_K_OFF:A_K_OFF + LANES]
    kn = k * lax.rsqrt(_head_mean_sq(k, j_bf) + NORM_EPS) * kg_ref[...]
    k_ref[...] = _rope(kn, c, s).astype(BF16)
    v = p_ref[:, A_V_OFF:A_V_OFF + LANES]
    for t in range(tm // ATT_KB):
        vt_ref[t] = v[t * ATT_KB:(t + 1) * ATT_KB, :].T.astype(BF16)


def _attn_prep(p3, cos_t, sin_t, qg, kg, j_bf):
    B, L, _ = p3.shape
    tm = min(1024, L)
    tab = pl.BlockSpec((tm, LANES), lambda b, i: (i, 0))
    vec = pl.BlockSpec((1, LANES), lambda b, i: (0, 0))
    return pl.pallas_call(
        _attn_prep_kernel,
        grid=(B, L // tm),
        in_specs=[pl.BlockSpec((None, tm, HY_OFF), lambda b, i: (b, i, 0)), tab, tab, vec, vec,
                  pl.BlockSpec((LANES, LANES), lambda b, i: (0, 0))],
        out_specs=[pl.BlockSpec((None, ATT_Q_HEADS, LANES, tm), lambda b, i: (b, 0, 0, i)),
                   pl.BlockSpec((None, tm, LANES), lambda b, i: (b, i, 0)),
                   pl.BlockSpec((None, tm // ATT_KB, LANES, ATT_KB), lambda b, i: (b, i, 0, 0))],
        out_shape=[jax.ShapeDtypeStruct((B, ATT_Q_HEADS, LANES, L), BF16),
                   jax.ShapeDtypeStruct((B, L, LANES), BF16),
                   jax.ShapeDtypeStruct((B, L // ATT_KB, LANES, ATT_KB), BF16)],
        compiler_params=_cparams(("parallel", "parallel"), 32),
        name="attn_prep",
    )(p3, cos_t, sin_t, qg, kg, j_bf)


def _flash_kernel(qt_ref, k_ref, vt_ref, o_ref, *, nkb, nsub):
    tq = qt_ref.shape[2]
    group = ATT_Q_HEADS // ATT_KV_HEADS

    def body(c2, carry):
        scores = []
        for t in range(nsub):
            st = pl.multiple_of((c2 * nsub + t) * ATT_KB, ATT_KB)
            kblk = k_ref[pl.ds(st, ATT_KB), :]
            scores.append([jnp.dot(kblk, qt_ref[h], preferred_element_type=F32)
                           for h in range(ATT_Q_HEADS)])
        for t in range(nsub):
            probs, stats = [], []
            for h in range(ATT_Q_HEADS):
                m, l, _ = carry[h]
                s = scores[t][h]
                m_new = jnp.maximum(m, jnp.max(s, axis=0, keepdims=True))
                alpha = jnp.exp2(m - m_new)
                p = jnp.exp2(s - m_new)
                stats.append((m_new, alpha * l + jnp.sum(p, axis=0, keepdims=True), alpha))
                probs.append(p.astype(BF16))
            new = []
            for h in range(ATT_Q_HEADS):
                j = h // group
                vt = vt_ref[c2 * nsub + t, HEAD_DIM * j:HEAD_DIM * (j + 1), :]
                m_new, l, alpha = stats[h]
                acc = alpha * carry[h][2] + jnp.dot(vt, probs[h], preferred_element_type=F32)
                new.append((m_new, l, acc))
            carry = tuple(new)
        return carry

    init = tuple((jnp.full((1, tq), -jnp.inf, F32), jnp.zeros((1, tq), F32),
                  jnp.zeros((HEAD_DIM, tq), F32)) for _ in range(ATT_Q_HEADS))
    final = lax.fori_loop(0, nkb // nsub, body, init)
    outs = [acc / l for _, l, acc in final]
    o_ref[...] = jnp.concatenate(outs, axis=0).T.astype(o_ref.dtype)


def _flash(qt, k, vt):
    B, _, _, L = qt.shape
    tq = min(256, L)
    return pl.pallas_call(
        functools.partial(_flash_kernel, nkb=L // ATT_KB, nsub=math.gcd(L // ATT_KB, 4)),
        grid=(B, L // tq),
        in_specs=[pl.BlockSpec((None, ATT_Q_HEADS, LANES, tq), lambda b, i: (b, 0, 0, i)),
                  pl.BlockSpec((None, L, LANES), lambda b, i: (b, 0, 0)),
                  pl.BlockSpec((None, L // ATT_KB, LANES, ATT_KB), lambda b, i: (b, 0, 0, 0))],
        out_specs=pl.BlockSpec((None, tq, ATT_Q_W), lambda b, i: (b, i, 0)),
        out_shape=jax.ShapeDtypeStruct((B, L, ATT_Q_W), BF16),
        compiler_params=_cparams(("parallel", "arbitrary"), 40),
        name="flash_attn",
    )(qt, k, vt)


def _conv3_tile(main, prev8, next8, w, first, last):
    tb = main.shape[0]
    row = lax.broadcasted_iota(jnp.int32, (tb, 1), 0)
    before = jnp.where(first, 0.0, prev8[SUBLANES - 1:SUBLANES, :])
    after = jnp.where(last, 0.0, next8[0:1, :])
    up = jnp.where(row == 0, before, pltpu.roll(main, 1, 0))
    dn = jnp.where(row == tb - 1, after, pltpu.roll(main, tb - 1, 0))
    return up * w[0:1, :] + main * w[1:2, :] + dn * w[2:3, :]


def _halo_specs(tb, L, col_fn):
    nb8 = L // SUBLANES
    r8 = tb // SUBLANES
    main = pl.BlockSpec((None, tb, LANES), lambda b, w, s: (b, s, col_fn(w)))
    prev = pl.BlockSpec((None, SUBLANES, LANES),
                        lambda b, w, s: (b, jnp.maximum(s * r8 - 1, 0), col_fn(w)))
    nxt = pl.BlockSpec((None, SUBLANES, LANES),
                       lambda b, w, s: (b, jnp.minimum((s + 1) * r8, nb8 - 1), col_fn(w)))
    return [main, prev, nxt]


def _hy_pro_kernel(x0_ref, x0p_ref, x0n_ref, x1_ref, x1p_ref, x1n_ref,
                   hv_ref, hvp_ref, hvn_ref, w0_ref, w1_ref, w2_ref,
                   b0_ref, b1_ref, b2_ref, z_ref, x0u_ref):
    first = pl.program_id(2) == 0
    last = pl.program_id(2) == pl.num_programs(2) - 1
    x0 = _conv3_tile(x0_ref[...], x0p_ref[...], x0n_ref[...], w0_ref[...], first, last) + b0_ref[...]
    x1 = _conv3_tile(x1_ref[...], x1p_ref[...], x1n_ref[...], w1_ref[...], first, last) + b1_ref[...]
    hv = _conv3_tile(hv_ref[...], hvp_ref[...], hvn_ref[...], w2_ref[...], first, last) + b2_ref[...]
    z_ref[...] = hv * x1
    x0u_ref[...] = x0


def _hy_prologue(p3, hcw, hcb):
    B, L, _ = p3.shape
    tb = min(1024, L)
    nw = HY_WIDTH // LANES
    in_specs = []
    for piece in range(3):
        in_specs += _halo_specs(tb, L, lambda w, piece=piece: HY_BLK + nw * piece + w)
    for piece in range(3):
        in_specs.append(pl.BlockSpec((3, LANES), lambda b, w, s, piece=piece: (0, nw * piece + w)))
    for piece in range(3):
        in_specs.append(pl.BlockSpec((1, LANES), lambda b, w, s, piece=piece: (0, nw * piece + w)))
    out_spec = pl.BlockSpec((None, tb, LANES), lambda b, w, s: (b, s, w))
    out_sh = jax.ShapeDtypeStruct((B, L, HY_WIDTH), F32)
    args = [p3] * 9 + [hcw] * 3 + [hcb] * 3
    return pl.pallas_call(
        _hy_pro_kernel,
        grid=(B, nw, L // tb),
        in_specs=in_specs,
        out_specs=[out_spec, out_spec],
        out_shape=[out_sh, out_sh],
        compiler_params=_cparams(("parallel", "parallel", "parallel"), 32),
        name="hyena_prologue",
    )(*args)


def _sconv_kernel(sb_ref, sc_ref, scp_ref, scn_ref, sh_ref, shp_ref, shn_ref, w_ref, o_ref):
    first = pl.program_id(2) == 0
    last = pl.program_id(2) == pl.num_programs(2) - 1
    y = _conv3_tile(sc_ref[...] * sh_ref[...], scp_ref[...] * shp_ref[...],
                    scn_ref[...] * shn_ref[...], w_ref[...], first, last)
    o_ref[...] = (sb_ref[...] * y).astype(o_ref.dtype)


def _sconv(p3, scw):
    B, L, _ = p3.shape
    tb = min(1024, L)
    nw = SC_WIDTH // LANES
    in_specs = [pl.BlockSpec((None, tb, LANES), lambda b, w, s: (b, s, SC_BLK + w))]
    in_specs += _halo_specs(tb, L, lambda w: SC_BLK + nw + w)
    in_specs += _halo_specs(tb, L, lambda w: SC_BLK + 2 * nw + w)
    in_specs.append(pl.BlockSpec((3, LANES), lambda b, w, s: (0, w)))
    return pl.pallas_call(
        _sconv_kernel,
        grid=(B, nw, L // tb),
        in_specs=in_specs,
        out_specs=pl.BlockSpec((None, tb, LANES), lambda b, w, s: (b, s, w)),
        out_shape=jax.ShapeDtypeStruct((B, L, SC_WIDTH), BF16),
        compiler_params=_cparams(("parallel", "parallel", "parallel"), 32),
        name="short_conv",
    )(*([p3] * 7 + [scw]))


def _hy_filter_kernel(z_ref, w1_ref, b1_ref, w2_ref, b2_ref, w3_ref, fr_ref, dl_ref,
                      k_ref, ss_ref, *, L):
    step = pl.program_id(0)
    tr = z_ref.shape[0]
    z = z_ref[...]
    h = jnp.sin(fr_ref[0:1, :] * (_mm3(z, w1_ref[...]) + b1_ref[...]))
    h = jnp.sin(fr_ref[1:2, :] * (_mm3(h, w2_ref[...]) + b2_ref[...]))
    h3 = _mm3(h, w3_ref[...])
    row = step * tr + lax.broadcasted_iota(jnp.int32, (tr, 1), 0)
    val = jnp.where(row < L, h3[:, :HY_WIDTH], h3[:, HY_WIDTH:])
    val = val * jnp.exp(-z[:, 0:1] * dl_ref[...])
    val = jnp.where(row == L, 0.0, val)
    k_ref[...] = val

    @pl.when(step == 0)
    def _():
        ss_ref[...] = jnp.zeros_like(ss_ref)

    ss_ref[...] += jnp.sum(val * val, axis=0, keepdims=True)


def _hy_filter_raw(z2, w1p, b1, w2, b2, w3, freq, deltas, L):
    n = 2 * L
    tr = min(512, n)
    H = HY_FILTER_HIDDEN
    full = lambda shape: pl.BlockSpec(shape, lambda i: (0, 0))
    return pl.pallas_call(
        functools.partial(_hy_filter_kernel, L=L),
        grid=(n // tr,),
        in_specs=[pl.BlockSpec((tr, H), lambda i: (i, 0)),
                  full((H, H)), full((1, H)), full((H, H)), full((1, H)),
                  full((H, 2 * HY_WIDTH)), full((2, H)), full((1, HY_WIDTH))],
        out_specs=[pl.BlockSpec((tr, HY_WIDTH), lambda i: (i, 0)), full((1, HY_WIDTH))],
        out_shape=[jax.ShapeDtypeStruct((n, HY_WIDTH), F32),
                   jax.ShapeDtypeStruct((1, HY_WIDTH), F32)],
        compiler_params=_cparams(("arbitrary",), 32),
        name="hyena_filter",
    )(z2, w1p, b1, w2, b2, w3, freq, deltas)


def _fft_dims(L):
    n = 2 * L
    n2 = 128 if n >= 16384 else 64
    return n // n2, n2


def _fft_consts(n1, n2, r):
    n = n1 * n2
    a1 = 2.0 * np.pi * np.outer(np.arange(n1), np.arange(n1)) / n1
    c1, s1 = np.cos(a1), np.sin(a1)
    a2 = 2.0 * np.pi * np.outer(np.arange(n2), np.arange(n2)) / n2
    c2, s2 = np.cos(a2), np.sin(a2)
    f1 = np.concatenate([c1[:, :r], -s1[:, :r]], axis=0)
    f1_full = np.concatenate([c1, -s1], axis=0)
    m2 = np.block([[c2, s2], [-s2, c2]])
    m2i = np.block([[c2, -s2], [s2, c2]])
    fi = np.concatenate([c1[:r, :], -s1[:r, :]], axis=1)
    aw = 2.0 * np.pi * np.arange(n1) / n
    w_re = np.broadcast_to(np.cos(aw)[:, None], (n1, LANES)).astype(np.float32)
    w_im = np.broadcast_to(-np.sin(aw)[:, None], (n1, LANES)).astype(np.float32)
    return dict(f1=_np_split(f1), f1_full=_np_split(f1_full), m2=_np_split(m2),
                m2i=_np_split(m2i), fi=_np_split(fi),
                w_re=jnp.asarray(w_re), w_im=jnp.asarray(w_im))


def _cmul(a_re, a_im, b_re, b_im):
    return a_re * b_re - a_im * b_im, a_re * b_im + a_im * b_re


def _fft_stage1(src_ref, a_ref, f_hi, f_lo, w_re, w_im, n1, n2, r, passes):
    def body(i2, carry):
        t_re, t_im = carry
        u_re, u_im = _cmul(t_re, t_im, w_re, w_im)
        i = 2 * i2
        xs = jnp.concatenate([src_ref[pl.ds(i, r, stride=n2), :],
                              src_ref[pl.ds(i + 1, r, stride=n2), :]], axis=1)
        a = _mm_const(f_hi, f_lo, xs, passes)
        for half, (tr, ti) in enumerate(((t_re, t_im), (u_re, u_im))):
            ar = a[:n1, LANES * half:LANES * (half + 1)]
            ai = a[n1:, LANES * half:LANES * (half + 1)]
            base = pl.multiple_of((i + half) * 2 * n1, 2 * n1)
            a_ref[pl.ds(base, n1), :] = ar * tr - ai * ti
            a_ref[pl.ds(base + n1, n1), :] = ar * ti + ai * tr
        return _cmul(u_re, u_im, w_re, w_im)

    ones = jnp.ones((n1, LANES), F32)
    lax.fori_loop(0, n2 // 2, body, (ones, jnp.zeros((n1, LANES), F32)))


def _filter_fft_kernel(k_ref, ss_ref, f_hi_ref, f_lo_ref, m_hi_ref, m_lo_ref, wre_ref, wim_ref,
                       kr_ref, ki_ref, a_ref, *, n1, n2, kc):
    c = pl.program_id(1)

    @pl.when(c == 0)
    def _():
        _fft_stage1(k_ref, a_ref, f_hi_ref[...], f_lo_ref[...], wre_ref[...], wim_ref[...],
                    n1, n2, n1, FFT_PASSES_FILTER)

    scale = lax.rsqrt(ss_ref[...] + NORM_EPS) * (1.0 / (n1 * n2))

    def body(kk, _):
        k1 = c * kc + kk
        ar = a_ref[pl.ds(k1, n2, stride=2 * n1), :]
        ai = a_ref[pl.ds(n1 + k1, n2, stride=2 * n1), :]
        x = _mm_const(m_hi_ref[...], m_lo_ref[...], jnp.concatenate([ar, ai], axis=0),
                      FFT_PASSES_FILTER)
        kr_ref[kk] = x[:n2] * scale
        ki_ref[kk] = x[n2:] * scale
        return 0

    lax.fori_loop(0, kc, body, 0)


def _filter_fft(k2raw, sumsq, cst, n1, n2):
    n = n1 * n2
    kc = min(16, n1)
    nw = HY_WIDTH // LANES
    f_hi, f_lo = cst["f1_full"]
    m_hi, m_lo = cst["m2"]
    full = lambda a: pl.BlockSpec(a.shape, lambda w, c: (0,) * a.ndim)
    out_spec = pl.BlockSpec((kc, n2, LANES), lambda w, c: (c, 0, w))
    out_sh = jax.ShapeDtypeStruct((n1, n2, HY_WIDTH), F32)
    return pl.pallas_call(
        functools.partial(_filter_fft_kernel, n1=n1, n2=n2, kc=kc),
        grid=(nw, n1 // kc),
        in_specs=[pl.BlockSpec((n, LANES), lambda w, c: (0, w)),
                  pl.BlockSpec((1, LANES), lambda w, c: (0, w)),
                  full(f_hi), full(f_lo), full(m_hi), full(m_lo),
                  full(cst["w_re"]), full(cst["w_im"])],
        out_specs=[out_spec, out_spec],
        out_shape=[out_sh, out_sh],
        scratch_shapes=[pltpu.VMEM((2 * n, LANES), F32)],
        compiler_params=_cparams(("parallel", "arbitrary"), 48),
        name="hyena_filter_fft",
    )(k2raw, sumsq, f_hi, f_lo, m_hi, m_lo, cst["w_re"], cst["w_im"])


def _hy_conv_kernel(z_ref, x0_ref, hb_ref, kr_ref, ki_ref, f_hi_ref, f_lo_ref,
                    m_hi_ref, m_lo_ref, mi_hi_ref, mi_lo_ref, fi_hi_ref, fi_lo_ref,
                    wre_ref, wim_ref, o_ref, a_ref, *, n1, n2, kc):
    c = pl.program_id(2)
    r = n1 // 2

    @pl.when(c == 0)
    def _():
        _fft_stage1(z_ref, a_ref, f_hi_ref[...], f_lo_ref[...], wre_ref[...], wim_ref[...],
                    n1, n2, r, FFT_PASSES_DATA)

    def mid(kk2, _):
        k1 = c * kc + 2 * kk2
        ld = lambda off: a_ref[pl.ds(off, n2, stride=2 * n1), :]
        xin = jnp.concatenate([jnp.concatenate([ld(k1), ld(k1 + 1)], axis=1),
                               jnp.concatenate([ld(n1 + k1), ld(n1 + k1 + 1)], axis=1)], axis=0)
        x = _mm_const(m_hi_ref[...], m_lo_ref[...], xin, FFT_PASSES_DATA)
        xr, xi = x[:n2], x[n2:]
        kr = jnp.concatenate([kr_ref[2 * kk2], kr_ref[2 * kk2 + 1]], axis=1)
        ki = jnp.concatenate([ki_ref[2 * kk2], ki_ref[2 * kk2 + 1]], axis=1)
        y = jnp.concatenate([xr * kr - xi * ki, xr * ki + xi * kr], axis=0)
        bq = _mm_const(mi_hi_ref[...], mi_lo_ref[...], y, FFT_PASSES_DATA)
        for half in range(2):
            sl = slice(LANES * half, LANES * (half + 1))
            a_ref[pl.ds(k1 + half, n2, stride=2 * n1), :] = bq[:n2, sl]
            a_ref[pl.ds(n1 + k1 + half, n2, stride=2 * n1), :] = bq[n2:, sl]
        return 0

    lax.fori_loop(0, kc // 2, mid, 0, unroll=4)

    @pl.when(c == pl.num_programs(2) - 1)
    def _():
        w_re = wre_ref[...]
        w_im = wim_ref[...]
        hb = hb_ref[...]

        def body(i2, carry):
            t_re, t_im = carry
            u_re, u_im = _cmul(t_re, t_im, w_re, w_im)
            i = 2 * i2
            cols = []
            for half, (tr, ti) in enumerate(((t_re, t_im), (u_re, u_im))):
                base = pl.multiple_of((i + half) * 2 * n1, 2 * n1)
                br = a_ref[pl.ds(base, n1), :]
                bi = a_ref[pl.ds(base + n1, n1), :]
                cols.append(jnp.concatenate([br * tr + bi * ti, bi * tr - br * ti], axis=0))
            y = _mm_const(fi_hi_ref[...], fi_lo_ref[...], jnp.concatenate(cols, axis=1),
                          FFT_PASSES_DATA)
            for half in range(2):
                rows = pl.ds(i + half, r, stride=n2)
                yh = y[:, LANES * half:LANES * (half + 1)]
                o_ref[rows, :] = (x0_ref[rows, :] * (yh + z_ref[rows, :] * hb)).astype(o_ref.dtype)
            return _cmul(u_re, u_im, w_re, w_im)

        ones = jnp.ones((n1, LANES), F32)
        lax.fori_loop(0, n2 // 2, body, (ones, jnp.zeros((n1, LANES), F32)))


def _hy_conv(z, x0u, hbias, kf_re, kf_im, cst, n1, n2):
    B, L, _ = z.shape
    kc = min(16, n1)
    nw = HY_WIDTH // LANES
    seq = pl.BlockSpec((None, L, LANES), lambda b, w, c: (b, 0, w))
    kf_spec = pl.BlockSpec((kc, n2, LANES), lambda b, w, c: (c, 0, w))
    full = lambda a: pl.BlockSpec(a.shape, lambda b, w, c: (0,) * a.ndim)
    consts = [*cst["f1"], *cst["m2"], *cst["m2i"], *cst["fi"], cst["w_re"], cst["w_im"]]
    return pl.pallas_call(
        functools.partial(_hy_conv_kernel, n1=n1, n2=n2, kc=kc),
        grid=(B, nw, n1 // kc),
        in_specs=[seq, seq, pl.BlockSpec((1, LANES), lambda b, w, c: (0, w)), kf_spec, kf_spec]
                 + [full(a) for a in consts],
        out_specs=seq,
        out_shape=jax.ShapeDtypeStruct((B, L, HY_WIDTH), F32),
        scratch_shapes=[pltpu.VMEM((2 * n1 * n2, LANES), F32)],
        compiler_params=_cparams(("parallel", "parallel", "arbitrary"), 56),
        name="hyena_conv",
    )(z, x0u, hbias, kf_re, kf_im, *consts)


T_D0, T_D1, T_WQF, T_WQB, T_WKF, T_WKB, T_GCF, T_GCB, T_BD = range(9)


def _ret_tables_kernel(rl_ref, rh_ref, t_ref):
    C = RET_CHUNK
    i = lax.broadcasted_iota(jnp.int32, (C, LANES), 0).astype(F32)
    jn = lax.broadcasted_iota(jnp.int32, (C, LANES), 1).astype(F32)
    log_g = lambda e: jnp.log1p(-jnp.exp2(-e))
    diff = i - jn
    for hp in range(2):
        lgf = log_g(rh_ref[hp, 0:1, :])
        lgb = log_g(rh_ref[hp, 1:2, :])
        fwd = jnp.exp(jnp.where(diff >= 0, diff, 0.0) * lgf)
        bwd = jnp.exp(jnp.where(diff < 0, -diff, 0.0) * lgb)
        t_ref[T_D0 + hp] = jnp.where(diff >= 0, fwd, bwd)
    lf = log_g(rl_ref[0:1, :])
    lb = log_g(rl_ref[1:2, :])
    t_ref[T_WQF] = jnp.exp((i + 1.0) * lf)
    t_ref[T_WQB] = jnp.exp((C - i) * lb)
    t_ref[T_WKF] = jnp.exp((C - 1.0 - i) * lf)
    t_ref[T_WKB] = jnp.exp(i * lb)
    bd = jnp.where((i < HEAD_DIM) == (jn < HEAD_DIM), 1.0, 0.0)
    t_ref[T_GCF] = jnp.exp(C * lf) * bd
    t_ref[T_GCB] = jnp.exp(C * lb) * bd
    t_ref[T_BD] = bd


def _ret_tables(rde):
    per_lane = jnp.repeat(rde.reshape(2, 2, 2), HEAD_DIM, axis=2)
    per_lane = per_lane.transpose(1, 0, 2)
    per_head = jnp.broadcast_to(rde.reshape(2, 2, 2, 1), (2, 2, 2, LANES))
    per_head = per_head.transpose(1, 2, 0, 3)
    return pl.pallas_call(
        _ret_tables_kernel,
        grid=(2,),
        in_specs=[pl.BlockSpec((None, 2, LANES), lambda j: (j, 0, 0)),
                  pl.BlockSpec((None, 2, 2, LANES), lambda j: (j, 0, 0, 0))],
        out_specs=pl.BlockSpec((None, 9, RET_CHUNK, LANES), lambda j: (j, 0, 0, 0)),
        out_shape=jax.ShapeDtypeStruct((2, 9, RET_CHUNK, LANES), F32),
        compiler_params=_cparams(("parallel",), 32),
        name="retention_tables",
    )(per_lane, per_head)


def _ret_state_kernel(rk_ref, rv_ref, c_ref, s_ref, t_ref, st_ref, r_ref):
    C = RET_CHUNK
    nch = rk_ref.shape[0] // C

    @pl.when(pl.program_id(2) == 0)
    def _():
        r_ref[...] = jnp.zeros_like(r_ref)

    wkb = t_ref[T_WKB]
    gcb = t_ref[T_GCB]
    bd = t_ref[T_BD]
    state = r_ref[...]
    for n in range(nch - 1, -1, -1):
        st_ref[n] = state
        sl = slice(n * C, (n + 1) * C)
        k = _rope(rk_ref[sl, :], c_ref[sl, :], s_ref[sl, :]) * (HEAD_DIM ** -0.5)
        kw = (k * wkb).astype(BF16)
        v = rv_ref[sl, :].astype(BF16)
        kv = lax.dot_general(kw, v, (((0,), (0,)), ((), ())), preferred_element_type=F32)
        state = gcb * state + bd * kv
    r_ref[...] = state


def _ret_main_kernel(rq_ref, rk_ref, rv_ref, rg_ref, c_ref, s_ref, t_ref, st_ref, j_ref,
                     o_ref, sf_ref):
    C = RET_CHUNK
    nch = rq_ref.shape[0] // C

    @pl.when(pl.program_id(2) == 0)
    def _():
        sf_ref[...] = jnp.zeros_like(sf_ref)

    lane = lax.broadcasted_iota(jnp.int32, (1, LANES), 1)
    lo64 = lane < HEAD_DIM
    j_bf = j_ref[...]
    state = sf_ref[...]
    for n in range(nch):
        sl = slice(n * C, (n + 1) * C)
        cs, sn = c_ref[sl, :], s_ref[sl, :]
        q = _rope(rq_ref[sl, :], cs, sn)
        k = _rope(rk_ref[sl, :], cs, sn) * (HEAD_DIM ** -0.5)
        vb = rv_ref[sl, :].astype(BF16)
        qb = q.astype(BF16)
        kb = k.astype(BF16)
        zero = jnp.zeros_like(qb)
        parts = []
        for hp in range(2):
            qm = jnp.where(lo64, qb, zero) if hp == 0 else jnp.where(lo64, zero, qb)
            sc = lax.dot_general(qm, kb, (((1,), (1,)), ((), ())), preferred_element_type=F32)
            sd = (sc * t_ref[T_D0 + hp]).astype(BF16)
            parts.append(jnp.dot(sd, vb, preferred_element_type=F32))
        o = jnp.where(lo64, parts[0], parts[1])
        o = o + jnp.dot((q * t_ref[T_WQF]).astype(BF16), state.astype(BF16),
                        preferred_element_type=F32)
        o = o + jnp.dot((q * t_ref[T_WQB]).astype(BF16), st_ref[n].astype(BF16),
                        preferred_element_type=F32)
        kw = (k * t_ref[T_WKF]).astype(BF16)
        kv = lax.dot_general(kw, vb, (((0,), (0,)), ((), ())), preferred_element_type=F32)
        state = t_ref[T_GCF] * state + t_ref[T_BD] * kv
        ret = o * lax.rsqrt(_head_mean_sq(o, j_bf) + NORM_EPS)
        g = rg_ref[sl, :]
        o_ref[sl, :] = (ret * (g * _sigmoid(g))).astype(o_ref.dtype)
    sf_ref[...] = state


def _retention(p3, cos_t, sin_t, tables, j_bf):
    B, L, _ = p3.shape
    C = RET_CHUNK
    tb = min(1024, L)
    nblk = L // tb
    nch = tb // C
    nw = RET_W // LANES
    col = lambda piece: (lambda b, j, s: (b, s, RET_BLK + nw * piece + j))
    colr = lambda piece: (lambda b, j, s: (b, nblk - 1 - s, RET_BLK + nw * piece + j))
    seq = lambda fn: pl.BlockSpec((None, tb, LANES), fn)
    tab_spec = pl.BlockSpec((None, 9, C, LANES), lambda b, j, s: (j, 0, 0, 0))
    states = pl.pallas_call(
        _ret_state_kernel,
        grid=(B, nw, nblk),
        in_specs=[seq(colr(1)), seq(colr(2)),
                  pl.BlockSpec((tb, LANES), lambda b, j, s: (nblk - 1 - s, 0)),
                  pl.BlockSpec((tb, LANES), lambda b, j, s: (nblk - 1 - s, 0)),
                  tab_spec],
        out_specs=pl.BlockSpec((None, None, nch, LANES, LANES),
                               lambda b, j, s: (b, j, nblk - 1 - s, 0, 0)),
        out_shape=jax.ShapeDtypeStruct((B, nw, L // C, LANES, LANES), F32),
        scratch_shapes=[pltpu.VMEM((LANES, LANES), F32)],
        compiler_params=_cparams(("parallel", "parallel", "arbitrary"), 32),
        name="retention_state",
    )(p3, p3, cos_t, sin_t, tables)
    return pl.pallas_call(
        _ret_main_kernel,
        grid=(B, nw, nblk),
        in_specs=[seq(col(0)), seq(col(1)), seq(col(2)), seq(col(3)),
                  pl.BlockSpec((tb, LANES), lambda b, j, s: (s, 0)),
                  pl.BlockSpec((tb, LANES), lambda b, j, s: (s, 0)),
                  tab_spec,
                  pl.BlockSpec((None, None, nch, LANES, LANES), lambda b, j, s: (b, j, s, 0, 0)),
                  pl.BlockSpec((LANES, LANES), lambda b, j, s: (0, 0))],
        out_specs=pl.BlockSpec((None, tb, LANES), lambda b, j, s: (b, s, j)),
        out_shape=jax.ShapeDtypeStruct((B, L, RET_W), BF16),
        scratch_shapes=[pltpu.VMEM((LANES, LANES), F32)],
        compiler_params=_cparams(("parallel", "parallel", "arbitrary"), 32),
        name="retention_main",
    )(p3, p3, p3, p3, cos_t, sin_t, tables, states, j_bf)


def _merge_kernel(oa_ref, ob_ref, oc_ref, od_ref, g0_ref, g1_ref, g2_ref, g3_ref,
                  wb_ref, wo_ref, x_ref, gn_ref, h_ref):
    merged = None
    for n, (br, gr) in enumerate(((oa_ref, g0_ref), (ob_ref, g1_ref),
                                  (oc_ref, g2_ref), (od_ref, g3_ref))):
        term = _sigmoid(gr[...]) * jnp.dot(br[...].astype(BF16), wb_ref[n],
                                                preferred_element_type=F32)
        merged = term if merged is None else merged + term
    y = jnp.dot(merged.astype(BF16), wo_ref[...], preferred_element_type=F32)
    ms = jnp.mean(y * y, axis=-1, keepdims=True)
    h_ref[...] = x_ref[...] + y * lax.rsqrt(ms + NORM_EPS) * gn_ref[...]


def _merge(oa, ob, oc, od, p2, wb_bf, wo_bf, x2, gn):
    T = x2.shape[0]
    tm = min(512, T)
    br = pl.BlockSpec((tm, BRANCH_W), lambda i: (i, 0))
    gate = lambda n: pl.BlockSpec((tm, D_MODEL), lambda i, n=n: (i, GATE_BLK + n))
    row = pl.BlockSpec((tm, D_MODEL), lambda i: (i, 0))
    return pl.pallas_call(
        _merge_kernel,
        grid=(T // tm,),
        in_specs=[br, br, br, br, gate(0), gate(1), gate(2), gate(3),
                  pl.BlockSpec((N_BRANCH, BRANCH_W, D_MODEL), lambda i: (0, 0, 0)),
                  pl.BlockSpec((D_MODEL, D_MODEL), lambda i: (0, 0)),
                  row, pl.BlockSpec((1, D_MODEL), lambda i: (0, 0))],
        out_specs=row,
        out_shape=jax.ShapeDtypeStruct((T, D_MODEL), F32),
        compiler_params=_cparams(("parallel",), 48),
        name="merge_out_proj",
    )(oa, ob, oc, od, p2, p2, p2, p2, wb_bf, wo_bf, x2, gn)


def _ffn_kernel(h_ref, g2_ref, wg_ref, wu_ref, wd_ref, g3_ref, o_ref, hn_ref, acc_ref):
    j = pl.program_id(1)

    @pl.when(j == 0)
    def _():
        h = h_ref[...]
        ms = jnp.mean(h * h, axis=-1, keepdims=True)
        hn_ref[...] = (h * lax.rsqrt(ms + NORM_EPS) * g2_ref[...]).astype(BF16)
        acc_ref[...] = jnp.zeros_like(acc_ref)

    hn = hn_ref[...]
    g = jnp.dot(hn, wg_ref[...], preferred_element_type=F32)
    u = jnp.dot(hn, wu_ref[...], preferred_element_type=F32)
    a = (g * _sigmoid(g) * u).astype(BF16)
    acc_ref[...] += jnp.dot(a, wd_ref[...], preferred_element_type=F32)

    @pl.when(j == pl.num_programs(1) - 1)
    def _():
        f = acc_ref[...]
        ms = jnp.mean(f * f, axis=-1, keepdims=True)
        o_ref[...] = h_ref[...] + f * lax.rsqrt(ms + NORM_EPS) * g3_ref[...]


def _ffn(h2, g2, wg_bf, wu_bf, wd_bf, g3):
    T = h2.shape[0]
    tm = min(1024, T)
    tf = 256
    row = pl.BlockSpec((tm, D_MODEL), lambda i, j: (i, 0))
    vec = pl.BlockSpec((1, D_MODEL), lambda i, j: (0, 0))
    return pl.pallas_call(
        _ffn_kernel,
        grid=(T // tm, D_FF // tf),
        in_specs=[row, vec,
                  pl.BlockSpec((D_MODEL, tf), lambda i, j: (0, j)),
                  pl.BlockSpec((D_MODEL, tf), lambda i, j: (0, j)),
                  pl.BlockSpec((tf, D_MODEL), lambda i, j: (j, 0)),
                  vec],
        out_specs=row,
        out_shape=jax.ShapeDtypeStruct((T, D_MODEL), F32),
        scratch_shapes=[pltpu.VMEM((tm, D_MODEL), BF16), pltpu.VMEM((tm, D_MODEL), F32)],
        compiler_params=_cparams(("parallel", "arbitrary"), 40),
        name="ffn",
    )(h2, g2, wg_bf, wu_bf, wd_bf, g3)


def _rope_tables(L):
    rows = L // GRID_W
    r = jnp.repeat(jnp.arange(rows, dtype=F32), GRID_W)
    c = jnp.tile(jnp.arange(GRID_W, dtype=F32), rows)
    inv = ROPE_BASE ** (-jnp.arange(ROPE_FREQS, dtype=F32) / ROPE_FREQS)
    ar = r[:, None] * inv
    ac = c[:, None] * inv
    cos64 = jnp.concatenate([jnp.cos(ar), jnp.cos(ar), jnp.cos(ac), jnp.cos(ac)], axis=1)
    sin64 = jnp.concatenate([-jnp.sin(ar), jnp.sin(ar), -jnp.sin(ac), jnp.sin(ac)], axis=1)
    return jnp.tile(cos64, (1, 2)), jnp.tile(sin64, (1, 2))


def _filter_features(L):
    t = jnp.linspace(0.0, 1.0, L, dtype=F32)[:, None]
    f = jnp.linspace(1e-4, HY_BANDS - 1, HY_BANDS, dtype=F32)
    ang = (2.0 * math.pi / L) * jnp.arange(L, dtype=F32)[:, None] * f[None, :]
    z = jnp.concatenate([t, jnp.cos(ang), -jnp.sin(ang)], axis=-1)
    z = jnp.pad(z, ((0, 0), (0, HY_FILTER_HIDDEN - HY_EMB)))
    idx = np.concatenate([np.arange(L), [0], np.arange(L - 1, 0, -1)])
    return z[idx]


def _head_mean_matrix():
    i = np.arange(LANES)
    j = ((i[:, None] // HEAD_DIM) == (i[None, :] // HEAD_DIM)).astype(np.float32) / HEAD_DIM
    return jnp.asarray(j.astype(BF16))


def _trunk(x, wts, fft_dims=None):
    B, L, _ = x.shape
    T = B * L
    n1, n2 = fft_dims or _fft_dims(L)
    cst = _fft_consts(n1, n2, n1 // 2)
    cos_t, sin_t = _rope_tables(L)
    z2 = _filter_features(L)
    deltas = jnp.abs(jnp.linspace(math.log(HY_TARGET) / HY_SLOW_DECAY,
                                  math.log(HY_TARGET) / HY_FAST_DECAY, HY_WIDTH, dtype=F32))[None, :]
    j_bf = _head_mean_matrix()
    depth = wts["w_in"].shape[0]
    x2 = x.reshape(T, D_MODEL)
    for l in range(depth):
        ng = wts["norm_gains"][l]
        w1p = jnp.pad(wts["hy_w1"][l], ((0, HY_FILTER_HIDDEN - HY_EMB), (0, 0)))
        k2raw, sumsq = _hy_filter_raw(z2, w1p, wts["hy_b1"][l][None, :], wts["hy_w2"][l],
                                      wts["hy_b2"][l][None, :], wts["hy_w3"][l],
                                      wts["hy_freq"][l], deltas, L)
        kf_re, kf_im = _filter_fft(k2raw, sumsq, cst, n1, n2)

        p2 = _in_proj(x2, ng[0][None, :], wts["w_in_bf"][l])
        p3 = p2.reshape(B, L, IN_COLS)

        qg = jnp.tile(wts["qk_norm"][l, 0], 2)[None, :]
        kg = jnp.tile(wts["qk_norm"][l, 1], 2)[None, :]
        qt, k, vt = _attn_prep(p3, cos_t, sin_t, qg, kg, j_bf)
        out_a = _flash(qt, k, vt)

        z, x0u = _hy_prologue(p3, wts["hy_conv_w"][l], wts["hy_conv_b"][l][None, :])
        out_b = _hy_conv(z, x0u, wts["hy_bias"][l][None, :], kf_re, kf_im, cst, n1, n2)

        tables = _ret_tables(wts["ret_decay_exp"][l])
        out_c = _retention(p3, cos_t, sin_t, tables, j_bf)

        out_d = _sconv(p3, wts["sc_conv_w"][l])

        h2 = _merge(out_a.reshape(T, BRANCH_W), out_b.reshape(T, BRANCH_W),
                    out_c.reshape(T, BRANCH_W), out_d.reshape(T, BRANCH_W),
                    p2, wts["w_branch_bf"][l], wts["w_out_bf"][l], x2, ng[1][None, :])
        x2 = _ffn(h2, ng[2][None, :], wts["w_gate_bf"][l], wts["w_up_bf"][l],
                  wts["w_ffn_out_bf"][l], ng[3][None, :])
    return x2.reshape(B, L, D_MODEL)


def kernel(x_prompt, x_sample, norm_gains, w_in, qk_norm, hy_conv_w, hy_conv_b, hy_w1, hy_b1, hy_w2,
           hy_b2, hy_w3, hy_freq, hy_bias, ret_decay_exp, sc_conv_w, w_branch, w_out, w_ffn_in,
           w_ffn_out):
    wts = dict(norm_gains=norm_gains, w_in=w_in, qk_norm=qk_norm, hy_conv_w=hy_conv_w,
               hy_conv_b=hy_conv_b, hy_w1=hy_w1, hy_b1=hy_b1, hy_w2=hy_w2, hy_b2=hy_b2,
               hy_w3=hy_w3, hy_freq=hy_freq, hy_bias=hy_bias, ret_decay_exp=ret_decay_exp,
               sc_conv_w=sc_conv_w,
               w_in_bf=w_in.astype(BF16), w_branch_bf=w_branch.astype(BF16),
               w_out_bf=w_out.astype(BF16), w_gate_bf=w_ffn_in[..., :D_FF].astype(BF16),
               w_up_bf=w_ffn_in[..., D_FF:].astype(BF16), w_ffn_out_bf=w_ffn_out.astype(BF16))
    return _trunk(x_prompt, wts), _trunk(x_sample, wts)
```

```python
import functools
import math

import numpy as np
import jax
import jax.numpy as jnp
from jax import lax
from jax.experimental import pallas as pl
from jax.experimental.pallas import tpu as pltpu

F32 = jnp.float32
BF16 = jnp.bfloat16

D_MODEL = 1024
GRID_W = 64
N_BRANCH = 4
BRANCH_W = 256
HEAD_DIM = 64
ATT_Q_HEADS = 4
ATT_KV_HEADS = 2
ROPE_BASE = 10000.0
ROPE_FREQS = HEAD_DIM // 4
HY_WIDTH = BRANCH_W
HY_EMB = 33
HY_BANDS = (HY_EMB - 1) // 2
HY_FILTER_HIDDEN = 64
HY_FAST_DECAY = 0.3
HY_SLOW_DECAY = 1.5
HY_TARGET = 1e-2
RET_HEADS = 4
RET_W = RET_HEADS * HEAD_DIM
RET_CHUNK = 128
SC_WIDTH = BRANCH_W
D_FF = 2816
NORM_EPS = 1e-6

ATT_Q_W = ATT_Q_HEADS * HEAD_DIM
ATT_KV_W = ATT_KV_HEADS * HEAD_DIM
A_K_OFF = ATT_Q_W
A_V_OFF = A_K_OFF + ATT_KV_W
HY_OFF = A_V_OFF + ATT_KV_W
RET_OFF = HY_OFF + 3 * HY_WIDTH
SC_OFF = RET_OFF + 4 * RET_W
GATE_OFF = SC_OFF + 3 * SC_WIDTH
IN_COLS = GATE_OFF + N_BRANCH * D_MODEL

LANES = 128
SUBLANES = 8
HALO = 2 * SUBLANES
HY_BLK = HY_OFF // LANES
RET_BLK = RET_OFF // LANES
SC_BLK = SC_OFF // LANES
GATE_BLK = GATE_OFF // D_MODEL
ATT_KB = 512
Q_SCALE = HEAD_DIM ** -0.5 * math.log2(math.e)
FFT_PASSES_FILTER = 3
FFT_PASSES_DATA = 1
MIB = 1 << 20


def _cparams(sem, vmem_mib):
    return pltpu.CompilerParams(dimension_semantics=sem, vmem_limit_bytes=vmem_mib * MIB)


def _sigmoid(x):
    return 1.0 / (1.0 + jnp.exp(-x))


def _split(x):
    hi = x.astype(BF16)
    lo = (x - hi.astype(F32)).astype(BF16)
    return hi, lo


def _np_split(a64):
    a32 = np.asarray(a64, np.float32)
    hi = a32.astype(BF16)
    lo = (a32 - hi.astype(np.float32)).astype(BF16)
    return jnp.asarray(hi), jnp.asarray(lo)


def _mm_const(a_hi, a_lo, x, passes):
    if passes == 1:
        return jnp.dot(a_hi, x.astype(BF16), preferred_element_type=F32)
    xh, xl = _split(x)
    out = jnp.dot(a_hi, xh, preferred_element_type=F32)
    if passes >= 3:
        out = out + jnp.dot(a_lo, xh, preferred_element_type=F32)
        out = out + jnp.dot(a_hi, xl, preferred_element_type=F32)
    return out


def _mm3(a, b):
    ah, al = _split(a)
    bh, bl = _split(b)
    out = jnp.dot(ah, bh, preferred_element_type=F32)
    out = out + jnp.dot(al, bh, preferred_element_type=F32)
    return out + jnp.dot(ah, bl, preferred_element_type=F32)


def _head_mean_sq(x, j_bf):
    hi, lo = _split(x * x)
    return (jnp.dot(hi, j_bf, preferred_element_type=F32)
            + jnp.dot(lo, j_bf, preferred_element_type=F32))


def _rope(x, c, s):
    lane = lax.broadcasted_iota(jnp.int32, (1, LANES), 1)
    is_b = (lane & 16) != 0
    partner = jnp.where(is_b, pltpu.roll(x, 16, 1), pltpu.roll(x, LANES - 16, 1))
    return x * c + partner * s


def _in_proj_kernel(x_ref, g_ref, w_ref, o_ref, xn_ref):
    @pl.when(pl.program_id(1) == 0)
    def _():
        x = x_ref[...]
        ms = jnp.mean(x * x, axis=-1, keepdims=True)
        xn_ref[...] = (x * lax.rsqrt(ms + NORM_EPS) * g_ref[...]).astype(BF16)

    o_ref[...] = jnp.dot(xn_ref[...], w_ref[...], preferred_element_type=F32).astype(o_ref.dtype)


def _in_proj(x2, g, w_bf):
    T = x2.shape[0]
    tm = min(1024, T)
    tn = IN_COLS // 4
    return pl.pallas_call(
        _in_proj_kernel,
        grid=(T // tm, IN_COLS // tn),
        in_specs=[pl.BlockSpec((tm, D_MODEL), lambda i, j: (i, 0)),
                  pl.BlockSpec((1, D_MODEL), lambda i, j: (0, 0)),
                  pl.BlockSpec((D_MODEL, tn), lambda i, j: (0, j))],
        out_specs=pl.BlockSpec((tm, tn), lambda i, j: (i, j)),
        out_shape=jax.ShapeDtypeStruct((T, IN_COLS), BF16),
        scratch_shapes=[pltpu.VMEM((tm, D_MODEL), BF16)],
        compiler_params=_cparams(("parallel", "arbitrary"), 40),
        name="in_proj",
    )(x2, g, w_bf)


def _attn_prep_kernel(p_ref, c_ref, s_ref, qg_ref, kg_ref, j_ref, qt_ref, k_ref, vt_ref):
    c = c_ref[...]
    s = s_ref[...]
    j_bf = j_ref[...]
    tm = p_ref.shape[0]
    zeros = jnp.zeros((HEAD_DIM, tm), BF16)
    for blk in range(2):
        q = p_ref[:, LANES * blk:LANES * (blk + 1)].astype(F32)
        qn = q * lax.rsqrt(_head_mean_sq(q, j_bf) + NORM_EPS) * qg_ref[...]
        qt = (_rope(qn, c, s) * Q_SCALE).T.astype(BF16)
        lo, hi = qt[:HEAD_DIM], qt[HEAD_DIM:]
        if blk == 0:
            qt_ref[0] = jnp.concatenate([lo, zeros], axis=0)
            qt_ref[1] = jnp.concatenate([hi, zeros], axis=0)
        else:
            qt_ref[2] = jnp.concatenate([zeros, lo], axis=0)
            qt_ref[3] = jnp.concatenate([zeros, hi], axis=0)
    k = p_ref[:, A_K_OFF:A_K_OFF + LANES].astype(F32)
    kn = k * lax.rsqrt(_head_mean_sq(k, j_bf) + NORM_EPS) * kg_ref[...]
    k_ref[...] = _rope(kn, c, s).astype(BF16)
    v = p_ref[:, A_V_OFF:A_V_OFF + LANES].astype(F32)
    for t in range(tm // ATT_KB):
        vt_ref[t] = v[t * ATT_KB:(t + 1) * ATT_KB, :].T.astype(BF16)


def _attn_prep(p3, cos_t, sin_t, qg, kg, j_bf):
    B, L, _ = p3.shape
    tm = min(1024, L)
    tab = pl.BlockSpec((tm, LANES), lambda b, i: (i, 0))
    vec = pl.BlockSpec((1, LANES), lambda b, i: (0, 0))
    return pl.pallas_call(
        _attn_prep_kernel,
        grid=(B, L // tm),
        in_specs=[pl.BlockSpec((None, tm, HY_OFF), lambda b, i: (b, i, 0)), tab, tab, vec, vec,
                  pl.BlockSpec((LANES, LANES), lambda b, i: (0, 0))],
        out_specs=[pl.BlockSpec((None, ATT_Q_HEADS, LANES, tm), lambda b, i: (b, 0, 0, i)),
                   pl.BlockSpec((None, tm, LANES), lambda b, i: (b, i, 0)),
                   pl.BlockSpec((None, tm // ATT_KB, LANES, ATT_KB), lambda b, i: (b, i, 0, 0))],
        out_shape=[jax.ShapeDtypeStruct((B, ATT_Q_HEADS, LANES, L), BF16),
                   jax.ShapeDtypeStruct((B, L, LANES), BF16),
                   jax.ShapeDtypeStruct((B, L // ATT_KB, LANES, ATT_KB), BF16)],
        compiler_params=_cparams(("parallel", "parallel"), 32),
        name="attn_prep",
    )(p3, cos_t, sin_t, qg, kg, j_bf)


def _flash_kernel(qt_ref, k_ref, vt_ref, o_ref, *, nkb, nsub):
    tq = qt_ref.shape[2]
    group = ATT_Q_HEADS // ATT_KV_HEADS

    def body(c2, carry):
        scores = []
        for t in range(nsub):
            st = pl.multiple_of((c2 * nsub + t) * ATT_KB, ATT_KB)
            kblk = k_ref[pl.ds(st, ATT_KB), :]
            scores.append([jnp.dot(kblk, qt_ref[h], preferred_element_type=F32)
                           for h in range(ATT_Q_HEADS)])
        for t in range(nsub):
            probs, stats = [], []
            for h in range(ATT_Q_HEADS):
                m, l, _ = carry[h]
                s = scores[t][h]
                m_new = jnp.maximum(m, jnp.max(s, axis=0, keepdims=True))
                alpha = jnp.exp2(m - m_new)
                p = jnp.exp2(s - m_new)
                stats.append((m_new, alpha * l + jnp.sum(p, axis=0, keepdims=True), alpha))
                probs.append(p.astype(BF16))
            new = []
            for h in range(ATT_Q_HEADS):
                j = h // group
                vt = vt_ref[c2 * nsub + t, HEAD_DIM * j:HEAD_DIM * (j + 1), :]
                m_new, l, alpha = stats[h]
                acc = alpha * carry[h][2] + jnp.dot(vt, probs[h], preferred_element_type=F32)
                new.append((m_new, l, acc))
            carry = tuple(new)
        return carry

    init = tuple((jnp.full((1, tq), -jnp.inf, F32), jnp.zeros((1, tq), F32),
                  jnp.zeros((HEAD_DIM, tq), F32)) for _ in range(ATT_Q_HEADS))
    final = lax.fori_loop(0, nkb // nsub, body, init)
    outs = [acc / l for _, l, acc in final]
    o_ref[...] = jnp.concatenate(outs, axis=0).T.astype(o_ref.dtype)


def _flash(qt, k, vt):
    B, _, _, L = qt.shape
    tq = min(256, L)
    return pl.pallas_call(
        functools.partial(_flash_kernel, nkb=L // ATT_KB, nsub=math.gcd(L // ATT_KB, 4)),
        grid=(B, L // tq),
        in_specs=[pl.BlockSpec((None, ATT_Q_HEADS, LANES, tq), lambda b, i: (b, 0, 0, i)),
                  pl.BlockSpec((None, L, LANES), lambda b, i: (b, 0, 0)),
                  pl.BlockSpec((None, L // ATT_KB, LANES, ATT_KB), lambda b, i: (b, 0, 0, 0))],
        out_specs=pl.BlockSpec((None, tq, ATT_Q_W), lambda b, i: (b, i, 0)),
        out_shape=jax.ShapeDtypeStruct((B, L, ATT_Q_W), BF16),
        compiler_params=_cparams(("parallel", "arbitrary"), 40),
        name="flash_attn",
    )(qt, k, vt)


def _conv3_tile(main, prev_t, next_t, w, first, last):
    tb = main.shape[0]
    row = lax.broadcasted_iota(jnp.int32, (tb, 1), 0)
    before = jnp.where(first, 0.0, prev_t[HALO - 1:HALO, :])
    after = jnp.where(last, 0.0, next_t[0:1, :])
    up = jnp.where(row == 0, before, pltpu.roll(main, 1, 0))
    dn = jnp.where(row == tb - 1, after, pltpu.roll(main, tb - 1, 0))
    return up * w[0:1, :] + main * w[1:2, :] + dn * w[2:3, :]


def _halo_specs(tb, L, col_fn):
    nbh = L // HALO
    rh = tb // HALO
    main = pl.BlockSpec((None, tb, LANES), lambda b, w, s: (b, s, col_fn(w)))
    prev = pl.BlockSpec((None, HALO, LANES),
                        lambda b, w, s: (b, jnp.maximum(s * rh - 1, 0), col_fn(w)))
    nxt = pl.BlockSpec((None, HALO, LANES),
                       lambda b, w, s: (b, jnp.minimum((s + 1) * rh, nbh - 1), col_fn(w)))
    return [main, prev, nxt]


def _f32(ref):
    return ref[...].astype(F32)


def _hy_pro_kernel(x0_ref, x0p_ref, x0n_ref, x1_ref, x1p_ref, x1n_ref,
                   hv_ref, hvp_ref, hvn_ref, w0_ref, w1_ref, w2_ref,
                   b0_ref, b1_ref, b2_ref, z_ref, x0u_ref):
    first = pl.program_id(2) == 0
    last = pl.program_id(2) == pl.num_programs(2) - 1
    x0 = _conv3_tile(_f32(x0_ref), _f32(x0p_ref), _f32(x0n_ref), w0_ref[...], first, last) + b0_ref[...]
    x1 = _conv3_tile(_f32(x1_ref), _f32(x1p_ref), _f32(x1n_ref), w1_ref[...], first, last) + b1_ref[...]
    hv = _conv3_tile(_f32(hv_ref), _f32(hvp_ref), _f32(hvn_ref), w2_ref[...], first, last) + b2_ref[...]
    z_ref[...] = hv * x1
    x0u_ref[...] = x0


def _hy_prologue(p3, hcw, hcb):
    B, L, _ = p3.shape
    tb = min(1024, L)
    nw = HY_WIDTH // LANES
    in_specs = []
    for piece in range(3):
        in_specs += _halo_specs(tb, L, lambda w, piece=piece: HY_BLK + nw * piece + w)
    for piece in range(3):
        in_specs.append(pl.BlockSpec((3, LANES), lambda b, w, s, piece=piece: (0, nw * piece + w)))
    for piece in range(3):
        in_specs.append(pl.BlockSpec((1, LANES), lambda b, w, s, piece=piece: (0, nw * piece + w)))
    out_spec = pl.BlockSpec((None, tb, LANES), lambda b, w, s: (b, s, w))
    out_sh = jax.ShapeDtypeStruct((B, L, HY_WIDTH), F32)
    args = [p3] * 9 + [hcw] * 3 + [hcb] * 3
    return pl.pallas_call(
        _hy_pro_kernel,
        grid=(B, nw, L // tb),
        in_specs=in_specs,
        out_specs=[out_spec, out_spec],
        out_shape=[out_sh, out_sh],
        compiler_params=_cparams(("parallel", "parallel", "parallel"), 32),
        name="hyena_prologue",
    )(*args)


def _sconv_kernel(sb_ref, sc_ref, scp_ref, scn_ref, sh_ref, shp_ref, shn_ref, w_ref, o_ref):
    first = pl.program_id(2) == 0
    last = pl.program_id(2) == pl.num_programs(2) - 1
    y = _conv3_tile(_f32(sc_ref) * _f32(sh_ref), _f32(scp_ref) * _f32(shp_ref),
                    _f32(scn_ref) * _f32(shn_ref), w_ref[...], first, last)
    o_ref[...] = (_f32(sb_ref) * y).astype(o_ref.dtype)


def _sconv(p3, scw):
    B, L, _ = p3.shape
    tb = min(1024, L)
    nw = SC_WIDTH // LANES
    in_specs = [pl.BlockSpec((None, tb, LANES), lambda b, w, s: (b, s, SC_BLK + w))]
    in_specs += _halo_specs(tb, L, lambda w: SC_BLK + nw + w)
    in_specs += _halo_specs(tb, L, lambda w: SC_BLK + 2 * nw + w)
    in_specs.append(pl.BlockSpec((3, LANES), lambda b, w, s: (0, w)))
    return pl.pallas_call(
        _sconv_kernel,
        grid=(B, nw, L // tb),
        in_specs=in_specs,
        out_specs=pl.BlockSpec((None, tb, LANES), lambda b, w, s: (b, s, w)),
        out_shape=jax.ShapeDtypeStruct((B, L, SC_WIDTH), BF16),
        compiler_params=_cparams(("parallel", "parallel", "parallel"), 32),
        name="short_conv",
    )(*([p3] * 7 + [scw]))


def _hy_filter_kernel(z_ref, w1_ref, b1_ref, w2_ref, b2_ref, w3_ref, fr_ref, dl_ref,
                      k_ref, ss_ref, *, L):
    step = pl.program_id(0)
    tr = z_ref.shape[0]
    z = z_ref[...]
    h = jnp.sin(fr_ref[0:1, :] * (_mm3(z, w1_ref[...]) + b1_ref[...]))
    h = jnp.sin(fr_ref[1:2, :] * (_mm3(h, w2_ref[...]) + b2_ref[...]))
    h3 = _mm3(h, w3_ref[...])
    row = step * tr + lax.broadcasted_iota(jnp.int32, (tr, 1), 0)
    val = jnp.where(row < L, h3[:, :HY_WIDTH], h3[:, HY_WIDTH:])
    val = val * jnp.exp(-z[:, 0:1] * dl_ref[...])
    val = jnp.where(row == L, 0.0, val)
    k_ref[...] = val

    @pl.when(step == 0)
    def _():
        ss_ref[...] = jnp.zeros_like(ss_ref)

    ss_ref[...] += jnp.sum(val * val, axis=0, keepdims=True)


def _hy_filter_raw(z2, w1p, b1, w2, b2, w3, freq, deltas, L):
    n = 2 * L
    tr = min(512, n)
    H = HY_FILTER_HIDDEN
    full = lambda shape: pl.BlockSpec(shape, lambda i: (0, 0))
    return pl.pallas_call(
        functools.partial(_hy_filter_kernel, L=L),
        grid=(n // tr,),
        in_specs=[pl.BlockSpec((tr, H), lambda i: (i, 0)),
                  full((H, H)), full((1, H)), full((H, H)), full((1, H)),
                  full((H, 2 * HY_WIDTH)), full((2, H)), full((1, HY_WIDTH))],
        out_specs=[pl.BlockSpec((tr, HY_WIDTH), lambda i: (i, 0)), full((1, HY_WIDTH))],
        out_shape=[jax.ShapeDtypeStruct((n, HY_WIDTH), F32),
                   jax.ShapeDtypeStruct((1, HY_WIDTH), F32)],
        compiler_params=_cparams(("arbitrary",), 32),
        name="hyena_filter",
    )(z2, w1p, b1, w2, b2, w3, freq, deltas)


def _fft_dims(L):
    n = 2 * L
    n2 = 128 if n >= 16384 else 64
    return n // n2, n2


def _fft_consts(n1, n2, r):
    n = n1 * n2
    a1 = 2.0 * np.pi * np.outer(np.arange(n1), np.arange(n1)) / n1
    c1, s1 = np.cos(a1), np.sin(a1)
    a2 = 2.0 * np.pi * np.outer(np.arange(n2), np.arange(n2)) / n2
    c2, s2 = np.cos(a2), np.sin(a2)
    f1 = np.concatenate([c1[:, :r], -s1[:, :r]], axis=0)
    f1_full = np.concatenate([c1, -s1], axis=0)
    m2 = np.block([[c2, s2], [-s2, c2]])
    m2i = np.block([[c2, -s2], [s2, c2]])
    fi = np.concatenate([c1[:r, :], -s1[:r, :]], axis=1)
    aw = 2.0 * np.pi * np.arange(n1) / n
    w_re = np.broadcast_to(np.cos(aw)[:, None], (n1, LANES)).astype(np.float32)
    w_im = np.broadcast_to(-np.sin(aw)[:, None], (n1, LANES)).astype(np.float32)
    return dict(f1=_np_split(f1), f1_full=_np_split(f1_full), m2=_np_split(m2),
                m2i=_np_split(m2i), fi=_np_split(fi),
                w_re=jnp.asarray(w_re), w_im=jnp.asarray(w_im))


def _cmul(a_re, a_im, b_re, b_im):
    return a_re * b_re - a_im * b_im, a_re * b_im + a_im * b_re


def _fft_stage1(src_ref, a_ref, f_hi, f_lo, w_re, w_im, n1, n2, r, passes):
    def body(i2, carry):
        t_re, t_im = carry
        u_re, u_im = _cmul(t_re, t_im, w_re, w_im)
        i = 2 * i2
        xs = jnp.concatenate([src_ref[pl.ds(i, r, stride=n2), :],
                              src_ref[pl.ds(i + 1, r, stride=n2), :]], axis=1)
        a = _mm_const(f_hi, f_lo, xs, passes)
        for half, (tr, ti) in enumerate(((t_re, t_im), (u_re, u_im))):
            ar = a[:n1, LANES * half:LANES * (half + 1)]
            ai = a[n1:, LANES * half:LANES * (half + 1)]
            base = pl.multiple_of((i + half) * 2 * n1, 2 * n1)
            a_ref[pl.ds(base, n1), :] = ar * tr - ai * ti
            a_ref[pl.ds(base + n1, n1), :] = ar * ti + ai * tr
        return _cmul(u_re, u_im, w_re, w_im)

    ones = jnp.ones((n1, LANES), F32)
    lax.fori_loop(0, n2 // 2, body, (ones, jnp.zeros((n1, LANES), F32)))


def _filter_fft_kernel(k_ref, ss_ref, f_hi_ref, f_lo_ref, m_hi_ref, m_lo_ref, wre_ref, wim_ref,
                       kr_ref, ki_ref, a_ref, *, n1, n2, kc):
    c = pl.program_id(1)

    @pl.when(c == 0)
    def _():
        _fft_stage1(k_ref, a_ref, f_hi_ref[...], f_lo_ref[...], wre_ref[...], wim_ref[...],
                    n1, n2, n1, FFT_PASSES_FILTER)

    scale = lax.rsqrt(ss_ref[...] + NORM_EPS) * (1.0 / (n1 * n2))

    def body(kk, _):
        k1 = c * kc + kk
        ar = a_ref[pl.ds(k1, n2, stride=2 * n1), :]
        ai = a_ref[pl.ds(n1 + k1, n2, stride=2 * n1), :]
        x = _mm_const(m_hi_ref[...], m_lo_ref[...], jnp.concatenate([ar, ai], axis=0),
                      FFT_PASSES_FILTER)
        kr_ref[kk] = x[:n2] * scale
        ki_ref[kk] = x[n2:] * scale
        return 0

    lax.fori_loop(0, kc, body, 0)


def _filter_fft(k2raw, sumsq, cst, n1, n2):
    n = n1 * n2
    kc = min(16, n1)
    nw = HY_WIDTH // LANES
    f_hi, f_lo = cst["f1_full"]
    m_hi, m_lo = cst["m2"]
    full = lambda a: pl.BlockSpec(a.shape, lambda w, c: (0,) * a.ndim)
    out_spec = pl.BlockSpec((kc, n2, LANES), lambda w, c: (c, 0, w))
    out_sh = jax.ShapeDtypeStruct((n1, n2, HY_WIDTH), F32)
    return pl.pallas_call(
        functools.partial(_filter_fft_kernel, n1=n1, n2=n2, kc=kc),
        grid=(nw, n1 // kc),
        in_specs=[pl.BlockSpec((n, LANES), lambda w, c: (0, w)),
                  pl.BlockSpec((1, LANES), lambda w, c: (0, w)),
                  full(f_hi), full(f_lo), full(m_hi), full(m_lo),
                  full(cst["w_re"]), full(cst["w_im"])],
        out_specs=[out_spec, out_spec],
        out_shape=[out_sh, out_sh],
        scratch_shapes=[pltpu.VMEM((2 * n, LANES), F32)],
        compiler_params=_cparams(("parallel", "arbitrary"), 48),
        name="hyena_filter_fft",
    )(k2raw, sumsq, f_hi, f_lo, m_hi, m_lo, cst["w_re"], cst["w_im"])


def _hy_conv_kernel(z_ref, x0_ref, hb_ref, kr_ref, ki_ref, f_hi_ref, f_lo_ref,
                    m_hi_ref, m_lo_ref, mi_hi_ref, mi_lo_ref, fi_hi_ref, fi_lo_ref,
                    wre_ref, wim_ref, o_ref, a_ref, *, n1, n2, kc):
    c = pl.program_id(2)
    r = n1 // 2

    @pl.when(c == 0)
    def _():
        _fft_stage1(z_ref, a_ref, f_hi_ref[...], f_lo_ref[...], wre_ref[...], wim_ref[...],
                    n1, n2, r, FFT_PASSES_DATA)

    def mid(kk2, _):
        k1 = c * kc + 2 * kk2
        ld = lambda off: a_ref[pl.ds(off, n2, stride=2 * n1), :]
        xin = jnp.concatenate([jnp.concatenate([ld(k1), ld(k1 + 1)], axis=1),
                               jnp.concatenate([ld(n1 + k1), ld(n1 + k1 + 1)], axis=1)], axis=0)
        x = _mm_const(m_hi_ref[...], m_lo_ref[...], xin, FFT_PASSES_DATA)
        xr, xi = x[:n2], x[n2:]
        kr = jnp.concatenate([kr_ref[2 * kk2], kr_ref[2 * kk2 + 1]], axis=1)
        ki = jnp.concatenate([ki_ref[2 * kk2], ki_ref[2 * kk2 + 1]], axis=1)
        y = jnp.concatenate([xr * kr - xi * ki, xr * ki + xi * kr], axis=0)
        bq = _mm_const(mi_hi_ref[...], mi_lo_ref[...], y, FFT_PASSES_DATA)
        for half in range(2):
            sl = slice(LANES * half, LANES * (half + 1))
            a_ref[pl.ds(k1 + half, n2, stride=2 * n1), :] = bq[:n2, sl]
            a_ref[pl.ds(n1 + k1 + half, n2, stride=2 * n1), :] = bq[n2:, sl]
        return 0

    lax.fori_loop(0, kc // 2, mid, 0, unroll=4)

    @pl.when(c == pl.num_programs(2) - 1)
    def _():
        w_re = wre_ref[...]
        w_im = wim_ref[...]
        hb = hb_ref[...]

        def body(i2, carry):
            t_re, t_im = carry
            u_re, u_im = _cmul(t_re, t_im, w_re, w_im)
            i = 2 * i2
            cols = []
            for half, (tr, ti) in enumerate(((t_re, t_im), (u_re, u_im))):
                base = pl.multiple_of((i + half) * 2 * n1, 2 * n1)
                br = a_ref[pl.ds(base, n1), :]
                bi = a_ref[pl.ds(base + n1, n1), :]
                cols.append(jnp.concatenate([br * tr + bi * ti, bi * tr - br * ti], axis=0))
            y = _mm_const(fi_hi_ref[...], fi_lo_ref[...], jnp.concatenate(cols, axis=1),
                          FFT_PASSES_DATA)
            for half in range(2):
                rows = pl.ds(i + half, r, stride=n2)
                yh = y[:, LANES * half:LANES * (half + 1)]
                o_ref[rows, :] = (x0_ref[rows, :] * (yh + z_ref[rows, :] * hb)).astype(o_ref.dtype)
            return _cmul(u_re, u_im, w_re, w_im)

        ones = jnp.ones((n1, LANES), F32)
        lax.fori_loop(0, n2 // 2, body, (ones, jnp.zeros((n1, LANES), F32)))


def _hy_conv(z, x0u, hbias, kf_re, kf_im, cst, n1, n2):
    B, L, _ = z.shape
    kc = min(16, n1)
    nw = HY_WIDTH // LANES
    seq = pl.BlockSpec((None, L, LANES), lambda b, w, c: (b, 0, w))
    kf_spec = pl.BlockSpec((kc, n2, LANES), lambda b, w, c: (c, 0, w))
    full = lambda a: pl.BlockSpec(a.shape, lambda b, w, c: (0,) * a.ndim)
    consts = [*cst["f1"], *cst["m2"], *cst["m2i"], *cst["fi"], cst["w_re"], cst["w_im"]]
    return pl.pallas_call(
        functools.partial(_hy_conv_kernel, n1=n1, n2=n2, kc=kc),
        grid=(B, nw, n1 // kc),
        in_specs=[seq, seq, pl.BlockSpec((1, LANES), lambda b, w, c: (0, w)), kf_spec, kf_spec]
                 + [full(a) for a in consts],
        out_specs=seq,
        out_shape=jax.ShapeDtypeStruct((B, L, HY_WIDTH), F32),
        scratch_shapes=[pltpu.VMEM((2 * n1 * n2, LANES), F32)],
        compiler_params=_cparams(("parallel", "parallel", "arbitrary"), 56),
        name="hyena_conv",
    )(z, x0u, hbias, kf_re, kf_im, *consts)


T_D0, T_D1, T_WQF, T_WQB, T_WKF, T_WKB, T_GCF, T_GCB, T_BD = range(9)


def _ret_tables_kernel(rl_ref, rh_ref, t_ref):
    C = RET_CHUNK
    i = lax.broadcasted_iota(jnp.int32, (C, LANES), 0).astype(F32)
    jn = lax.broadcasted_iota(jnp.int32, (C, LANES), 1).astype(F32)
    log_g = lambda e: jnp.log1p(-jnp.exp2(-e))
    diff = i - jn
    for hp in range(2):
        lgf = log_g(rh_ref[hp, 0:1, :])
        lgb = log_g(rh_ref[hp, 1:2, :])
        fwd = jnp.exp(jnp.where(diff >= 0, diff, 0.0) * lgf)
        bwd = jnp.exp(jnp.where(diff < 0, -diff, 0.0) * lgb)
        t_ref[T_D0 + hp] = jnp.where(diff >= 0, fwd, bwd)
    lf = log_g(rl_ref[0:1, :])
    lb = log_g(rl_ref[1:2, :])
    t_ref[T_WQF] = jnp.exp((i + 1.0) * lf)
    t_ref[T_WQB] = jnp.exp((C - i) * lb)
    t_ref[T_WKF] = jnp.exp((C - 1.0 - i) * lf)
    t_ref[T_WKB] = jnp.exp(i * lb)
    bd = jnp.where((i < HEAD_DIM) == (jn < HEAD_DIM), 1.0, 0.0)
    t_ref[T_GCF] = jnp.exp(C * lf) * bd
    t_ref[T_GCB] = jnp.exp(C * lb) * bd
    t_ref[T_BD] = bd


def _ret_tables(rde):
    per_lane = jnp.repeat(rde.reshape(2, 2, 2), HEAD_DIM, axis=2)
    per_lane = per_lane.transpose(1, 0, 2)
    per_head = jnp.broadcast_to(rde.reshape(2, 2, 2, 1), (2, 2, 2, LANES))
    per_head = per_head.transpose(1, 2, 0, 3)
    return pl.pallas_call(
        _ret_tables_kernel,
        grid=(2,),
        in_specs=[pl.BlockSpec((None, 2, LANES), lambda j: (j, 0, 0)),
                  pl.BlockSpec((None, 2, 2, LANES), lambda j: (j, 0, 0, 0))],
        out_specs=pl.BlockSpec((None, 9, RET_CHUNK, LANES), lambda j: (j, 0, 0, 0)),
        out_shape=jax.ShapeDtypeStruct((2, 9, RET_CHUNK, LANES), F32),
        compiler_params=_cparams(("parallel",), 32),
        name="retention_tables",
    )(per_lane, per_head)


def _ret_state_kernel(rk_ref, rv_ref, c_ref, s_ref, t_ref, st_ref, r_ref):
    C = RET_CHUNK
    nch = rk_ref.shape[0] // C

    @pl.when(pl.program_id(2) == 0)
    def _():
        r_ref[...] = jnp.zeros_like(r_ref)

    wkb = t_ref[T_WKB]
    gcb = t_ref[T_GCB]
    bd = t_ref[T_BD]
    state = r_ref[...]
    for n in range(nch - 1, -1, -1):
        st_ref[n] = state
        sl = slice(n * C, (n + 1) * C)
        k = _rope(rk_ref[sl, :].astype(F32), c_ref[sl, :], s_ref[sl, :]) * (HEAD_DIM ** -0.5)
        kw = (k * wkb).astype(BF16)
        v = rv_ref[sl, :]
        kv = lax.dot_general(kw, v, (((0,), (0,)), ((), ())), preferred_element_type=F32)
        state = gcb * state + bd * kv
    r_ref[...] = state


def _ret_main_kernel(rq_ref, rk_ref, rv_ref, rg_ref, c_ref, s_ref, t_ref, st_ref, j_ref,
                     o_ref, sf_ref):
    C = RET_CHUNK
    nch = rq_ref.shape[0] // C

    @pl.when(pl.program_id(2) == 0)
    def _():
        sf_ref[...] = jnp.zeros_like(sf_ref)

    lane = lax.broadcasted_iota(jnp.int32, (1, LANES), 1)
    lo64 = lane < HEAD_DIM
    j_bf = j_ref[...]
    state = sf_ref[...]
    for n in range(nch):
        sl = slice(n * C, (n + 1) * C)
        cs, sn = c_ref[sl, :], s_ref[sl, :]
        q = _rope(rq_ref[sl, :].astype(F32), cs, sn)
        k = _rope(rk_ref[sl, :].astype(F32), cs, sn) * (HEAD_DIM ** -0.5)
        vb = rv_ref[sl, :]
        qb = q.astype(BF16)
        kb = k.astype(BF16)
        zero = jnp.zeros_like(qb)
        parts = []
        for hp in range(2):
            qm = jnp.where(lo64, qb, zero) if hp == 0 else jnp.where(lo64, zero, qb)
            sc = lax.dot_general(qm, kb, (((1,), (1,)), ((), ())), preferred_element_type=F32)
            sd = (sc * t_ref[T_D0 + hp]).astype(BF16)
            parts.append(jnp.dot(sd, vb, preferred_element_type=F32))
        o = jnp.where(lo64, parts[0], parts[1])
        o = o + jnp.dot((q * t_ref[T_WQF]).astype(BF16), state.astype(BF16),
                        preferred_element_type=F32)
        o = o + jnp.dot((q * t_ref[T_WQB]).astype(BF16), st_ref[n].astype(BF16),
                        preferred_element_type=F32)
        kw = (k * t_ref[T_WKF]).astype(BF16)
        kv = lax.dot_general(kw, vb, (((0,), (0,)), ((), ())), preferred_element_type=F32)
        state = t_ref[T_GCF] * state + t_ref[T_BD] * kv
        ret = o * lax.rsqrt(_head_mean_sq(o, j_bf) + NORM_EPS)
        g = rg_ref[sl, :].astype(F32)
        o_ref[sl, :] = (ret * (g * _sigmoid(g))).astype(o_ref.dtype)
    sf_ref[...] = state


def _retention(p3, cos_t, sin_t, tables, j_bf):
    B, L, _ = p3.shape
    C = RET_CHUNK
    tb = min(1024, L)
    nblk = L // tb
    nch = tb // C
    nw = RET_W // LANES
    col = lambda piece: (lambda b, j, s: (b, s, RET_BLK + nw * piece + j))
    colr = lambda piece: (lambda b, j, s: (b, nblk - 1 - s, RET_BLK + nw * piece + j))
    seq = lambda fn: pl.BlockSpec((None, tb, LANES), fn)
    tab_spec = pl.BlockSpec((None, 9, C, LANES), lambda b, j, s: (j, 0, 0, 0))
    states = pl.pallas_call(
        _ret_state_kernel,
        grid=(B, nw, nblk),
        in_specs=[seq(colr(1)), seq(colr(2)),
                  pl.BlockSpec((tb, LANES), lambda b, j, s: (nblk - 1 - s, 0)),
                  pl.BlockSpec((tb, LANES), lambda b, j, s: (nblk - 1 - s, 0)),
                  tab_spec],
        out_specs=pl.BlockSpec((None, None, nch, LANES, LANES),
                               lambda b, j, s: (b, j, nblk - 1 - s, 0, 0)),
        out_shape=jax.ShapeDtypeStruct((B, nw, L // C, LANES, LANES), F32),
        scratch_shapes=[pltpu.VMEM((LANES, LANES), F32)],
        compiler_params=_cparams(("parallel", "parallel", "arbitrary"), 32),
        name="retention_state",
    )(p3, p3, cos_t, sin_t, tables)
    return pl.pallas_call(
        _ret_main_kernel,
        grid=(B, nw, nblk),
        in_specs=[seq(col(0)), seq(col(1)), seq(col(2)), seq(col(3)),
                  pl.BlockSpec((tb, LANES), lambda b, j, s: (s, 0)),
                  pl.BlockSpec((tb, LANES), lambda b, j, s: (s, 0)),
                  tab_spec,
                  pl.BlockSpec((None, None, nch, LANES, LANES), lambda b, j, s: (b, j, s, 0, 0)),
                  pl.BlockSpec((LANES, LANES), lambda b, j, s: (0, 0))],
        out_specs=pl.BlockSpec((None, tb, LANES), lambda b, j, s: (b, s, j)),
        out_shape=jax.ShapeDtypeStruct((B, L, RET_W), BF16),
        scratch_shapes=[pltpu.VMEM((LANES, LANES), F32)],
        compiler_params=_cparams(("parallel", "parallel", "arbitrary"), 32),
        name="retention_main",
    )(p3, p3, p3, p3, cos_t, sin_t, tables, states, j_bf)


def _merge_kernel(oa_ref, ob_ref, oc_ref, od_ref, g0_ref, g1_ref, g2_ref, g3_ref,
                  wb_ref, wo_ref, x_ref, gn_ref, h_ref):
    merged = None
    for n, (br, gr) in enumerate(((oa_ref, g0_ref), (ob_ref, g1_ref),
                                  (oc_ref, g2_ref), (od_ref, g3_ref))):
        term = _sigmoid(gr[...].astype(F32)) * jnp.dot(br[...].astype(BF16), wb_ref[n],
                                                preferred_element_type=F32)
        merged = term if merged is None else merged + term
    y = jnp.dot(merged.astype(BF16), wo_ref[...], preferred_element_type=F32)
    ms = jnp.mean(y * y, axis=-1, keepdims=True)
    h_ref[...] = x_ref[...] + y * lax.rsqrt(ms + NORM_EPS) * gn_ref[...]


def _merge(oa, ob, oc, od, p2, wb_bf, wo_bf, x2, gn):
    T = x2.shape[0]
    tm = min(1024, T)
    br = pl.BlockSpec((tm, BRANCH_W), lambda i: (i, 0))
    gate = lambda n: pl.BlockSpec((tm, D_MODEL), lambda i, n=n: (i, GATE_BLK + n))
    row = pl.BlockSpec((tm, D_MODEL), lambda i: (i, 0))
    return pl.pallas_call(
        _merge_kernel,
        grid=(T // tm,),
        in_specs=[br, br, br, br, gate(0), gate(1), gate(2), gate(3),
                  pl.BlockSpec((N_BRANCH, BRANCH_W, D_MODEL), lambda i: (0, 0, 0)),
                  pl.BlockSpec((D_MODEL, D_MODEL), lambda i: (0, 0)),
                  row, pl.BlockSpec((1, D_MODEL), lambda i: (0, 0))],
        out_specs=row,
        out_shape=jax.ShapeDtypeStruct((T, D_MODEL), F32),
        compiler_params=_cparams(("parallel",), 48),
        name="merge_out_proj",
    )(oa, ob, oc, od, p2, p2, p2, p2, wb_bf, wo_bf, x2, gn)


def _ffn_kernel(h_ref, g2_ref, wg_ref, wu_ref, wd_ref, g3_ref, o_ref, hn_ref, acc_ref):
    j = pl.program_id(1)

    @pl.when(j == 0)
    def _():
        h = h_ref[...]
        ms = jnp.mean(h * h, axis=-1, keepdims=True)
        hn_ref[...] = (h * lax.rsqrt(ms + NORM_EPS) * g2_ref[...]).astype(BF16)
        acc_ref[...] = jnp.zeros_like(acc_ref)

    hn = hn_ref[...]
    g = jnp.dot(hn, wg_ref[...], preferred_element_type=F32)
    u = jnp.dot(hn, wu_ref[...], preferred_element_type=F32)
    a = (g * _sigmoid(g) * u).astype(BF16)
    acc_ref[...] += jnp.dot(a, wd_ref[...], preferred_element_type=F32)

    @pl.when(j == pl.num_programs(1) - 1)
    def _():
        f = acc_ref[...]
        ms = jnp.mean(f * f, axis=-1, keepdims=True)
        o_ref[...] = h_ref[...] + f * lax.rsqrt(ms + NORM_EPS) * g3_ref[...]


def _ffn(h2, g2, wg_bf, wu_bf, wd_bf, g3):
    T = h2.shape[0]
    tm = min(1024, T)
    tf = 256
    row = pl.BlockSpec((tm, D_MODEL), lambda i, j: (i, 0))
    vec = pl.BlockSpec((1, D_MODEL), lambda i, j: (0, 0))
    return pl.pallas_call(
        _ffn_kernel,
        grid=(T // tm, D_FF // tf),
        in_specs=[row, vec,
                  pl.BlockSpec((D_MODEL, tf), lambda i, j: (0, j)),
                  pl.BlockSpec((D_MODEL, tf), lambda i, j: (0, j)),
                  pl.BlockSpec((tf, D_MODEL), lambda i, j: (j, 0)),
                  vec],
        out_specs=row,
        out_shape=jax.ShapeDtypeStruct((T, D_MODEL), F32),
        scratch_shapes=[pltpu.VMEM((tm, D_MODEL), BF16), pltpu.VMEM((tm, D_MODEL), F32)],
        compiler_params=_cparams(("parallel", "arbitrary"), 40),
        name="ffn",
    )(h2, g2, wg_bf, wu_bf, wd_bf, g3)


def _rope_tables(L):
    rows = L // GRID_W
    r = jnp.repeat(jnp.arange(rows, dtype=F32), GRID_W)
    c = jnp.tile(jnp.arange(GRID_W, dtype=F32), rows)
    inv = ROPE_BASE ** (-jnp.arange(ROPE_FREQS, dtype=F32) / ROPE_FREQS)
    ar = r[:, None] * inv
    ac = c[:, None] * inv
    cos64 = jnp.concatenate([jnp.cos(ar), jnp.cos(ar), jnp.cos(ac), jnp.cos(ac)], axis=1)
    sin64 = jnp.concatenate([-jnp.sin(ar), jnp.sin(ar), -jnp.sin(ac), jnp.sin(ac)], axis=1)
    return jnp.tile(cos64, (1, 2)), jnp.tile(sin64, (1, 2))


def _filter_features(L):
    t = jnp.linspace(0.0, 1.0, L, dtype=F32)[:, None]
    f = jnp.linspace(1e-4, HY_BANDS - 1, HY_BANDS, dtype=F32)
    ang = (2.0 * math.pi / L) * jnp.arange(L, dtype=F32)[:, None] * f[None, :]
    z = jnp.concatenate([t, jnp.cos(ang), -jnp.sin(ang)], axis=-1)
    z = jnp.pad(z, ((0, 0), (0, HY_FILTER_HIDDEN - HY_EMB)))
    idx = np.concatenate([np.arange(L), [0], np.arange(L - 1, 0, -1)])
    return z[idx]


def _head_mean_matrix():
    i = np.arange(LANES)
    j = ((i[:, None] // HEAD_DIM) == (i[None, :] // HEAD_DIM)).astype(np.float32) / HEAD_DIM
    return jnp.asarray(j.astype(BF16))


def _trunk(x, wts, fft_dims=None):
    B, L, _ = x.shape
    T = B * L
    n1, n2 = fft_dims or _fft_dims(L)
    cst = _fft_consts(n1, n2, n1 // 2)
    cos_t, sin_t = _rope_tables(L)
    z2 = _filter_features(L)
    deltas = jnp.abs(jnp.linspace(math.log(HY_TARGET) / HY_SLOW_DECAY,
                                  math.log(HY_TARGET) / HY_FAST_DECAY, HY_WIDTH, dtype=F32))[None, :]
    j_bf = _head_mean_matrix()
    depth = wts["w_in"].shape[0]
    x2 = x.reshape(T, D_MODEL)
    for l in range(depth):
        ng = wts["norm_gains"][l]
        w1p = jnp.pad(wts["hy_w1"][l], ((0, HY_FILTER_HIDDEN - HY_EMB), (0, 0)))
        k2raw, sumsq = _hy_filter_raw(z2, w1p, wts["hy_b1"][l][None, :], wts["hy_w2"][l],
                                      wts["hy_b2"][l][None, :], wts["hy_w3"][l],
                                      wts["hy_freq"][l], deltas, L)
        kf_re, kf_im = _filter_fft(k2raw, sumsq, cst, n1, n2)

        p2 = _in_proj(x2, ng[0][None, :], wts["w_in_bf"][l])
        p3 = p2.reshape(B, L, IN_COLS)

        qg = jnp.tile(wts["qk_norm"][l, 0], 2)[None, :]
        kg = jnp.tile(wts["qk_norm"][l, 1], 2)[None, :]
        qt, k, vt = _attn_prep(p3, cos_t, sin_t, qg, kg, j_bf)
        out_a = _flash(qt, k, vt)

        z, x0u = _hy_prologue(p3, wts["hy_conv_w"][l], wts["hy_conv_b"][l][None, :])
        out_b = _hy_conv(z, x0u, wts["hy_bias"][l][None, :], kf_re, kf_im, cst, n1, n2)

        tables = _ret_tables(wts["ret_decay_exp"][l])
        out_c = _retention(p3, cos_t, sin_t, tables, j_bf)

        out_d = _sconv(p3, wts["sc_conv_w"][l])

        h2 = _merge(out_a.reshape(T, BRANCH_W), out_b.reshape(T, BRANCH_W),
                    out_c.reshape(T, BRANCH_W), out_d.reshape(T, BRANCH_W),
                    p2, wts["w_branch_bf"][l], wts["w_out_bf"][l], x2, ng[1][None, :])
        x2 = _ffn(h2, ng[2][None, :], wts["w_gate_bf"][l], wts["w_up_bf"][l],
                  wts["w_ffn_out_bf"][l], ng[3][None, :])
    return x2.reshape(B, L, D_MODEL)


def kernel(x_prompt, x_sample, norm_gains, w_in, qk_norm, hy_conv_w, hy_conv_b, hy_w1, hy_b1, hy_w2,
           hy_b2, hy_w3, hy_freq, hy_bias, ret_decay_exp, sc_conv_w, w_branch, w_out, w_ffn_in,
           w_ffn_out):
    wts = dict(norm_gains=norm_gains, w_in=w_in, qk_norm=qk_norm, hy_conv_w=hy_conv_w,
               hy_conv_b=hy_conv_b, hy_w1=hy_w1, hy_b1=hy_b1, hy_w2=hy_w2, hy_b2=hy_b2,
               hy_w3=hy_w3, hy_freq=hy_freq, hy_bias=hy_bias, ret_decay_exp=ret_decay_exp,
               sc_conv_w=sc_conv_w,
               w_in_bf=w_in.astype(BF16), w_branch_bf=w_branch.astype(BF16),
               w_out_bf=w_out.astype(BF16), w_gate_bf=w_ffn_in[..., :D_FF].astype(BF16),
               w_up_bf=w_ffn_in[..., D_FF:].astype(BF16), w_ffn_out_bf=w_ffn_out.astype(BF16))
    return _trunk(x_prompt, wts), _trunk(x_sample, wts)
```

```python
import functools
import math

import numpy as np
import jax
import jax.numpy as jnp
from jax import lax
from jax.experimental import pallas as pl
from jax.experimental.pallas import tpu as pltpu

F32 = jnp.float32
BF16 = jnp.bfloat16

D_MODEL = 1024
GRID_W = 64
N_BRANCH = 4
BRANCH_W = 256
HEAD_DIM = 64
ATT_Q_HEADS = 4
ATT_KV_HEADS = 2
ROPE_BASE = 10000.0
ROPE_FREQS = HEAD_DIM // 4
HY_WIDTH = BRANCH_W
HY_EMB = 33
HY_BANDS = (HY_EMB - 1) // 2
HY_FILTER_HIDDEN = 64
HY_FAST_DECAY = 0.3
HY_SLOW_DECAY = 1.5
HY_TARGET = 1e-2
RET_HEADS = 4
RET_W = RET_HEADS * HEAD_DIM
RET_CHUNK = 128
SC_WIDTH = BRANCH_W
D_FF = 2816
NORM_EPS = 1e-6

ATT_Q_W = ATT_Q_HEADS * HEAD_DIM
ATT_KV_W = ATT_KV_HEADS * HEAD_DIM
A_K_OFF = ATT_Q_W
A_V_OFF = A_K_OFF + ATT_KV_W
HY_OFF = A_V_OFF + ATT_KV_W
RET_OFF = HY_OFF + 3 * HY_WIDTH
SC_OFF = RET_OFF + 4 * RET_W
GATE_OFF = SC_OFF + 3 * SC_WIDTH
IN_COLS = GATE_OFF + N_BRANCH * D_MODEL

LANES = 128
SUBLANES = 8
MXU_W = 256
HALO = 2 * SUBLANES
HY_BLK = HY_OFF // LANES
RET_BLK = RET_OFF // LANES
SC_BLK = SC_OFF // LANES
GATE_BLK = GATE_OFF // D_MODEL
ATT_KB = 512
VT_ROWS = HEAD_DIM + HALO
Q_SCALE = HEAD_DIM ** -0.5 * math.log2(math.e)
FFT_PASSES_FILTER = 3
FFT_PASSES_DATA = 1
MIB = 1 << 20


def _cparams(sem, vmem_mib):
    return pltpu.CompilerParams(dimension_semantics=sem, vmem_limit_bytes=vmem_mib * MIB)


def _sigmoid(x):
    return 1.0 / (1.0 + jnp.exp(-x))


def _split(x):
    hi = x.astype(BF16)
    lo = (x - hi.astype(F32)).astype(BF16)
    return hi, lo


def _np_split(a64):
    a32 = np.asarray(a64, np.float32)
    hi = a32.astype(BF16)
    lo = (a32 - hi.astype(np.float32)).astype(BF16)
    return jnp.asarray(hi), jnp.asarray(lo)


def _mm_const(a_hi, a_lo, x, passes):
    if passes == 1:
        return jnp.dot(a_hi, x.astype(BF16), preferred_element_type=F32)
    xh, xl = _split(x)
    out = jnp.dot(a_hi, xh, preferred_element_type=F32)
    if passes >= 3:
        out = out + jnp.dot(a_lo, xh, preferred_element_type=F32)
        out = out + jnp.dot(a_hi, xl, preferred_element_type=F32)
    return out


def _mm3(a, b):
    ah, al = _split(a)
    bh, bl = _split(b)
    out = jnp.dot(ah, bh, preferred_element_type=F32)
    out = out + jnp.dot(al, bh, preferred_element_type=F32)
    return out + jnp.dot(ah, bl, preferred_element_type=F32)


def _head_mean_sq(x, j_bf):
    hi, lo = _split(x * x)
    return (jnp.dot(hi, j_bf, preferred_element_type=F32)
            + jnp.dot(lo, j_bf, preferred_element_type=F32))


def _rope(x, c, s):
    lane = lax.broadcasted_iota(jnp.int32, (1, LANES), 1)
    is_b = (lane & 16) != 0
    partner = jnp.where(is_b, pltpu.roll(x, 16, 1), pltpu.roll(x, LANES - 16, 1))
    return x * c + partner * s


def _col_chunks(n, parts):
    tiles = n // MXU_W
    bounds = [MXU_W * (tiles * i // parts) for i in range(parts + 1)]
    return list(zip(bounds[:-1], bounds[1:]))


def _in_proj_kernel(x_ref, g_ref, w_ref, o_ref):
    x = x_ref[...]
    ms = jnp.mean(x * x, axis=-1, keepdims=True)
    xn = (x * lax.rsqrt(ms + NORM_EPS) * g_ref[...]).astype(BF16)
    for lo, hi in _col_chunks(IN_COLS, 4):
        o_ref[:, lo:hi] = jnp.dot(xn, w_ref[:, lo:hi],
                                  preferred_element_type=F32).astype(o_ref.dtype)


def _in_proj(x2, g, w_bf):
    T = x2.shape[0]
    tm = min(512, T)
    return pl.pallas_call(
        _in_proj_kernel,
        grid=(T // tm,),
        in_specs=[pl.BlockSpec((tm, D_MODEL), lambda i: (i, 0)),
                  pl.BlockSpec((1, D_MODEL), lambda i: (0, 0)),
                  pl.BlockSpec((D_MODEL, IN_COLS), lambda i: (0, 0))],
        out_specs=pl.BlockSpec((tm, IN_COLS), lambda i: (i, 0)),
        out_shape=jax.ShapeDtypeStruct((T, IN_COLS), BF16),
        compiler_params=_cparams(("parallel",), 48),
        name="in_proj",
    )(x2, g, w_bf)


def _attn_prep_kernel(p_ref, c_ref, s_ref, qg_ref, kg_ref, j_ref, qt_ref, k_ref, vt_ref):
    c = c_ref[...]
    s = s_ref[...]
    j_bf = j_ref[...]
    tm = p_ref.shape[0]
    zeros = jnp.zeros((HEAD_DIM, tm), BF16)
    for blk in range(2):
        q = p_ref[:, LANES * blk:LANES * (blk + 1)].astype(F32)
        qn = q * lax.rsqrt(_head_mean_sq(q, j_bf) + NORM_EPS) * qg_ref[...]
        qt = (_rope(qn, c, s) * Q_SCALE).T.astype(BF16)
        lo, hi = qt[:HEAD_DIM], qt[HEAD_DIM:]
        if blk == 0:
            qt_ref[0] = jnp.concatenate([lo, zeros], axis=0)
            qt_ref[1] = jnp.concatenate([hi, zeros], axis=0)
        else:
            qt_ref[2] = jnp.concatenate([zeros, lo], axis=0)
            qt_ref[3] = jnp.concatenate([zeros, hi], axis=0)
    k = p_ref[:, A_K_OFF:A_K_OFF + LANES].astype(F32)
    kn = k * lax.rsqrt(_head_mean_sq(k, j_bf) + NORM_EPS) * kg_ref[...]
    k_ref[...] = _rope(kn, c, s).astype(BF16)
    v = p_ref[:, A_V_OFF:A_V_OFF + LANES].astype(F32)
    ones = jnp.ones((VT_ROWS - HEAD_DIM, ATT_KB), BF16)
    for t in range(tm // ATT_KB):
        vt = v[t * ATT_KB:(t + 1) * ATT_KB, :].T.astype(BF16)
        vt_ref[t] = jnp.concatenate([vt[:HEAD_DIM], ones, vt[HEAD_DIM:], ones], axis=0)


def _attn_prep(p3, cos_t, sin_t, qg, kg, j_bf):
    B, L, _ = p3.shape
    tm = min(1024, L)
    tab = pl.BlockSpec((tm, LANES), lambda b, i: (i, 0))
    vec = pl.BlockSpec((1, LANES), lambda b, i: (0, 0))
    return pl.pallas_call(
        _attn_prep_kernel,
        grid=(B, L // tm),
        in_specs=[pl.BlockSpec((None, tm, HY_OFF), lambda b, i: (b, i, 0)), tab, tab, vec, vec,
                  pl.BlockSpec((LANES, LANES), lambda b, i: (0, 0))],
        out_specs=[pl.BlockSpec((None, ATT_Q_HEADS, LANES, tm), lambda b, i: (b, 0, 0, i)),
                   pl.BlockSpec((None, tm, LANES), lambda b, i: (b, i, 0)),
                   pl.BlockSpec((None, tm // ATT_KB, ATT_KV_HEADS * VT_ROWS, ATT_KB),
                                lambda b, i: (b, i, 0, 0))],
        out_shape=[jax.ShapeDtypeStruct((B, ATT_Q_HEADS, LANES, L), BF16),
                   jax.ShapeDtypeStruct((B, L, LANES), BF16),
                   jax.ShapeDtypeStruct((B, L // ATT_KB, ATT_KV_HEADS * VT_ROWS, ATT_KB), BF16)],
        compiler_params=_cparams(("parallel", "parallel"), 32),
        name="attn_prep",
    )(p3, cos_t, sin_t, qg, kg, j_bf)


def _flash_kernel(qt_ref, k_ref, vt_ref, o_ref, sa_ref, sb_ref, *, ngrp, nsub):
    tq = qt_ref.shape[2]
    group = ATT_Q_HEADS // ATT_KV_HEADS

    def scores_into(s_ref, g):
        for t in range(nsub):
            st = pl.multiple_of((g * nsub + t) * ATT_KB, ATT_KB)
            kblk = k_ref[pl.ds(st, ATT_KB), :]
            for h in range(ATT_Q_HEADS):
                s_ref[t * ATT_Q_HEADS + h] = jnp.dot(kblk, qt_ref[h],
                                                     preferred_element_type=F32)

    def consume(s_ref, g, carry):
        for t in range(nsub):
            probs, stats = [], []
            for h in range(ATT_Q_HEADS):
                m, _ = carry[h]
                s = s_ref[t * ATT_Q_HEADS + h]
                m_new = jnp.maximum(m, jnp.max(s, axis=0, keepdims=True))
                stats.append((m_new, jnp.exp2(m - m_new)))
                probs.append(jnp.exp2((s - m_new).astype(BF16)))
            new = []
            for h in range(ATT_Q_HEADS):
                j = h // group
                vt = vt_ref[g * nsub + t, VT_ROWS * j:VT_ROWS * (j + 1), :]
                m_new, alpha = stats[h]
                acc = alpha * carry[h][1] + jnp.dot(vt, probs[h], preferred_element_type=F32)
                new.append((m_new, acc))
            carry = tuple(new)
        return carry

    def body(i, carry):
        g = 2 * i
        scores_into(sb_ref, g + 1)
        carry = consume(sa_ref, g, carry)
        scores_into(sa_ref, g + 2)
        return consume(sb_ref, g + 1, carry)

    carry = tuple((jnp.full((1, tq), -jnp.inf, F32), jnp.zeros((VT_ROWS, tq), F32))
                  for _ in range(ATT_Q_HEADS))
    scores_into(sa_ref, 0)
    if ngrp == 1:
        carry = consume(sa_ref, 0, carry)
    else:
        carry = lax.fori_loop(0, ngrp // 2 - 1, body, carry)
        scores_into(sb_ref, ngrp - 1)
        carry = consume(sa_ref, ngrp - 2, carry)
        carry = consume(sb_ref, ngrp - 1, carry)
    outs = [acc[:HEAD_DIM] / acc[HEAD_DIM:HEAD_DIM + 1] for _, acc in carry]
    o_ref[...] = jnp.concatenate(outs, axis=0).T.astype(o_ref.dtype)


def _flash(qt, k, vt):
    B, _, _, L = qt.shape
    tq = min(256, L)
    nsub = math.gcd(L // ATT_KB, 2)
    ngrp = L // (ATT_KB * nsub)
    assert ngrp == 1 or ngrp % 2 == 0
    s_scratch = pltpu.VMEM((nsub * ATT_Q_HEADS, ATT_KB, tq), F32)
    return pl.pallas_call(
        functools.partial(_flash_kernel, ngrp=ngrp, nsub=nsub),
        grid=(B, L // tq),
        in_specs=[pl.BlockSpec((None, ATT_Q_HEADS, LANES, tq), lambda b, i: (b, 0, 0, i)),
                  pl.BlockSpec((None, L, LANES), lambda b, i: (b, 0, 0)),
                  pl.BlockSpec((None, L // ATT_KB, ATT_KV_HEADS * VT_ROWS, ATT_KB),
                               lambda b, i: (b, 0, 0, 0))],
        out_specs=pl.BlockSpec((None, tq, ATT_Q_W), lambda b, i: (b, i, 0)),
        out_shape=jax.ShapeDtypeStruct((B, L, ATT_Q_W), BF16),
        scratch_shapes=[s_scratch, s_scratch],
        compiler_params=_cparams(("parallel", "arbitrary"), 48),
        name="flash_attn",
    )(qt, k, vt)


def _conv3_tile(main, prev_t, next_t, w, first, last):
    tb = main.shape[0]
    row = lax.broadcasted_iota(jnp.int32, (tb, 1), 0)
    before = jnp.where(first, 0.0, prev_t[HALO - 1:HALO, :])
    after = jnp.where(last, 0.0, next_t[0:1, :])
    up = jnp.where(row == 0, before, pltpu.roll(main, 1, 0))
    dn = jnp.where(row == tb - 1, after, pltpu.roll(main, tb - 1, 0))
    return up * w[0:1, :] + main * w[1:2, :] + dn * w[2:3, :]


def _halo_specs(tb, L, col_fn):
    nbh = L // HALO
    rh = tb // HALO
    main = pl.BlockSpec((None, tb, LANES), lambda b, w, s: (b, s, col_fn(w)))
    prev = pl.BlockSpec((None, HALO, LANES),
                        lambda b, w, s: (b, jnp.maximum(s * rh - 1, 0), col_fn(w)))
    nxt = pl.BlockSpec((None, HALO, LANES),
                       lambda b, w, s: (b, jnp.minimum((s + 1) * rh, nbh - 1), col_fn(w)))
    return [main, prev, nxt]


def _f32(ref):
    return ref[...].astype(F32)


def _hy_pro_kernel(x0_ref, x0p_ref, x0n_ref, x1_ref, x1p_ref, x1n_ref,
                   hv_ref, hvp_ref, hvn_ref, w0_ref, w1_ref, w2_ref,
                   b0_ref, b1_ref, b2_ref, z_ref, x0u_ref):
    first = pl.program_id(2) == 0
    last = pl.program_id(2) == pl.num_programs(2) - 1
    x0 = _conv3_tile(_f32(x0_ref), _f32(x0p_ref), _f32(x0n_ref), w0_ref[...], first, last) + b0_ref[...]
    x1 = _conv3_tile(_f32(x1_ref), _f32(x1p_ref), _f32(x1n_ref), w1_ref[...], first, last) + b1_ref[...]
    hv = _conv3_tile(_f32(hv_ref), _f32(hvp_ref), _f32(hvn_ref), w2_ref[...], first, last) + b2_ref[...]
    z_ref[...] = hv * x1
    x0u_ref[...] = x0


def _hy_prologue(p3, hcw, hcb):
    B, L, _ = p3.shape
    tb = min(1024, L)
    nw = HY_WIDTH // LANES
    in_specs = []
    for piece in range(3):
        in_specs += _halo_specs(tb, L, lambda w, piece=piece: HY_BLK + nw * piece + w)
    for piece in range(3):
        in_specs.append(pl.BlockSpec((3, LANES), lambda b, w, s, piece=piece: (0, nw * piece + w)))
    for piece in range(3):
        in_specs.append(pl.BlockSpec((1, LANES), lambda b, w, s, piece=piece: (0, nw * piece + w)))
    out_spec = pl.BlockSpec((None, tb, LANES), lambda b, w, s: (b, s, w))
    out_sh = jax.ShapeDtypeStruct((B, L, HY_WIDTH), F32)
    args = [p3] * 9 + [hcw] * 3 + [hcb] * 3
    return pl.pallas_call(
        _hy_pro_kernel,
        grid=(B, nw, L // tb),
        in_specs=in_specs,
        out_specs=[out_spec, out_spec],
        out_shape=[out_sh, out_sh],
        compiler_params=_cparams(("parallel", "parallel", "parallel"), 32),
        name="hyena_prologue",
    )(*args)


def _sconv_kernel(sb_ref, sc_ref, scp_ref, scn_ref, sh_ref, shp_ref, shn_ref, w_ref, o_ref):
    first = pl.program_id(2) == 0
    last = pl.program_id(2) == pl.num_programs(2) - 1
    y = _conv3_tile(_f32(sc_ref) * _f32(sh_ref), _f32(scp_ref) * _f32(shp_ref),
                    _f32(scn_ref) * _f32(shn_ref), w_ref[...], first, last)
    o_ref[...] = (_f32(sb_ref) * y).astype(o_ref.dtype)


def _sconv(p3, scw):
    B, L, _ = p3.shape
    tb = min(1024, L)
    nw = SC_WIDTH // LANES
    in_specs = [pl.BlockSpec((None, tb, LANES), lambda b, w, s: (b, s, SC_BLK + w))]
    in_specs += _halo_specs(tb, L, lambda w: SC_BLK + nw + w)
    in_specs += _halo_specs(tb, L, lambda w: SC_BLK + 2 * nw + w)
    in_specs.append(pl.BlockSpec((3, LANES), lambda b, w, s: (0, w)))
    return pl.pallas_call(
        _sconv_kernel,
        grid=(B, nw, L // tb),
        in_specs=in_specs,
        out_specs=pl.BlockSpec((None, tb, LANES), lambda b, w, s: (b, s, w)),
        out_shape=jax.ShapeDtypeStruct((B, L, SC_WIDTH), BF16),
        compiler_params=_cparams(("parallel", "parallel", "parallel"), 32),
        name="short_conv",
    )(*([p3] * 7 + [scw]))


def _hy_filter_kernel(z_ref, w1_ref, b1_ref, w2_ref, b2_ref, w3_ref, fr_ref, dl_ref,
                      k_ref, ss_ref, *, L):
    step = pl.program_id(0)
    tr = z_ref.shape[0]
    z = z_ref[...]
    h = jnp.sin(fr_ref[0:1, :] * (_mm3(z, w1_ref[...]) + b1_ref[...]))
    h = jnp.sin(fr_ref[1:2, :] * (_mm3(h, w2_ref[...]) + b2_ref[...]))
    h3 = _mm3(h, w3_ref[...])
    row = step * tr + lax.broadcasted_iota(jnp.int32, (tr, 1), 0)
    val = jnp.where(row < L, h3[:, :HY_WIDTH], h3[:, HY_WIDTH:])
    val = val * jnp.exp(-z[:, 0:1] * dl_ref[...])
    val = jnp.where(row == L, 0.0, val)
    k_ref[...] = val

    @pl.when(step == 0)
    def _():
        ss_ref[...] = jnp.zeros_like(ss_ref)

    ss_ref[...] += jnp.sum(val * val, axis=0, keepdims=True)


def _hy_filter_raw(z2, w1p, b1, w2, b2, w3, freq, deltas, L):
    n = 2 * L
    tr = min(512, n)
    H = HY_FILTER_HIDDEN
    full = lambda shape: pl.BlockSpec(shape, lambda i: (0, 0))
    return pl.pallas_call(
        functools.partial(_hy_filter_kernel, L=L),
        grid=(n // tr,),
        in_specs=[pl.BlockSpec((tr, H), lambda i: (i, 0)),
                  full((H, H)), full((1, H)), full((H, H)), full((1, H)),
                  full((H, 2 * HY_WIDTH)), full((2, H)), full((1, HY_WIDTH))],
        out_specs=[pl.BlockSpec((tr, HY_WIDTH), lambda i: (i, 0)), full((1, HY_WIDTH))],
        out_shape=[jax.ShapeDtypeStruct((n, HY_WIDTH), F32),
                   jax.ShapeDtypeStruct((1, HY_WIDTH), F32)],
        compiler_params=_cparams(("arbitrary",), 32),
        name="hyena_filter",
    )(z2, w1p, b1, w2, b2, w3, freq, deltas)


def _fft_dims(L):
    n = 2 * L
    n2 = 128 if n >= 16384 else 64
    return n // n2, n2


def _fft_consts(n1, n2, r):
    n = n1 * n2
    a1 = 2.0 * np.pi * np.outer(np.arange(n1), np.arange(n1)) / n1
    c1, s1 = np.cos(a1), np.sin(a1)
    a2 = 2.0 * np.pi * np.outer(np.arange(n2), np.arange(n2)) / n2
    c2, s2 = np.cos(a2), np.sin(a2)
    f1 = np.concatenate([c1[:, :r], -s1[:, :r]], axis=0)
    f1_full = np.concatenate([c1, -s1], axis=0)
    m2 = np.block([[c2, s2], [-s2, c2]])
    m2i = np.block([[c2, -s2], [s2, c2]])
    fi = np.concatenate([c1[:r, :], -s1[:r, :]], axis=1)
    aw = 2.0 * np.pi * np.arange(n1) / n
    w_re = np.broadcast_to(np.cos(aw)[:, None], (n1, LANES)).astype(np.float32)
    w_im = np.broadcast_to(-np.sin(aw)[:, None], (n1, LANES)).astype(np.float32)
    return dict(f1=_np_split(f1), f1_full=_np_split(f1_full), m2=_np_split(m2),
                m2i=_np_split(m2i), fi=_np_split(fi),
                w_re=jnp.asarray(w_re), w_im=jnp.asarray(w_im))


def _cmul(a_re, a_im, b_re, b_im):
    return a_re * b_re - a_im * b_im, a_re * b_im + a_im * b_re


def _fft_stage1(src_ref, a_ref, f_hi, f_lo, w_re, w_im, n1, n2, r, passes):
    def body(i2, carry):
        t_re, t_im = carry
        u_re, u_im = _cmul(t_re, t_im, w_re, w_im)
        i = 2 * i2
        xs = jnp.concatenate([src_ref[pl.ds(i, r, stride=n2), :],
                              src_ref[pl.ds(i + 1, r, stride=n2), :]], axis=1)
        a = _mm_const(f_hi, f_lo, xs, passes)
        for half, (tr, ti) in enumerate(((t_re, t_im), (u_re, u_im))):
            ar = a[:n1, LANES * half:LANES * (half + 1)]
            ai = a[n1:, LANES * half:LANES * (half + 1)]
            base = pl.multiple_of((i + half) * 2 * n1, 2 * n1)
            a_ref[pl.ds(base, n1), :] = ar * tr - ai * ti
            a_ref[pl.ds(base + n1, n1), :] = ar * ti + ai * tr
        return _cmul(u_re, u_im, w_re, w_im)

    ones = jnp.ones((n1, LANES), F32)
    lax.fori_loop(0, n2 // 2, body, (ones, jnp.zeros((n1, LANES), F32)))


def _filter_fft_kernel(k_ref, ss_ref, f_hi_ref, f_lo_ref, m_hi_ref, m_lo_ref, wre_ref, wim_ref,
                       kr_ref, ki_ref, a_ref, *, n1, n2, kc):
    c = pl.program_id(1)

    @pl.when(c == 0)
    def _():
        _fft_stage1(k_ref, a_ref, f_hi_ref[...], f_lo_ref[...], wre_ref[...], wim_ref[...],
                    n1, n2, n1, FFT_PASSES_FILTER)

    scale = lax.rsqrt(ss_ref[...] + NORM_EPS) * (1.0 / (n1 * n2))

    def body(kk, _):
        k1 = c * kc + kk
        ar = a_ref[pl.ds(k1, n2, stride=2 * n1), :]
        ai = a_ref[pl.ds(n1 + k1, n2, stride=2 * n1), :]
        x = _mm_const(m_hi_ref[...], m_lo_ref[...], jnp.concatenate([ar, ai], axis=0),
                      FFT_PASSES_FILTER)
        kr_ref[kk] = x[:n2] * scale
        ki_ref[kk] = x[n2:] * scale
        return 0

    lax.fori_loop(0, kc, body, 0)


def _filter_fft(k2raw, sumsq, cst, n1, n2):
    n = n1 * n2
    kc = min(16, n1)
    nw = HY_WIDTH // LANES
    f_hi, f_lo = cst["f1_full"]
    m_hi, m_lo = cst["m2"]
    full = lambda a: pl.BlockSpec(a.shape, lambda w, c: (0,) * a.ndim)
    out_spec = pl.BlockSpec((kc, n2, LANES), lambda w, c: (c, 0, w))
    out_sh = jax.ShapeDtypeStruct((n1, n2, HY_WIDTH), F32)
    return pl.pallas_call(
        functools.partial(_filter_fft_kernel, n1=n1, n2=n2, kc=kc),
        grid=(nw, n1 // kc),
        in_specs=[pl.BlockSpec((n, LANES), lambda w, c: (0, w)),
                  pl.BlockSpec((1, LANES), lambda w, c: (0, w)),
                  full(f_hi), full(f_lo), full(m_hi), full(m_lo),
                  full(cst["w_re"]), full(cst["w_im"])],
        out_specs=[out_spec, out_spec],
        out_shape=[out_sh, out_sh],
        scratch_shapes=[pltpu.VMEM((2 * n, LANES), F32)],
        compiler_params=_cparams(("parallel", "arbitrary"), 48),
        name="hyena_filter_fft",
    )(k2raw, sumsq, f_hi, f_lo, m_hi, m_lo, cst["w_re"], cst["w_im"])


def _hy_conv_kernel(z_ref, x0_ref, hb_ref, kr_ref, ki_ref, f_hi_ref, f_lo_ref,
                    m_hi_ref, m_lo_ref, mi_hi_ref, mi_lo_ref, fi_hi_ref, fi_lo_ref,
                    wre_ref, wim_ref, o_ref, a_ref, *, n1, n2, kc):
    c = pl.program_id(2)
    r = n1 // 2

    @pl.when(c == 0)
    def _():
        _fft_stage1(z_ref, a_ref, f_hi_ref[...], f_lo_ref[...], wre_ref[...], wim_ref[...],
                    n1, n2, r, FFT_PASSES_DATA)

    def mid(kk2, _):
        k1 = c * kc + 2 * kk2
        ld = lambda off: a_ref[pl.ds(off, n2, stride=2 * n1), :]
        xin = jnp.concatenate([jnp.concatenate([ld(k1), ld(k1 + 1)], axis=1),
                               jnp.concatenate([ld(n1 + k1), ld(n1 + k1 + 1)], axis=1)], axis=0)
        x = _mm_const(m_hi_ref[...], m_lo_ref[...], xin, FFT_PASSES_DATA)
        xr, xi = x[:n2], x[n2:]
        kr = jnp.concatenate([kr_ref[2 * kk2], kr_ref[2 * kk2 + 1]], axis=1)
        ki = jnp.concatenate([ki_ref[2 * kk2], ki_ref[2 * kk2 + 1]], axis=1)
        y = jnp.concatenate([xr * kr - xi * ki, xr * ki + xi * kr], axis=0)
        bq = _mm_const(mi_hi_ref[...], mi_lo_ref[...], y, FFT_PASSES_DATA)
        for half in range(2):
            sl = slice(LANES * half, LANES * (half + 1))
            a_ref[pl.ds(k1 + half, n2, stride=2 * n1), :] = bq[:n2, sl]
            a_ref[pl.ds(n1 + k1 + half, n2, stride=2 * n1), :] = bq[n2:, sl]
        return 0

    lax.fori_loop(0, kc // 2, mid, 0, unroll=4)

    @pl.when(c == pl.num_programs(2) - 1)
    def _():
        w_re = wre_ref[...]
        w_im = wim_ref[...]
        hb = hb_ref[...]

        def body(i2, carry):
            t_re, t_im = carry
            u_re, u_im = _cmul(t_re, t_im, w_re, w_im)
            i = 2 * i2
            cols = []
            for half, (tr, ti) in enumerate(((t_re, t_im), (u_re, u_im))):
                base = pl.multiple_of((i + half) * 2 * n1, 2 * n1)
                br = a_ref[pl.ds(base, n1), :]
                bi = a_ref[pl.ds(base + n1, n1), :]
                cols.append(jnp.concatenate([br * tr + bi * ti, bi * tr - br * ti], axis=0))
            y = _mm_const(fi_hi_ref[...], fi_lo_ref[...], jnp.concatenate(cols, axis=1),
                          FFT_PASSES_DATA)
            for half in range(2):
                rows = pl.ds(i + half, r, stride=n2)
                yh = y[:, LANES * half:LANES * (half + 1)]
                o_ref[rows, :] = (x0_ref[rows, :] * (yh + z_ref[rows, :] * hb)).astype(o_ref.dtype)
            return _cmul(u_re, u_im, w_re, w_im)

        ones = jnp.ones((n1, LANES), F32)
        lax.fori_loop(0, n2 // 2, body, (ones, jnp.zeros((n1, LANES), F32)))


def _hy_conv(z, x0u, hbias, kf_re, kf_im, cst, n1, n2):
    B, L, _ = z.shape
    kc = min(16, n1)
    nw = HY_WIDTH // LANES
    seq = pl.BlockSpec((None, L, LANES), lambda b, w, c: (b, 0, w))
    kf_spec = pl.BlockSpec((kc, n2, LANES), lambda b, w, c: (c, 0, w))
    full = lambda a: pl.BlockSpec(a.shape, lambda b, w, c: (0,) * a.ndim)
    consts = [*cst["f1"], *cst["m2"], *cst["m2i"], *cst["fi"], cst["w_re"], cst["w_im"]]
    return pl.pallas_call(
        functools.partial(_hy_conv_kernel, n1=n1, n2=n2, kc=kc),
        grid=(B, nw, n1 // kc),
        in_specs=[seq, seq, pl.BlockSpec((1, LANES), lambda b, w, c: (0, w)), kf_spec, kf_spec]
                 + [full(a) for a in consts],
        out_specs=seq,
        out_shape=jax.ShapeDtypeStruct((B, L, HY_WIDTH), F32),
        scratch_shapes=[pltpu.VMEM((2 * n1 * n2, LANES), F32)],
        compiler_params=_cparams(("parallel", "parallel", "arbitrary"), 56),
        name="hyena_conv",
    )(z, x0u, hbias, kf_re, kf_im, *consts)


T_D0, T_D1, T_WQF, T_WQB, T_WKF, T_WKB, T_GCF, T_GCB, T_BD = range(9)


def _ret_tables_kernel(rl_ref, rh_ref, t_ref):
    C = RET_CHUNK
    i = lax.broadcasted_iota(jnp.int32, (C, LANES), 0).astype(F32)
    jn = lax.broadcasted_iota(jnp.int32, (C, LANES), 1).astype(F32)
    log_g = lambda e: jnp.log1p(-jnp.exp2(-e))
    diff = i - jn
    for hp in range(2):
        lgf = log_g(rh_ref[hp, 0:1, :])
        lgb = log_g(rh_ref[hp, 1:2, :])
        fwd = jnp.exp(jnp.where(diff >= 0, diff, 0.0) * lgf)
        bwd = jnp.exp(jnp.where(diff < 0, -diff, 0.0) * lgb)
        t_ref[T_D0 + hp] = jnp.where(diff >= 0, fwd, bwd)
    lf = log_g(rl_ref[0:1, :])
    lb = log_g(rl_ref[1:2, :])
    t_ref[T_WQF] = jnp.exp((i + 1.0) * lf)
    t_ref[T_WQB] = jnp.exp((C - i) * lb)
    t_ref[T_WKF] = jnp.exp((C - 1.0 - i) * lf)
    t_ref[T_WKB] = jnp.exp(i * lb)
    bd = jnp.where((i < HEAD_DIM) == (jn < HEAD_DIM), 1.0, 0.0)
    t_ref[T_GCF] = jnp.exp(C * lf) * bd
    t_ref[T_GCB] = jnp.exp(C * lb) * bd
    t_ref[T_BD] = bd


def _ret_tables(rde):
    per_lane = jnp.repeat(rde.reshape(2, 2, 2), HEAD_DIM, axis=2)
    per_lane = per_lane.transpose(1, 0, 2)
    per_head = jnp.broadcast_to(rde.reshape(2, 2, 2, 1), (2, 2, 2, LANES))
    per_head = per_head.transpose(1, 2, 0, 3)
    return pl.pallas_call(
        _ret_tables_kernel,
        grid=(2,),
        in_specs=[pl.BlockSpec((None, 2, LANES), lambda j: (j, 0, 0)),
                  pl.BlockSpec((None, 2, 2, LANES), lambda j: (j, 0, 0, 0))],
        out_specs=pl.BlockSpec((None, 9, RET_CHUNK, LANES), lambda j: (j, 0, 0, 0)),
        out_shape=jax.ShapeDtypeStruct((2, 9, RET_CHUNK, LANES), F32),
        compiler_params=_cparams(("parallel",), 32),
        name="retention_tables",
    )(per_lane, per_head)


def _ret_state_kernel(rk_ref, rv_ref, c_ref, s_ref, t_ref, st_ref, r_ref):
    C = RET_CHUNK
    nch = rk_ref.shape[0] // C

    @pl.when(pl.program_id(2) == 0)
    def _():
        r_ref[...] = jnp.zeros_like(r_ref)

    wkb = t_ref[T_WKB]
    gcb = t_ref[T_GCB]
    bd = t_ref[T_BD]
    state = r_ref[...]
    for n in range(nch - 1, -1, -1):
        st_ref[n] = state
        sl = slice(n * C, (n + 1) * C)
        k = _rope(rk_ref[sl, :].astype(F32), c_ref[sl, :], s_ref[sl, :]) * (HEAD_DIM ** -0.5)
        kw = (k * wkb).astype(BF16)
        v = rv_ref[sl, :]
        kv = lax.dot_general(kw, v, (((0,), (0,)), ((), ())), preferred_element_type=F32)
        state = gcb * state + bd * kv
    r_ref[...] = state


def _ret_main_kernel(rq_ref, rk_ref, rv_ref, rg_ref, c_ref, s_ref, t_ref, st_ref, j_ref,
                     o_ref, sf_ref):
    C = RET_CHUNK
    nch = rq_ref.shape[0] // C

    @pl.when(pl.program_id(2) == 0)
    def _():
        sf_ref[...] = jnp.zeros_like(sf_ref)

    lane = lax.broadcasted_iota(jnp.int32, (1, LANES), 1)
    lo64 = lane < HEAD_DIM
    j_bf = j_ref[...]
    state = sf_ref[...]
    for n in range(nch):
        sl = slice(n * C, (n + 1) * C)
        cs, sn = c_ref[sl, :], s_ref[sl, :]
        q = _rope(rq_ref[sl, :].astype(F32), cs, sn)
        k = _rope(rk_ref[sl, :].astype(F32), cs, sn) * (HEAD_DIM ** -0.5)
        vb = rv_ref[sl, :]
        qb = q.astype(BF16)
        kb = k.astype(BF16)
        zero = jnp.zeros_like(qb)
        parts = []
        for hp in range(2):
            qm = jnp.where(lo64, qb, zero) if hp == 0 else jnp.where(lo64, zero, qb)
            sc = lax.dot_general(qm, kb, (((1,), (1,)), ((), ())), preferred_element_type=F32)
            sd = (sc * t_ref[T_D0 + hp]).astype(BF16)
            parts.append(jnp.dot(sd, vb, preferred_element_type=F32))
        o = jnp.where(lo64, parts[0], parts[1])
        o = o + jnp.dot((q * t_ref[T_WQF]).astype(BF16), state.astype(BF16),
                        preferred_element_type=F32)
        o = o + jnp.dot((q * t_ref[T_WQB]).astype(BF16), st_ref[n].astype(BF16),
                        preferred_element_type=F32)
        kw = (k * t_ref[T_WKF]).astype(BF16)
        kv = lax.dot_general(kw, vb, (((0,), (0,)), ((), ())), preferred_element_type=F32)
        state = t_ref[T_GCF] * state + t_ref[T_BD] * kv
        ret = o * lax.rsqrt(_head_mean_sq(o, j_bf) + NORM_EPS)
        g = rg_ref[sl, :].astype(F32)
        o_ref[sl, :] = (ret * (g * _sigmoid(g))).astype(o_ref.dtype)
    sf_ref[...] = state


def _retention(p3, cos_t, sin_t, tables, j_bf):
    B, L, _ = p3.shape
    C = RET_CHUNK
    tb = min(1024, L)
    nblk = L // tb
    nch = tb // C
    nw = RET_W // LANES
    col = lambda piece: (lambda b, j, s: (b, s, RET_BLK + nw * piece + j))
    colr = lambda piece: (lambda b, j, s: (b, nblk - 1 - s, RET_BLK + nw * piece + j))
    seq = lambda fn: pl.BlockSpec((None, tb, LANES), fn)
    tab_spec = pl.BlockSpec((None, 9, C, LANES), lambda b, j, s: (j, 0, 0, 0))
    states = pl.pallas_call(
        _ret_state_kernel,
        grid=(B, nw, nblk),
        in_specs=[seq(colr(1)), seq(colr(2)),
                  pl.BlockSpec((tb, LANES), lambda b, j, s: (nblk - 1 - s, 0)),
                  pl.BlockSpec((tb, LANES), lambda b, j, s: (nblk - 1 - s, 0)),
                  tab_spec],
        out_specs=pl.BlockSpec((None, None, nch, LANES, LANES),
                               lambda b, j, s: (b, j, nblk - 1 - s, 0, 0)),
        out_shape=jax.ShapeDtypeStruct((B, nw, L // C, LANES, LANES), F32),
        scratch_shapes=[pltpu.VMEM((LANES, LANES), F32)],
        compiler_params=_cparams(("parallel", "parallel", "arbitrary"), 32),
        name="retention_state",
    )(p3, p3, cos_t, sin_t, tables)
    return pl.pallas_call(
        _ret_main_kernel,
        grid=(B, nw, nblk),
        in_specs=[seq(col(0)), seq(col(1)), seq(col(2)), seq(col(3)),
                  pl.BlockSpec((tb, LANES), lambda b, j, s: (s, 0)),
                  pl.BlockSpec((tb, LANES), lambda b, j, s: (s, 0)),
                  tab_spec,
                  pl.BlockSpec((None, None, nch, LANES, LANES), lambda b, j, s: (b, j, s, 0, 0)),
                  pl.BlockSpec((LANES, LANES), lambda b, j, s: (0, 0))],
        out_specs=pl.BlockSpec((None, tb, LANES), lambda b, j, s: (b, s, j)),
        out_shape=jax.ShapeDtypeStruct((B, L, RET_W), BF16),
        scratch_shapes=[pltpu.VMEM((LANES, LANES), F32)],
        compiler_params=_cparams(("parallel", "parallel", "arbitrary"), 32),
        name="retention_main",
    )(p3, p3, p3, p3, cos_t, sin_t, tables, states, j_bf)


def _merge_kernel(oa_ref, ob_ref, oc_ref, od_ref, g0_ref, g1_ref, g2_ref, g3_ref,
                  wb_ref, wo_ref, x_ref, gn_ref, h_ref):
    merged = None
    for n, (br, gr) in enumerate(((oa_ref, g0_ref), (ob_ref, g1_ref),
                                  (oc_ref, g2_ref), (od_ref, g3_ref))):
        term = _sigmoid(gr[...].astype(F32)) * jnp.dot(br[...].astype(BF16), wb_ref[n],
                                                preferred_element_type=F32)
        merged = term if merged is None else merged + term
    y = jnp.dot(merged.astype(BF16), wo_ref[...], preferred_element_type=F32)
    ms = jnp.mean(y * y, axis=-1, keepdims=True)
    h_ref[...] = x_ref[...] + y * lax.rsqrt(ms + NORM_EPS) * gn_ref[...]


def _merge(oa, ob, oc, od, p2, wb_bf, wo_bf, x2, gn):
    T = x2.shape[0]
    tm = min(1024, T)
    br = pl.BlockSpec((tm, BRANCH_W), lambda i: (i, 0))
    gate = lambda n: pl.BlockSpec((tm, D_MODEL), lambda i, n=n: (i, GATE_BLK + n))
    row = pl.BlockSpec((tm, D_MODEL), lambda i: (i, 0))
    return pl.pallas_call(
        _merge_kernel,
        grid=(T // tm,),
        in_specs=[br, br, br, br, gate(0), gate(1), gate(2), gate(3),
                  pl.BlockSpec((N_BRANCH, BRANCH_W, D_MODEL), lambda i: (0, 0, 0)),
                  pl.BlockSpec((D_MODEL, D_MODEL), lambda i: (0, 0)),
                  row, pl.BlockSpec((1, D_MODEL), lambda i: (0, 0))],
        out_specs=row,
        out_shape=jax.ShapeDtypeStruct((T, D_MODEL), F32),
        compiler_params=_cparams(("parallel",), 48),
        name="merge_out_proj",
    )(oa, ob, oc, od, p2, p2, p2, p2, wb_bf, wo_bf, x2, gn)


def _ffn_kernel(h_ref, g2_ref, wg_ref, wu_ref, wd_ref, g3_ref, o_ref):
    h = h_ref[...]
    ms = jnp.mean(h * h, axis=-1, keepdims=True)
    hn = (h * lax.rsqrt(ms + NORM_EPS) * g2_ref[...]).astype(BF16)
    f = None
    for lo, hi in _col_chunks(D_FF, 2):
        g = jnp.dot(hn, wg_ref[:, lo:hi], preferred_element_type=F32)
        u = jnp.dot(hn, wu_ref[:, lo:hi], preferred_element_type=F32)
        a = (g * _sigmoid(g) * u).astype(BF16)
        part = jnp.dot(a, wd_ref[lo:hi, :], preferred_element_type=F32)
        f = part if f is None else f + part
    ms = jnp.mean(f * f, axis=-1, keepdims=True)
    o_ref[...] = h + f * lax.rsqrt(ms + NORM_EPS) * g3_ref[...]


def _ffn(h2, g2, wg_bf, wu_bf, wd_bf, g3):
    T = h2.shape[0]
    tm = min(512, T)
    row = pl.BlockSpec((tm, D_MODEL), lambda i: (i, 0))
    vec = pl.BlockSpec((1, D_MODEL), lambda i: (0, 0))
    return pl.pallas_call(
        _ffn_kernel,
        grid=(T // tm,),
        in_specs=[row, vec,
                  pl.BlockSpec((D_MODEL, D_FF), lambda i: (0, 0)),
                  pl.BlockSpec((D_MODEL, D_FF), lambda i: (0, 0)),
                  pl.BlockSpec((D_FF, D_MODEL), lambda i: (0, 0)),
                  vec],
        out_specs=row,
        out_shape=jax.ShapeDtypeStruct((T, D_MODEL), F32),
        compiler_params=_cparams(("parallel",), 48),
        name="ffn",
    )(h2, g2, wg_bf, wu_bf, wd_bf, g3)


def _rope_tables(L):
    rows = L // GRID_W
    r = jnp.repeat(jnp.arange(rows, dtype=F32), GRID_W)
    c = jnp.tile(jnp.arange(GRID_W, dtype=F32), rows)
    inv = ROPE_BASE ** (-jnp.arange(ROPE_FREQS, dtype=F32) / ROPE_FREQS)
    ar = r[:, None] * inv
    ac = c[:, None] * inv
    cos64 = jnp.concatenate([jnp.cos(ar), jnp.cos(ar), jnp.cos(ac), jnp.cos(ac)], axis=1)
    sin64 = jnp.concatenate([-jnp.sin(ar), jnp.sin(ar), -jnp.sin(ac), jnp.sin(ac)], axis=1)
    return jnp.tile(cos64, (1, 2)), jnp.tile(sin64, (1, 2))


def _filter_features(L):
    t = jnp.linspace(0.0, 1.0, L, dtype=F32)[:, None]
    f = jnp.linspace(1e-4, HY_BANDS - 1, HY_BANDS, dtype=F32)
    ang = (2.0 * math.pi / L) * jnp.arange(L, dtype=F32)[:, None] * f[None, :]
    z = jnp.concatenate([t, jnp.cos(ang), -jnp.sin(ang)], axis=-1)
    z = jnp.pad(z, ((0, 0), (0, HY_FILTER_HIDDEN - HY_EMB)))
    idx = np.concatenate([np.arange(L), [0], np.arange(L - 1, 0, -1)])
    return z[idx]


def _head_mean_matrix():
    i = np.arange(LANES)
    j = ((i[:, None] // HEAD_DIM) == (i[None, :] // HEAD_DIM)).astype(np.float32) / HEAD_DIM
    return jnp.asarray(j.astype(BF16))


def _trunk(x, wts, fft_dims=None):
    B, L, _ = x.shape
    T = B * L
    n1, n2 = fft_dims or _fft_dims(L)
    cst = _fft_consts(n1, n2, n1 // 2)
    cos_t, sin_t = _rope_tables(L)
    z2 = _filter_features(L)
    deltas = jnp.abs(jnp.linspace(math.log(HY_TARGET) / HY_SLOW_DECAY,
                                  math.log(HY_TARGET) / HY_FAST_DECAY, HY_WIDTH, dtype=F32))[None, :]
    j_bf = _head_mean_matrix()
    depth = wts["w_in"].shape[0]
    x2 = x.reshape(T, D_MODEL)
    for l in range(depth):
        ng = wts["norm_gains"][l]
        w1p = jnp.pad(wts["hy_w1"][l], ((0, HY_FILTER_HIDDEN - HY_EMB), (0, 0)))
        k2raw, sumsq = _hy_filter_raw(z2, w1p, wts["hy_b1"][l][None, :], wts["hy_w2"][l],
                                      wts["hy_b2"][l][None, :], wts["hy_w3"][l],
                                      wts["hy_freq"][l], deltas, L)
        kf_re, kf_im = _filter_fft(k2raw, sumsq, cst, n1, n2)

        p2 = _in_proj(x2, ng[0][None, :], wts["w_in_bf"][l])
        p3 = p2.reshape(B, L, IN_COLS)

        qg = jnp.tile(wts["qk_norm"][l, 0], 2)[None, :]
        kg = jnp.tile(wts["qk_norm"][l, 1], 2)[None, :]
        qt, k, vt = _attn_prep(p3, cos_t, sin_t, qg, kg, j_bf)
        out_a = _flash(qt, k, vt)

        z, x0u = _hy_prologue(p3, wts["hy_conv_w"][l], wts["hy_conv_b"][l][None, :])
        out_b = _hy_conv(z, x0u, wts["hy_bias"][l][None, :], kf_re, kf_im, cst, n1, n2)

        tables = _ret_tables(wts["ret_decay_exp"][l])
        out_c = _retention(p3, cos_t, sin_t, tables, j_bf)

        out_d = _sconv(p3, wts["sc_conv_w"][l])

        h2 = _merge(out_a.reshape(T, BRANCH_W), out_b.reshape(T, BRANCH_W),
                    out_c.reshape(T, BRANCH_W), out_d.reshape(T, BRANCH_W),
                    p2, wts["w_branch_bf"][l], wts["w_out_bf"][l], x2, ng[1][None, :])
        x2 = _ffn(h2, ng[2][None, :], wts["w_gate_bf"][l], wts["w_up_bf"][l],
                  wts["w_ffn_out_bf"][l], ng[3][None, :])
    return x2.reshape(B, L, D_MODEL)


def kernel(x_prompt, x_sample, norm_gains, w_in, qk_norm, hy_conv_w, hy_conv_b, hy_w1, hy_b1, hy_w2,
           hy_b2, hy_w3, hy_freq, hy_bias, ret_decay_exp, sc_conv_w, w_branch, w_out, w_ffn_in,
           w_ffn_out):
    wts = dict(norm_gains=norm_gains, w_in=w_in, qk_norm=qk_norm, hy_conv_w=hy_conv_w,
               hy_conv_b=hy_conv_b, hy_w1=hy_w1, hy_b1=hy_b1, hy_w2=hy_w2, hy_b2=hy_b2,
               hy_w3=hy_w3, hy_freq=hy_freq, hy_bias=hy_bias, ret_decay_exp=ret_decay_exp,
               sc_conv_w=sc_conv_w,
               w_in_bf=w_in.astype(BF16), w_branch_bf=w_branch.astype(BF16),
               w_out_bf=w_out.astype(BF16), w_gate_bf=w_ffn_in[..., :D_FF].astype(BF16),
               w_up_bf=w_ffn_in[..., D_FF:].astype(BF16), w_ffn_out_bf=w_ffn_out.astype(BF16))
    return _trunk(x_prompt, wts), _trunk(x_sample, wts)
```

```python
import functools
import math

import numpy as np
import jax
import jax.numpy as jnp
from jax import lax
from jax.experimental import pallas as pl
from jax.experimental.pallas import tpu as pltpu

F32 = jnp.float32
BF16 = jnp.bfloat16

D_MODEL = 1024
GRID_W = 64
N_BRANCH = 4
BRANCH_W = 256
HEAD_DIM = 64
ATT_Q_HEADS = 4
ATT_KV_HEADS = 2
ROPE_BASE = 10000.0
ROPE_FREQS = HEAD_DIM // 4
HY_WIDTH = BRANCH_W
HY_EMB = 33
HY_BANDS = (HY_EMB - 1) // 2
HY_FILTER_HIDDEN = 64
HY_FAST_DECAY = 0.3
HY_SLOW_DECAY = 1.5
HY_TARGET = 1e-2
RET_HEADS = 4
RET_W = RET_HEADS * HEAD_DIM
RET_CHUNK = 128
SC_WIDTH = BRANCH_W
D_FF = 2816
NORM_EPS = 1e-6

ATT_Q_W = ATT_Q_HEADS * HEAD_DIM
ATT_KV_W = ATT_KV_HEADS * HEAD_DIM
A_K_OFF = ATT_Q_W
A_V_OFF = A_K_OFF + ATT_KV_W
HY_OFF = A_V_OFF + ATT_KV_W
RET_OFF = HY_OFF + 3 * HY_WIDTH
SC_OFF = RET_OFF + 4 * RET_W
GATE_OFF = SC_OFF + 3 * SC_WIDTH
IN_COLS = GATE_OFF + N_BRANCH * D_MODEL

LANES = 128
SUBLANES = 8
MXU_W = 256
HALO = 2 * SUBLANES
HY_BLK = HY_OFF // LANES
RET_BLK = RET_OFF // LANES
SC_BLK = SC_OFF // LANES
GATE_BLK = GATE_OFF // D_MODEL
ATT_KB = 512
VT_ROWS = HEAD_DIM + HALO
Q_SCALE = HEAD_DIM ** -0.5 * math.log2(math.e)
FFT_PASSES_FILTER = 3
FFT_PASSES_DATA = 1
FFT_CH = 8
MIB = 1 << 20


def _cparams(sem, vmem_mib):
    return pltpu.CompilerParams(dimension_semantics=sem, vmem_limit_bytes=vmem_mib * MIB)


def _sigmoid(x):
    return 0.5 * jnp.tanh(0.5 * x) + 0.5


def _split(x):
    hi = x.astype(BF16)
    lo = (x - hi.astype(F32)).astype(BF16)
    return hi, lo


def _np_split(a64):
    a32 = np.asarray(a64, np.float32)
    hi = a32.astype(BF16)
    lo = (a32 - hi.astype(np.float32)).astype(BF16)
    return jnp.asarray(hi), jnp.asarray(lo)


def _mm_const(a_hi, a_lo, x, passes):
    if passes == 1:
        return jnp.dot(a_hi, x.astype(BF16), preferred_element_type=F32)
    xh, xl = _split(x)
    out = jnp.dot(a_hi, xh, preferred_element_type=F32)
    if passes >= 3:
        out = out + jnp.dot(a_lo, xh, preferred_element_type=F32)
        out = out + jnp.dot(a_hi, xl, preferred_element_type=F32)
    return out


def _mm3(a, b):
    ah, al = _split(a)
    bh, bl = _split(b)
    out = jnp.dot(ah, bh, preferred_element_type=F32)
    out = out + jnp.dot(al, bh, preferred_element_type=F32)
    return out + jnp.dot(ah, bl, preferred_element_type=F32)


def _head_mean_sq(x, j_bf):
    hi, lo = _split(x * x)
    return (jnp.dot(hi, j_bf, preferred_element_type=F32)
            + jnp.dot(lo, j_bf, preferred_element_type=F32))


def _rope(x, c, s):
    lane = lax.broadcasted_iota(jnp.int32, (1, LANES), 1)
    is_b = (lane & 16) != 0
    partner = jnp.where(is_b, pltpu.roll(x, 16, 1), pltpu.roll(x, LANES - 16, 1))
    return x * c + partner * s


def _col_chunks(n, parts):
    tiles = n // MXU_W
    bounds = [MXU_W * (tiles * i // parts) for i in range(parts + 1)]
    return list(zip(bounds[:-1], bounds[1:]))


def _in_proj_kernel(x_ref, g_ref, w_ref, o_ref):
    x = x_ref[...]
    ms = jnp.mean(x * x, axis=-1, keepdims=True)
    xn = (x * lax.rsqrt(ms + NORM_EPS) * g_ref[...]).astype(BF16)
    for lo, hi in _col_chunks(IN_COLS, 4):
        o_ref[:, lo:hi] = jnp.dot(xn, w_ref[:, lo:hi],
                                  preferred_element_type=F32).astype(o_ref.dtype)


def _in_proj(x2, g, w_bf):
    T = x2.shape[0]
    tm = min(512, T)
    return pl.pallas_call(
        _in_proj_kernel,
        grid=(T // tm,),
        in_specs=[pl.BlockSpec((tm, D_MODEL), lambda i: (i, 0)),
                  pl.BlockSpec((1, D_MODEL), lambda i: (0, 0)),
                  pl.BlockSpec((D_MODEL, IN_COLS), lambda i: (0, 0))],
        out_specs=pl.BlockSpec((tm, IN_COLS), lambda i: (i, 0)),
        out_shape=jax.ShapeDtypeStruct((T, IN_COLS), BF16),
        compiler_params=_cparams(("parallel",), 48),
        name="in_proj",
    )(x2, g, w_bf)


def _attn_prep_kernel(p_ref, c_ref, s_ref, qg_ref, kg_ref, j_ref, qt_ref, k_ref, vt_ref):
    c = c_ref[...]
    s = s_ref[...]
    j_bf = j_ref[...]
    tm = p_ref.shape[0]
    zeros = jnp.zeros((HEAD_DIM, tm), BF16)
    for blk in range(2):
        q = p_ref[:, LANES * blk:LANES * (blk + 1)].astype(F32)
        qn = q * lax.rsqrt(_head_mean_sq(q, j_bf) + NORM_EPS) * qg_ref[...]
        qt = (_rope(qn, c, s) * Q_SCALE).T.astype(BF16)
        lo, hi = qt[:HEAD_DIM], qt[HEAD_DIM:]
        if blk == 0:
            qt_ref[0] = jnp.concatenate([lo, zeros], axis=0)
            qt_ref[1] = jnp.concatenate([hi, zeros], axis=0)
        else:
            qt_ref[2] = jnp.concatenate([zeros, lo], axis=0)
            qt_ref[3] = jnp.concatenate([zeros, hi], axis=0)
    k = p_ref[:, A_K_OFF:A_K_OFF + LANES].astype(F32)
    kn = k * lax.rsqrt(_head_mean_sq(k, j_bf) + NORM_EPS) * kg_ref[...]
    k_ref[...] = _rope(kn, c, s).astype(BF16)
    v = p_ref[:, A_V_OFF:A_V_OFF + LANES].astype(F32)
    ones = jnp.ones((VT_ROWS - HEAD_DIM, ATT_KB), BF16)
    for t in range(tm // ATT_KB):
        vt = v[t * ATT_KB:(t + 1) * ATT_KB, :].T.astype(BF16)
        vt_ref[t] = jnp.concatenate([vt[:HEAD_DIM], ones, vt[HEAD_DIM:], ones], axis=0)


def _attn_prep(p3, cos_t, sin_t, qg, kg, j_bf):
    B, L, _ = p3.shape
    tm = min(1024, L)
    tab = pl.BlockSpec((tm, LANES), lambda b, i: (i, 0))
    vec = pl.BlockSpec((1, LANES), lambda b, i: (0, 0))
    return pl.pallas_call(
        _attn_prep_kernel,
        grid=(B, L // tm),
        in_specs=[pl.BlockSpec((None, tm, HY_OFF), lambda b, i: (b, i, 0)), tab, tab, vec, vec,
                  pl.BlockSpec((LANES, LANES), lambda b, i: (0, 0))],
        out_specs=[pl.BlockSpec((None, ATT_Q_HEADS, LANES, tm), lambda b, i: (b, 0, 0, i)),
                   pl.BlockSpec((None, tm, LANES), lambda b, i: (b, i, 0)),
                   pl.BlockSpec((None, tm // ATT_KB, ATT_KV_HEADS * VT_ROWS, ATT_KB),
                                lambda b, i: (b, i, 0, 0))],
        out_shape=[jax.ShapeDtypeStruct((B, ATT_Q_HEADS, LANES, L), BF16),
                   jax.ShapeDtypeStruct((B, L, LANES), BF16),
                   jax.ShapeDtypeStruct((B, L // ATT_KB, ATT_KV_HEADS * VT_ROWS, ATT_KB), BF16)],
        compiler_params=_cparams(("parallel", "parallel"), 32),
        name="attn_prep",
    )(p3, cos_t, sin_t, qg, kg, j_bf)


def _flash_kernel(qt_ref, k_ref, vt_ref, o_ref, sa_ref, sb_ref, *, ngrp, nsub):
    tq = qt_ref.shape[2]
    group = ATT_Q_HEADS // ATT_KV_HEADS

    def scores_into(s_ref, g):
        for t in range(nsub):
            st = pl.multiple_of((g * nsub + t) * ATT_KB, ATT_KB)
            kblk = k_ref[pl.ds(st, ATT_KB), :]
            for h in range(ATT_Q_HEADS):
                s_ref[t * ATT_Q_HEADS + h] = jnp.dot(kblk, qt_ref[h],
                                                     preferred_element_type=F32)

    def consume(s_ref, g, carry):
        for t in range(nsub):
            probs, stats = [], []
            for h in range(ATT_Q_HEADS):
                m, _ = carry[h]
                s = s_ref[t * ATT_Q_HEADS + h]
                m_new = jnp.maximum(m, jnp.max(s, axis=0, keepdims=True))
                stats.append((m_new, jnp.exp2(m - m_new)))
                probs.append(jnp.exp2((s - m_new).astype(BF16)))
            new = []
            for h in range(ATT_Q_HEADS):
                j = h // group
                vt = vt_ref[g * nsub + t, VT_ROWS * j:VT_ROWS * (j + 1), :]
                m_new, alpha = stats[h]
                acc = alpha * carry[h][1] + jnp.dot(vt, probs[h], preferred_element_type=F32)
                new.append((m_new, acc))
            carry = tuple(new)
        return carry

    def body(i, carry):
        g = 2 * i
        scores_into(sb_ref, g + 1)
        carry = consume(sa_ref, g, carry)
        scores_into(sa_ref, g + 2)
        return consume(sb_ref, g + 1, carry)

    carry = tuple((jnp.full((1, tq), -jnp.inf, F32), jnp.zeros((VT_ROWS, tq), F32))
                  for _ in range(ATT_Q_HEADS))
    scores_into(sa_ref, 0)
    if ngrp == 1:
        carry = consume(sa_ref, 0, carry)
    else:
        carry = lax.fori_loop(0, ngrp // 2 - 1, body, carry)
        scores_into(sb_ref, ngrp - 1)
        carry = consume(sa_ref, ngrp - 2, carry)
        carry = consume(sb_ref, ngrp - 1, carry)
    outs = [acc[:HEAD_DIM] / acc[HEAD_DIM:HEAD_DIM + 1] for _, acc in carry]
    o_ref[...] = jnp.concatenate(outs, axis=0).T.astype(o_ref.dtype)


def _flash(qt, k, vt):
    B, _, _, L = qt.shape
    tq = min(256, L)
    nsub = math.gcd(L // ATT_KB, 2)
    ngrp = L // (ATT_KB * nsub)
    assert ngrp == 1 or ngrp % 2 == 0
    s_scratch = pltpu.VMEM((nsub * ATT_Q_HEADS, ATT_KB, tq), F32)
    return pl.pallas_call(
        functools.partial(_flash_kernel, ngrp=ngrp, nsub=nsub),
        grid=(B, L // tq),
        in_specs=[pl.BlockSpec((None, ATT_Q_HEADS, LANES, tq), lambda b, i: (b, 0, 0, i)),
                  pl.BlockSpec((None, L, LANES), lambda b, i: (b, 0, 0)),
                  pl.BlockSpec((None, L // ATT_KB, ATT_KV_HEADS * VT_ROWS, ATT_KB),
                               lambda b, i: (b, 0, 0, 0))],
        out_specs=pl.BlockSpec((None, tq, ATT_Q_W), lambda b, i: (b, i, 0)),
        out_shape=jax.ShapeDtypeStruct((B, L, ATT_Q_W), BF16),
        scratch_shapes=[s_scratch, s_scratch],
        compiler_params=_cparams(("parallel", "arbitrary"), 48),
        name="flash_attn",
    )(qt, k, vt)


def _conv3_tile(main, prev_t, next_t, w, first, last):
    tb = main.shape[0]
    row = lax.broadcasted_iota(jnp.int32, (tb, 1), 0)
    before = jnp.where(first, 0.0, prev_t[HALO - 1:HALO, :])
    after = jnp.where(last, 0.0, next_t[0:1, :])
    up = jnp.where(row == 0, before, pltpu.roll(main, 1, 0))
    dn = jnp.where(row == tb - 1, after, pltpu.roll(main, tb - 1, 0))
    return up * w[0:1, :] + main * w[1:2, :] + dn * w[2:3, :]


def _halo_specs(tb, L, col_fn):
    nbh = L // HALO
    rh = tb // HALO
    main = pl.BlockSpec((None, tb, LANES), lambda b, w, s: (b, s, col_fn(w)))
    prev = pl.BlockSpec((None, HALO, LANES),
                        lambda b, w, s: (b, jnp.maximum(s * rh - 1, 0), col_fn(w)))
    nxt = pl.BlockSpec((None, HALO, LANES),
                       lambda b, w, s: (b, jnp.minimum((s + 1) * rh, nbh - 1), col_fn(w)))
    return [main, prev, nxt]


def _f32(ref):
    return ref[...].astype(F32)


def _hy_pro_kernel(x0_ref, x0p_ref, x0n_ref, x1_ref, x1p_ref, x1n_ref,
                   hv_ref, hvp_ref, hvn_ref, w0_ref, w1_ref, w2_ref,
                   b0_ref, b1_ref, b2_ref, z_ref, x0u_ref):
    first = pl.program_id(2) == 0
    last = pl.program_id(2) == pl.num_programs(2) - 1
    x0 = _conv3_tile(_f32(x0_ref), _f32(x0p_ref), _f32(x0n_ref), w0_ref[...], first, last) + b0_ref[...]
    x1 = _conv3_tile(_f32(x1_ref), _f32(x1p_ref), _f32(x1n_ref), w1_ref[...], first, last) + b1_ref[...]
    hv = _conv3_tile(_f32(hv_ref), _f32(hvp_ref), _f32(hvn_ref), w2_ref[...], first, last) + b2_ref[...]
    z_ref[...] = hv * x1
    x0u_ref[...] = x0


def _hy_prologue(p3, hcw, hcb):
    B, L, _ = p3.shape
    tb = min(1024, L)
    nw = HY_WIDTH // LANES
    in_specs = []
    for piece in range(3):
        in_specs += _halo_specs(tb, L, lambda w, piece=piece: HY_BLK + nw * piece + w)
    for piece in range(3):
        in_specs.append(pl.BlockSpec((3, LANES), lambda b, w, s, piece=piece: (0, nw * piece + w)))
    for piece in range(3):
        in_specs.append(pl.BlockSpec((1, LANES), lambda b, w, s, piece=piece: (0, nw * piece + w)))
    out_spec = pl.BlockSpec((None, tb, LANES), lambda b, w, s: (b, s, w))
    out_sh = jax.ShapeDtypeStruct((B, L, HY_WIDTH), F32)
    args = [p3] * 9 + [hcw] * 3 + [hcb] * 3
    return pl.pallas_call(
        _hy_pro_kernel,
        grid=(B, nw, L // tb),
        in_specs=in_specs,
        out_specs=[out_spec, out_spec],
        out_shape=[out_sh, out_sh],
        compiler_params=_cparams(("parallel", "parallel", "parallel"), 32),
        name="hyena_prologue",
    )(*args)


def _sconv_kernel(sb_ref, sc_ref, scp_ref, scn_ref, sh_ref, shp_ref, shn_ref, w_ref, o_ref):
    first = pl.program_id(2) == 0
    last = pl.program_id(2) == pl.num_programs(2) - 1
    y = _conv3_tile(_f32(sc_ref) * _f32(sh_ref), _f32(scp_ref) * _f32(shp_ref),
                    _f32(scn_ref) * _f32(shn_ref), w_ref[...], first, last)
    o_ref[...] = (_f32(sb_ref) * y).astype(o_ref.dtype)


def _sconv(p3, scw):
    B, L, _ = p3.shape
    tb = min(1024, L)
    nw = SC_WIDTH // LANES
    in_specs = [pl.BlockSpec((None, tb, LANES), lambda b, w, s: (b, s, SC_BLK + w))]
    in_specs += _halo_specs(tb, L, lambda w: SC_BLK + nw + w)
    in_specs += _halo_specs(tb, L, lambda w: SC_BLK + 2 * nw + w)
    in_specs.append(pl.BlockSpec((3, LANES), lambda b, w, s: (0, w)))
    return pl.pallas_call(
        _sconv_kernel,
        grid=(B, nw, L // tb),
        in_specs=in_specs,
        out_specs=pl.BlockSpec((None, tb, LANES), lambda b, w, s: (b, s, w)),
        out_shape=jax.ShapeDtypeStruct((B, L, SC_WIDTH), BF16),
        compiler_params=_cparams(("parallel", "parallel", "parallel"), 32),
        name="short_conv",
    )(*([p3] * 7 + [scw]))


def _hy_filter_kernel(z_ref, w1_ref, b1_ref, w2_ref, b2_ref, w3_ref, fr_ref, dl_ref,
                      k_ref, ss_ref, *, L):
    step = pl.program_id(0)
    tr = z_ref.shape[0]
    z = z_ref[...]
    h = jnp.sin(fr_ref[0:1, :] * (_mm3(z, w1_ref[...]) + b1_ref[...]))
    h = jnp.sin(fr_ref[1:2, :] * (_mm3(h, w2_ref[...]) + b2_ref[...]))
    h3 = _mm3(h, w3_ref[...])
    row = step * tr + lax.broadcasted_iota(jnp.int32, (tr, 1), 0)
    val = jnp.where(row < L, h3[:, :HY_WIDTH], h3[:, HY_WIDTH:])
    val = val * jnp.exp(-z[:, 0:1] * dl_ref[...])
    val = jnp.where(row == L, 0.0, val)
    k_ref[...] = val

    @pl.when(step == 0)
    def _():
        ss_ref[...] = jnp.zeros_like(ss_ref)

    ss_ref[...] += jnp.sum(val * val, axis=0, keepdims=True)


def _hy_filter_raw(z2, w1p, b1, w2, b2, w3, freq, deltas, L):
    n = 2 * L
    tr = min(512, n)
    H = HY_FILTER_HIDDEN
    full = lambda shape: pl.BlockSpec(shape, lambda i: (0, 0))
    return pl.pallas_call(
        functools.partial(_hy_filter_kernel, L=L),
        grid=(n // tr,),
        in_specs=[pl.BlockSpec((tr, H), lambda i: (i, 0)),
                  full((H, H)), full((1, H)), full((H, H)), full((1, H)),
                  full((H, 2 * HY_WIDTH)), full((2, H)), full((1, HY_WIDTH))],
        out_specs=[pl.BlockSpec((tr, HY_WIDTH), lambda i: (i, 0)), full((1, HY_WIDTH))],
        out_shape=[jax.ShapeDtypeStruct((n, HY_WIDTH), F32),
                   jax.ShapeDtypeStruct((1, HY_WIDTH), F32)],
        compiler_params=_cparams(("arbitrary",), 32),
        name="hyena_filter",
    )(z2, w1p, b1, w2, b2, w3, freq, deltas)


def _fft_dims(L):
    n = 2 * L
    n2 = 128 if n >= 32 * 128 else 64
    return n // n2, n2


def _fft_consts(n1, n2, r):
    n = n1 * n2
    a1 = 2.0 * np.pi * np.outer(np.arange(n1), np.arange(n1)) / n1
    c1, s1 = np.cos(a1), np.sin(a1)
    a2 = 2.0 * np.pi * np.outer(np.arange(n2), np.arange(n2)) / n2
    c2, s2 = np.cos(a2), np.sin(a2)
    f1 = np.concatenate([c1[:, :r], -s1[:, :r]], axis=0)
    f1_full = np.concatenate([c1, -s1], axis=0)
    m2 = np.block([[c2, s2], [-s2, c2]])
    m2i = np.block([[c2, -s2], [s2, c2]])
    fi = np.concatenate([c1[:r, :], -s1[:r, :]], axis=1)
    aw = 2.0 * np.pi * np.arange(n1) / n
    off = np.repeat(np.arange(FFT_CH), LANES)[None, :] * aw[:, None]
    step = np.broadcast_to((FFT_CH * aw)[:, None], (n1, LANES))
    f32 = lambda a: jnp.asarray(np.asarray(a, np.float32))
    return dict(f1=_np_split(f1), f1_full=_np_split(f1_full), m2=_np_split(m2),
                m2i=_np_split(m2i), fi=_np_split(fi),
                tw=(f32(np.cos(off)), f32(-np.sin(off)), f32(np.cos(step)), f32(-np.sin(step))))


def _cmul(a_re, a_im, b_re, b_im):
    return a_re * b_re - a_im * b_im, a_re * b_im + a_im * b_re


def _lane_tile(x, reps):
    return jnp.concatenate([x] * reps, axis=1)


def _lane_part(x, j):
    return x[:, LANES * j:LANES * (j + 1)]


def _chunk_twiddle(t_re, t_im, tw_refs):
    d_re, d_im, s_re, s_im = (ref[...] for ref in tw_refs)
    cur = _cmul(_lane_tile(t_re, FFT_CH), _lane_tile(t_im, FFT_CH), d_re, d_im)
    return cur, _cmul(t_re, t_im, s_re, s_im)


def _twiddle_init(n1):
    return jnp.ones((n1, LANES), F32), jnp.zeros((n1, LANES), F32)


def _fft_stage1(src_ref, a_ref, f_hi, f_lo, tw_refs, n1, n2, r, passes):
    def body(ci, carry):
        (t_re, t_im), nxt = _chunk_twiddle(*carry, tw_refs)
        i0 = ci * FFT_CH
        xs = jnp.concatenate([src_ref[pl.ds(i0 + j, r, stride=n2), :] for j in range(FFT_CH)],
                             axis=1)
        a = _mm_const(f_hi, f_lo, xs, passes)
        o_re, o_im = _cmul(a[:n1], a[n1:], t_re, t_im)
        for j in range(FFT_CH):
            base = pl.multiple_of((i0 + j) * 2 * n1, 2 * n1)
            a_ref[pl.ds(base, n1), :] = _lane_part(o_re, j)
            a_ref[pl.ds(base + n1, n1), :] = _lane_part(o_im, j)
        return nxt

    lax.fori_loop(0, n2 // FFT_CH, body, _twiddle_init(n1))


def _load_spectrum_rows(a_ref, k1, n1, n2):
    ld = lambda off: a_ref[pl.ds(off, n2, stride=2 * n1), :]
    return jnp.concatenate(
        [jnp.concatenate([ld(k1 + j) for j in range(FFT_CH)], axis=1),
         jnp.concatenate([ld(n1 + k1 + j) for j in range(FFT_CH)], axis=1)], axis=0)


def _filter_fft_kernel(k_ref, ss_ref, f_hi_ref, f_lo_ref, m_hi_ref, m_lo_ref,
                       dre_ref, dim_ref, sre_ref, sim_ref, kr_ref, ki_ref, a_ref, *, n1, n2):
    c = pl.program_id(1)

    @pl.when(c == 0)
    def _():
        _fft_stage1(k_ref, a_ref, f_hi_ref[...], f_lo_ref[...],
                    (dre_ref, dim_ref, sre_ref, sim_ref), n1, n2, n1, FFT_PASSES_FILTER)

    scale = _lane_tile(lax.rsqrt(ss_ref[...] + NORM_EPS) * (1.0 / (n1 * n2)), FFT_CH)
    xin = _load_spectrum_rows(a_ref, c * FFT_CH, n1, n2)
    x = _mm_const(m_hi_ref[...], m_lo_ref[...], xin, FFT_PASSES_FILTER)
    kr_ref[...] = x[:n2] * scale
    ki_ref[...] = x[n2:] * scale


def _filter_fft(k2raw, sumsq, cst, n1, n2):
    n = n1 * n2
    nw = HY_WIDTH // LANES
    consts = [*cst["f1_full"], *cst["m2"], *cst["tw"]]
    full = lambda a: pl.BlockSpec(a.shape, lambda w, c: (0,) * a.ndim)
    out_spec = pl.BlockSpec((None, None, n2, FFT_CH * LANES), lambda w, c: (w, c, 0, 0))
    out_sh = jax.ShapeDtypeStruct((nw, n1 // FFT_CH, n2, FFT_CH * LANES), F32)
    return pl.pallas_call(
        functools.partial(_filter_fft_kernel, n1=n1, n2=n2),
        grid=(nw, n1 // FFT_CH),
        in_specs=[pl.BlockSpec((n, LANES), lambda w, c: (0, w)),
                  pl.BlockSpec((1, LANES), lambda w, c: (0, w))] + [full(a) for a in consts],
        out_specs=[out_spec, out_spec],
        out_shape=[out_sh, out_sh],
        scratch_shapes=[pltpu.VMEM((2 * n, LANES), F32)],
        compiler_params=_cparams(("parallel", "arbitrary"), 48),
        name="hyena_filter_fft",
    )(k2raw, sumsq, *consts)


def _hy_conv_kernel(z_ref, x0_ref, hb_ref, kr_ref, ki_ref, f_hi_ref, f_lo_ref,
                    m_hi_ref, m_lo_ref, mi_hi_ref, mi_lo_ref, fi_hi_ref, fi_lo_ref,
                    dre_ref, dim_ref, sre_ref, sim_ref, o_ref, a_ref, *, n1, n2):
    c = pl.program_id(2)
    r = n1 // 2
    tw_refs = (dre_ref, dim_ref, sre_ref, sim_ref)

    @pl.when(c == 0)
    def _():
        _fft_stage1(z_ref, a_ref, f_hi_ref[...], f_lo_ref[...], tw_refs, n1, n2, r,
                    FFT_PASSES_DATA)

    k1 = c * FFT_CH
    x = _mm_const(m_hi_ref[...], m_lo_ref[...], _load_spectrum_rows(a_ref, k1, n1, n2),
                  FFT_PASSES_DATA)
    y_re, y_im = _cmul(x[:n2], x[n2:], kr_ref[...], ki_ref[...])
    bq = _mm_const(mi_hi_ref[...], mi_lo_ref[...], jnp.concatenate([y_re, y_im], axis=0),
                   FFT_PASSES_DATA)
    for j in range(FFT_CH):
        a_ref[pl.ds(k1 + j, n2, stride=2 * n1), :] = _lane_part(bq[:n2], j)
        a_ref[pl.ds(n1 + k1 + j, n2, stride=2 * n1), :] = _lane_part(bq[n2:], j)

    @pl.when(c == pl.num_programs(2) - 1)
    def _():
        hb = hb_ref[...]

        def body(ci, carry):
            (t_re, t_im), nxt = _chunk_twiddle(*carry, tw_refs)
            i0 = ci * FFT_CH
            rows = [pl.multiple_of((i0 + j) * 2 * n1, 2 * n1) for j in range(FFT_CH)]
            br = jnp.concatenate([a_ref[pl.ds(b0, n1), :] for b0 in rows], axis=1)
            bi = jnp.concatenate([a_ref[pl.ds(b0 + n1, n1), :] for b0 in rows], axis=1)
            bt = jnp.concatenate([br * t_re + bi * t_im, bi * t_re - br * t_im], axis=0)
            y = _mm_const(fi_hi_ref[...], fi_lo_ref[...], bt, FFT_PASSES_DATA)
            for j in range(FFT_CH):
                seq = pl.ds(i0 + j, r, stride=n2)
                o_ref[seq, :] = (x0_ref[seq, :] * (_lane_part(y, j) + z_ref[seq, :] * hb)
                                 ).astype(o_ref.dtype)
            return nxt

        lax.fori_loop(0, n2 // FFT_CH, body, _twiddle_init(n1))


def _hy_conv(z, x0u, hbias, kf_re, kf_im, cst, n1, n2):
    B, L, _ = z.shape
    nw = HY_WIDTH // LANES
    seq = pl.BlockSpec((None, L, LANES), lambda b, w, c: (b, 0, w))
    kf_spec = pl.BlockSpec((None, None, n2, FFT_CH * LANES), lambda b, w, c: (w, c, 0, 0))
    full = lambda a: pl.BlockSpec(a.shape, lambda b, w, c: (0,) * a.ndim)
    consts = [*cst["f1"], *cst["m2"], *cst["m2i"], *cst["fi"], *cst["tw"]]
    return pl.pallas_call(
        functools.partial(_hy_conv_kernel, n1=n1, n2=n2),
        grid=(B, nw, n1 // FFT_CH),
        in_specs=[seq, seq, pl.BlockSpec((1, LANES), lambda b, w, c: (0, w)), kf_spec, kf_spec]
                 + [full(a) for a in consts],
        out_specs=seq,
        out_shape=jax.ShapeDtypeStruct((B, L, HY_WIDTH), F32),
        scratch_shapes=[pltpu.VMEM((2 * n1 * n2, LANES), F32)],
        compiler_params=_cparams(("parallel", "parallel", "arbitrary"), 56),
        name="hyena_conv",
    )(z, x0u, hbias, kf_re, kf_im, *consts)


T_D0, T_D1, T_WQF, T_WQB, T_WKF, T_WKB, T_GCF, T_GCB, T_BD = range(9)


def _ret_tables_kernel(rl_ref, rh_ref, t_ref):
    C = RET_CHUNK
    i = lax.broadcasted_iota(jnp.int32, (C, LANES), 0).astype(F32)
    jn = lax.broadcasted_iota(jnp.int32, (C, LANES), 1).astype(F32)
    log_g = lambda e: jnp.log1p(-jnp.exp2(-e))
    diff = i - jn
    for hp in range(2):
        lgf = log_g(rh_ref[hp, 0:1, :])
        lgb = log_g(rh_ref[hp, 1:2, :])
        fwd = jnp.exp(jnp.where(diff >= 0, diff, 0.0) * lgf)
        bwd = jnp.exp(jnp.where(diff < 0, -diff, 0.0) * lgb)
        t_ref[T_D0 + hp] = jnp.where(diff >= 0, fwd, bwd)
    lf = log_g(rl_ref[0:1, :])
    lb = log_g(rl_ref[1:2, :])
    t_ref[T_WQF] = jnp.exp((i + 1.0) * lf)
    t_ref[T_WQB] = jnp.exp((C - i) * lb)
    t_ref[T_WKF] = jnp.exp((C - 1.0 - i) * lf)
    t_ref[T_WKB] = jnp.exp(i * lb)
    bd = jnp.where((i < HEAD_DIM) == (jn < HEAD_DIM), 1.0, 0.0)
    t_ref[T_GCF] = jnp.exp(C * lf) * bd
    t_ref[T_GCB] = jnp.exp(C * lb) * bd
    t_ref[T_BD] = bd


def _ret_tables(rde):
    per_lane = jnp.repeat(rde.reshape(2, 2, 2), HEAD_DIM, axis=2)
    per_lane = per_lane.transpose(1, 0, 2)
    per_head = jnp.broadcast_to(rde.reshape(2, 2, 2, 1), (2, 2, 2, LANES))
    per_head = per_head.transpose(1, 2, 0, 3)
    return pl.pallas_call(
        _ret_tables_kernel,
        grid=(2,),
        in_specs=[pl.BlockSpec((None, 2, LANES), lambda j: (j, 0, 0)),
                  pl.BlockSpec((None, 2, 2, LANES), lambda j: (j, 0, 0, 0))],
        out_specs=pl.BlockSpec((None, 9, RET_CHUNK, LANES), lambda j: (j, 0, 0, 0)),
        out_shape=jax.ShapeDtypeStruct((2, 9, RET_CHUNK, LANES), F32),
        compiler_params=_cparams(("parallel",), 32),
        name="retention_tables",
    )(per_lane, per_head)


def _ret_state_kernel(rk_ref, rv_ref, c_ref, s_ref, t_ref, st_ref, r_ref):
    C = RET_CHUNK
    nch = rk_ref.shape[0] // C

    @pl.when(pl.program_id(2) == 0)
    def _():
        r_ref[...] = jnp.zeros_like(r_ref)

    wkb = t_ref[T_WKB]
    gcb = t_ref[T_GCB]
    bd = t_ref[T_BD]
    state = r_ref[...]
    for n in range(nch - 1, -1, -1):
        st_ref[n] = state
        sl = slice(n * C, (n + 1) * C)
        k = _rope(rk_ref[sl, :].astype(F32), c_ref[sl, :], s_ref[sl, :]) * (HEAD_DIM ** -0.5)
        kw = (k * wkb).astype(BF16)
        v = rv_ref[sl, :]
        kv = lax.dot_general(kw, v, (((0,), (0,)), ((), ())), preferred_element_type=F32)
        state = gcb * state + bd * kv
    r_ref[...] = state


def _ret_main_kernel(rq_ref, rk_ref, rv_ref, rg_ref, c_ref, s_ref, t_ref, st_ref, j_ref,
                     o_ref, sf_ref):
    C = RET_CHUNK
    nch = rq_ref.shape[0] // C

    @pl.when(pl.program_id(2) == 0)
    def _():
        sf_ref[...] = jnp.zeros_like(sf_ref)

    lane = lax.broadcasted_iota(jnp.int32, (1, LANES), 1)
    lo64 = lane < HEAD_DIM
    j_bf = j_ref[...]
    state = sf_ref[...]
    for n in range(nch):
        sl = slice(n * C, (n + 1) * C)
        cs, sn = c_ref[sl, :], s_ref[sl, :]
        q = _rope(rq_ref[sl, :].astype(F32), cs, sn)
        k = _rope(rk_ref[sl, :].astype(F32), cs, sn) * (HEAD_DIM ** -0.5)
        vb = rv_ref[sl, :]
        qb = q.astype(BF16)
        kb = k.astype(BF16)
        zero = jnp.zeros_like(qb)
        parts = []
        for hp in range(2):
            qm = jnp.where(lo64, qb, zero) if hp == 0 else jnp.where(lo64, zero, qb)
            sc = lax.dot_general(qm, kb, (((1,), (1,)), ((), ())), preferred_element_type=F32)
            sd = (sc * t_ref[T_D0 + hp]).astype(BF16)
            parts.append(jnp.dot(sd, vb, preferred_element_type=F32))
        o = jnp.where(lo64, parts[0], parts[1])
        o = o + jnp.dot((q * t_ref[T_WQF]).astype(BF16), state.astype(BF16),
                        preferred_element_type=F32)
        o = o + jnp.dot((q * t_ref[T_WQB]).astype(BF16), st_ref[n].astype(BF16),
                        preferred_element_type=F32)
        kw = (k * t_ref[T_WKF]).astype(BF16)
        kv = lax.dot_general(kw, vb, (((0,), (0,)), ((), ())), preferred_element_type=F32)
        state = t_ref[T_GCF] * state + t_ref[T_BD] * kv
        ret = o * lax.rsqrt(_head_mean_sq(o, j_bf) + NORM_EPS)
        g = rg_ref[sl, :].astype(F32)
        o_ref[sl, :] = (ret * (g * _sigmoid(g))).astype(o_ref.dtype)
    sf_ref[...] = state


def _retention(p3, cos_t, sin_t, tables, j_bf):
    B, L, _ = p3.shape
    C = RET_CHUNK
    tb = min(1024, L)
    nblk = L // tb
    nch = tb // C
    nw = RET_W // LANES
    col = lambda piece: (lambda b, j, s: (b, s, RET_BLK + nw * piece + j))
    colr = lambda piece: (lambda b, j, s: (b, nblk - 1 - s, RET_BLK + nw * piece + j))
    seq = lambda fn: pl.BlockSpec((None, tb, LANES), fn)
    tab_spec = pl.BlockSpec((None, 9, C, LANES), lambda b, j, s: (j, 0, 0, 0))
    states = pl.pallas_call(
        _ret_state_kernel,
        grid=(B, nw, nblk),
        in_specs=[seq(colr(1)), seq(colr(2)),
                  pl.BlockSpec((tb, LANES), lambda b, j, s: (nblk - 1 - s, 0)),
                  pl.BlockSpec((tb, LANES), lambda b, j, s: (nblk - 1 - s, 0)),
                  tab_spec],
        out_specs=pl.BlockSpec((None, None, nch, LANES, LANES),
                               lambda b, j, s: (b, j, nblk - 1 - s, 0, 0)),
        out_shape=jax.ShapeDtypeStruct((B, nw, L // C, LANES, LANES), F32),
        scratch_shapes=[pltpu.VMEM((LANES, LANES), F32)],
        compiler_params=_cparams(("parallel", "parallel", "arbitrary"), 32),
        name="retention_state",
    )(p3, p3, cos_t, sin_t, tables)
    return pl.pallas_call(
        _ret_main_kernel,
        grid=(B, nw, nblk),
        in_specs=[seq(col(0)), seq(col(1)), seq(col(2)), seq(col(3)),
                  pl.BlockSpec((tb, LANES), lambda b, j, s: (s, 0)),
                  pl.BlockSpec((tb, LANES), lambda b, j, s: (s, 0)),
                  tab_spec,
                  pl.BlockSpec((None, None, nch, LANES, LANES), lambda b, j, s: (b, j, s, 0, 0)),
                  pl.BlockSpec((LANES, LANES), lambda b, j, s: (0, 0))],
        out_specs=pl.BlockSpec((None, tb, LANES), lambda b, j, s: (b, s, j)),
        out_shape=jax.ShapeDtypeStruct((B, L, RET_W), BF16),
        scratch_shapes=[pltpu.VMEM((LANES, LANES), F32)],
        compiler_params=_cparams(("parallel", "parallel", "arbitrary"), 32),
        name="retention_main",
    )(p3, p3, p3, p3, cos_t, sin_t, tables, states, j_bf)


def _merge_kernel(oa_ref, ob_ref, oc_ref, od_ref, g0_ref, g1_ref, g2_ref, g3_ref,
                  wb_ref, wo_ref, x_ref, gn_ref, h_ref):
    merged = None
    for n, (br, gr) in enumerate(((oa_ref, g0_ref), (ob_ref, g1_ref),
                                  (oc_ref, g2_ref), (od_ref, g3_ref))):
        gate = (0.5 * jnp.tanh(gr[...]) + 0.5).astype(F32)
        term = gate * jnp.dot(br[...].astype(BF16), wb_ref[n], preferred_element_type=F32)
        merged = term if merged is None else merged + term
    y = jnp.dot(merged.astype(BF16), wo_ref[...], preferred_element_type=F32)
    ms = jnp.mean(y * y, axis=-1, keepdims=True)
    h_ref[...] = x_ref[...] + y * lax.rsqrt(ms + NORM_EPS) * gn_ref[...]


def _merge(oa, ob, oc, od, p2, wb_bf, wo_bf, x2, gn):
    T = x2.shape[0]
    tm = min(1024, T)
    br = pl.BlockSpec((tm, BRANCH_W), lambda i: (i, 0))
    gate = lambda n: pl.BlockSpec((tm, D_MODEL), lambda i, n=n: (i, GATE_BLK + n))
    row = pl.BlockSpec((tm, D_MODEL), lambda i: (i, 0))
    return pl.pallas_call(
        _merge_kernel,
        grid=(T // tm,),
        in_specs=[br, br, br, br, gate(0), gate(1), gate(2), gate(3),
                  pl.BlockSpec((N_BRANCH, BRANCH_W, D_MODEL), lambda i: (0, 0, 0)),
                  pl.BlockSpec((D_MODEL, D_MODEL), lambda i: (0, 0)),
                  row, pl.BlockSpec((1, D_MODEL), lambda i: (0, 0))],
        out_specs=row,
        out_shape=jax.ShapeDtypeStruct((T, D_MODEL), F32),
        compiler_params=_cparams(("parallel",), 48),
        name="merge_out_proj",
    )(oa, ob, oc, od, p2, p2, p2, p2, wb_bf, wo_bf, x2, gn)


def _ffn_kernel(h_ref, g2_ref, wg_ref, wu_ref, wd_ref, g3_ref, o_ref):
    h = h_ref[...]
    ms = jnp.mean(h * h, axis=-1, keepdims=True)
    hn = (h * lax.rsqrt(ms + NORM_EPS) * g2_ref[...]).astype(BF16)
    f = None
    for lo, hi in _col_chunks(D_FF, 2):
        g = jnp.dot(hn, wg_ref[:, lo:hi], preferred_element_type=F32)
        u = jnp.dot(hn, wu_ref[:, lo:hi], preferred_element_type=F32)
        a = (g * _sigmoid(g) * u).astype(BF16)
        part = jnp.dot(a, wd_ref[lo:hi, :], preferred_element_type=F32)
        f = part if f is None else f + part
    ms = jnp.mean(f * f, axis=-1, keepdims=True)
    o_ref[...] = h + f * lax.rsqrt(ms + NORM_EPS) * g3_ref[...]


def _ffn(h2, g2, wg_bf, wu_bf, wd_bf, g3):
    T = h2.shape[0]
    tm = min(512, T)
    row = pl.BlockSpec((tm, D_MODEL), lambda i: (i, 0))
    vec = pl.BlockSpec((1, D_MODEL), lambda i: (0, 0))
    return pl.pallas_call(
        _ffn_kernel,
        grid=(T // tm,),
        in_specs=[row, vec,
                  pl.BlockSpec((D_MODEL, D_FF), lambda i: (0, 0)),
                  pl.BlockSpec((D_MODEL, D_FF), lambda i: (0, 0)),
                  pl.BlockSpec((D_FF, D_MODEL), lambda i: (0, 0)),
                  vec],
        out_specs=row,
        out_shape=jax.ShapeDtypeStruct((T, D_MODEL), F32),
        compiler_params=_cparams(("parallel",), 48),
        name="ffn",
    )(h2, g2, wg_bf, wu_bf, wd_bf, g3)


def _rope_tables(L):
    rows = L // GRID_W
    r = jnp.repeat(jnp.arange(rows, dtype=F32), GRID_W)
    c = jnp.tile(jnp.arange(GRID_W, dtype=F32), rows)
    inv = ROPE_BASE ** (-jnp.arange(ROPE_FREQS, dtype=F32) / ROPE_FREQS)
    ar = r[:, None] * inv
    ac = c[:, None] * inv
    cos64 = jnp.concatenate([jnp.cos(ar), jnp.cos(ar), jnp.cos(ac), jnp.cos(ac)], axis=1)
    sin64 = jnp.concatenate([-jnp.sin(ar), jnp.sin(ar), -jnp.sin(ac), jnp.sin(ac)], axis=1)
    return jnp.tile(cos64, (1, 2)), jnp.tile(sin64, (1, 2))


def _filter_features(L):
    t = jnp.linspace(0.0, 1.0, L, dtype=F32)[:, None]
    f = jnp.linspace(1e-4, HY_BANDS - 1, HY_BANDS, dtype=F32)
    ang = (2.0 * math.pi / L) * jnp.arange(L, dtype=F32)[:, None] * f[None, :]
    z = jnp.concatenate([t, jnp.cos(ang), -jnp.sin(ang)], axis=-1)
    z = jnp.pad(z, ((0, 0), (0, HY_FILTER_HIDDEN - HY_EMB)))
    idx = np.concatenate([np.arange(L), [0], np.arange(L - 1, 0, -1)])
    return z[idx]


def _head_mean_matrix():
    i = np.arange(LANES)
    j = ((i[:, None] // HEAD_DIM) == (i[None, :] // HEAD_DIM)).astype(np.float32) / HEAD_DIM
    return jnp.asarray(j.astype(BF16))


def _halve_gate_columns(w_in_bf):
    scale = jnp.where(jnp.arange(IN_COLS) >= GATE_OFF, 0.5, 1.0).astype(BF16)
    return w_in_bf * scale


def _trunk(x, wts, fft_dims=None):
    B, L, _ = x.shape
    T = B * L
    n1, n2 = fft_dims or _fft_dims(L)
    cst = _fft_consts(n1, n2, n1 // 2)
    cos_t, sin_t = _rope_tables(L)
    z2 = _filter_features(L)
    deltas = jnp.abs(jnp.linspace(math.log(HY_TARGET) / HY_SLOW_DECAY,
                                  math.log(HY_TARGET) / HY_FAST_DECAY, HY_WIDTH, dtype=F32))[None, :]
    j_bf = _head_mean_matrix()
    depth = wts["w_in"].shape[0]
    x2 = x.reshape(T, D_MODEL)
    for l in range(depth):
        ng = wts["norm_gains"][l]
        w1p = jnp.pad(wts["hy_w1"][l], ((0, HY_FILTER_HIDDEN - HY_EMB), (0, 0)))
        k2raw, sumsq = _hy_filter_raw(z2, w1p, wts["hy_b1"][l][None, :], wts["hy_w2"][l],
                                      wts["hy_b2"][l][None, :], wts["hy_w3"][l],
                                      wts["hy_freq"][l], deltas, L)
        kf_re, kf_im = _filter_fft(k2raw, sumsq, cst, n1, n2)

        p2 = _in_proj(x2, ng[0][None, :], wts["w_in_bf"][l])
        p3 = p2.reshape(B, L, IN_COLS)

        qg = jnp.tile(wts["qk_norm"][l, 0], 2)[None, :]
        kg = jnp.tile(wts["qk_norm"][l, 1], 2)[None, :]
        qt, k, vt = _attn_prep(p3, cos_t, sin_t, qg, kg, j_bf)
        out_a = _flash(qt, k, vt)

        z, x0u = _hy_prologue(p3, wts["hy_conv_w"][l], wts["hy_conv_b"][l][None, :])
        out_b = _hy_conv(z, x0u, wts["hy_bias"][l][None, :], kf_re, kf_im, cst, n1, n2)

        tables = _ret_tables(wts["ret_decay_exp"][l])
        out_c = _retention(p3, cos_t, sin_t, tables, j_bf)

        out_d = _sconv(p3, wts["sc_conv_w"][l])

        h2 = _merge(out_a.reshape(T, BRANCH_W), out_b.reshape(T, BRANCH_W),
                    out_c.reshape(T, BRANCH_W), out_d.reshape(T, BRANCH_W),
                    p2, wts["w_branch_bf"][l], wts["w_out_bf"][l], x2, ng[1][None, :])
        x2 = _ffn(h2, ng[2][None, :], wts["w_gate_bf"][l], wts["w_up_bf"][l],
                  wts["w_ffn_out_bf"][l], ng[3][None, :])
    return x2.reshape(B, L, D_MODEL)


def kernel(x_prompt, x_sample, norm_gains, w_in, qk_norm, hy_conv_w, hy_conv_b, hy_w1, hy_b1, hy_w2,
           hy_b2, hy_w3, hy_freq, hy_bias, ret_decay_exp, sc_conv_w, w_branch, w_out, w_ffn_in,
           w_ffn_out):
    wts = dict(norm_gains=norm_gains, w_in=w_in, qk_norm=qk_norm, hy_conv_w=hy_conv_w,
               hy_conv_b=hy_conv_b, hy_w1=hy_w1, hy_b1=hy_b1, hy_w2=hy_w2, hy_b2=hy_b2,
               hy_w3=hy_w3, hy_freq=hy_freq, hy_bias=hy_bias, ret_decay_exp=ret_decay_exp,
               sc_conv_w=sc_conv_w,
               w_in_bf=_halve_gate_columns(w_in.astype(BF16)), w_branch_bf=w_branch.astype(BF16),
               w_out_bf=w_out.astype(BF16), w_gate_bf=w_ffn_in[..., :D_FF].astype(BF16),
               w_up_bf=w_ffn_in[..., D_FF:].astype(BF16), w_ffn_out_bf=w_ffn_out.astype(BF16))
    return _trunk(x_prompt, wts), _trunk(x_sample, wts)
```

```python
import functools
import math

import numpy as np
import jax
import jax.numpy as jnp
from jax import lax
from jax.experimental import pallas as pl
from jax.experimental.pallas import tpu as pltpu

F32 = jnp.float32
BF16 = jnp.bfloat16

D_MODEL = 1024
GRID_W = 64
N_BRANCH = 4
BRANCH_W = 256
HEAD_DIM = 64
ATT_Q_HEADS = 4
ATT_KV_HEADS = 2
ROPE_BASE = 10000.0
ROPE_FREQS = HEAD_DIM // 4
HY_WIDTH = BRANCH_W
HY_EMB = 33
HY_BANDS = (HY_EMB - 1) // 2
HY_FILTER_HIDDEN = 64
HY_FAST_DECAY = 0.3
HY_SLOW_DECAY = 1.5
HY_TARGET = 1e-2
RET_HEADS = 4
RET_W = RET_HEADS * HEAD_DIM
RET_CHUNK = 128
SC_WIDTH = BRANCH_W
D_FF = 2816
NORM_EPS = 1e-6

ATT_Q_W = ATT_Q_HEADS * HEAD_DIM
ATT_KV_W = ATT_KV_HEADS * HEAD_DIM
A_K_OFF = ATT_Q_W
A_V_OFF = A_K_OFF + ATT_KV_W
HY_OFF = A_V_OFF + ATT_KV_W
RET_OFF = HY_OFF + 3 * HY_WIDTH
SC_OFF = RET_OFF + 4 * RET_W
GATE_OFF = SC_OFF + 3 * SC_WIDTH
IN_COLS = GATE_OFF + N_BRANCH * D_MODEL

LANES = 128
SUBLANES = 8
MXU_W = 256
HALO = 2 * SUBLANES
HY_BLK = HY_OFF // LANES
RET_BLK = RET_OFF // LANES
SC_BLK = SC_OFF // LANES
GATE_BLK = GATE_OFF // D_MODEL
ATT_KB = 512
VT_ROWS = HEAD_DIM + HALO
Q_SCALE = HEAD_DIM ** -0.5 * math.log2(math.e)
FFT_PASSES_FILTER = 3
FFT_PASSES_DATA = 1
FFT_CH = 8
FFT_SETS = 2
MIB = 1 << 20


def _cparams(sem, vmem_mib):
    return pltpu.CompilerParams(dimension_semantics=sem, vmem_limit_bytes=vmem_mib * MIB)


def _sigmoid(x):
    return 0.5 * jnp.tanh(0.5 * x) + 0.5


def _split(x):
    hi = x.astype(BF16)
    lo = (x - hi.astype(F32)).astype(BF16)
    return hi, lo


def _np_split(a64):
    a32 = np.asarray(a64, np.float32)
    hi = a32.astype(BF16)
    lo = (a32 - hi.astype(np.float32)).astype(BF16)
    return jnp.asarray(hi), jnp.asarray(lo)


def _mm_const(a_hi, a_lo, x, passes):
    if passes == 1:
        return jnp.dot(a_hi, x.astype(BF16), preferred_element_type=F32)
    xh, xl = _split(x)
    out = jnp.dot(a_hi, xh, preferred_element_type=F32)
    if passes >= 3:
        out = out + jnp.dot(a_lo, xh, preferred_element_type=F32)
        out = out + jnp.dot(a_hi, xl, preferred_element_type=F32)
    return out


def _mm3(a, b):
    ah, al = _split(a)
    bh, bl = _split(b)
    out = jnp.dot(ah, bh, preferred_element_type=F32)
    out = out + jnp.dot(al, bh, preferred_element_type=F32)
    return out + jnp.dot(ah, bl, preferred_element_type=F32)


def _head_mean_sq(x, j_bf):
    hi, lo = _split(x * x)
    return (jnp.dot(hi, j_bf, preferred_element_type=F32)
            + jnp.dot(lo, j_bf, preferred_element_type=F32))


def _rope(x, c, s):
    lane = lax.broadcasted_iota(jnp.int32, (1, LANES), 1)
    is_b = (lane & 16) != 0
    partner = jnp.where(is_b, pltpu.roll(x, 16, 1), pltpu.roll(x, LANES - 16, 1))
    return x * c + partner * s


def _col_chunks(n, parts):
    tiles = n // MXU_W
    bounds = [MXU_W * (tiles * i // parts) for i in range(parts + 1)]
    return list(zip(bounds[:-1], bounds[1:]))


def _in_proj_kernel(x_ref, g_ref, w_ref, o_ref):
    x = x_ref[...]
    ms = jnp.mean(x * x, axis=-1, keepdims=True)
    xn = (x * lax.rsqrt(ms + NORM_EPS) * g_ref[...]).astype(BF16)
    for lo, hi in _col_chunks(IN_COLS, 4):
        o_ref[:, lo:hi] = jnp.dot(xn, w_ref[:, lo:hi],
                                  preferred_element_type=F32).astype(o_ref.dtype)


def _in_proj(x2, g, w_bf):
    T = x2.shape[0]
    tm = min(512, T)
    return pl.pallas_call(
        _in_proj_kernel,
        grid=(T // tm,),
        in_specs=[pl.BlockSpec((tm, D_MODEL), lambda i: (i, 0)),
                  pl.BlockSpec((1, D_MODEL), lambda i: (0, 0)),
                  pl.BlockSpec((D_MODEL, IN_COLS), lambda i: (0, 0))],
        out_specs=pl.BlockSpec((tm, IN_COLS), lambda i: (i, 0)),
        out_shape=jax.ShapeDtypeStruct((T, IN_COLS), BF16),
        compiler_params=_cparams(("parallel",), 48),
        name="in_proj",
    )(x2, g, w_bf)


def _attn_prep_kernel(p_ref, c_ref, s_ref, qg_ref, kg_ref, j_ref, qt_ref, k_ref, vt_ref):
    c = c_ref[...]
    s = s_ref[...]
    j_bf = j_ref[...]
    tm = p_ref.shape[0]
    zeros = jnp.zeros((HEAD_DIM, tm), BF16)
    for blk in range(2):
        q = p_ref[:, LANES * blk:LANES * (blk + 1)].astype(F32)
        qn = q * lax.rsqrt(_head_mean_sq(q, j_bf) + NORM_EPS) * qg_ref[...]
        qt = (_rope(qn, c, s) * Q_SCALE).T.astype(BF16)
        lo, hi = qt[:HEAD_DIM], qt[HEAD_DIM:]
        if blk == 0:
            qt_ref[0] = jnp.concatenate([lo, zeros], axis=0)
            qt_ref[1] = jnp.concatenate([hi, zeros], axis=0)
        else:
            qt_ref[2] = jnp.concatenate([zeros, lo], axis=0)
            qt_ref[3] = jnp.concatenate([zeros, hi], axis=0)
    k = p_ref[:, A_K_OFF:A_K_OFF + LANES].astype(F32)
    kn = k * lax.rsqrt(_head_mean_sq(k, j_bf) + NORM_EPS) * kg_ref[...]
    k_ref[...] = _rope(kn, c, s).astype(BF16)
    v = p_ref[:, A_V_OFF:A_V_OFF + LANES].astype(F32)
    ones = jnp.ones((VT_ROWS - HEAD_DIM, ATT_KB), BF16)
    for t in range(tm // ATT_KB):
        vt = v[t * ATT_KB:(t + 1) * ATT_KB, :].T.astype(BF16)
        vt_ref[t] = jnp.concatenate([vt[:HEAD_DIM], ones, vt[HEAD_DIM:], ones], axis=0)


def _attn_prep(p3, cos_t, sin_t, qg, kg, j_bf):
    B, L, _ = p3.shape
    tm = min(1024, L)
    tab = pl.BlockSpec((tm, LANES), lambda b, i: (i, 0))
    vec = pl.BlockSpec((1, LANES), lambda b, i: (0, 0))
    return pl.pallas_call(
        _attn_prep_kernel,
        grid=(B, L // tm),
        in_specs=[pl.BlockSpec((None, tm, HY_OFF), lambda b, i: (b, i, 0)), tab, tab, vec, vec,
                  pl.BlockSpec((LANES, LANES), lambda b, i: (0, 0))],
        out_specs=[pl.BlockSpec((None, ATT_Q_HEADS, LANES, tm), lambda b, i: (b, 0, 0, i)),
                   pl.BlockSpec((None, tm, LANES), lambda b, i: (b, i, 0)),
                   pl.BlockSpec((None, tm // ATT_KB, ATT_KV_HEADS * VT_ROWS, ATT_KB),
                                lambda b, i: (b, i, 0, 0))],
        out_shape=[jax.ShapeDtypeStruct((B, ATT_Q_HEADS, LANES, L), BF16),
                   jax.ShapeDtypeStruct((B, L, LANES), BF16),
                   jax.ShapeDtypeStruct((B, L // ATT_KB, ATT_KV_HEADS * VT_ROWS, ATT_KB), BF16)],
        compiler_params=_cparams(("parallel", "parallel"), 32),
        name="attn_prep",
    )(p3, cos_t, sin_t, qg, kg, j_bf)


def _flash_kernel(qt_ref, k_ref, vt_ref, o_ref, sa_ref, ma_ref, sb_ref, mb_ref, *, ngrp, nsub):
    tq = qt_ref.shape[2]
    group = ATT_Q_HEADS // ATT_KV_HEADS

    def scores_into(slot, g):
        s_ref, mx_ref = slot
        for t in range(nsub):
            st = pl.multiple_of((g * nsub + t) * ATT_KB, ATT_KB)
            kblk = k_ref[pl.ds(st, ATT_KB), :]
            for h in range(ATT_Q_HEADS):
                s = jnp.dot(kblk, qt_ref[h], preferred_element_type=F32)
                s_ref[t * ATT_Q_HEADS + h] = s
                mx_ref[t * ATT_Q_HEADS + h] = jnp.max(s, axis=0, keepdims=True)

    def consume(slot, g, carry):
        s_ref, mx_ref = slot
        for t in range(nsub):
            probs, stats = [], []
            for h in range(ATT_Q_HEADS):
                m, _ = carry[h]
                s = s_ref[t * ATT_Q_HEADS + h]
                m_new = jnp.maximum(m, mx_ref[t * ATT_Q_HEADS + h])
                stats.append((m_new, jnp.exp2(m - m_new)))
                probs.append(jnp.exp2((s - m_new).astype(BF16)))
            new = []
            for h in range(ATT_Q_HEADS):
                j = h // group
                vt = vt_ref[g * nsub + t, VT_ROWS * j:VT_ROWS * (j + 1), :]
                m_new, alpha = stats[h]
                acc = alpha * carry[h][1] + jnp.dot(vt, probs[h], preferred_element_type=F32)
                new.append((m_new, acc))
            carry = tuple(new)
        return carry

    slot_a = (sa_ref, ma_ref)
    slot_b = (sb_ref, mb_ref)

    def body(i, carry):
        g = 2 * i
        scores_into(slot_b, g + 1)
        carry = consume(slot_a, g, carry)
        scores_into(slot_a, g + 2)
        return consume(slot_b, g + 1, carry)

    carry = tuple((jnp.full((1, tq), -jnp.inf, F32), jnp.zeros((VT_ROWS, tq), F32))
                  for _ in range(ATT_Q_HEADS))
    scores_into(slot_a, 0)
    if ngrp == 1:
        carry = consume(slot_a, 0, carry)
    else:
        carry = lax.fori_loop(0, ngrp // 2 - 1, body, carry)
        scores_into(slot_b, ngrp - 1)
        carry = consume(slot_a, ngrp - 2, carry)
        carry = consume(slot_b, ngrp - 1, carry)
    outs = [acc[:HEAD_DIM] / acc[HEAD_DIM:HEAD_DIM + 1] for _, acc in carry]
    o_ref[...] = jnp.concatenate(outs, axis=0).T.astype(o_ref.dtype)


def _flash(qt, k, vt):
    B, _, _, L = qt.shape
    tq = min(256, L)
    nsub = math.gcd(L // ATT_KB, 2)
    ngrp = L // (ATT_KB * nsub)
    assert ngrp == 1 or ngrp % 2 == 0
    s_scratch = pltpu.VMEM((nsub * ATT_Q_HEADS, ATT_KB, tq), F32)
    m_scratch = pltpu.VMEM((nsub * ATT_Q_HEADS, 1, tq), F32)
    return pl.pallas_call(
        functools.partial(_flash_kernel, ngrp=ngrp, nsub=nsub),
        grid=(B, L // tq),
        in_specs=[pl.BlockSpec((None, ATT_Q_HEADS, LANES, tq), lambda b, i: (b, 0, 0, i)),
                  pl.BlockSpec((None, L, LANES), lambda b, i: (b, 0, 0)),
                  pl.BlockSpec((None, L // ATT_KB, ATT_KV_HEADS * VT_ROWS, ATT_KB),
                               lambda b, i: (b, 0, 0, 0))],
        out_specs=pl.BlockSpec((None, tq, ATT_Q_W), lambda b, i: (b, i, 0)),
        out_shape=jax.ShapeDtypeStruct((B, L, ATT_Q_W), BF16),
        scratch_shapes=[s_scratch, m_scratch, s_scratch, m_scratch],
        compiler_params=_cparams(("parallel", "arbitrary"), 48),
        name="flash_attn",
    )(qt, k, vt)


def _conv3_tile(main, prev_t, next_t, w, first, last):
    tb = main.shape[0]
    row = lax.broadcasted_iota(jnp.int32, (tb, 1), 0)
    before = jnp.where(first, 0.0, prev_t[HALO - 1:HALO, :])
    after = jnp.where(last, 0.0, next_t[0:1, :])
    up = jnp.where(row == 0, before, pltpu.roll(main, 1, 0))
    dn = jnp.where(row == tb - 1, after, pltpu.roll(main, tb - 1, 0))
    return up * w[0:1, :] + main * w[1:2, :] + dn * w[2:3, :]


def _halo_specs(tb, L, col_fn):
    nbh = L // HALO
    rh = tb // HALO
    main = pl.BlockSpec((None, tb, LANES), lambda b, w, s: (b, s, col_fn(w)))
    prev = pl.BlockSpec((None, HALO, LANES),
                        lambda b, w, s: (b, jnp.maximum(s * rh - 1, 0), col_fn(w)))
    nxt = pl.BlockSpec((None, HALO, LANES),
                       lambda b, w, s: (b, jnp.minimum((s + 1) * rh, nbh - 1), col_fn(w)))
    return [main, prev, nxt]


def _f32(ref):
    return ref[...].astype(F32)


def _hy_pro_kernel(x0_ref, x0p_ref, x0n_ref, x1_ref, x1p_ref, x1n_ref,
                   hv_ref, hvp_ref, hvn_ref, w0_ref, w1_ref, w2_ref,
                   b0_ref, b1_ref, b2_ref, z_ref, x0u_ref):
    first = pl.program_id(2) == 0
    last = pl.program_id(2) == pl.num_programs(2) - 1
    x0 = _conv3_tile(_f32(x0_ref), _f32(x0p_ref), _f32(x0n_ref), w0_ref[...], first, last) + b0_ref[...]
    x1 = _conv3_tile(_f32(x1_ref), _f32(x1p_ref), _f32(x1n_ref), w1_ref[...], first, last) + b1_ref[...]
    hv = _conv3_tile(_f32(hv_ref), _f32(hvp_ref), _f32(hvn_ref), w2_ref[...], first, last) + b2_ref[...]
    z_ref[...] = hv * x1
    x0u_ref[...] = x0


def _hy_prologue(p3, hcw, hcb):
    B, L, _ = p3.shape
    tb = min(1024, L)
    nw = HY_WIDTH // LANES
    in_specs = []
    for piece in range(3):
        in_specs += _halo_specs(tb, L, lambda w, piece=piece: HY_BLK + nw * piece + w)
    for piece in range(3):
        in_specs.append(pl.BlockSpec((3, LANES), lambda b, w, s, piece=piece: (0, nw * piece + w)))
    for piece in range(3):
        in_specs.append(pl.BlockSpec((1, LANES), lambda b, w, s, piece=piece: (0, nw * piece + w)))
    out_spec = pl.BlockSpec((None, tb, LANES), lambda b, w, s: (b, s, w))
    out_sh = jax.ShapeDtypeStruct((B, L, HY_WIDTH), F32)
    args = [p3] * 9 + [hcw] * 3 + [hcb] * 3
    return pl.pallas_call(
        _hy_pro_kernel,
        grid=(B, nw, L // tb),
        in_specs=in_specs,
        out_specs=[out_spec, out_spec],
        out_shape=[out_sh, out_sh],
        compiler_params=_cparams(("parallel", "parallel", "parallel"), 32),
        name="hyena_prologue",
    )(*args)


def _sconv_kernel(sb_ref, sc_ref, scp_ref, scn_ref, sh_ref, shp_ref, shn_ref, w_ref, o_ref):
    first = pl.program_id(2) == 0
    last = pl.program_id(2) == pl.num_programs(2) - 1
    y = _conv3_tile(_f32(sc_ref) * _f32(sh_ref), _f32(scp_ref) * _f32(shp_ref),
                    _f32(scn_ref) * _f32(shn_ref), w_ref[...], first, last)
    o_ref[...] = (_f32(sb_ref) * y).astype(o_ref.dtype)


def _sconv(p3, scw):
    B, L, _ = p3.shape
    tb = min(1024, L)
    nw = SC_WIDTH // LANES
    in_specs = [pl.BlockSpec((None, tb, LANES), lambda b, w, s: (b, s, SC_BLK + w))]
    in_specs += _halo_specs(tb, L, lambda w: SC_BLK + nw + w)
    in_specs += _halo_specs(tb, L, lambda w: SC_BLK + 2 * nw + w)
    in_specs.append(pl.BlockSpec((3, LANES), lambda b, w, s: (0, w)))
    return pl.pallas_call(
        _sconv_kernel,
        grid=(B, nw, L // tb),
        in_specs=in_specs,
        out_specs=pl.BlockSpec((None, tb, LANES), lambda b, w, s: (b, s, w)),
        out_shape=jax.ShapeDtypeStruct((B, L, SC_WIDTH), BF16),
        compiler_params=_cparams(("parallel", "parallel", "parallel"), 32),
        name="short_conv",
    )(*([p3] * 7 + [scw]))


def _hy_filter_kernel(z_ref, w1_ref, b1_ref, w2_ref, b2_ref, w3_ref, fr_ref, dl_ref,
                      k_ref, ss_ref, *, L):
    step = pl.program_id(0)
    tr = z_ref.shape[0]
    z = z_ref[...]
    h = jnp.sin(fr_ref[0:1, :] * (_mm3(z, w1_ref[...]) + b1_ref[...]))
    h = jnp.sin(fr_ref[1:2, :] * (_mm3(h, w2_ref[...]) + b2_ref[...]))
    h3 = _mm3(h, w3_ref[...])
    row = step * tr + lax.broadcasted_iota(jnp.int32, (tr, 1), 0)
    val = jnp.where(row < L, h3[:, :HY_WIDTH], h3[:, HY_WIDTH:])
    val = val * jnp.exp(-z[:, 0:1] * dl_ref[...])
    val = jnp.where(row == L, 0.0, val)
    k_ref[...] = val

    @pl.when(step == 0)
    def _():
        ss_ref[...] = jnp.zeros_like(ss_ref)

    ss_ref[...] += jnp.sum(val * val, axis=0, keepdims=True)


def _hy_filter_raw(z2, w1p, b1, w2, b2, w3, freq, deltas, L):
    n = 2 * L
    tr = min(512, n)
    H = HY_FILTER_HIDDEN
    full = lambda shape: pl.BlockSpec(shape, lambda i: (0, 0))
    return pl.pallas_call(
        functools.partial(_hy_filter_kernel, L=L),
        grid=(n // tr,),
        in_specs=[pl.BlockSpec((tr, H), lambda i: (i, 0)),
                  full((H, H)), full((1, H)), full((H, H)), full((1, H)),
                  full((H, 2 * HY_WIDTH)), full((2, H)), full((1, HY_WIDTH))],
        out_specs=[pl.BlockSpec((tr, HY_WIDTH), lambda i: (i, 0)), full((1, HY_WIDTH))],
        out_shape=[jax.ShapeDtypeStruct((n, HY_WIDTH), F32),
                   jax.ShapeDtypeStruct((1, HY_WIDTH), F32)],
        compiler_params=_cparams(("arbitrary",), 32),
        name="hyena_filter",
    )(z2, w1p, b1, w2, b2, w3, freq, deltas)


def _fft_dims(L):
    n = 2 * L
    n2 = 128 if n >= 32 * 128 else 64
    return n // n2, n2


def _fft_consts(n1, n2, r):
    n = n1 * n2
    a1 = 2.0 * np.pi * np.outer(np.arange(n1), np.arange(n1)) / n1
    c1, s1 = np.cos(a1), np.sin(a1)
    a2 = 2.0 * np.pi * np.outer(np.arange(n2), np.arange(n2)) / n2
    c2, s2 = np.cos(a2), np.sin(a2)
    f1 = np.concatenate([c1[:, :r], -s1[:, :r]], axis=0)
    f1_full = np.concatenate([c1, -s1], axis=0)
    m2 = np.block([[c2, s2], [-s2, c2]])
    m2i = np.block([[c2, -s2], [s2, c2]])
    fi = np.concatenate([c1[:r, :], -s1[:r, :]], axis=1)
    aw = 2.0 * np.pi * np.arange(n1) / n
    off = np.repeat(np.arange(FFT_CH), LANES)[None, :] * aw[:, None]
    step = np.broadcast_to((FFT_CH * aw)[:, None], (n1, LANES))
    f32 = lambda a: jnp.asarray(np.asarray(a, np.float32))
    return dict(f1=_np_split(f1), f1_full=_np_split(f1_full), m2=_np_split(m2),
                m2i=_np_split(m2i), fi=_np_split(fi),
                tw=(f32(np.cos(off)), f32(-np.sin(off)), f32(np.cos(step)), f32(-np.sin(step))))


def _cmul(a_re, a_im, b_re, b_im):
    return a_re * b_re - a_im * b_im, a_re * b_im + a_im * b_re


def _lane_tile(x, reps):
    return jnp.concatenate([x] * reps, axis=1)


def _lane_part(x, j):
    return x[:, LANES * j:LANES * (j + 1)]


def _chunk_twiddle(t_re, t_im, tw_refs):
    d_re, d_im, s_re, s_im = (ref[...] for ref in tw_refs)
    cur = _cmul(_lane_tile(t_re, FFT_CH), _lane_tile(t_im, FFT_CH), d_re, d_im)
    return cur, _cmul(t_re, t_im, s_re, s_im)


def _twiddle_init(n1):
    return jnp.ones((n1, LANES), F32), jnp.zeros((n1, LANES), F32)


def _fft_stage1(src_ref, a_ref, f_hi, f_lo, tw_refs, n1, n2, r, passes):
    def body(ci, carry):
        i0s, tws = [], []
        for u in range(FFT_SETS):
            tw, carry = _chunk_twiddle(*carry, tw_refs)
            tws.append(tw)
            i0s.append((ci * FFT_SETS + u) * FFT_CH)
        prods = [_mm_const(f_hi, f_lo, jnp.concatenate(
            [src_ref[pl.ds(i0 + j, r, stride=n2), :] for j in range(FFT_CH)], axis=1), passes)
            for i0 in i0s]
        for i0, (t_re, t_im), a in zip(i0s, tws, prods):
            o_re, o_im = _cmul(a[:n1], a[n1:], t_re, t_im)
            for j in range(FFT_CH):
                base = pl.multiple_of((i0 + j) * 2 * n1, 2 * n1)
                a_ref[pl.ds(base, n1), :] = _lane_part(o_re, j)
                a_ref[pl.ds(base + n1, n1), :] = _lane_part(o_im, j)
        return carry

    lax.fori_loop(0, n2 // (FFT_CH * FFT_SETS), body, _twiddle_init(n1))


def _load_spectrum_rows(a_ref, k1, n1, n2):
    ld = lambda off: a_ref[pl.ds(off, n2, stride=2 * n1), :]
    return jnp.concatenate(
        [jnp.concatenate([ld(k1 + j) for j in range(FFT_CH)], axis=1),
         jnp.concatenate([ld(n1 + k1 + j) for j in range(FFT_CH)], axis=1)], axis=0)


def _filter_fft_kernel(k_ref, ss_ref, f_hi_ref, f_lo_ref, m_hi_ref, m_lo_ref,
                       dre_ref, dim_ref, sre_ref, sim_ref, kr_ref, ki_ref, a_ref, *, n1, n2):
    c = pl.program_id(1)

    @pl.when(c == 0)
    def _():
        _fft_stage1(k_ref, a_ref, f_hi_ref[...], f_lo_ref[...],
                    (dre_ref, dim_ref, sre_ref, sim_ref), n1, n2, n1, FFT_PASSES_FILTER)

    scale = _lane_tile(lax.rsqrt(ss_ref[...] + NORM_EPS) * (1.0 / (n1 * n2)), FFT_CH)
    xin = _load_spectrum_rows(a_ref, c * FFT_CH, n1, n2)
    x = _mm_const(m_hi_ref[...], m_lo_ref[...], xin, FFT_PASSES_FILTER)
    kr_ref[...] = x[:n2] * scale
    ki_ref[...] = x[n2:] * scale


def _filter_fft(k2raw, sumsq, cst, n1, n2):
    n = n1 * n2
    nw = HY_WIDTH // LANES
    consts = [*cst["f1_full"], *cst["m2"], *cst["tw"]]
    full = lambda a: pl.BlockSpec(a.shape, lambda w, c: (0,) * a.ndim)
    out_spec = pl.BlockSpec((None, None, n2, FFT_CH * LANES), lambda w, c: (w, c, 0, 0))
    out_sh = jax.ShapeDtypeStruct((nw, n1 // FFT_CH, n2, FFT_CH * LANES), F32)
    return pl.pallas_call(
        functools.partial(_filter_fft_kernel, n1=n1, n2=n2),
        grid=(nw, n1 // FFT_CH),
        in_specs=[pl.BlockSpec((n, LANES), lambda w, c: (0, w)),
                  pl.BlockSpec((1, LANES), lambda w, c: (0, w))] + [full(a) for a in consts],
        out_specs=[out_spec, out_spec],
        out_shape=[out_sh, out_sh],
        scratch_shapes=[pltpu.VMEM((2 * n, LANES), F32)],
        compiler_params=_cparams(("parallel", "arbitrary"), 48),
        name="hyena_filter_fft",
    )(k2raw, sumsq, *consts)


def _hy_conv_kernel(z_ref, x0_ref, hb_ref, kr_ref, ki_ref, f_hi_ref, f_lo_ref,
                    m_hi_ref, m_lo_ref, mi_hi_ref, mi_lo_ref, fi_hi_ref, fi_lo_ref,
                    dre_ref, dim_ref, sre_ref, sim_ref, o_ref, a_ref, *, n1, n2):
    c = pl.program_id(2)
    r = n1 // 2
    tw_refs = (dre_ref, dim_ref, sre_ref, sim_ref)

    @pl.when(c == 0)
    def _():
        _fft_stage1(z_ref, a_ref, f_hi_ref[...], f_lo_ref[...], tw_refs, n1, n2, r,
                    FFT_PASSES_DATA)

    k1s = [(c * FFT_SETS + u) * FFT_CH for u in range(FFT_SETS)]
    xs = [_mm_const(m_hi_ref[...], m_lo_ref[...], _load_spectrum_rows(a_ref, k1, n1, n2),
                    FFT_PASSES_DATA) for k1 in k1s]
    ys = [jnp.concatenate(_cmul(x[:n2], x[n2:], kr_ref[u], ki_ref[u]), axis=0)
          for u, x in enumerate(xs)]
    bqs = [_mm_const(mi_hi_ref[...], mi_lo_ref[...], y, FFT_PASSES_DATA) for y in ys]
    for k1, bq in zip(k1s, bqs):
        for j in range(FFT_CH):
            a_ref[pl.ds(k1 + j, n2, stride=2 * n1), :] = _lane_part(bq[:n2], j)
            a_ref[pl.ds(n1 + k1 + j, n2, stride=2 * n1), :] = _lane_part(bq[n2:], j)

    @pl.when(c == pl.num_programs(2) - 1)
    def _():
        hb = hb_ref[...]

        def body(ci, carry):
            i0s, bts = [], []
            for u in range(FFT_SETS):
                (t_re, t_im), carry = _chunk_twiddle(*carry, tw_refs)
                i0 = (ci * FFT_SETS + u) * FFT_CH
                rows = [pl.multiple_of((i0 + j) * 2 * n1, 2 * n1) for j in range(FFT_CH)]
                br = jnp.concatenate([a_ref[pl.ds(b0, n1), :] for b0 in rows], axis=1)
                bi = jnp.concatenate([a_ref[pl.ds(b0 + n1, n1), :] for b0 in rows], axis=1)
                bts.append(jnp.concatenate([br * t_re + bi * t_im, bi * t_re - br * t_im],
                                           axis=0))
                i0s.append(i0)
            ys = [_mm_const(fi_hi_ref[...], fi_lo_ref[...], bt, FFT_PASSES_DATA) for bt in bts]
            for i0, y in zip(i0s, ys):
                for j in range(FFT_CH):
                    seq = pl.ds(i0 + j, r, stride=n2)
                    o_ref[seq, :] = (x0_ref[seq, :] * (_lane_part(y, j) + z_ref[seq, :] * hb)
                                     ).astype(o_ref.dtype)
            return carry

        lax.fori_loop(0, n2 // (FFT_CH * FFT_SETS), body, _twiddle_init(n1))


def _hy_conv(z, x0u, hbias, kf_re, kf_im, cst, n1, n2):
    B, L, _ = z.shape
    nw = HY_WIDTH // LANES
    seq = pl.BlockSpec((None, L, LANES), lambda b, w, c: (b, 0, w))
    kf_spec = pl.BlockSpec((None, FFT_SETS, n2, FFT_CH * LANES), lambda b, w, c: (w, c, 0, 0))
    full = lambda a: pl.BlockSpec(a.shape, lambda b, w, c: (0,) * a.ndim)
    consts = [*cst["f1"], *cst["m2"], *cst["m2i"], *cst["fi"], *cst["tw"]]
    return pl.pallas_call(
        functools.partial(_hy_conv_kernel, n1=n1, n2=n2),
        grid=(B, nw, n1 // (FFT_CH * FFT_SETS)),
        in_specs=[seq, seq, pl.BlockSpec((1, LANES), lambda b, w, c: (0, w)), kf_spec, kf_spec]
                 + [full(a) for a in consts],
        out_specs=seq,
        out_shape=jax.ShapeDtypeStruct((B, L, HY_WIDTH), F32),
        scratch_shapes=[pltpu.VMEM((2 * n1 * n2, LANES), F32)],
        compiler_params=_cparams(("parallel", "parallel", "arbitrary"), 56),
        name="hyena_conv",
    )(z, x0u, hbias, kf_re, kf_im, *consts)


T_D0, T_D1, T_WQF, T_WQB, T_WKF, T_WKB, T_GCF, T_GCB, T_BD = range(9)


def _ret_tables_kernel(rl_ref, rh_ref, t_ref):
    C = RET_CHUNK
    i = lax.broadcasted_iota(jnp.int32, (C, LANES), 0).astype(F32)
    jn = lax.broadcasted_iota(jnp.int32, (C, LANES), 1).astype(F32)
    log_g = lambda e: jnp.log1p(-jnp.exp2(-e))
    diff = i - jn
    for hp in range(2):
        lgf = log_g(rh_ref[hp, 0:1, :])
        lgb = log_g(rh_ref[hp, 1:2, :])
        fwd = jnp.exp(jnp.where(diff >= 0, diff, 0.0) * lgf)
        bwd = jnp.exp(jnp.where(diff < 0, -diff, 0.0) * lgb)
        t_ref[T_D0 + hp] = jnp.where(diff >= 0, fwd, bwd)
    lf = log_g(rl_ref[0:1, :])
    lb = log_g(rl_ref[1:2, :])
    t_ref[T_WQF] = jnp.exp((i + 1.0) * lf)
    t_ref[T_WQB] = jnp.exp((C - i) * lb)
    t_ref[T_WKF] = jnp.exp((C - 1.0 - i) * lf)
    t_ref[T_WKB] = jnp.exp(i * lb)
    bd = jnp.where((i < HEAD_DIM) == (jn < HEAD_DIM), 1.0, 0.0)
    t_ref[T_GCF] = jnp.exp(C * lf) * bd
    t_ref[T_GCB] = jnp.exp(C * lb) * bd
    t_ref[T_BD] = bd


def _ret_tables(rde):
    per_lane = jnp.repeat(rde.reshape(2, 2, 2), HEAD_DIM, axis=2)
    per_lane = per_lane.transpose(1, 0, 2)
    per_head = jnp.broadcast_to(rde.reshape(2, 2, 2, 1), (2, 2, 2, LANES))
    per_head = per_head.transpose(1, 2, 0, 3)
    return pl.pallas_call(
        _ret_tables_kernel,
        grid=(2,),
        in_specs=[pl.BlockSpec((None, 2, LANES), lambda j: (j, 0, 0)),
                  pl.BlockSpec((None, 2, 2, LANES), lambda j: (j, 0, 0, 0))],
        out_specs=pl.BlockSpec((None, 9, RET_CHUNK, LANES), lambda j: (j, 0, 0, 0)),
        out_shape=jax.ShapeDtypeStruct((2, 9, RET_CHUNK, LANES), F32),
        compiler_params=_cparams(("parallel",), 32),
        name="retention_tables",
    )(per_lane, per_head)


def _ret_state_kernel(rk_ref, rv_ref, c_ref, s_ref, t_ref, st_ref, r_ref):
    C = RET_CHUNK
    nch = rk_ref.shape[0] // C

    @pl.when(pl.program_id(2) == 0)
    def _():
        r_ref[...] = jnp.zeros_like(r_ref)

    wkb = t_ref[T_WKB]
    gcb = t_ref[T_GCB]
    bd = t_ref[T_BD]
    state = r_ref[...]
    for n in range(nch - 1, -1, -1):
        st_ref[n] = state
        sl = slice(n * C, (n + 1) * C)
        k = _rope(rk_ref[sl, :].astype(F32), c_ref[sl, :], s_ref[sl, :]) * (HEAD_DIM ** -0.5)
        kw = (k * wkb).astype(BF16)
        v = rv_ref[sl, :]
        kv = lax.dot_general(kw, v, (((0,), (0,)), ((), ())), preferred_element_type=F32)
        state = gcb * state + bd * kv
    r_ref[...] = state


def _ret_main_kernel(rq_ref, rk_ref, rv_ref, rg_ref, c_ref, s_ref, t_ref, st_ref, j_ref,
                     o_ref, sf_ref):
    C = RET_CHUNK
    nch = rq_ref.shape[0] // C

    @pl.when(pl.program_id(2) == 0)
    def _():
        sf_ref[...] = jnp.zeros_like(sf_ref)

    lane = lax.broadcasted_iota(jnp.int32, (1, LANES), 1)
    lo64 = lane < HEAD_DIM
    j_bf = j_ref[...]
    chunks = [slice(n * C, (n + 1) * C) for n in range(nch)]
    nt = (((1,), (1,)), ((), ()))
    tn = (((0,), (0,)), ((), ()))
    q, k, vb = [], [], []
    for sl in chunks:
        cs, sn = c_ref[sl, :], s_ref[sl, :]
        q.append(_rope(rq_ref[sl, :].astype(F32), cs, sn))
        k.append(_rope(rk_ref[sl, :].astype(F32), cs, sn) * (HEAD_DIM ** -0.5))
        vb.append(rv_ref[sl, :])
    scores = []
    for n in range(nch):
        qb = q[n].astype(BF16)
        kb = k[n].astype(BF16)
        zero = jnp.zeros_like(qb)
        scores.append([lax.dot_general(jnp.where(lo64, qb, zero), kb, nt,
                                       preferred_element_type=F32),
                       lax.dot_general(jnp.where(lo64, zero, qb), kb, nt,
                                       preferred_element_type=F32)])
    kvs = [lax.dot_general((k[n] * t_ref[T_WKF]).astype(BF16), vb[n], tn,
                           preferred_element_type=F32) for n in range(nch)]
    cross_b = [jnp.dot((q[n] * t_ref[T_WQB]).astype(BF16), st_ref[n].astype(BF16),
                       preferred_element_type=F32) for n in range(nch)]
    intra = []
    for n in range(nch):
        parts = [jnp.dot((scores[n][hp] * t_ref[T_D0 + hp]).astype(BF16), vb[n],
                         preferred_element_type=F32) for hp in range(2)]
        intra.append(jnp.where(lo64, parts[0], parts[1]))
    state = sf_ref[...]
    states = []
    for n in range(nch):
        states.append(state.astype(BF16))
        state = t_ref[T_GCF] * state + t_ref[T_BD] * kvs[n]
    sf_ref[...] = state
    outs = [intra[n] + cross_b[n]
            + jnp.dot((q[n] * t_ref[T_WQF]).astype(BF16), states[n], preferred_element_type=F32)
            for n in range(nch)]
    mean_sq = [_head_mean_sq(o, j_bf) for o in outs]
    for n, sl in enumerate(chunks):
        ret = outs[n] * lax.rsqrt(mean_sq[n] + NORM_EPS)
        g = rg_ref[sl, :].astype(F32)
        o_ref[sl, :] = (ret * (g * _sigmoid(g))).astype(o_ref.dtype)


def _retention(p3, cos_t, sin_t, tables, j_bf):
    B, L, _ = p3.shape
    C = RET_CHUNK
    tb = min(1024, L)
    nblk = L // tb
    nch = tb // C
    nw = RET_W // LANES
    col = lambda piece: (lambda b, j, s: (b, s, RET_BLK + nw * piece + j))
    colr = lambda piece: (lambda b, j, s: (b, nblk - 1 - s, RET_BLK + nw * piece + j))
    seq = lambda fn: pl.BlockSpec((None, tb, LANES), fn)
    tab_spec = pl.BlockSpec((None, 9, C, LANES), lambda b, j, s: (j, 0, 0, 0))
    states = pl.pallas_call(
        _ret_state_kernel,
        grid=(B, nw, nblk),
        in_specs=[seq(colr(1)), seq(colr(2)),
                  pl.BlockSpec((tb, LANES), lambda b, j, s: (nblk - 1 - s, 0)),
                  pl.BlockSpec((tb, LANES), lambda b, j, s: (nblk - 1 - s, 0)),
                  tab_spec],
        out_specs=pl.BlockSpec((None, None, nch, LANES, LANES),
                               lambda b, j, s: (b, j, nblk - 1 - s, 0, 0)),
        out_shape=jax.ShapeDtypeStruct((B, nw, L // C, LANES, LANES), F32),
        scratch_shapes=[pltpu.VMEM((LANES, LANES), F32)],
        compiler_params=_cparams(("parallel", "parallel", "arbitrary"), 32),
        name="retention_state",
    )(p3, p3, cos_t, sin_t, tables)
    return pl.pallas_call(
        _ret_main_kernel,
        grid=(B, nw, nblk),
        in_specs=[seq(col(0)), seq(col(1)), seq(col(2)), seq(col(3)),
                  pl.BlockSpec((tb, LANES), lambda b, j, s: (s, 0)),
                  pl.BlockSpec((tb, LANES), lambda b, j, s: (s, 0)),
                  tab_spec,
                  pl.BlockSpec((None, None, nch, LANES, LANES), lambda b, j, s: (b, j, s, 0, 0)),
                  pl.BlockSpec((LANES, LANES), lambda b, j, s: (0, 0))],
        out_specs=pl.BlockSpec((None, tb, LANES), lambda b, j, s: (b, s, j)),
        out_shape=jax.ShapeDtypeStruct((B, L, RET_W), BF16),
        scratch_shapes=[pltpu.VMEM((LANES, LANES), F32)],
        compiler_params=_cparams(("parallel", "parallel", "arbitrary"), 32),
        name="retention_main",
    )(p3, p3, p3, p3, cos_t, sin_t, tables, states, j_bf)


def _merge_kernel(oa_ref, ob_ref, oc_ref, od_ref, g0_ref, g1_ref, g2_ref, g3_ref,
                  wb_ref, wo_ref, x_ref, gn_ref, h_ref):
    tm = x_ref.shape[0]
    nparts = 2 if tm % (2 * HALO) == 0 else 1
    parts = [slice(i * tm // nparts, (i + 1) * tm // nparts) for i in range(nparts)]
    branches = ((oa_ref, g0_ref), (ob_ref, g1_ref), (oc_ref, g2_ref), (od_ref, g3_ref))
    proj = [[jnp.dot(br[rows, :].astype(BF16), wb_ref[n], preferred_element_type=F32)
             for n, (br, _) in enumerate(branches)] for rows in parts]
    merged = []
    for pi, rows in enumerate(parts):
        acc = None
        for n, (_, gr) in enumerate(branches):
            gate = (0.5 * jnp.tanh(gr[rows, :]) + 0.5).astype(F32)
            term = gate * proj[pi][n]
            acc = term if acc is None else acc + term
        merged.append(acc.astype(BF16))
    ys = [jnp.dot(m, wo_ref[...], preferred_element_type=F32) for m in merged]
    for rows, y in zip(parts, ys):
        ms = jnp.mean(y * y, axis=-1, keepdims=True)
        h_ref[rows, :] = x_ref[rows, :] + y * lax.rsqrt(ms + NORM_EPS) * gn_ref[...]


def _merge(oa, ob, oc, od, p2, wb_bf, wo_bf, x2, gn):
    T = x2.shape[0]
    tm = min(1024, T)
    br = pl.BlockSpec((tm, BRANCH_W), lambda i: (i, 0))
    gate = lambda n: pl.BlockSpec((tm, D_MODEL), lambda i, n=n: (i, GATE_BLK + n))
    row = pl.BlockSpec((tm, D_MODEL), lambda i: (i, 0))
    return pl.pallas_call(
        _merge_kernel,
        grid=(T // tm,),
        in_specs=[br, br, br, br, gate(0), gate(1), gate(2), gate(3),
                  pl.BlockSpec((N_BRANCH, BRANCH_W, D_MODEL), lambda i: (0, 0, 0)),
                  pl.BlockSpec((D_MODEL, D_MODEL), lambda i: (0, 0)),
                  row, pl.BlockSpec((1, D_MODEL), lambda i: (0, 0))],
        out_specs=row,
        out_shape=jax.ShapeDtypeStruct((T, D_MODEL), F32),
        compiler_params=_cparams(("parallel",), 48),
        name="merge_out_proj",
    )(oa, ob, oc, od, p2, p2, p2, p2, wb_bf, wo_bf, x2, gn)


def _ffn_kernel(h_ref, g2_ref, wg_ref, wu_ref, wd_ref, g3_ref, o_ref):
    h = h_ref[...]
    ms = jnp.mean(h * h, axis=-1, keepdims=True)
    hn = (h * lax.rsqrt(ms + NORM_EPS) * g2_ref[...]).astype(BF16)
    f = None
    for lo, hi in _col_chunks(D_FF, 2):
        g = jnp.dot(hn, wg_ref[:, lo:hi], preferred_element_type=F32)
        u = jnp.dot(hn, wu_ref[:, lo:hi], preferred_element_type=F32)
        a = (g * _sigmoid(g) * u).astype(BF16)
        part = jnp.dot(a, wd_ref[lo:hi, :], preferred_element_type=F32)
        f = part if f is None else f + part
    ms = jnp.mean(f * f, axis=-1, keepdims=True)
    o_ref[...] = h + f * lax.rsqrt(ms + NORM_EPS) * g3_ref[...]


def _ffn(h2, g2, wg_bf, wu_bf, wd_bf, g3):
    T = h2.shape[0]
    tm = min(512, T)
    row = pl.BlockSpec((tm, D_MODEL), lambda i: (i, 0))
    vec = pl.BlockSpec((1, D_MODEL), lambda i: (0, 0))
    return pl.pallas_call(
        _ffn_kernel,
        grid=(T // tm,),
        in_specs=[row, vec,
                  pl.BlockSpec((D_MODEL, D_FF), lambda i: (0, 0)),
                  pl.BlockSpec((D_MODEL, D_FF), lambda i: (0, 0)),
                  pl.BlockSpec((D_FF, D_MODEL), lambda i: (0, 0)),
                  vec],
        out_specs=row,
        out_shape=jax.ShapeDtypeStruct((T, D_MODEL), F32),
        compiler_params=_cparams(("parallel",), 48),
        name="ffn",
    )(h2, g2, wg_bf, wu_bf, wd_bf, g3)


def _rope_tables(L):
    rows = L // GRID_W
    r = jnp.repeat(jnp.arange(rows, dtype=F32), GRID_W)
    c = jnp.tile(jnp.arange(GRID_W, dtype=F32), rows)
    inv = ROPE_BASE ** (-jnp.arange(ROPE_FREQS, dtype=F32) / ROPE_FREQS)
    ar = r[:, None] * inv
    ac = c[:, None] * inv
    cos64 = jnp.concatenate([jnp.cos(ar), jnp.cos(ar), jnp.cos(ac), jnp.cos(ac)], axis=1)
    sin64 = jnp.concatenate([-jnp.sin(ar), jnp.sin(ar), -jnp.sin(ac), jnp.sin(ac)], axis=1)
    return jnp.tile(cos64, (1, 2)), jnp.tile(sin64, (1, 2))


def _filter_features(L):
    t = jnp.linspace(0.0, 1.0, L, dtype=F32)[:, None]
    f = jnp.linspace(1e-4, HY_BANDS - 1, HY_BANDS, dtype=F32)
    ang = (2.0 * math.pi / L) * jnp.arange(L, dtype=F32)[:, None] * f[None, :]
    z = jnp.concatenate([t, jnp.cos(ang), -jnp.sin(ang)], axis=-1)
    z = jnp.pad(z, ((0, 0), (0, HY_FILTER_HIDDEN - HY_EMB)))
    idx = np.concatenate([np.arange(L), [0], np.arange(L - 1, 0, -1)])
    return z[idx]


def _head_mean_matrix():
    i = np.arange(LANES)
    j = ((i[:, None] // HEAD_DIM) == (i[None, :] // HEAD_DIM)).astype(np.float32) / HEAD_DIM
    return jnp.asarray(j.astype(BF16))


def _halve_gate_columns(w_in_bf):
    scale = jnp.where(jnp.arange(IN_COLS) >= GATE_OFF, 0.5, 1.0).astype(BF16)
    return w_in_bf * scale


def _trunk(x, wts, fft_dims=None):
    B, L, _ = x.shape
    T = B * L
    n1, n2 = fft_dims or _fft_dims(L)
    cst = _fft_consts(n1, n2, n1 // 2)
    cos_t, sin_t = _rope_tables(L)
    z2 = _filter_features(L)
    deltas = jnp.abs(jnp.linspace(math.log(HY_TARGET) / HY_SLOW_DECAY,
                                  math.log(HY_TARGET) / HY_FAST_DECAY, HY_WIDTH, dtype=F32))[None, :]
    j_bf = _head_mean_matrix()
    depth = wts["w_in"].shape[0]
    x2 = x.reshape(T, D_MODEL)
    for l in range(depth):
        ng = wts["norm_gains"][l]
        w1p = jnp.pad(wts["hy_w1"][l], ((0, HY_FILTER_HIDDEN - HY_EMB), (0, 0)))
        k2raw, sumsq = _hy_filter_raw(z2, w1p, wts["hy_b1"][l][None, :], wts["hy_w2"][l],
                                      wts["hy_b2"][l][None, :], wts["hy_w3"][l],
                                      wts["hy_freq"][l], deltas, L)
        kf_re, kf_im = _filter_fft(k2raw, sumsq, cst, n1, n2)

        p2 = _in_proj(x2, ng[0][None, :], wts["w_in_bf"][l])
        p3 = p2.reshape(B, L, IN_COLS)

        qg = jnp.tile(wts["qk_norm"][l, 0], 2)[None, :]
        kg = jnp.tile(wts["qk_norm"][l, 1], 2)[None, :]
        qt, k, vt = _attn_prep(p3, cos_t, sin_t, qg, kg, j_bf)
        out_a = _flash(qt, k, vt)

        z, x0u = _hy_prologue(p3, wts["hy_conv_w"][l], wts["hy_conv_b"][l][None, :])
        out_b = _hy_conv(z, x0u, wts["hy_bias"][l][None, :], kf_re, kf_im, cst, n1, n2)

        tables = _ret_tables(wts["ret_decay_exp"][l])
        out_c = _retention(p3, cos_t, sin_t, tables, j_bf)

        out_d = _sconv(p3, wts["sc_conv_w"][l])

        h2 = _merge(out_a.reshape(T, BRANCH_W), out_b.reshape(T, BRANCH_W),
                    out_c.reshape(T, BRANCH_W), out_d.reshape(T, BRANCH_W),
                    p2, wts["w_branch_bf"][l], wts["w_out_bf"][l], x2, ng[1][None, :])
        x2 = _ffn(h2, ng[2][None, :], wts["w_gate_bf"][l], wts["w_up_bf"][l],
                  wts["w_ffn_out_bf"][l], ng[3][None, :])
    return x2.reshape(B, L, D_MODEL)


def kernel(x_prompt, x_sample, norm_gains, w_in, qk_norm, hy_conv_w, hy_conv_b, hy_w1, hy_b1, hy_w2,
           hy_b2, hy_w3, hy_freq, hy_bias, ret_decay_exp, sc_conv_w, w_branch, w_out, w_ffn_in,
           w_ffn_out):
    wts = dict(norm_gains=norm_gains, w_in=w_in, qk_norm=qk_norm, hy_conv_w=hy_conv_w,
               hy_conv_b=hy_conv_b, hy_w1=hy_w1, hy_b1=hy_b1, hy_w2=hy_w2, hy_b2=hy_b2,
               hy_w3=hy_w3, hy_freq=hy_freq, hy_bias=hy_bias, ret_decay_exp=ret_decay_exp,
               sc_conv_w=sc_conv_w,
               w_in_bf=_halve_gate_columns(w_in.astype(BF16)), w_branch_bf=w_branch.astype(BF16),
               w_out_bf=w_out.astype(BF16), w_gate_bf=w_ffn_in[..., :D_FF].astype(BF16),
               w_up_bf=w_ffn_in[..., D_FF:].astype(BF16), w_ffn_out_bf=w_ffn_out.astype(BF16))
    return _trunk(x_prompt, wts), _trunk(x_sample, wts)
```

```python
import functools
import math

import numpy as np
import jax
import jax.numpy as jnp
from jax import lax
from jax.experimental import pallas as pl
from jax.experimental.pallas import tpu as pltpu

F32 = jnp.float32
BF16 = jnp.bfloat16

D_MODEL = 1024
GRID_W = 64
N_BRANCH = 4
BRANCH_W = 256
HEAD_DIM = 64
ATT_Q_HEADS = 4
ATT_KV_HEADS = 2
ROPE_BASE = 10000.0
ROPE_FREQS = HEAD_DIM // 4
HY_WIDTH = BRANCH_W
HY_EMB = 33
HY_BANDS = (HY_EMB - 1) // 2
HY_FILTER_HIDDEN = 64
HY_FAST_DECAY = 0.3
HY_SLOW_DECAY = 1.5
HY_TARGET = 1e-2
RET_HEADS = 4
RET_W = RET_HEADS * HEAD_DIM
RET_CHUNK = 128
SC_WIDTH = BRANCH_W
D_FF = 2816
NORM_EPS = 1e-6

ATT_Q_W = ATT_Q_HEADS * HEAD_DIM
ATT_KV_W = ATT_KV_HEADS * HEAD_DIM
A_K_OFF = ATT_Q_W
A_V_OFF = A_K_OFF + ATT_KV_W
HY_OFF = A_V_OFF + ATT_KV_W
RET_OFF = HY_OFF + 3 * HY_WIDTH
SC_OFF = RET_OFF + 4 * RET_W
GATE_OFF = SC_OFF + 3 * SC_WIDTH
IN_COLS = GATE_OFF + N_BRANCH * D_MODEL

LANES = 128
SUBLANES = 8
MXU_W = 256
HALO = 2 * SUBLANES
HY_BLK = HY_OFF // LANES
RET_BLK = RET_OFF // LANES
SC_BLK = SC_OFF // LANES
GATE_BLK = GATE_OFF // D_MODEL
ATT_KB = 512
VT_ROWS = HEAD_DIM + HALO
Q_SCALE = HEAD_DIM ** -0.5 * math.log2(math.e)
FFT_PASSES_FILTER = 3
FFT_PASSES_DATA = 1
FFT_CH = 8
FFT_SETS = 1
MIB = 1 << 20


def _cparams(sem, vmem_mib):
    return pltpu.CompilerParams(dimension_semantics=sem, vmem_limit_bytes=vmem_mib * MIB)


def _sigmoid(x):
    return 0.5 * jnp.tanh(0.5 * x) + 0.5


def _split(x):
    hi = x.astype(BF16)
    lo = (x - hi.astype(F32)).astype(BF16)
    return hi, lo


def _np_split(a64):
    a32 = np.asarray(a64, np.float32)
    hi = a32.astype(BF16)
    lo = (a32 - hi.astype(np.float32)).astype(BF16)
    return jnp.asarray(hi), jnp.asarray(lo)


def _mm_const(a_hi, a_lo, x, passes):
    if passes == 1:
        return jnp.dot(a_hi, x.astype(BF16), preferred_element_type=F32)
    xh, xl = _split(x)
    out = jnp.dot(a_hi, xh, preferred_element_type=F32)
    if passes >= 3:
        out = out + jnp.dot(a_lo, xh, preferred_element_type=F32)
        out = out + jnp.dot(a_hi, xl, preferred_element_type=F32)
    return out


def _mm3(a, b):
    ah, al = _split(a)
    bh, bl = _split(b)
    out = jnp.dot(ah, bh, preferred_element_type=F32)
    out = out + jnp.dot(al, bh, preferred_element_type=F32)
    return out + jnp.dot(ah, bl, preferred_element_type=F32)


def _head_mean_sq(x, j_bf):
    hi, lo = _split(x * x)
    return (jnp.dot(hi, j_bf, preferred_element_type=F32)
            + jnp.dot(lo, j_bf, preferred_element_type=F32))


def _rope(x, c, s):
    lane = lax.broadcasted_iota(jnp.int32, (1, LANES), 1)
    is_b = (lane & 16) != 0
    partner = jnp.where(is_b, pltpu.roll(x, 16, 1), pltpu.roll(x, LANES - 16, 1))
    return x * c + partner * s


def _col_chunks(n, parts):
    tiles = n // MXU_W
    bounds = [MXU_W * (tiles * i // parts) for i in range(parts + 1)]
    return list(zip(bounds[:-1], bounds[1:]))


def _in_proj_kernel(x_ref, g_ref, w_ref, o_ref):
    x = x_ref[...]
    ms = jnp.mean(x * x, axis=-1, keepdims=True)
    xn = (x * lax.rsqrt(ms + NORM_EPS) * g_ref[...]).astype(BF16)
    for lo, hi in _col_chunks(IN_COLS, 4):
        o_ref[:, lo:hi] = jnp.dot(xn, w_ref[:, lo:hi],
                                  preferred_element_type=F32).astype(o_ref.dtype)


def _in_proj(x2, g, w_bf):
    T = x2.shape[0]
    tm = min(512, T)
    return pl.pallas_call(
        _in_proj_kernel,
        grid=(T // tm,),
        in_specs=[pl.BlockSpec((tm, D_MODEL), lambda i: (i, 0)),
                  pl.BlockSpec((1, D_MODEL), lambda i: (0, 0)),
                  pl.BlockSpec((D_MODEL, IN_COLS), lambda i: (0, 0))],
        out_specs=pl.BlockSpec((tm, IN_COLS), lambda i: (i, 0)),
        out_shape=jax.ShapeDtypeStruct((T, IN_COLS), BF16),
        compiler_params=_cparams(("parallel",), 48),
        name="in_proj",
    )(x2, g, w_bf)


def _attn_prep_kernel(p_ref, c_ref, s_ref, qg_ref, kg_ref, j_ref, qt_ref, k_ref, vt_ref):
    c = c_ref[...]
    s = s_ref[...]
    j_bf = j_ref[...]
    tm = p_ref.shape[0]
    zeros = jnp.zeros((HEAD_DIM, tm), BF16)
    for blk in range(2):
        q = p_ref[:, LANES * blk:LANES * (blk + 1)].astype(F32)
        qn = q * lax.rsqrt(_head_mean_sq(q, j_bf) + NORM_EPS) * qg_ref[...]
        qt = (_rope(qn, c, s) * Q_SCALE).T.astype(BF16)
        lo, hi = qt[:HEAD_DIM], qt[HEAD_DIM:]
        if blk == 0:
            qt_ref[0] = jnp.concatenate([lo, zeros], axis=0)
            qt_ref[1] = jnp.concatenate([hi, zeros], axis=0)
        else:
            qt_ref[2] = jnp.concatenate([zeros, lo], axis=0)
            qt_ref[3] = jnp.concatenate([zeros, hi], axis=0)
    k = p_ref[:, A_K_OFF:A_K_OFF + LANES].astype(F32)
    kn = k * lax.rsqrt(_head_mean_sq(k, j_bf) + NORM_EPS) * kg_ref[...]
    k_ref[...] = _rope(kn, c, s).astype(BF16)
    v = p_ref[:, A_V_OFF:A_V_OFF + LANES].astype(F32)
    ones = jnp.ones((VT_ROWS - HEAD_DIM, ATT_KB), BF16)
    for t in range(tm // ATT_KB):
        vt = v[t * ATT_KB:(t + 1) * ATT_KB, :].T.astype(BF16)
        vt_ref[t] = jnp.concatenate([vt[:HEAD_DIM], ones, vt[HEAD_DIM:], ones], axis=0)


def _attn_prep(p3, cos_t, sin_t, qg, kg, j_bf):
    B, L, _ = p3.shape
    tm = min(1024, L)
    tab = pl.BlockSpec((tm, LANES), lambda b, i: (i, 0))
    vec = pl.BlockSpec((1, LANES), lambda b, i: (0, 0))
    return pl.pallas_call(
        _attn_prep_kernel,
        grid=(B, L // tm),
        in_specs=[pl.BlockSpec((None, tm, HY_OFF), lambda b, i: (b, i, 0)), tab, tab, vec, vec,
                  pl.BlockSpec((LANES, LANES), lambda b, i: (0, 0))],
        out_specs=[pl.BlockSpec((None, ATT_Q_HEADS, LANES, tm), lambda b, i: (b, 0, 0, i)),
                   pl.BlockSpec((None, tm, LANES), lambda b, i: (b, i, 0)),
                   pl.BlockSpec((None, tm // ATT_KB, ATT_KV_HEADS * VT_ROWS, ATT_KB),
                                lambda b, i: (b, i, 0, 0))],
        out_shape=[jax.ShapeDtypeStruct((B, ATT_Q_HEADS, LANES, L), BF16),
                   jax.ShapeDtypeStruct((B, L, LANES), BF16),
                   jax.ShapeDtypeStruct((B, L // ATT_KB, ATT_KV_HEADS * VT_ROWS, ATT_KB), BF16)],
        compiler_params=_cparams(("parallel", "parallel"), 32),
        name="attn_prep",
    )(p3, cos_t, sin_t, qg, kg, j_bf)


def _flash_kernel(qt_ref, k_ref, vt_ref, o_ref, sa_ref, ma_ref, sb_ref, mb_ref, *, ngrp, nsub):
    tq = qt_ref.shape[2]
    group = ATT_Q_HEADS // ATT_KV_HEADS

    def scores_into(slot, g):
        s_ref, mx_ref = slot
        for t in range(nsub):
            st = pl.multiple_of((g * nsub + t) * ATT_KB, ATT_KB)
            kblk = k_ref[pl.ds(st, ATT_KB), :]
            for h in range(ATT_Q_HEADS):
                s = jnp.dot(kblk, qt_ref[h], preferred_element_type=F32)
                s_ref[t * ATT_Q_HEADS + h] = s
                mx_ref[t * ATT_Q_HEADS + h] = jnp.max(s, axis=0, keepdims=True)

    def consume(slot, g, carry):
        s_ref, mx_ref = slot
        for t in range(nsub):
            probs, stats = [], []
            for h in range(ATT_Q_HEADS):
                m, _ = carry[h]
                s = s_ref[t * ATT_Q_HEADS + h]
                m_new = jnp.maximum(m, mx_ref[t * ATT_Q_HEADS + h])
                stats.append((m_new, jnp.exp2(m - m_new)))
                probs.append(jnp.exp2((s - m_new).astype(BF16)))
            new = []
            for h in range(ATT_Q_HEADS):
                j = h // group
                vt = vt_ref[g * nsub + t, VT_ROWS * j:VT_ROWS * (j + 1), :]
                m_new, alpha = stats[h]
                acc = alpha * carry[h][1] + jnp.dot(vt, probs[h], preferred_element_type=F32)
                new.append((m_new, acc))
            carry = tuple(new)
        return carry

    slot_a = (sa_ref, ma_ref)
    slot_b = (sb_ref, mb_ref)

    def body(i, carry):
        g = 2 * i
        scores_into(slot_b, g + 1)
        carry = consume(slot_a, g, carry)
        scores_into(slot_a, g + 2)
        return consume(slot_b, g + 1, carry)

    carry = tuple((jnp.full((1, tq), -jnp.inf, F32), jnp.zeros((VT_ROWS, tq), F32))
                  for _ in range(ATT_Q_HEADS))
    scores_into(slot_a, 0)
    if ngrp == 1:
        carry = consume(slot_a, 0, carry)
    else:
        carry = lax.fori_loop(0, ngrp // 2 - 1, body, carry)
        scores_into(slot_b, ngrp - 1)
        carry = consume(slot_a, ngrp - 2, carry)
        carry = consume(slot_b, ngrp - 1, carry)
    outs = [acc[:HEAD_DIM] / acc[HEAD_DIM:HEAD_DIM + 1] for _, acc in carry]
    o_ref[...] = jnp.concatenate(outs, axis=0).T.astype(o_ref.dtype)


def _flash(qt, k, vt):
    B, _, _, L = qt.shape
    tq = min(256, L)
    nsub = math.gcd(L // ATT_KB, 2)
    ngrp = L // (ATT_KB * nsub)
    assert ngrp == 1 or ngrp % 2 == 0
    s_scratch = pltpu.VMEM((nsub * ATT_Q_HEADS, ATT_KB, tq), F32)
    m_scratch = pltpu.VMEM((nsub * ATT_Q_HEADS, 1, tq), F32)
    return pl.pallas_call(
        functools.partial(_flash_kernel, ngrp=ngrp, nsub=nsub),
        grid=(B, L // tq),
        in_specs=[pl.BlockSpec((None, ATT_Q_HEADS, LANES, tq), lambda b, i: (b, 0, 0, i)),
                  pl.BlockSpec((None, L, LANES), lambda b, i: (b, 0, 0)),
                  pl.BlockSpec((None, L // ATT_KB, ATT_KV_HEADS * VT_ROWS, ATT_KB),
                               lambda b, i: (b, 0, 0, 0))],
        out_specs=pl.BlockSpec((None, tq, ATT_Q_W), lambda b, i: (b, i, 0)),
        out_shape=jax.ShapeDtypeStruct((B, L, ATT_Q_W), BF16),
        scratch_shapes=[s_scratch, m_scratch, s_scratch, m_scratch],
        compiler_params=_cparams(("parallel", "arbitrary"), 48),
        name="flash_attn",
    )(qt, k, vt)


def _conv3_tile(main, prev_t, next_t, w, first, last):
    tb = main.shape[0]
    row = lax.broadcasted_iota(jnp.int32, (tb, 1), 0)
    before = jnp.where(first, 0.0, prev_t[HALO - 1:HALO, :])
    after = jnp.where(last, 0.0, next_t[0:1, :])
    up = jnp.where(row == 0, before, pltpu.roll(main, 1, 0))
    dn = jnp.where(row == tb - 1, after, pltpu.roll(main, tb - 1, 0))
    return up * w[0:1, :] + main * w[1:2, :] + dn * w[2:3, :]


def _halo_specs(tb, L, col_fn):
    nbh = L // HALO
    rh = tb // HALO
    main = pl.BlockSpec((None, tb, LANES), lambda b, w, s: (b, s, col_fn(w)))
    prev = pl.BlockSpec((None, HALO, LANES),
                        lambda b, w, s: (b, jnp.maximum(s * rh - 1, 0), col_fn(w)))
    nxt = pl.BlockSpec((None, HALO, LANES),
                       lambda b, w, s: (b, jnp.minimum((s + 1) * rh, nbh - 1), col_fn(w)))
    return [main, prev, nxt]


def _f32(ref):
    return ref[...].astype(F32)


def _hy_pro_kernel(x0_ref, x0p_ref, x0n_ref, x1_ref, x1p_ref, x1n_ref,
                   hv_ref, hvp_ref, hvn_ref, w0_ref, w1_ref, w2_ref,
                   b0_ref, b1_ref, b2_ref, z_ref, x0u_ref):
    first = pl.program_id(2) == 0
    last = pl.program_id(2) == pl.num_programs(2) - 1
    x0 = _conv3_tile(_f32(x0_ref), _f32(x0p_ref), _f32(x0n_ref), w0_ref[...], first, last) + b0_ref[...]
    x1 = _conv3_tile(_f32(x1_ref), _f32(x1p_ref), _f32(x1n_ref), w1_ref[...], first, last) + b1_ref[...]
    hv = _conv3_tile(_f32(hv_ref), _f32(hvp_ref), _f32(hvn_ref), w2_ref[...], first, last) + b2_ref[...]
    z_ref[...] = hv * x1
    x0u_ref[...] = x0


def _hy_prologue(p3, hcw, hcb):
    B, L, _ = p3.shape
    tb = min(1024, L)
    nw = HY_WIDTH // LANES
    in_specs = []
    for piece in range(3):
        in_specs += _halo_specs(tb, L, lambda w, piece=piece: HY_BLK + nw * piece + w)
    for piece in range(3):
        in_specs.append(pl.BlockSpec((3, LANES), lambda b, w, s, piece=piece: (0, nw * piece + w)))
    for piece in range(3):
        in_specs.append(pl.BlockSpec((1, LANES), lambda b, w, s, piece=piece: (0, nw * piece + w)))
    out_spec = pl.BlockSpec((None, tb, LANES), lambda b, w, s: (b, s, w))
    out_sh = jax.ShapeDtypeStruct((B, L, HY_WIDTH), F32)
    args = [p3] * 9 + [hcw] * 3 + [hcb] * 3
    return pl.pallas_call(
        _hy_pro_kernel,
        grid=(B, nw, L // tb),
        in_specs=in_specs,
        out_specs=[out_spec, out_spec],
        out_shape=[out_sh, out_sh],
        compiler_params=_cparams(("parallel", "parallel", "parallel"), 32),
        name="hyena_prologue",
    )(*args)


def _sconv_kernel(sb_ref, sc_ref, scp_ref, scn_ref, sh_ref, shp_ref, shn_ref, w_ref, o_ref):
    first = pl.program_id(2) == 0
    last = pl.program_id(2) == pl.num_programs(2) - 1
    y = _conv3_tile(_f32(sc_ref) * _f32(sh_ref), _f32(scp_ref) * _f32(shp_ref),
                    _f32(scn_ref) * _f32(shn_ref), w_ref[...], first, last)
    o_ref[...] = (_f32(sb_ref) * y).astype(o_ref.dtype)


def _sconv(p3, scw):
    B, L, _ = p3.shape
    tb = min(1024, L)
    nw = SC_WIDTH // LANES
    in_specs = [pl.BlockSpec((None, tb, LANES), lambda b, w, s: (b, s, SC_BLK + w))]
    in_specs += _halo_specs(tb, L, lambda w: SC_BLK + nw + w)
    in_specs += _halo_specs(tb, L, lambda w: SC_BLK + 2 * nw + w)
    in_specs.append(pl.BlockSpec((3, LANES), lambda b, w, s: (0, w)))
    return pl.pallas_call(
        _sconv_kernel,
        grid=(B, nw, L // tb),
        in_specs=in_specs,
        out_specs=pl.BlockSpec((None, tb, LANES), lambda b, w, s: (b, s, w)),
        out_shape=jax.ShapeDtypeStruct((B, L, SC_WIDTH), BF16),
        compiler_params=_cparams(("parallel", "parallel", "parallel"), 32),
        name="short_conv",
    )(*([p3] * 7 + [scw]))


def _hy_filter_kernel(z_ref, w1_ref, b1_ref, w2_ref, b2_ref, w3_ref, fr_ref, dl_ref,
                      k_ref, ss_ref, *, L):
    step = pl.program_id(0)
    tr = z_ref.shape[0]
    z = z_ref[...]
    h = jnp.sin(fr_ref[0:1, :] * (_mm3(z, w1_ref[...]) + b1_ref[...]))
    h = jnp.sin(fr_ref[1:2, :] * (_mm3(h, w2_ref[...]) + b2_ref[...]))
    h3 = _mm3(h, w3_ref[...])
    row = step * tr + lax.broadcasted_iota(jnp.int32, (tr, 1), 0)
    val = jnp.where(row < L, h3[:, :HY_WIDTH], h3[:, HY_WIDTH:])
    val = val * jnp.exp(-z[:, 0:1] * dl_ref[...])
    val = jnp.where(row == L, 0.0, val)
    k_ref[...] = val

    @pl.when(step == 0)
    def _():
        ss_ref[...] = jnp.zeros_like(ss_ref)

    ss_ref[...] += jnp.sum(val * val, axis=0, keepdims=True)


def _hy_filter_raw(z2, w1p, b1, w2, b2, w3, freq, deltas, L):
    n = 2 * L
    tr = min(512, n)
    H = HY_FILTER_HIDDEN
    full = lambda shape: pl.BlockSpec(shape, lambda i: (0, 0))
    return pl.pallas_call(
        functools.partial(_hy_filter_kernel, L=L),
        grid=(n // tr,),
        in_specs=[pl.BlockSpec((tr, H), lambda i: (i, 0)),
                  full((H, H)), full((1, H)), full((H, H)), full((1, H)),
                  full((H, 2 * HY_WIDTH)), full((2, H)), full((1, HY_WIDTH))],
        out_specs=[pl.BlockSpec((tr, HY_WIDTH), lambda i: (i, 0)), full((1, HY_WIDTH))],
        out_shape=[jax.ShapeDtypeStruct((n, HY_WIDTH), F32),
                   jax.ShapeDtypeStruct((1, HY_WIDTH), F32)],
        compiler_params=_cparams(("arbitrary",), 32),
        name="hyena_filter",
    )(z2, w1p, b1, w2, b2, w3, freq, deltas)


def _fft_dims(L):
    n = 2 * L
    n2 = 128 if n >= 32 * 128 else 64
    return n // n2, n2


def _fft_consts(n1, n2, r):
    n = n1 * n2
    a1 = 2.0 * np.pi * np.outer(np.arange(n1), np.arange(n1)) / n1
    c1, s1 = np.cos(a1), np.sin(a1)
    a2 = 2.0 * np.pi * np.outer(np.arange(n2), np.arange(n2)) / n2
    c2, s2 = np.cos(a2), np.sin(a2)
    f1 = np.block([[c1[:, :r], s1[:, :r]], [-s1[:, :r], c1[:, :r]]])
    f1_full = np.concatenate([c1, -s1], axis=0)
    m2 = np.block([[c2, s2], [-s2, c2]])
    m2i = np.block([[c2, -s2], [s2, c2]])
    fi = np.block([[c1[:r, :], -s1[:r, :]], [s1[:r, :], c1[:r, :]]])
    aw = 2.0 * np.pi * np.arange(n1) / n
    off = np.repeat(np.arange(FFT_CH), LANES)[None, :] * aw[:, None]
    step = np.broadcast_to((FFT_CH * aw)[:, None], (n1, LANES))
    f32 = lambda a: jnp.asarray(np.asarray(a, np.float32))
    return dict(f1=_np_split(f1), f1_full=_np_split(f1_full), m2=_np_split(m2),
                m2i=_np_split(m2i), fi=_np_split(fi),
                tw=(f32(np.cos(off)), f32(-np.sin(off)), f32(np.cos(step)), f32(-np.sin(step))))


def _cmul(a_re, a_im, b_re, b_im):
    return a_re * b_re - a_im * b_im, a_re * b_im + a_im * b_re


def _lane_tile(x, reps):
    return jnp.concatenate([x] * reps, axis=1)


def _lane_part(x, j):
    return x[:, LANES * j:LANES * (j + 1)]


def _chunk_twiddle(t_re, t_im, tw_refs):
    d_re, d_im, s_re, s_im = (ref[...] for ref in tw_refs)
    cur = _cmul(_lane_tile(t_re, FFT_CH), _lane_tile(t_im, FFT_CH), d_re, d_im)
    return cur, _cmul(t_re, t_im, s_re, s_im)


def _twiddle_init(n1):
    return jnp.ones((n1, LANES), F32), jnp.zeros((n1, LANES), F32)


def _fft_stage1(load_rows, a_ref, f_hi, f_lo, tw_refs, n1, n2, passes):
    def body(ci, carry):
        i0s, tws = [], []
        for u in range(FFT_SETS):
            tw, carry = _chunk_twiddle(*carry, tw_refs)
            tws.append(tw)
            i0s.append((ci * FFT_SETS + u) * FFT_CH)
        prods = [_mm_const(f_hi, f_lo, jnp.concatenate(
            [load_rows(i0 + j) for j in range(FFT_CH)], axis=1), passes)
            for i0 in i0s]
        for i0, (t_re, t_im), a in zip(i0s, tws, prods):
            o_re, o_im = _cmul(a[:n1], a[n1:], t_re, t_im)
            for j in range(FFT_CH):
                base = pl.multiple_of((i0 + j) * 2 * n1, 2 * n1)
                a_ref[pl.ds(base, n1), :] = _lane_part(o_re, j)
                a_ref[pl.ds(base + n1, n1), :] = _lane_part(o_im, j)
        return carry

    lax.fori_loop(0, n2 // (FFT_CH * FFT_SETS), body, _twiddle_init(n1))


def _load_spectrum_rows(a_ref, k1, n1, n2):
    ld = lambda off: a_ref[pl.ds(off, n2, stride=2 * n1), :]
    return jnp.concatenate(
        [jnp.concatenate([ld(k1 + j) for j in range(FFT_CH)], axis=1),
         jnp.concatenate([ld(n1 + k1 + j) for j in range(FFT_CH)], axis=1)], axis=0)


def _filter_fft_kernel(k_ref, ss_ref, f_hi_ref, f_lo_ref, m_hi_ref, m_lo_ref,
                       dre_ref, dim_ref, sre_ref, sim_ref, kr_ref, ki_ref, a_ref, *, n1, n2):
    c = pl.program_id(1)

    @pl.when(c == 0)
    def _():
        _fft_stage1(lambda i: k_ref[pl.ds(i, n1, stride=n2), :], a_ref,
                    f_hi_ref[...], f_lo_ref[...], (dre_ref, dim_ref, sre_ref, sim_ref),
                    n1, n2, FFT_PASSES_FILTER)

    scale = _lane_tile(lax.rsqrt(ss_ref[...] + NORM_EPS) * (1.0 / (n1 * n2)), FFT_CH)
    xin = _load_spectrum_rows(a_ref, c * FFT_CH, n1, n2)
    x = _mm_const(m_hi_ref[...], m_lo_ref[...], xin, FFT_PASSES_FILTER)
    kr_ref[...] = x[:n2] * scale
    ki_ref[...] = x[n2:] * scale


def _filter_fft(k2raw, sumsq, cst, n1, n2):
    n = n1 * n2
    nw = HY_WIDTH // LANES
    consts = [*cst["f1_full"], *cst["m2"], *cst["tw"]]
    full = lambda a: pl.BlockSpec(a.shape, lambda w, c: (0,) * a.ndim)
    out_spec = pl.BlockSpec((None, None, n2, FFT_CH * LANES), lambda w, c: (w, c, 0, 0))
    out_sh = jax.ShapeDtypeStruct((nw, n1 // FFT_CH, n2, FFT_CH * LANES), F32)
    return pl.pallas_call(
        functools.partial(_filter_fft_kernel, n1=n1, n2=n2),
        grid=(nw, n1 // FFT_CH),
        in_specs=[pl.BlockSpec((n, LANES), lambda w, c: (0, w)),
                  pl.BlockSpec((1, LANES), lambda w, c: (0, w))] + [full(a) for a in consts],
        out_specs=[out_spec, out_spec],
        out_shape=[out_sh, out_sh],
        scratch_shapes=[pltpu.VMEM((2 * n, LANES), F32)],
        compiler_params=_cparams(("parallel", "arbitrary"), 48),
        name="hyena_filter_fft",
    )(k2raw, sumsq, *consts)


def _hy_conv_kernel(z_ref, x0_ref, hb_ref, kr_ref, ki_ref, f_hi_ref, f_lo_ref,
                    m_hi_ref, m_lo_ref, mi_hi_ref, mi_lo_ref, fi_hi_ref, fi_lo_ref,
                    dre_ref, dim_ref, sre_ref, sim_ref, o_ref, a_ref, *, n1, n2):
    c = pl.program_id(2)
    r = n1 // 2
    tw_refs = (dre_ref, dim_ref, sre_ref, sim_ref)

    def load_pair(i):
        seq = pl.ds(i, r, stride=n2)
        return jnp.concatenate([z_ref[0, seq, :], z_ref[1, seq, :]], axis=0)

    @pl.when(c == 0)
    def _():
        _fft_stage1(load_pair, a_ref, f_hi_ref[...], f_lo_ref[...], tw_refs, n1, n2,
                    FFT_PASSES_DATA)

    k1s = [(c * FFT_SETS + u) * FFT_CH for u in range(FFT_SETS)]
    xs = [_mm_const(m_hi_ref[...], m_lo_ref[...], _load_spectrum_rows(a_ref, k1, n1, n2),
                    FFT_PASSES_DATA) for k1 in k1s]
    ys = [jnp.concatenate(_cmul(x[:n2], x[n2:], kr_ref[u], ki_ref[u]), axis=0)
          for u, x in enumerate(xs)]
    bqs = [_mm_const(mi_hi_ref[...], mi_lo_ref[...], y, FFT_PASSES_DATA) for y in ys]
    for k1, bq in zip(k1s, bqs):
        for j in range(FFT_CH):
            a_ref[pl.ds(k1 + j, n2, stride=2 * n1), :] = _lane_part(bq[:n2], j)
            a_ref[pl.ds(n1 + k1 + j, n2, stride=2 * n1), :] = _lane_part(bq[n2:], j)

    @pl.when(c == pl.num_programs(2) - 1)
    def _():
        hb = hb_ref[...]

        def body(ci, carry):
            i0s, bts = [], []
            for u in range(FFT_SETS):
                (t_re, t_im), carry = _chunk_twiddle(*carry, tw_refs)
                i0 = (ci * FFT_SETS + u) * FFT_CH
                rows = [pl.multiple_of((i0 + j) * 2 * n1, 2 * n1) for j in range(FFT_CH)]
                br = jnp.concatenate([a_ref[pl.ds(b0, n1), :] for b0 in rows], axis=1)
                bi = jnp.concatenate([a_ref[pl.ds(b0 + n1, n1), :] for b0 in rows], axis=1)
                bts.append(jnp.concatenate([br * t_re + bi * t_im, bi * t_re - br * t_im],
                                           axis=0))
                i0s.append(i0)
            ys = [_mm_const(fi_hi_ref[...], fi_lo_ref[...], bt, FFT_PASSES_DATA) for bt in bts]
            for i0, y in zip(i0s, ys):
                for j in range(FFT_CH):
                    seq = pl.ds(i0 + j, r, stride=n2)
                    for row in range(2):
                        conv = _lane_part(y[row * r:(row + 1) * r], j)
                        o_ref[row, seq, :] = (x0_ref[row, seq, :]
                                              * (conv + z_ref[row, seq, :] * hb)
                                              ).astype(o_ref.dtype)
            return carry

        lax.fori_loop(0, n2 // (FFT_CH * FFT_SETS), body, _twiddle_init(n1))


def _hy_conv(z, x0u, hbias, kf_re, kf_im, cst, n1, n2):
    B, L, _ = z.shape
    assert B % 2 == 0
    nw = HY_WIDTH // LANES
    seq = pl.BlockSpec((2, L, LANES), lambda b, w, c: (b, 0, w), pipeline_mode=pl.Buffered(1))
    kf_spec = pl.BlockSpec((None, FFT_SETS, n2, FFT_CH * LANES), lambda b, w, c: (w, c, 0, 0))
    full = lambda a: pl.BlockSpec(a.shape, lambda b, w, c: (0,) * a.ndim)
    consts = [*cst["f1"], *cst["m2"], *cst["m2i"], *cst["fi"], *cst["tw"]]
    return pl.pallas_call(
        functools.partial(_hy_conv_kernel, n1=n1, n2=n2),
        grid=(B // 2, nw, n1 // (FFT_CH * FFT_SETS)),
        in_specs=[seq, seq, pl.BlockSpec((1, LANES), lambda b, w, c: (0, w)), kf_spec, kf_spec]
                 + [full(a) for a in consts],
        out_specs=seq,
        out_shape=jax.ShapeDtypeStruct((B, L, HY_WIDTH), F32),
        scratch_shapes=[pltpu.VMEM((2 * n1 * n2, LANES), F32)],
        compiler_params=_cparams(("parallel", "parallel", "arbitrary"), 56),
        name="hyena_conv",
    )(z, x0u, hbias, kf_re, kf_im, *consts)


T_D0, T_D1, T_WQF, T_WQB, T_WKF, T_WKB, T_GCF, T_GCB, T_BD = range(9)


def _ret_tables_kernel(rl_ref, rh_ref, t_ref):
    C = RET_CHUNK
    i = lax.broadcasted_iota(jnp.int32, (C, LANES), 0).astype(F32)
    jn = lax.broadcasted_iota(jnp.int32, (C, LANES), 1).astype(F32)
    log_g = lambda e: jnp.log1p(-jnp.exp2(-e))
    diff = i - jn
    for hp in range(2):
        lgf = log_g(rh_ref[hp, 0:1, :])
        lgb = log_g(rh_ref[hp, 1:2, :])
        fwd = jnp.exp(jnp.where(diff >= 0, diff, 0.0) * lgf)
        bwd = jnp.exp(jnp.where(diff < 0, -diff, 0.0) * lgb)
        t_ref[T_D0 + hp] = jnp.where(diff >= 0, fwd, bwd)
    lf = log_g(rl_ref[0:1, :])
    lb = log_g(rl_ref[1:2, :])
    t_ref[T_WQF] = jnp.exp((i + 1.0) * lf)
    t_ref[T_WQB] = jnp.exp((C - i) * lb)
    t_ref[T_WKF] = jnp.exp((C - 1.0 - i) * lf)
    t_ref[T_WKB] = jnp.exp(i * lb)
    bd = jnp.where((i < HEAD_DIM) == (jn < HEAD_DIM), 1.0, 0.0)
    t_ref[T_GCF] = jnp.exp(C * lf) * bd
    t_ref[T_GCB] = jnp.exp(C * lb) * bd
    t_ref[T_BD] = bd


def _ret_tables(rde):
    per_lane = jnp.repeat(rde.reshape(2, 2, 2), HEAD_DIM, axis=2)
    per_lane = per_lane.transpose(1, 0, 2)
    per_head = jnp.broadcast_to(rde.reshape(2, 2, 2, 1), (2, 2, 2, LANES))
    per_head = per_head.transpose(1, 2, 0, 3)
    return pl.pallas_call(
        _ret_tables_kernel,
        grid=(2,),
        in_specs=[pl.BlockSpec((None, 2, LANES), lambda j: (j, 0, 0)),
                  pl.BlockSpec((None, 2, 2, LANES), lambda j: (j, 0, 0, 0))],
        out_specs=pl.BlockSpec((None, 9, RET_CHUNK, LANES), lambda j: (j, 0, 0, 0)),
        out_shape=jax.ShapeDtypeStruct((2, 9, RET_CHUNK, LANES), F32),
        compiler_params=_cparams(("parallel",), 32),
        name="retention_tables",
    )(per_lane, per_head)


def _ret_state_kernel(rk_ref, rv_ref, c_ref, s_ref, t_ref, st_ref, r_ref):
    C = RET_CHUNK
    nch = rk_ref.shape[0] // C

    @pl.when(pl.program_id(2) == 0)
    def _():
        r_ref[...] = jnp.zeros_like(r_ref)

    wkb = t_ref[T_WKB]
    gcb = t_ref[T_GCB]
    bd = t_ref[T_BD]
    state = r_ref[...]
    for n in range(nch - 1, -1, -1):
        st_ref[n] = state
        sl = slice(n * C, (n + 1) * C)
        k = _rope(rk_ref[sl, :].astype(F32), c_ref[sl, :], s_ref[sl, :]) * (HEAD_DIM ** -0.5)
        kw = (k * wkb).astype(BF16)
        v = rv_ref[sl, :]
        kv = lax.dot_general(kw, v, (((0,), (0,)), ((), ())), preferred_element_type=F32)
        state = gcb * state + bd * kv
    r_ref[...] = state


def _ret_main_kernel(rq_ref, rk_ref, rv_ref, rg_ref, c_ref, s_ref, t_ref, st_ref, j_ref,
                     o_ref, sf_ref):
    C = RET_CHUNK
    nch = rq_ref.shape[0] // C

    @pl.when(pl.program_id(2) == 0)
    def _():
        sf_ref[...] = jnp.zeros_like(sf_ref)

    lane = lax.broadcasted_iota(jnp.int32, (1, LANES), 1)
    lo64 = lane < HEAD_DIM
    j_bf = j_ref[...]
    chunks = [slice(n * C, (n + 1) * C) for n in range(nch)]
    nt = (((1,), (1,)), ((), ()))
    tn = (((0,), (0,)), ((), ()))
    q, k, vb = [], [], []
    for sl in chunks:
        cs, sn = c_ref[sl, :], s_ref[sl, :]
        q.append(_rope(rq_ref[sl, :].astype(F32), cs, sn))
        k.append(_rope(rk_ref[sl, :].astype(F32), cs, sn) * (HEAD_DIM ** -0.5))
        vb.append(rv_ref[sl, :])
    scores = []
    for n in range(nch):
        qb = q[n].astype(BF16)
        kb = k[n].astype(BF16)
        zero = jnp.zeros_like(qb)
        scores.append([lax.dot_general(jnp.where(lo64, qb, zero), kb, nt,
                                       preferred_element_type=F32),
                       lax.dot_general(jnp.where(lo64, zero, qb), kb, nt,
                                       preferred_element_type=F32)])
    kvs = [lax.dot_general((k[n] * t_ref[T_WKF]).astype(BF16), vb[n], tn,
                           preferred_element_type=F32) for n in range(nch)]
    cross_b = [jnp.dot((q[n] * t_ref[T_WQB]).astype(BF16), st_ref[n].astype(BF16),
                       preferred_element_type=F32) for n in range(nch)]
    intra = []
    for n in range(nch):
        parts = [jnp.dot((scores[n][hp] * t_ref[T_D0 + hp]).astype(BF16), vb[n],
                         preferred_element_type=F32) for hp in range(2)]
        intra.append(jnp.where(lo64, parts[0], parts[1]))
    state = sf_ref[...]
    states = []
    for n in range(nch):
        states.append(state.astype(BF16))
        state = t_ref[T_GCF] * state + t_ref[T_BD] * kvs[n]
    sf_ref[...] = state
    outs = [intra[n] + cross_b[n]
            + jnp.dot((q[n] * t_ref[T_WQF]).astype(BF16), states[n], preferred_element_type=F32)
            for n in range(nch)]
    mean_sq = [_head_mean_sq(o, j_bf) for o in outs]
    for n, sl in enumerate(chunks):
        ret = outs[n] * lax.rsqrt(mean_sq[n] + NORM_EPS)
        g = rg_ref[sl, :].astype(F32)
        o_ref[sl, :] = (ret * (g * _sigmoid(g))).astype(o_ref.dtype)


def _retention(p3, cos_t, sin_t, tables, j_bf):
    B, L, _ = p3.shape
    C = RET_CHUNK
    tb = min(1024, L)
    nblk = L // tb
    nch = tb // C
    nw = RET_W // LANES
    col = lambda piece: (lambda b, j, s: (b, s, RET_BLK + nw * piece + j))
    colr = lambda piece: (lambda b, j, s: (b, nblk - 1 - s, RET_BLK + nw * piece + j))
    seq = lambda fn: pl.BlockSpec((None, tb, LANES), fn)
    tab_spec = pl.BlockSpec((None, 9, C, LANES), lambda b, j, s: (j, 0, 0, 0))
    states = pl.pallas_call(
        _ret_state_kernel,
        grid=(B, nw, nblk),
        in_specs=[seq(colr(1)), seq(colr(2)),
                  pl.BlockSpec((tb, LANES), lambda b, j, s: (nblk - 1 - s, 0)),
                  pl.BlockSpec((tb, LANES), lambda b, j, s: (nblk - 1 - s, 0)),
                  tab_spec],
        out_specs=pl.BlockSpec((None, None, nch, LANES, LANES),
                               lambda b, j, s: (b, j, nblk - 1 - s, 0, 0)),
        out_shape=jax.ShapeDtypeStruct((B, nw, L // C, LANES, LANES), F32),
        scratch_shapes=[pltpu.VMEM((LANES, LANES), F32)],
        compiler_params=_cparams(("parallel", "parallel", "arbitrary"), 32),
        name="retention_state",
    )(p3, p3, cos_t, sin_t, tables)
    return pl.pallas_call(
        _ret_main_kernel,
        grid=(B, nw, nblk),
        in_specs=[seq(col(0)), seq(col(1)), seq(col(2)), seq(col(3)),
                  pl.BlockSpec((tb, LANES), lambda b, j, s: (s, 0)),
                  pl.BlockSpec((tb, LANES), lambda b, j, s: (s, 0)),
                  tab_spec,
                  pl.BlockSpec((None, None, nch, LANES, LANES), lambda b, j, s: (b, j, s, 0, 0)),
                  pl.BlockSpec((LANES, LANES), lambda b, j, s: (0, 0))],
        out_specs=pl.BlockSpec((None, tb, LANES), lambda b, j, s: (b, s, j)),
        out_shape=jax.ShapeDtypeStruct((B, L, RET_W), BF16),
        scratch_shapes=[pltpu.VMEM((LANES, LANES), F32)],
        compiler_params=_cparams(("parallel", "parallel", "arbitrary"), 32),
        name="retention_main",
    )(p3, p3, p3, p3, cos_t, sin_t, tables, states, j_bf)


def _merge_kernel(oa_ref, ob_ref, oc_ref, od_ref, g0_ref, g1_ref, g2_ref, g3_ref,
                  wb_ref, wo_ref, x_ref, gn_ref, h_ref):
    tm = x_ref.shape[0]
    nparts = 2 if tm % (2 * HALO) == 0 else 1
    parts = [slice(i * tm // nparts, (i + 1) * tm // nparts) for i in range(nparts)]
    branches = ((oa_ref, g0_ref), (ob_ref, g1_ref), (oc_ref, g2_ref), (od_ref, g3_ref))
    proj = [[jnp.dot(br[rows, :].astype(BF16), wb_ref[n], preferred_element_type=F32)
             for n, (br, _) in enumerate(branches)] for rows in parts]
    merged = []
    for pi, rows in enumerate(parts):
        acc = None
        for n, (_, gr) in enumerate(branches):
            gate = (0.5 * jnp.tanh(gr[rows, :]) + 0.5).astype(F32)
            term = gate * proj[pi][n]
            acc = term if acc is None else acc + term
        merged.append(acc.astype(BF16))
    ys = [jnp.dot(m, wo_ref[...], preferred_element_type=F32) for m in merged]
    for rows, y in zip(parts, ys):
        ms = jnp.mean(y * y, axis=-1, keepdims=True)
        h_ref[rows, :] = x_ref[rows, :] + y * lax.rsqrt(ms + NORM_EPS) * gn_ref[...]


def _merge(oa, ob, oc, od, p2, wb_bf, wo_bf, x2, gn):
    T = x2.shape[0]
    tm = min(1024, T)
    br = pl.BlockSpec((tm, BRANCH_W), lambda i: (i, 0))
    gate = lambda n: pl.BlockSpec((tm, D_MODEL), lambda i, n=n: (i, GATE_BLK + n))
    row = pl.BlockSpec((tm, D_MODEL), lambda i: (i, 0))
    return pl.pallas_call(
        _merge_kernel,
        grid=(T // tm,),
        in_specs=[br, br, br, br, gate(0), gate(1), gate(2), gate(3),
                  pl.BlockSpec((N_BRANCH, BRANCH_W, D_MODEL), lambda i: (0, 0, 0)),
                  pl.BlockSpec((D_MODEL, D_MODEL), lambda i: (0, 0)),
                  row, pl.BlockSpec((1, D_MODEL), lambda i: (0, 0))],
        out_specs=row,
        out_shape=jax.ShapeDtypeStruct((T, D_MODEL), F32),
        compiler_params=_cparams(("parallel",), 48),
        name="merge_out_proj",
    )(oa, ob, oc, od, p2, p2, p2, p2, wb_bf, wo_bf, x2, gn)


def _ffn_kernel(h_ref, g2_ref, wg_ref, wu_ref, wd_ref, g3_ref, o_ref):
    h = h_ref[...]
    ms = jnp.mean(h * h, axis=-1, keepdims=True)
    hn = (h * lax.rsqrt(ms + NORM_EPS) * g2_ref[...]).astype(BF16)
    f = None
    for lo, hi in _col_chunks(D_FF, 2):
        g = jnp.dot(hn, wg_ref[:, lo:hi], preferred_element_type=F32)
        u = jnp.dot(hn, wu_ref[:, lo:hi], preferred_element_type=F32)
        a = (g * _sigmoid(g) * u).astype(BF16)
        part = jnp.dot(a, wd_ref[lo:hi, :], preferred_element_type=F32)
        f = part if f is None else f + part
    ms = jnp.mean(f * f, axis=-1, keepdims=True)
    o_ref[...] = h + f * lax.rsqrt(ms + NORM_EPS) * g3_ref[...]


def _ffn(h2, g2, wg_bf, wu_bf, wd_bf, g3):
    T = h2.shape[0]
    tm = min(512, T)
    row = pl.BlockSpec((tm, D_MODEL), lambda i: (i, 0))
    vec = pl.BlockSpec((1, D_MODEL), lambda i: (0, 0))
    return pl.pallas_call(
        _ffn_kernel,
        grid=(T // tm,),
        in_specs=[row, vec,
                  pl.BlockSpec((D_MODEL, D_FF), lambda i: (0, 0)),
                  pl.BlockSpec((D_MODEL, D_FF), lambda i: (0, 0)),
                  pl.BlockSpec((D_FF, D_MODEL), lambda i: (0, 0)),
                  vec],
        out_specs=row,
        out_shape=jax.ShapeDtypeStruct((T, D_MODEL), F32),
        compiler_params=_cparams(("parallel",), 48),
        name="ffn",
    )(h2, g2, wg_bf, wu_bf, wd_bf, g3)


def _rope_tables(L):
    rows = L // GRID_W
    r = jnp.repeat(jnp.arange(rows, dtype=F32), GRID_W)
    c = jnp.tile(jnp.arange(GRID_W, dtype=F32), rows)
    inv = ROPE_BASE ** (-jnp.arange(ROPE_FREQS, dtype=F32) / ROPE_FREQS)
    ar = r[:, None] * inv
    ac = c[:, None] * inv
    cos64 = jnp.concatenate([jnp.cos(ar), jnp.cos(ar), jnp.cos(ac), jnp.cos(ac)], axis=1)
    sin64 = jnp.concatenate([-jnp.sin(ar), jnp.sin(ar), -jnp.sin(ac), jnp.sin(ac)], axis=1)
    return jnp.tile(cos64, (1, 2)), jnp.tile(sin64, (1, 2))


def _filter_features(L):
    t = jnp.linspace(0.0, 1.0, L, dtype=F32)[:, None]
    f = jnp.linspace(1e-4, HY_BANDS - 1, HY_BANDS, dtype=F32)
    ang = (2.0 * math.pi / L) * jnp.arange(L, dtype=F32)[:, None] * f[None, :]
    z = jnp.concatenate([t, jnp.cos(ang), -jnp.sin(ang)], axis=-1)
    z = jnp.pad(z, ((0, 0), (0, HY_FILTER_HIDDEN - HY_EMB)))
    idx = np.concatenate([np.arange(L), [0], np.arange(L - 1, 0, -1)])
    return z[idx]


def _head_mean_matrix():
    i = np.arange(LANES)
    j = ((i[:, None] // HEAD_DIM) == (i[None, :] // HEAD_DIM)).astype(np.float32) / HEAD_DIM
    return jnp.asarray(j.astype(BF16))


def _halve_gate_columns(w_in_bf):
    scale = jnp.where(jnp.arange(IN_COLS) >= GATE_OFF, 0.5, 1.0).astype(BF16)
    return w_in_bf * scale


def _trunk(x, wts, fft_dims=None):
    B, L, _ = x.shape
    T = B * L
    n1, n2 = fft_dims or _fft_dims(L)
    cst = _fft_consts(n1, n2, n1 // 2)
    cos_t, sin_t = _rope_tables(L)
    z2 = _filter_features(L)
    deltas = jnp.abs(jnp.linspace(math.log(HY_TARGET) / HY_SLOW_DECAY,
                                  math.log(HY_TARGET) / HY_FAST_DECAY, HY_WIDTH, dtype=F32))[None, :]
    j_bf = _head_mean_matrix()
    depth = wts["w_in"].shape[0]
    x2 = x.reshape(T, D_MODEL)
    for l in range(depth):
        ng = wts["norm_gains"][l]
        w1p = jnp.pad(wts["hy_w1"][l], ((0, HY_FILTER_HIDDEN - HY_EMB), (0, 0)))
        k2raw, sumsq = _hy_filter_raw(z2, w1p, wts["hy_b1"][l][None, :], wts["hy_w2"][l],
                                      wts["hy_b2"][l][None, :], wts["hy_w3"][l],
                                      wts["hy_freq"][l], deltas, L)
        kf_re, kf_im = _filter_fft(k2raw, sumsq, cst, n1, n2)

        p2 = _in_proj(x2, ng[0][None, :], wts["w_in_bf"][l])
        p3 = p2.reshape(B, L, IN_COLS)

        qg = jnp.tile(wts["qk_norm"][l, 0], 2)[None, :]
        kg = jnp.tile(wts["qk_norm"][l, 1], 2)[None, :]
        qt, k, vt = _attn_prep(p3, cos_t, sin_t, qg, kg, j_bf)
        out_a = _flash(qt, k, vt)

        z, x0u = _hy_prologue(p3, wts["hy_conv_w"][l], wts["hy_conv_b"][l][None, :])
        out_b = _hy_conv(z, x0u, wts["hy_bias"][l][None, :], kf_re, kf_im, cst, n1, n2)

        tables = _ret_tables(wts["ret_decay_exp"][l])
        out_c = _retention(p3, cos_t, sin_t, tables, j_bf)

        out_d = _sconv(p3, wts["sc_conv_w"][l])

        h2 = _merge(out_a.reshape(T, BRANCH_W), out_b.reshape(T, BRANCH_W),
                    out_c.reshape(T, BRANCH_W), out_d.reshape(T, BRANCH_W),
                    p2, wts["w_branch_bf"][l], wts["w_out_bf"][l], x2, ng[1][None, :])
        x2 = _ffn(h2, ng[2][None, :], wts["w_gate_bf"][l], wts["w_up_bf"][l],
                  wts["w_ffn_out_bf"][l], ng[3][None, :])
    return x2.reshape(B, L, D_MODEL)


def kernel(x_prompt, x_sample, norm_gains, w_in, qk_norm, hy_conv_w, hy_conv_b, hy_w1, hy_b1, hy_w2,
           hy_b2, hy_w3, hy_freq, hy_bias, ret_decay_exp, sc_conv_w, w_branch, w_out, w_ffn_in,
           w_ffn_out):
    wts = dict(norm_gains=norm_gains, w_in=w_in, qk_norm=qk_norm, hy_conv_w=hy_conv_w,
               hy_conv_b=hy_conv_b, hy_w1=hy_w1, hy_b1=hy_b1, hy_w2=hy_w2, hy_b2=hy_b2,
               hy_w3=hy_w3, hy_freq=hy_freq, hy_bias=hy_bias, ret_decay_exp=ret_decay_exp,
               sc_conv_w=sc_conv_w,
               w_in_bf=_halve_gate_columns(w_in.astype(BF16)), w_branch_bf=w_branch.astype(BF16),
               w_out_bf=w_out.astype(BF16), w_gate_bf=w_ffn_in[..., :D_FF].astype(BF16),
               w_up_bf=w_ffn_in[..., D_FF:].astype(BF16), w_ffn_out_bf=w_ffn_out.astype(BF16))
    return _trunk(x_prompt, wts), _trunk(x_sample, wts)
```

```python
import functools
import math

import numpy as np
import jax
import jax.numpy as jnp
from jax import lax
from jax.experimental import pallas as pl
from jax.experimental.pallas import tpu as pltpu

F32 = jnp.float32
BF16 = jnp.bfloat16

D_MODEL = 1024
GRID_W = 64
N_BRANCH = 4
BRANCH_W = 256
HEAD_DIM = 64
ATT_Q_HEADS = 4
ATT_KV_HEADS = 2
ROPE_BASE = 10000.0
ROPE_FREQS = HEAD_DIM // 4
HY_WIDTH = BRANCH_W
HY_EMB = 33
HY_BANDS = (HY_EMB - 1) // 2
HY_FILTER_HIDDEN = 64
HY_FAST_DECAY = 0.3
HY_SLOW_DECAY = 1.5
HY_TARGET = 1e-2
RET_HEADS = 4
RET_W = RET_HEADS * HEAD_DIM
RET_CHUNK = 128
SC_WIDTH = BRANCH_W
D_FF = 2816
NORM_EPS = 1e-6

ATT_Q_W = ATT_Q_HEADS * HEAD_DIM
ATT_KV_W = ATT_KV_HEADS * HEAD_DIM
A_K_OFF = ATT_Q_W
A_V_OFF = A_K_OFF + ATT_KV_W
HY_OFF = A_V_OFF + ATT_KV_W
RET_OFF = HY_OFF + 3 * HY_WIDTH
SC_OFF = RET_OFF + 4 * RET_W
GATE_OFF = SC_OFF + 3 * SC_WIDTH
IN_COLS = GATE_OFF + N_BRANCH * D_MODEL

LANES = 128
SUBLANES = 8
MXU_W = 256
HALO = 2 * SUBLANES
HY_BLK = HY_OFF // LANES
RET_BLK = RET_OFF // LANES
SC_BLK = SC_OFF // LANES
GATE_BLK = GATE_OFF // D_MODEL
ATT_KB = 512
VT_ROWS = HEAD_DIM + HALO
Q_SCALE = HEAD_DIM ** -0.5 * math.log2(math.e)
FFT_PASSES_FILTER = 3
FFT_PASSES_DATA = 1
FFT_CH = 8
FFT_SETS = 1
MIB = 1 << 20


def _cparams(sem, vmem_mib):
    return pltpu.CompilerParams(dimension_semantics=sem, vmem_limit_bytes=vmem_mib * MIB)


def _sigmoid(x):
    return 0.5 * jnp.tanh(0.5 * x) + 0.5


def _split(x):
    hi = x.astype(BF16)
    lo = (x - hi.astype(F32)).astype(BF16)
    return hi, lo


def _np_split(a64):
    a32 = np.asarray(a64, np.float32)
    hi = a32.astype(BF16)
    lo = (a32 - hi.astype(np.float32)).astype(BF16)
    return jnp.asarray(hi), jnp.asarray(lo)


def _mm_const(a_hi, a_lo, x, passes):
    if passes == 1:
        return jnp.dot(a_hi, x.astype(BF16), preferred_element_type=F32)
    xh, xl = _split(x)
    out = jnp.dot(a_hi, xh, preferred_element_type=F32)
    if passes >= 3:
        out = out + jnp.dot(a_lo, xh, preferred_element_type=F32)
        out = out + jnp.dot(a_hi, xl, preferred_element_type=F32)
    return out


def _mm3(a, b):
    ah, al = _split(a)
    bh, bl = _split(b)
    out = jnp.dot(ah, bh, preferred_element_type=F32)
    out = out + jnp.dot(al, bh, preferred_element_type=F32)
    return out + jnp.dot(ah, bl, preferred_element_type=F32)


def _head_mean_sq(x, j_bf):
    hi, lo = _split(x * x)
    return (jnp.dot(hi, j_bf, preferred_element_type=F32)
            + jnp.dot(lo, j_bf, preferred_element_type=F32))


def _rope(x, c, s):
    lane = lax.broadcasted_iota(jnp.int32, (1, LANES), 1)
    is_b = (lane & 16) != 0
    partner = jnp.where(is_b, pltpu.roll(x, 16, 1), pltpu.roll(x, LANES - 16, 1))
    return x * c + partner * s


def _col_chunks(n, parts):
    tiles = n // MXU_W
    bounds = [MXU_W * (tiles * i // parts) for i in range(parts + 1)]
    return list(zip(bounds[:-1], bounds[1:]))


def _in_proj_kernel(x_ref, g_ref, w_ref, o_ref):
    x = x_ref[...]
    ms = jnp.mean(x * x, axis=-1, keepdims=True)
    xn = (x * lax.rsqrt(ms + NORM_EPS) * g_ref[...]).astype(BF16)
    for lo, hi in _col_chunks(IN_COLS, 4):
        o_ref[:, lo:hi] = jnp.dot(xn, w_ref[:, lo:hi],
                                  preferred_element_type=F32).astype(o_ref.dtype)


def _in_proj(x2, g, w_bf):
    T = x2.shape[0]
    tm = min(512, T)
    return pl.pallas_call(
        _in_proj_kernel,
        grid=(T // tm,),
        in_specs=[pl.BlockSpec((tm, D_MODEL), lambda i: (i, 0)),
                  pl.BlockSpec((1, D_MODEL), lambda i: (0, 0)),
                  pl.BlockSpec((D_MODEL, IN_COLS), lambda i: (0, 0))],
        out_specs=pl.BlockSpec((tm, IN_COLS), lambda i: (i, 0)),
        out_shape=jax.ShapeDtypeStruct((T, IN_COLS), BF16),
        compiler_params=_cparams(("parallel",), 48),
        name="in_proj",
    )(x2, g, w_bf)


def _attn_prep_kernel(p_ref, c_ref, s_ref, qg_ref, kg_ref, j_ref, qt_ref, k_ref, vt_ref):
    c = c_ref[...]
    s = s_ref[...]
    j_bf = j_ref[...]
    tm = p_ref.shape[0]
    zeros = jnp.zeros((HEAD_DIM, tm), BF16)
    for blk in range(2):
        q = p_ref[:, LANES * blk:LANES * (blk + 1)].astype(F32)
        qn = q * lax.rsqrt(_head_mean_sq(q, j_bf) + NORM_EPS) * qg_ref[...]
        qt = (_rope(qn, c, s) * Q_SCALE).T.astype(BF16)
        lo, hi = qt[:HEAD_DIM], qt[HEAD_DIM:]
        if blk == 0:
            qt_ref[0] = jnp.concatenate([lo, zeros], axis=0)
            qt_ref[1] = jnp.concatenate([hi, zeros], axis=0)
        else:
            qt_ref[2] = jnp.concatenate([zeros, lo], axis=0)
            qt_ref[3] = jnp.concatenate([zeros, hi], axis=0)
    k = p_ref[:, A_K_OFF:A_K_OFF + LANES].astype(F32)
    kn = k * lax.rsqrt(_head_mean_sq(k, j_bf) + NORM_EPS) * kg_ref[...]
    k_ref[...] = _rope(kn, c, s).astype(BF16)
    v = p_ref[:, A_V_OFF:A_V_OFF + LANES].astype(F32)
    ones = jnp.ones((VT_ROWS - HEAD_DIM, ATT_KB), BF16)
    for t in range(tm // ATT_KB):
        vt = v[t * ATT_KB:(t + 1) * ATT_KB, :].T.astype(BF16)
        vt_ref[t] = jnp.concatenate([vt[:HEAD_DIM], ones, vt[HEAD_DIM:], ones], axis=0)


def _attn_prep(p3, cos_t, sin_t, qg, kg, j_bf):
    B, L, _ = p3.shape
    tm = min(1024, L)
    tab = pl.BlockSpec((tm, LANES), lambda b, i: (i, 0))
    vec = pl.BlockSpec((1, LANES), lambda b, i: (0, 0))
    return pl.pallas_call(
        _attn_prep_kernel,
        grid=(B, L // tm),
        in_specs=[pl.BlockSpec((None, tm, HY_OFF), lambda b, i: (b, i, 0)), tab, tab, vec, vec,
                  pl.BlockSpec((LANES, LANES), lambda b, i: (0, 0))],
        out_specs=[pl.BlockSpec((None, ATT_Q_HEADS, LANES, tm), lambda b, i: (b, 0, 0, i)),
                   pl.BlockSpec((None, tm, LANES), lambda b, i: (b, i, 0)),
                   pl.BlockSpec((None, tm // ATT_KB, ATT_KV_HEADS * VT_ROWS, ATT_KB),
                                lambda b, i: (b, i, 0, 0))],
        out_shape=[jax.ShapeDtypeStruct((B, ATT_Q_HEADS, LANES, L), BF16),
                   jax.ShapeDtypeStruct((B, L, LANES), BF16),
                   jax.ShapeDtypeStruct((B, L // ATT_KB, ATT_KV_HEADS * VT_ROWS, ATT_KB), BF16)],
        compiler_params=_cparams(("parallel", "parallel"), 32),
        name="attn_prep",
    )(p3, cos_t, sin_t, qg, kg, j_bf)


def _flash_kernel(qt_ref, k_ref, vt_ref, o_ref, sa_ref, ma_ref, sb_ref, mb_ref, *, ngrp, nsub):
    tq = qt_ref.shape[2]
    group = ATT_Q_HEADS // ATT_KV_HEADS

    def produce(slot, g, t, h):
        s_ref, mx_ref = slot
        st = pl.multiple_of((g * nsub + t) * ATT_KB, ATT_KB)
        s = jnp.dot(k_ref[pl.ds(st, ATT_KB), :], qt_ref[h],
                    preferred_element_type=F32)
        s_ref[t * ATT_Q_HEADS + h] = s
        mx_ref[t * ATT_Q_HEADS + h] = jnp.max(s, axis=0, keepdims=True)

    def consume(slot, g, t, h, state):
        s_ref, mx_ref = slot
        m, acc = state
        m_new = jnp.maximum(m, mx_ref[t * ATT_Q_HEADS + h])
        alpha = jnp.exp2(m - m_new)
        p = jnp.exp2((s_ref[t * ATT_Q_HEADS + h] - m_new).astype(BF16))
        j = h // group
        vt = vt_ref[g * nsub + t, VT_ROWS * j:VT_ROWS * (j + 1), :]
        return m_new, alpha * acc + jnp.dot(vt, p, preferred_element_type=F32)

    def step(slot_in, g_in, slot_out, g_out, carry):
        carry = list(carry)
        for t in range(nsub):
            for h in range(ATT_Q_HEADS):
                if slot_out is not None:
                    produce(slot_out, g_out, t, h)
                if slot_in is not None:
                    carry[h] = consume(slot_in, g_in, t, h, carry[h])
        return tuple(carry)

    slot_a = (sa_ref, ma_ref)
    slot_b = (sb_ref, mb_ref)

    def body(i, carry):
        g = 2 * i
        carry = step(slot_a, g, slot_b, g + 1, carry)
        return step(slot_b, g + 1, slot_a, g + 2, carry)

    carry = tuple((jnp.full((1, tq), -jnp.inf, F32), jnp.zeros((VT_ROWS, tq), F32))
                  for _ in range(ATT_Q_HEADS))
    carry = step(None, None, slot_a, 0, carry)
    if ngrp > 1:
        carry = lax.fori_loop(0, ngrp // 2 - 1, body, carry)
        carry = step(slot_a, ngrp - 2, slot_b, ngrp - 1, carry)
        carry = step(slot_b, ngrp - 1, None, None, carry)
    else:
        carry = step(slot_a, 0, None, None, carry)
    outs = [acc[:HEAD_DIM] / acc[HEAD_DIM:HEAD_DIM + 1] for _, acc in carry]
    o_ref[...] = jnp.concatenate(outs, axis=0).T.astype(o_ref.dtype)


def _flash(qt, k, vt):
    B, _, _, L = qt.shape
    tq = min(256, L)
    nkb = L // ATT_KB
    nsub = 2 if nkb % 4 == 0 and nkb >= 8 else 1
    ngrp = L // (ATT_KB * nsub)
    assert ngrp == 1 or ngrp % 2 == 0
    s_scratch = pltpu.VMEM((nsub * ATT_Q_HEADS, ATT_KB, tq), F32)
    m_scratch = pltpu.VMEM((nsub * ATT_Q_HEADS, 1, tq), F32)
    return pl.pallas_call(
        functools.partial(_flash_kernel, ngrp=ngrp, nsub=nsub),
        grid=(B, L // tq),
        in_specs=[pl.BlockSpec((None, ATT_Q_HEADS, LANES, tq), lambda b, i: (b, 0, 0, i)),
                  pl.BlockSpec((None, L, LANES), lambda b, i: (b, 0, 0)),
                  pl.BlockSpec((None, L // ATT_KB, ATT_KV_HEADS * VT_ROWS, ATT_KB),
                               lambda b, i: (b, 0, 0, 0))],
        out_specs=pl.BlockSpec((None, tq, ATT_Q_W), lambda b, i: (b, i, 0)),
        out_shape=jax.ShapeDtypeStruct((B, L, ATT_Q_W), BF16),
        scratch_shapes=[s_scratch, m_scratch, s_scratch, m_scratch],
        compiler_params=_cparams(("parallel", "arbitrary"), 48),
        name="flash_attn",
    )(qt, k, vt)


def _conv3_tile(main, prev_t, next_t, w, first, last):
    tb = main.shape[0]
    w0, w1, w2 = w[0:1, :], w[1:2, :], w[2:3, :]
    body = pltpu.roll(main, 1, 0) * w0 + main * w1 + pltpu.roll(main, tb - 1, 0) * w2
    before = jnp.where(first, 0.0, prev_t[HALO - 1:HALO, :])
    after = jnp.where(last, 0.0, next_t[0:1, :])
    row_first = before * w0 + main[0:1, :] * w1 + main[1:2, :] * w2
    row_last = main[tb - 2:tb - 1, :] * w0 + main[tb - 1:tb, :] * w1 + after * w2
    row = lax.broadcasted_iota(jnp.int32, (HALO, 1), 0)
    head = jnp.where(row == 0, row_first, body[:HALO])
    tail = jnp.where(row == HALO - 1, row_last, body[tb - HALO:])
    return jnp.concatenate([head, body[HALO:tb - HALO], tail], axis=0)


def _halo_specs(tb, L, col_fn):
    nbh = L // HALO
    rh = tb // HALO
    main = pl.BlockSpec((None, tb, LANES), lambda b, w, s: (b, s, col_fn(w)))
    prev = pl.BlockSpec((None, HALO, LANES),
                        lambda b, w, s: (b, jnp.maximum(s * rh - 1, 0), col_fn(w)))
    nxt = pl.BlockSpec((None, HALO, LANES),
                       lambda b, w, s: (b, jnp.minimum((s + 1) * rh, nbh - 1), col_fn(w)))
    return [main, prev, nxt]


def _f32(ref):
    return ref[...].astype(F32)


def _hy_pro_kernel(x0_ref, x0p_ref, x0n_ref, x1_ref, x1p_ref, x1n_ref,
                   hv_ref, hvp_ref, hvn_ref, w0_ref, w1_ref, w2_ref,
                   b0_ref, b1_ref, b2_ref, z_ref, x0u_ref):
    first = pl.program_id(2) == 0
    last = pl.program_id(2) == pl.num_programs(2) - 1
    x0 = _conv3_tile(_f32(x0_ref), _f32(x0p_ref), _f32(x0n_ref), w0_ref[...], first, last) + b0_ref[...]
    x1 = _conv3_tile(_f32(x1_ref), _f32(x1p_ref), _f32(x1n_ref), w1_ref[...], first, last) + b1_ref[...]
    hv = _conv3_tile(_f32(hv_ref), _f32(hvp_ref), _f32(hvn_ref), w2_ref[...], first, last) + b2_ref[...]
    z_ref[...] = hv * x1
    x0u_ref[...] = x0


def _hy_prologue(p3, hcw, hcb):
    B, L, _ = p3.shape
    tb = min(1024, L)
    nw = HY_WIDTH // LANES
    in_specs = []
    for piece in range(3):
        in_specs += _halo_specs(tb, L, lambda w, piece=piece: HY_BLK + nw * piece + w)
    for piece in range(3):
        in_specs.append(pl.BlockSpec((3, LANES), lambda b, w, s, piece=piece: (0, nw * piece + w)))
    for piece in range(3):
        in_specs.append(pl.BlockSpec((1, LANES), lambda b, w, s, piece=piece: (0, nw * piece + w)))
    out_spec = pl.BlockSpec((None, tb, LANES), lambda b, w, s: (b, s, w))
    out_sh = jax.ShapeDtypeStruct((B, L, HY_WIDTH), F32)
    args = [p3] * 9 + [hcw] * 3 + [hcb] * 3
    return pl.pallas_call(
        _hy_pro_kernel,
        grid=(B, nw, L // tb),
        in_specs=in_specs,
        out_specs=[out_spec, out_spec],
        out_shape=[out_sh, out_sh],
        compiler_params=_cparams(("parallel", "parallel", "parallel"), 32),
        name="hyena_prologue",
    )(*args)


def _sconv_kernel(sb_ref, sc_ref, scp_ref, scn_ref, sh_ref, shp_ref, shn_ref, w_ref, o_ref):
    first = pl.program_id(2) == 0
    last = pl.program_id(2) == pl.num_programs(2) - 1
    y = _conv3_tile(_f32(sc_ref) * _f32(sh_ref), _f32(scp_ref) * _f32(shp_ref),
                    _f32(scn_ref) * _f32(shn_ref), w_ref[...], first, last)
    o_ref[...] = (_f32(sb_ref) * y).astype(o_ref.dtype)


def _sconv(p3, scw):
    B, L, _ = p3.shape
    tb = min(1024, L)
    nw = SC_WIDTH // LANES
    in_specs = [pl.BlockSpec((None, tb, LANES), lambda b, w, s: (b, s, SC_BLK + w))]
    in_specs += _halo_specs(tb, L, lambda w: SC_BLK + nw + w)
    in_specs += _halo_specs(tb, L, lambda w: SC_BLK + 2 * nw + w)
    in_specs.append(pl.BlockSpec((3, LANES), lambda b, w, s: (0, w)))
    return pl.pallas_call(
        _sconv_kernel,
        grid=(B, nw, L // tb),
        in_specs=in_specs,
        out_specs=pl.BlockSpec((None, tb, LANES), lambda b, w, s: (b, s, w)),
        out_shape=jax.ShapeDtypeStruct((B, L, SC_WIDTH), BF16),
        compiler_params=_cparams(("parallel", "parallel", "parallel"), 32),
        name="short_conv",
    )(*([p3] * 7 + [scw]))


def _hy_filter_kernel(z_ref, w1_ref, b1_ref, w2_ref, b2_ref, w3_ref, fr_ref, dl_ref,
                      k_ref, ss_ref, *, L):
    step = pl.program_id(0)
    tr = z_ref.shape[0]
    z = z_ref[...]
    h = jnp.sin(fr_ref[0:1, :] * (_mm3(z, w1_ref[...]) + b1_ref[...]))
    h = jnp.sin(fr_ref[1:2, :] * (_mm3(h, w2_ref[...]) + b2_ref[...]))
    h3 = _mm3(h, w3_ref[...])
    row = step * tr + lax.broadcasted_iota(jnp.int32, (tr, 1), 0)
    val = jnp.where(row < L, h3[:, :HY_WIDTH], h3[:, HY_WIDTH:])
    val = val * jnp.exp(-z[:, 0:1] * dl_ref[...])
    val = jnp.where(row == L, 0.0, val)
    k_ref[...] = val

    @pl.when(step == 0)
    def _():
        ss_ref[...] = jnp.zeros_like(ss_ref)

    ss_ref[...] += jnp.sum(val * val, axis=0, keepdims=True)


def _hy_filter_raw(z2, w1p, b1, w2, b2, w3, freq, deltas, L):
    n = 2 * L
    tr = min(512, n)
    H = HY_FILTER_HIDDEN
    full = lambda shape: pl.BlockSpec(shape, lambda i: (0, 0))
    return pl.pallas_call(
        functools.partial(_hy_filter_kernel, L=L),
        grid=(n // tr,),
        in_specs=[pl.BlockSpec((tr, H), lambda i: (i, 0)),
                  full((H, H)), full((1, H)), full((H, H)), full((1, H)),
                  full((H, 2 * HY_WIDTH)), full((2, H)), full((1, HY_WIDTH))],
        out_specs=[pl.BlockSpec((tr, HY_WIDTH), lambda i: (i, 0)), full((1, HY_WIDTH))],
        out_shape=[jax.ShapeDtypeStruct((n, HY_WIDTH), F32),
                   jax.ShapeDtypeStruct((1, HY_WIDTH), F32)],
        compiler_params=_cparams(("arbitrary",), 32),
        name="hyena_filter",
    )(z2, w1p, b1, w2, b2, w3, freq, deltas)


def _fft_dims(L):
    n = 2 * L
    n2 = 128 if n >= 32 * 128 else 64
    return n // n2, n2


def _fft_consts(n1, n2, r):
    n = n1 * n2
    a1 = 2.0 * np.pi * np.outer(np.arange(n1), np.arange(n1)) / n1
    c1, s1 = np.cos(a1), np.sin(a1)
    a2 = 2.0 * np.pi * np.outer(np.arange(n2), np.arange(n2)) / n2
    c2, s2 = np.cos(a2), np.sin(a2)
    f1 = np.block([[c1[:, :r], s1[:, :r]], [-s1[:, :r], c1[:, :r]]])
    f1_full = np.concatenate([c1, -s1], axis=0)
    m2 = np.block([[c2, s2], [-s2, c2]])
    m2i = np.block([[c2, -s2], [s2, c2]])
    fi = np.block([[c1[:r, :], -s1[:r, :]], [s1[:r, :], c1[:r, :]]])
    aw = 2.0 * np.pi * np.arange(n1) / n
    off = np.repeat(np.arange(FFT_CH), LANES)[None, :] * aw[:, None]
    step = np.broadcast_to((FFT_CH * aw)[:, None], (n1, LANES))
    f32 = lambda a: jnp.asarray(np.asarray(a, np.float32))
    return dict(f1=_np_split(f1), f1_full=_np_split(f1_full), m2=_np_split(m2),
                m2i=_np_split(m2i), fi=_np_split(fi),
                tw=(f32(np.cos(off)), f32(-np.sin(off)), f32(np.cos(step)), f32(-np.sin(step))))


def _cmul(a_re, a_im, b_re, b_im):
    return a_re * b_re - a_im * b_im, a_re * b_im + a_im * b_re


def _lane_tile(x, reps):
    return jnp.concatenate([x] * reps, axis=1)


def _lane_part(x, j):
    return x[:, LANES * j:LANES * (j + 1)]


def _chunk_twiddle(t_re, t_im, tw_refs):
    d_re, d_im, s_re, s_im = (ref[...] for ref in tw_refs)
    cur = _cmul(_lane_tile(t_re, FFT_CH), _lane_tile(t_im, FFT_CH), d_re, d_im)
    return cur, _cmul(t_re, t_im, s_re, s_im)


def _twiddle_init(n1):
    return jnp.ones((n1, LANES), F32), jnp.zeros((n1, LANES), F32)


def _fft_stage1(load_rows, a_ref, f_hi, f_lo, tw_refs, n1, n2, passes):
    def body(ci, carry):
        i0s, tws = [], []
        for u in range(FFT_SETS):
            tw, carry = _chunk_twiddle(*carry, tw_refs)
            tws.append(tw)
            i0s.append((ci * FFT_SETS + u) * FFT_CH)
        prods = [_mm_const(f_hi, f_lo, jnp.concatenate(
            [load_rows(i0 + j) for j in range(FFT_CH)], axis=1), passes)
            for i0 in i0s]
        for i0, (t_re, t_im), a in zip(i0s, tws, prods):
            o_re, o_im = _cmul(a[:n1], a[n1:], t_re, t_im)
            for j in range(FFT_CH):
                base = pl.multiple_of((i0 + j) * 2 * n1, 2 * n1)
                a_ref[pl.ds(base, n1), :] = _lane_part(o_re, j)
                a_ref[pl.ds(base + n1, n1), :] = _lane_part(o_im, j)
        return carry

    lax.fori_loop(0, n2 // (FFT_CH * FFT_SETS), body, _twiddle_init(n1))


def _load_spectrum_rows(a_ref, k1, n1, n2):
    ld = lambda off: a_ref[pl.ds(off, n2, stride=2 * n1), :]
    return jnp.concatenate(
        [jnp.concatenate([ld(k1 + j) for j in range(FFT_CH)], axis=1),
         jnp.concatenate([ld(n1 + k1 + j) for j in range(FFT_CH)], axis=1)], axis=0)


def _filter_fft_kernel(k_ref, ss_ref, f_hi_ref, f_lo_ref, m_hi_ref, m_lo_ref,
                       dre_ref, dim_ref, sre_ref, sim_ref, kr_ref, ki_ref, a_ref, *, n1, n2):
    c = pl.program_id(1)

    @pl.when(c == 0)
    def _():
        _fft_stage1(lambda i: k_ref[pl.ds(i, n1, stride=n2), :], a_ref,
                    f_hi_ref[...], f_lo_ref[...], (dre_ref, dim_ref, sre_ref, sim_ref),
                    n1, n2, FFT_PASSES_FILTER)

    scale = _lane_tile(lax.rsqrt(ss_ref[...] + NORM_EPS) * (1.0 / (n1 * n2)), FFT_CH)
    xin = _load_spectrum_rows(a_ref, c * FFT_CH, n1, n2)
    x = _mm_const(m_hi_ref[...], m_lo_ref[...], xin, FFT_PASSES_FILTER)
    kr_ref[...] = x[:n2] * scale
    ki_ref[...] = x[n2:] * scale


def _filter_fft(k2raw, sumsq, cst, n1, n2):
    n = n1 * n2
    nw = HY_WIDTH // LANES
    consts = [*cst["f1_full"], *cst["m2"], *cst["tw"]]
    full = lambda a: pl.BlockSpec(a.shape, lambda w, c: (0,) * a.ndim)
    out_spec = pl.BlockSpec((None, None, n2, FFT_CH * LANES), lambda w, c: (w, c, 0, 0))
    out_sh = jax.ShapeDtypeStruct((nw, n1 // FFT_CH, n2, FFT_CH * LANES), F32)
    return pl.pallas_call(
        functools.partial(_filter_fft_kernel, n1=n1, n2=n2),
        grid=(nw, n1 // FFT_CH),
        in_specs=[pl.BlockSpec((n, LANES), lambda w, c: (0, w)),
                  pl.BlockSpec((1, LANES), lambda w, c: (0, w))] + [full(a) for a in consts],
        out_specs=[out_spec, out_spec],
        out_shape=[out_sh, out_sh],
        scratch_shapes=[pltpu.VMEM((2 * n, LANES), F32)],
        compiler_params=_cparams(("parallel", "arbitrary"), 48),
        name="hyena_filter_fft",
    )(k2raw, sumsq, *consts)


def _hy_conv_kernel(z_ref, x0_ref, hb_ref, kr_ref, ki_ref, f_hi_ref, f_lo_ref,
                    m_hi_ref, m_lo_ref, mi_hi_ref, mi_lo_ref, fi_hi_ref, fi_lo_ref,
                    dre_ref, dim_ref, sre_ref, sim_ref, o_ref, a_ref, *, n1, n2):
    c = pl.program_id(2)
    r = n1 // 2
    tw_refs = (dre_ref, dim_ref, sre_ref, sim_ref)

    def load_pair(i):
        seq = pl.ds(i, r, stride=n2)
        return jnp.concatenate([z_ref[0, seq, :], z_ref[1, seq, :]], axis=0)

    @pl.when(c == 0)
    def _():
        _fft_stage1(load_pair, a_ref, f_hi_ref[...], f_lo_ref[...], tw_refs, n1, n2,
                    FFT_PASSES_DATA)

    k1s = [(c * FFT_SETS + u) * FFT_CH for u in range(FFT_SETS)]
    xs = [_mm_const(m_hi_ref[...], m_lo_ref[...], _load_spectrum_rows(a_ref, k1, n1, n2),
                    FFT_PASSES_DATA) for k1 in k1s]
    ys = [jnp.concatenate(_cmul(x[:n2], x[n2:], kr_ref[u], ki_ref[u]), axis=0)
          for u, x in enumerate(xs)]
    bqs = [_mm_const(mi_hi_ref[...], mi_lo_ref[...], y, FFT_PASSES_DATA) for y in ys]
    for k1, bq in zip(k1s, bqs):
        for j in range(FFT_CH):
            a_ref[pl.ds(k1 + j, n2, stride=2 * n1), :] = _lane_part(bq[:n2], j)
            a_ref[pl.ds(n1 + k1 + j, n2, stride=2 * n1), :] = _lane_part(bq[n2:], j)

    @pl.when(c == pl.num_programs(2) - 1)
    def _():
        hb = hb_ref[...]

        def body(ci, carry):
            i0s, bts = [], []
            for u in range(FFT_SETS):
                (t_re, t_im), carry = _chunk_twiddle(*carry, tw_refs)
                i0 = (ci * FFT_SETS + u) * FFT_CH
                rows = [pl.multiple_of((i0 + j) * 2 * n1, 2 * n1) for j in range(FFT_CH)]
                br = jnp.concatenate([a_ref[pl.ds(b0, n1), :] for b0 in rows], axis=1)
                bi = jnp.concatenate([a_ref[pl.ds(b0 + n1, n1), :] for b0 in rows], axis=1)
                bts.append(jnp.concatenate([br * t_re + bi * t_im, bi * t_re - br * t_im],
                                           axis=0))
                i0s.append(i0)
            ys = [_mm_const(fi_hi_ref[...], fi_lo_ref[...], bt, FFT_PASSES_DATA) for bt in bts]
            for i0, y in zip(i0s, ys):
                for j in range(FFT_CH):
                    seq = pl.ds(i0 + j, r, stride=n2)
                    for row in range(2):
                        conv = _lane_part(y[row * r:(row + 1) * r], j)
                        o_ref[row, seq, :] = (x0_ref[row, seq, :]
                                              * (conv + z_ref[row, seq, :] * hb)
                                              ).astype(o_ref.dtype)
            return carry

        lax.fori_loop(0, n2 // (FFT_CH * FFT_SETS), body, _twiddle_init(n1))


def _hy_conv(z, x0u, hbias, kf_re, kf_im, cst, n1, n2):
    B, L, _ = z.shape
    assert B % 2 == 0
    nw = HY_WIDTH // LANES
    seq = pl.BlockSpec((2, L, LANES), lambda b, w, c: (b, 0, w), pipeline_mode=pl.Buffered(1))
    kf_spec = pl.BlockSpec((None, FFT_SETS, n2, FFT_CH * LANES), lambda b, w, c: (w, c, 0, 0))
    full = lambda a: pl.BlockSpec(a.shape, lambda b, w, c: (0,) * a.ndim)
    consts = [*cst["f1"], *cst["m2"], *cst["m2i"], *cst["fi"], *cst["tw"]]
    return pl.pallas_call(
        functools.partial(_hy_conv_kernel, n1=n1, n2=n2),
        grid=(B // 2, nw, n1 // (FFT_CH * FFT_SETS)),
        in_specs=[seq, seq, pl.BlockSpec((1, LANES), lambda b, w, c: (0, w)), kf_spec, kf_spec]
                 + [full(a) for a in consts],
        out_specs=seq,
        out_shape=jax.ShapeDtypeStruct((B, L, HY_WIDTH), F32),
        scratch_shapes=[pltpu.VMEM((2 * n1 * n2, LANES), F32)],
        compiler_params=_cparams(("parallel", "parallel", "arbitrary"), 56),
        name="hyena_conv",
    )(z, x0u, hbias, kf_re, kf_im, *consts)


T_D0, T_D1, T_WQF, T_WQB, T_WKF, T_WKB, T_GCF, T_GCB, T_BD = range(9)


def _ret_tables_kernel(rl_ref, rh_ref, t_ref):
    C = RET_CHUNK
    i = lax.broadcasted_iota(jnp.int32, (C, LANES), 0).astype(F32)
    jn = lax.broadcasted_iota(jnp.int32, (C, LANES), 1).astype(F32)
    log_g = lambda e: jnp.log1p(-jnp.exp2(-e))
    diff = i - jn
    for hp in range(2):
        lgf = log_g(rh_ref[hp, 0:1, :])
        lgb = log_g(rh_ref[hp, 1:2, :])
        fwd = jnp.exp(jnp.where(diff >= 0, diff, 0.0) * lgf)
        bwd = jnp.exp(jnp.where(diff < 0, -diff, 0.0) * lgb)
        t_ref[T_D0 + hp] = jnp.where(diff >= 0, fwd, bwd)
    lf = log_g(rl_ref[0:1, :])
    lb = log_g(rl_ref[1:2, :])
    t_ref[T_WQF] = jnp.exp((i + 1.0) * lf)
    t_ref[T_WQB] = jnp.exp((C - i) * lb)
    t_ref[T_WKF] = jnp.exp((C - 1.0 - i) * lf)
    t_ref[T_WKB] = jnp.exp(i * lb)
    bd = jnp.where((i < HEAD_DIM) == (jn < HEAD_DIM), 1.0, 0.0)
    t_ref[T_GCF] = jnp.exp(C * lf) * bd
    t_ref[T_GCB] = jnp.exp(C * lb) * bd
    t_ref[T_BD] = bd


def _ret_tables(rde):
    per_lane = jnp.repeat(rde.reshape(2, 2, 2), HEAD_DIM, axis=2)
    per_lane = per_lane.transpose(1, 0, 2)
    per_head = jnp.broadcast_to(rde.reshape(2, 2, 2, 1), (2, 2, 2, LANES))
    per_head = per_head.transpose(1, 2, 0, 3)
    return pl.pallas_call(
        _ret_tables_kernel,
        grid=(2,),
        in_specs=[pl.BlockSpec((None, 2, LANES), lambda j: (j, 0, 0)),
                  pl.BlockSpec((None, 2, 2, LANES), lambda j: (j, 0, 0, 0))],
        out_specs=pl.BlockSpec((None, 9, RET_CHUNK, LANES), lambda j: (j, 0, 0, 0)),
        out_shape=jax.ShapeDtypeStruct((2, 9, RET_CHUNK, LANES), F32),
        compiler_params=_cparams(("parallel",), 32),
        name="retention_tables",
    )(per_lane, per_head)


def _ret_state_kernel(rk_ref, rv_ref, c_ref, s_ref, t_ref, st_ref, r_ref):
    C = RET_CHUNK
    nch = rk_ref.shape[0] // C

    @pl.when(pl.program_id(2) == 0)
    def _():
        r_ref[...] = jnp.zeros_like(r_ref)

    wkb = t_ref[T_WKB]
    gcb = t_ref[T_GCB]
    bd = t_ref[T_BD]
    state = r_ref[...]
    for n in range(nch - 1, -1, -1):
        st_ref[n] = state
        sl = slice(n * C, (n + 1) * C)
        k = _rope(rk_ref[sl, :].astype(F32), c_ref[sl, :], s_ref[sl, :]) * (HEAD_DIM ** -0.5)
        kw = (k * wkb).astype(BF16)
        v = rv_ref[sl, :]
        kv = lax.dot_general(kw, v, (((0,), (0,)), ((), ())), preferred_element_type=F32)
        state = gcb * state + bd * kv
    r_ref[...] = state


def _ret_main_kernel(rq_ref, rk_ref, rv_ref, rg_ref, c_ref, s_ref, t_ref, st_ref, j_ref,
                     o_ref, sf_ref):
    C = RET_CHUNK
    nch = rq_ref.shape[0] // C

    @pl.when(pl.program_id(2) == 0)
    def _():
        sf_ref[...] = jnp.zeros_like(sf_ref)

    lane = lax.broadcasted_iota(jnp.int32, (1, LANES), 1)
    lo64 = lane < HEAD_DIM
    j_bf = j_ref[...]
    chunks = [slice(n * C, (n + 1) * C) for n in range(nch)]
    nt = (((1,), (1,)), ((), ()))
    tn = (((0,), (0,)), ((), ()))
    q, k, vb = [], [], []
    for sl in chunks:
        cs, sn = c_ref[sl, :], s_ref[sl, :]
        q.append(_rope(rq_ref[sl, :].astype(F32), cs, sn))
        k.append(_rope(rk_ref[sl, :].astype(F32), cs, sn) * (HEAD_DIM ** -0.5))
        vb.append(rv_ref[sl, :])
    scores = []
    for n in range(nch):
        qb = q[n].astype(BF16)
        kb = k[n].astype(BF16)
        zero = jnp.zeros_like(qb)
        scores.append([lax.dot_general(jnp.where(lo64, qb, zero), kb, nt,
                                       preferred_element_type=F32),
                       lax.dot_general(jnp.where(lo64, zero, qb), kb, nt,
                                       preferred_element_type=F32)])
    kvs = [lax.dot_general((k[n] * t_ref[T_WKF]).astype(BF16), vb[n], tn,
                           preferred_element_type=F32) for n in range(nch)]
    cross_b = [jnp.dot((q[n] * t_ref[T_WQB]).astype(BF16), st_ref[n].astype(BF16),
                       preferred_element_type=F32) for n in range(nch)]
    intra = []
    for n in range(nch):
        parts = [jnp.dot((scores[n][hp] * t_ref[T_D0 + hp]).astype(BF16), vb[n],
                         preferred_element_type=F32) for hp in range(2)]
        intra.append(jnp.where(lo64, parts[0], parts[1]))
    state = sf_ref[...]
    states = []
    for n in range(nch):
        states.append(state.astype(BF16))
        state = t_ref[T_GCF] * state + t_ref[T_BD] * kvs[n]
    sf_ref[...] = state
    outs = [intra[n] + cross_b[n]
            + jnp.dot((q[n] * t_ref[T_WQF]).astype(BF16), states[n], preferred_element_type=F32)
            for n in range(nch)]
    mean_sq = [_head_mean_sq(o, j_bf) for o in outs]
    for n, sl in enumerate(chunks):
        ret = outs[n] * lax.rsqrt(mean_sq[n] + NORM_EPS)
        g = rg_ref[sl, :].astype(F32)
        o_ref[sl, :] = (ret * (g * _sigmoid(g))).astype(o_ref.dtype)


def _retention(p3, cos_t, sin_t, tables, j_bf):
    B, L, _ = p3.shape
    C = RET_CHUNK
    tb = min(1024, L)
    nblk = L // tb
    nch = tb // C
    nw = RET_W // LANES
    col = lambda piece: (lambda b, j, s: (b, s, RET_BLK + nw * piece + j))
    colr = lambda piece: (lambda b, j, s: (b, nblk - 1 - s, RET_BLK + nw * piece + j))
    seq = lambda fn: pl.BlockSpec((None, tb, LANES), fn)
    tab_spec = pl.BlockSpec((None, 9, C, LANES), lambda b, j, s: (j, 0, 0, 0))
    states = pl.pallas_call(
        _ret_state_kernel,
        grid=(B, nw, nblk),
        in_specs=[seq(colr(1)), seq(colr(2)),
                  pl.BlockSpec((tb, LANES), lambda b, j, s: (nblk - 1 - s, 0)),
                  pl.BlockSpec((tb, LANES), lambda b, j, s: (nblk - 1 - s, 0)),
                  tab_spec],
        out_specs=pl.BlockSpec((None, None, nch, LANES, LANES),
                               lambda b, j, s: (b, j, nblk - 1 - s, 0, 0)),
        out_shape=jax.ShapeDtypeStruct((B, nw, L // C, LANES, LANES), F32),
        scratch_shapes=[pltpu.VMEM((LANES, LANES), F32)],
        compiler_params=_cparams(("parallel", "parallel", "arbitrary"), 32),
        name="retention_state",
    )(p3, p3, cos_t, sin_t, tables)
    return pl.pallas_call(
        _ret_main_kernel,
        grid=(B, nw, nblk),
        in_specs=[seq(col(0)), seq(col(1)), seq(col(2)), seq(col(3)),
                  pl.BlockSpec((tb, LANES), lambda b, j, s: (s, 0)),
                  pl.BlockSpec((tb, LANES), lambda b, j, s: (s, 0)),
                  tab_spec,
                  pl.BlockSpec((None, None, nch, LANES, LANES), lambda b, j, s: (b, j, s, 0, 0)),
                  pl.BlockSpec((LANES, LANES), lambda b, j, s: (0, 0))],
        out_specs=pl.BlockSpec((None, tb, LANES), lambda b, j, s: (b, s, j)),
        out_shape=jax.ShapeDtypeStruct((B, L, RET_W), BF16),
        scratch_shapes=[pltpu.VMEM((LANES, LANES), F32)],
        compiler_params=_cparams(("parallel", "parallel", "arbitrary"), 32),
        name="retention_main",
    )(p3, p3, p3, p3, cos_t, sin_t, tables, states, j_bf)


def _merge_kernel(oa_ref, ob_ref, oc_ref, od_ref, g0_ref, g1_ref, g2_ref, g3_ref,
                  wb_ref, wo_ref, x_ref, gn_ref, h_ref):
    tm = x_ref.shape[0]
    nparts = 2 if tm % (2 * HALO) == 0 else 1
    parts = [slice(i * tm // nparts, (i + 1) * tm // nparts) for i in range(nparts)]
    branches = ((oa_ref, g0_ref), (ob_ref, g1_ref), (oc_ref, g2_ref), (od_ref, g3_ref))
    proj = [[jnp.dot(br[rows, :].astype(BF16), wb_ref[n], preferred_element_type=F32)
             for n, (br, _) in enumerate(branches)] for rows in parts]
    merged = []
    for pi, rows in enumerate(parts):
        acc = None
        for n, (_, gr) in enumerate(branches):
            gate = (0.5 * jnp.tanh(gr[rows, :]) + 0.5).astype(F32)
            term = gate * proj[pi][n]
            acc = term if acc is None else acc + term
        merged.append(acc.astype(BF16))
    ys = [jnp.dot(m, wo_ref[...], preferred_element_type=F32) for m in merged]
    for rows, y in zip(parts, ys):
        ms = jnp.mean(y * y, axis=-1, keepdims=True)
        h_ref[rows, :] = x_ref[rows, :] + y * lax.rsqrt(ms + NORM_EPS) * gn_ref[...]


def _merge(oa, ob, oc, od, p2, wb_bf, wo_bf, x2, gn):
    T = x2.shape[0]
    tm = min(1024, T)
    br = pl.BlockSpec((tm, BRANCH_W), lambda i: (i, 0))
    gate = lambda n: pl.BlockSpec((tm, D_MODEL), lambda i, n=n: (i, GATE_BLK + n))
    row = pl.BlockSpec((tm, D_MODEL), lambda i: (i, 0))
    return pl.pallas_call(
        _merge_kernel,
        grid=(T // tm,),
        in_specs=[br, br, br, br, gate(0), gate(1), gate(2), gate(3),
                  pl.BlockSpec((N_BRANCH, BRANCH_W, D_MODEL), lambda i: (0, 0, 0)),
                  pl.BlockSpec((D_MODEL, D_MODEL), lambda i: (0, 0)),
                  row, pl.BlockSpec((1, D_MODEL), lambda i: (0, 0))],
        out_specs=row,
        out_shape=jax.ShapeDtypeStruct((T, D_MODEL), F32),
        compiler_params=_cparams(("parallel",), 48),
        name="merge_out_proj",
    )(oa, ob, oc, od, p2, p2, p2, p2, wb_bf, wo_bf, x2, gn)


def _ffn_kernel(h_ref, g2_ref, wg_ref, wu_ref, wd_ref, g3_ref, o_ref):
    h = h_ref[...]
    ms = jnp.mean(h * h, axis=-1, keepdims=True)
    hn = (h * lax.rsqrt(ms + NORM_EPS) * g2_ref[...]).astype(BF16)
    f = None
    for lo, hi in _col_chunks(D_FF, 2):
        g = jnp.dot(hn, wg_ref[:, lo:hi], preferred_element_type=F32)
        u = jnp.dot(hn, wu_ref[:, lo:hi], preferred_element_type=F32)
        a = (g * _sigmoid(g) * u).astype(BF16)
        part = jnp.dot(a, wd_ref[lo:hi, :], preferred_element_type=F32)
        f = part if f is None else f + part
    ms = jnp.mean(f * f, axis=-1, keepdims=True)
    o_ref[...] = h + f * lax.rsqrt(ms + NORM_EPS) * g3_ref[...]


def _ffn(h2, g2, wg_bf, wu_bf, wd_bf, g3):
    T = h2.shape[0]
    tm = min(512, T)
    row = pl.BlockSpec((tm, D_MODEL), lambda i: (i, 0))
    vec = pl.BlockSpec((1, D_MODEL), lambda i: (0, 0))
    return pl.pallas_call(
        _ffn_kernel,
        grid=(T // tm,),
        in_specs=[row, vec,
                  pl.BlockSpec((D_MODEL, D_FF), lambda i: (0, 0)),
                  pl.BlockSpec((D_MODEL, D_FF), lambda i: (0, 0)),
                  pl.BlockSpec((D_FF, D_MODEL), lambda i: (0, 0)),
                  vec],
        out_specs=row,
        out_shape=jax.ShapeDtypeStruct((T, D_MODEL), F32),
        compiler_params=_cparams(("parallel",), 48),
        name="ffn",
    )(h2, g2, wg_bf, wu_bf, wd_bf, g3)


def _rope_tables(L):
    rows = L // GRID_W
    r = jnp.repeat(jnp.arange(rows, dtype=F32), GRID_W)
    c = jnp.tile(jnp.arange(GRID_W, dtype=F32), rows)
    inv = ROPE_BASE ** (-jnp.arange(ROPE_FREQS, dtype=F32) / ROPE_FREQS)
    ar = r[:, None] * inv
    ac = c[:, None] * inv
    cos64 = jnp.concatenate([jnp.cos(ar), jnp.cos(ar), jnp.cos(ac), jnp.cos(ac)], axis=1)
    sin64 = jnp.concatenate([-jnp.sin(ar), jnp.sin(ar), -jnp.sin(ac), jnp.sin(ac)], axis=1)
    return jnp.tile(cos64, (1, 2)), jnp.tile(sin64, (1, 2))


def _filter_features(L):
    t = np.linspace(0.0, 1.0, L)[:, None]
    f = np.linspace(1e-4, HY_BANDS - 1, HY_BANDS)
    ang = (2.0 * math.pi / L) * np.arange(L)[:, None] * f[None, :]
    z = np.concatenate([t, np.cos(ang), -np.sin(ang)], axis=-1)
    z = np.pad(z, ((0, 0), (0, HY_FILTER_HIDDEN - HY_EMB)))
    idx = np.concatenate([np.arange(L), [0], np.arange(L - 1, 0, -1)])
    return jnp.asarray(z[idx].astype(np.float32))


def _head_mean_matrix():
    i = np.arange(LANES)
    j = ((i[:, None] // HEAD_DIM) == (i[None, :] // HEAD_DIM)).astype(np.float32) / HEAD_DIM
    return jnp.asarray(j.astype(BF16))


def _halve_gate_columns(w_in_bf):
    scale = jnp.where(jnp.arange(IN_COLS) >= GATE_OFF, 0.5, 1.0).astype(BF16)
    return w_in_bf * scale


def _trunk(x, wts, fft_dims=None):
    B, L, _ = x.shape
    T = B * L
    n1, n2 = fft_dims or _fft_dims(L)
    cst = _fft_consts(n1, n2, n1 // 2)
    cos_t, sin_t = _rope_tables(L)
    z2 = _filter_features(L)
    deltas = jnp.abs(jnp.linspace(math.log(HY_TARGET) / HY_SLOW_DECAY,
                                  math.log(HY_TARGET) / HY_FAST_DECAY, HY_WIDTH, dtype=F32))[None, :]
    j_bf = _head_mean_matrix()
    depth = wts["w_in"].shape[0]
    x2 = x.reshape(T, D_MODEL)
    for l in range(depth):
        ng = wts["norm_gains"][l]
        w1p = jnp.pad(wts["hy_w1"][l], ((0, HY_FILTER_HIDDEN - HY_EMB), (0, 0)))
        k2raw, sumsq = _hy_filter_raw(z2, w1p, wts["hy_b1"][l][None, :], wts["hy_w2"][l],
                                      wts["hy_b2"][l][None, :], wts["hy_w3"][l],
                                      wts["hy_freq"][l], deltas, L)
        kf_re, kf_im = _filter_fft(k2raw, sumsq, cst, n1, n2)

        p2 = _in_proj(x2, ng[0][None, :], wts["w_in_bf"][l])
        p3 = p2.reshape(B, L, IN_COLS)

        qg = jnp.tile(wts["qk_norm"][l, 0], 2)[None, :]
        kg = jnp.tile(wts["qk_norm"][l, 1], 2)[None, :]
        qt, k, vt = _attn_prep(p3, cos_t, sin_t, qg, kg, j_bf)
        out_a = _flash(qt, k, vt)

        z, x0u = _hy_prologue(p3, wts["hy_conv_w"][l], wts["hy_conv_b"][l][None, :])
        out_b = _hy_conv(z, x0u, wts["hy_bias"][l][None, :], kf_re, kf_im, cst, n1, n2)

        tables = _ret_tables(wts["ret_decay_exp"][l])
        out_c = _retention(p3, cos_t, sin_t, tables, j_bf)

        out_d = _sconv(p3, wts["sc_conv_w"][l])

        h2 = _merge(out_a.reshape(T, BRANCH_W), out_b.reshape(T, BRANCH_W),
                    out_c.reshape(T, BRANCH_W), out_d.reshape(T, BRANCH_W),
                    p2, wts["w_branch_bf"][l], wts["w_out_bf"][l], x2, ng[1][None, :])
        x2 = _ffn(h2, ng[2][None, :], wts["w_gate_bf"][l], wts["w_up_bf"][l],
                  wts["w_ffn_out_bf"][l], ng[3][None, :])
    return x2.reshape(B, L, D_MODEL)


def kernel(x_prompt, x_sample, norm_gains, w_in, qk_norm, hy_conv_w, hy_conv_b, hy_w1, hy_b1, hy_w2,
           hy_b2, hy_w3, hy_freq, hy_bias, ret_decay_exp, sc_conv_w, w_branch, w_out, w_ffn_in,
           w_ffn_out):
    wts = dict(norm_gains=norm_gains, w_in=w_in, qk_norm=qk_norm, hy_conv_w=hy_conv_w,
               hy_conv_b=hy_conv_b, hy_w1=hy_w1, hy_b1=hy_b1, hy_w2=hy_w2, hy_b2=hy_b2,
               hy_w3=hy_w3, hy_freq=hy_freq, hy_bias=hy_bias, ret_decay_exp=ret_decay_exp,
               sc_conv_w=sc_conv_w,
               w_in_bf=_halve_gate_columns(w_in.astype(BF16)), w_branch_bf=w_branch.astype(BF16),
               w_out_bf=w_out.astype(BF16), w_gate_bf=w_ffn_in[..., :D_FF].astype(BF16),
               w_up_bf=w_ffn_in[..., D_FF:].astype(BF16), w_ffn_out_bf=w_ffn_out.astype(BF16))
    return _trunk(x_prompt, wts), _trunk(x_sample, wts)
```

```python
import functools
import math

import numpy as np
import jax
import jax.numpy as jnp
from jax import lax
from jax.experimental import pallas as pl
from jax.experimental.pallas import tpu as pltpu

F32 = jnp.float32
BF16 = jnp.bfloat16

D_MODEL = 1024
GRID_W = 64
N_BRANCH = 4
BRANCH_W = 256
HEAD_DIM = 64
ATT_Q_HEADS = 4
ATT_KV_HEADS = 2
ROPE_BASE = 10000.0
ROPE_FREQS = HEAD_DIM // 4
HY_WIDTH = BRANCH_W
HY_EMB = 33
HY_BANDS = (HY_EMB - 1) // 2
HY_FILTER_HIDDEN = 64
HY_FAST_DECAY = 0.3
HY_SLOW_DECAY = 1.5
HY_TARGET = 1e-2
RET_HEADS = 4
RET_W = RET_HEADS * HEAD_DIM
RET_CHUNK = 128
SC_WIDTH = BRANCH_W
D_FF = 2816
NORM_EPS = 1e-6

ATT_Q_W = ATT_Q_HEADS * HEAD_DIM
ATT_KV_W = ATT_KV_HEADS * HEAD_DIM
A_K_OFF = ATT_Q_W
A_V_OFF = A_K_OFF + ATT_KV_W
HY_OFF = A_V_OFF + ATT_KV_W
RET_OFF = HY_OFF + 3 * HY_WIDTH
SC_OFF = RET_OFF + 4 * RET_W
GATE_OFF = SC_OFF + 3 * SC_WIDTH
IN_COLS = GATE_OFF + N_BRANCH * D_MODEL

LANES = 128
SUBLANES = 8
MXU_W = 256
HALO = 2 * SUBLANES
HY_BLK = HY_OFF // LANES
RET_BLK = RET_OFF // LANES
SC_BLK = SC_OFF // LANES
GATE_BLK = GATE_OFF // D_MODEL
ATT_KB = 512
ATT_TQ = 256
VT_ROWS = HEAD_DIM + HALO
Q_SCALE = HEAD_DIM ** -0.5 * math.log2(math.e)
FFT_PASSES_FILTER = 3
FFT_PASSES_DATA = 1
FFT_CH = 8
FFT_SETS = 1
MIB = 1 << 20


def _cparams(sem, vmem_mib):
    return pltpu.CompilerParams(dimension_semantics=sem, vmem_limit_bytes=vmem_mib * MIB)


def _sigmoid(x):
    return 0.5 * jnp.tanh(0.5 * x) + 0.5


def _split(x):
    hi = x.astype(BF16)
    lo = (x - hi.astype(F32)).astype(BF16)
    return hi, lo


def _np_split(a64):
    a32 = np.asarray(a64, np.float32)
    hi = a32.astype(BF16)
    lo = (a32 - hi.astype(np.float32)).astype(BF16)
    return jnp.asarray(hi), jnp.asarray(lo)


def _mm_const(a_hi, a_lo, x, passes):
    if passes == 1:
        return jnp.dot(a_hi, x.astype(BF16), preferred_element_type=F32)
    xh, xl = _split(x)
    out = jnp.dot(a_hi, xh, preferred_element_type=F32)
    if passes >= 3:
        out = out + jnp.dot(a_lo, xh, preferred_element_type=F32)
        out = out + jnp.dot(a_hi, xl, preferred_element_type=F32)
    return out


def _mm3(a, b):
    ah, al = _split(a)
    bh, bl = _split(b)
    out = jnp.dot(ah, bh, preferred_element_type=F32)
    out = out + jnp.dot(al, bh, preferred_element_type=F32)
    return out + jnp.dot(ah, bl, preferred_element_type=F32)


def _head_mean_sq(x, j_bf):
    hi, lo = _split(x * x)
    return (jnp.dot(hi, j_bf, preferred_element_type=F32)
            + jnp.dot(lo, j_bf, preferred_element_type=F32))


def _rope(x, c, s):
    lane = lax.broadcasted_iota(jnp.int32, (1, LANES), 1)
    is_b = (lane & 16) != 0
    partner = jnp.where(is_b, pltpu.roll(x, 16, 1), pltpu.roll(x, LANES - 16, 1))
    return x * c + partner * s


def _col_chunks(n, parts):
    tiles = n // MXU_W
    bounds = [MXU_W * (tiles * i // parts) for i in range(parts + 1)]
    return list(zip(bounds[:-1], bounds[1:]))


def _in_proj_kernel(x_ref, g_ref, w_ref, o_ref):
    x = x_ref[...]
    ms = jnp.mean(x * x, axis=-1, keepdims=True)
    xn = (x * lax.rsqrt(ms + NORM_EPS) * g_ref[...]).astype(BF16)
    for lo, hi in _col_chunks(IN_COLS, 4):
        o_ref[:, lo:hi] = jnp.dot(xn, w_ref[:, lo:hi],
                                  preferred_element_type=F32).astype(o_ref.dtype)


def _in_proj(x2, g, w_bf):
    T = x2.shape[0]
    tm = min(512, T)
    return pl.pallas_call(
        _in_proj_kernel,
        grid=(T // tm,),
        in_specs=[pl.BlockSpec((tm, D_MODEL), lambda i: (i, 0)),
                  pl.BlockSpec((1, D_MODEL), lambda i: (0, 0)),
                  pl.BlockSpec((D_MODEL, IN_COLS), lambda i: (0, 0))],
        out_specs=pl.BlockSpec((tm, IN_COLS), lambda i: (i, 0)),
        out_shape=jax.ShapeDtypeStruct((T, IN_COLS), BF16),
        compiler_params=_cparams(("parallel",), 48),
        name="in_proj",
    )(x2, g, w_bf)


def _attn_prep_kernel(p_ref, c_ref, s_ref, qg_ref, kg_ref, j_ref, qt_ref, k_ref, vt_ref):
    c = c_ref[...]
    s = s_ref[...]
    j_bf = j_ref[...]
    tm = p_ref.shape[0]
    tq = qt_ref.shape[-1]
    zeros = jnp.zeros((HEAD_DIM, tm), BF16)
    for blk in range(2):
        q = p_ref[:, LANES * blk:LANES * (blk + 1)].astype(F32)
        qn = q * lax.rsqrt(_head_mean_sq(q, j_bf) + NORM_EPS) * qg_ref[...]
        qt = (_rope(qn, c, s) * Q_SCALE).astype(BF16).T
        lo, hi = qt[:HEAD_DIM], qt[HEAD_DIM:]
        if blk == 0:
            heads = (jnp.concatenate([lo, zeros], axis=0), jnp.concatenate([hi, zeros], axis=0))
        else:
            heads = (jnp.concatenate([zeros, lo], axis=0), jnp.concatenate([zeros, hi], axis=0))
        for hp, qh in enumerate(heads):
            for t in range(tm // tq):
                qt_ref[2 * blk + hp, t] = qh[:, t * tq:(t + 1) * tq]
    k = p_ref[:, A_K_OFF:A_K_OFF + LANES].astype(F32)
    kn = k * lax.rsqrt(_head_mean_sq(k, j_bf) + NORM_EPS) * kg_ref[...]
    k_ref[...] = _rope(kn, c, s).astype(BF16)
    v = p_ref[:, A_V_OFF:A_V_OFF + LANES]
    ones = jnp.ones((VT_ROWS - HEAD_DIM, ATT_KB), BF16)
    for t in range(tm // ATT_KB):
        vt = v[t * ATT_KB:(t + 1) * ATT_KB, :].T
        vt_ref[t] = jnp.concatenate([vt[:HEAD_DIM], ones, vt[HEAD_DIM:], ones], axis=0)


def _attn_prep(p3, cos_t, sin_t, qg, kg, j_bf):
    B, L, _ = p3.shape
    tm = min(1024, L)
    tq = min(ATT_TQ, L)
    tab = pl.BlockSpec((tm, LANES), lambda b, i: (i, 0))
    vec = pl.BlockSpec((1, LANES), lambda b, i: (0, 0))
    return pl.pallas_call(
        _attn_prep_kernel,
        grid=(B, L // tm),
        in_specs=[pl.BlockSpec((None, tm, HY_OFF), lambda b, i: (b, i, 0)), tab, tab, vec, vec,
                  pl.BlockSpec((LANES, LANES), lambda b, i: (0, 0))],
        out_specs=[pl.BlockSpec((None, ATT_Q_HEADS, tm // tq, LANES, tq),
                                lambda b, i: (b, 0, i, 0, 0)),
                   pl.BlockSpec((None, tm, LANES), lambda b, i: (b, i, 0)),
                   pl.BlockSpec((None, tm // ATT_KB, ATT_KV_HEADS * VT_ROWS, ATT_KB),
                                lambda b, i: (b, i, 0, 0))],
        out_shape=[jax.ShapeDtypeStruct((B, ATT_Q_HEADS, L // tq, LANES, tq), BF16),
                   jax.ShapeDtypeStruct((B, L, LANES), BF16),
                   jax.ShapeDtypeStruct((B, L // ATT_KB, ATT_KV_HEADS * VT_ROWS, ATT_KB), BF16)],
        compiler_params=_cparams(("parallel", "parallel"), 32),
        name="attn_prep",
    )(p3, cos_t, sin_t, qg, kg, j_bf)


def _flash_kernel(qt_ref, k_ref, vt_ref, o_ref, sa_ref, ma_ref, sb_ref, mb_ref, *,
                  ngrp, nsub, ntile):
    tq = ATT_TQ
    group = ATT_Q_HEADS // ATT_KV_HEADS

    def produce(slot, qi, g, t, h):
        s_ref, mx_ref = slot
        st = pl.multiple_of((g * nsub + t) * ATT_KB, ATT_KB)
        s = jnp.dot(k_ref[pl.ds(st, ATT_KB), :], qt_ref[h, qi],
                    preferred_element_type=F32)
        s_ref[t * ATT_Q_HEADS + h] = s
        mx_ref[t * ATT_Q_HEADS + h] = jnp.max(s, axis=0, keepdims=True)

    def consume(slot, g, t, h, state):
        s_ref, mx_ref = slot
        m, acc = state
        m_new = jnp.maximum(m, mx_ref[t * ATT_Q_HEADS + h])
        alpha = jnp.exp2(m - m_new)
        p = jnp.exp2((s_ref[t * ATT_Q_HEADS + h] - m_new).astype(BF16))
        j = h // group
        vt = vt_ref[g * nsub + t, VT_ROWS * j:VT_ROWS * (j + 1), :]
        return m_new, alpha * acc + jnp.dot(vt, p, preferred_element_type=F32)

    def step(slot_in, g_in, slot_out, qi_out, g_out, carry):
        carry = list(carry)
        for t in range(nsub):
            for h in range(ATT_Q_HEADS):
                if slot_out is not None:
                    produce(slot_out, qi_out, g_out, t, h)
                if slot_in is not None:
                    carry[h] = consume(slot_in, g_in, t, h, carry[h])
        return tuple(carry)

    def fresh():
        return tuple((jnp.full((1, tq), -jnp.inf, F32), jnp.zeros((VT_ROWS, tq), F32))
                     for _ in range(ATT_Q_HEADS))

    def finalize(qi, carry):
        outs = [acc[:HEAD_DIM] / acc[HEAD_DIM:HEAD_DIM + 1] for _, acc in carry]
        rows = pl.ds(pl.multiple_of(qi * tq, tq), tq)
        o_ref[rows, :] = jnp.concatenate(outs, axis=0).T.astype(o_ref.dtype)

    slot_a = (sa_ref, ma_ref)
    slot_b = (sb_ref, mb_ref)

    if ngrp == 1:
        def tile(qi, _):
            step(None, None, slot_a, qi, 0, ())
            finalize(qi, step(slot_a, 0, None, None, None, fresh()))
            return 0
    else:
        step(None, None, slot_a, 0, 0, ())

        def tile(qi, _):
            def body(i, carry):
                g = 2 * i
                carry = step(slot_a, g, slot_b, qi, g + 1, carry)
                return step(slot_b, g + 1, slot_a, qi, g + 2, carry)

            carry = lax.fori_loop(0, ngrp // 2 - 1, body, fresh())
            carry = step(slot_a, ngrp - 2, slot_b, qi, ngrp - 1, carry)
            carry = step(slot_b, ngrp - 1, slot_a, jnp.minimum(qi + 1, ntile - 1), 0, carry)
            finalize(qi, carry)
            return 0

    lax.fori_loop(0, ntile, tile, 0)


def _flash(qt, k, vt):
    B, _, nqt, _, tq = qt.shape
    L = nqt * tq
    ntile = math.gcd(nqt, 4)
    nkb = L // ATT_KB
    nsub = 2 if nkb % 4 == 0 and nkb >= 8 else 1
    ngrp = L // (ATT_KB * nsub)
    assert ngrp == 1 or ngrp % 2 == 0
    s_scratch = pltpu.VMEM((nsub * ATT_Q_HEADS, ATT_KB, tq), F32)
    m_scratch = pltpu.VMEM((nsub * ATT_Q_HEADS, 1, tq), F32)
    return pl.pallas_call(
        functools.partial(_flash_kernel, ngrp=ngrp, nsub=nsub, ntile=ntile),
        grid=(B, nqt // ntile),
        in_specs=[pl.BlockSpec((None, ATT_Q_HEADS, ntile, LANES, tq),
                               lambda b, i: (b, 0, i, 0, 0)),
                  pl.BlockSpec((None, L, LANES), lambda b, i: (b, 0, 0)),
                  pl.BlockSpec((None, L // ATT_KB, ATT_KV_HEADS * VT_ROWS, ATT_KB),
                               lambda b, i: (b, 0, 0, 0))],
        out_specs=pl.BlockSpec((None, ntile * tq, ATT_Q_W), lambda b, i: (b, i, 0)),
        out_shape=jax.ShapeDtypeStruct((B, L, ATT_Q_W), BF16),
        scratch_shapes=[s_scratch, m_scratch, s_scratch, m_scratch],
        compiler_params=_cparams(("parallel", "arbitrary"), 48),
        name="flash_attn",
    )(qt, k, vt)


def _conv3_tile(main, prev_t, next_t, w, first, last):
    tb = main.shape[0]
    w0, w1, w2 = w[0:1, :], w[1:2, :], w[2:3, :]
    body = pltpu.roll(main, 1, 0) * w0 + main * w1 + pltpu.roll(main, tb - 1, 0) * w2
    before = jnp.where(first, 0.0, prev_t[HALO - 1:HALO, :])
    after = jnp.where(last, 0.0, next_t[0:1, :])
    row_first = before * w0 + main[0:1, :] * w1 + main[1:2, :] * w2
    row_last = main[tb - 2:tb - 1, :] * w0 + main[tb - 1:tb, :] * w1 + after * w2
    row = lax.broadcasted_iota(jnp.int32, (HALO, 1), 0)
    head = jnp.where(row == 0, row_first, body[:HALO])
    tail = jnp.where(row == HALO - 1, row_last, body[tb - HALO:])
    return jnp.concatenate([head, body[HALO:tb - HALO], tail], axis=0)


def _halo_specs(tb, L, col_fn):
    nbh = L // HALO
    rh = tb // HALO
    main = pl.BlockSpec((None, tb, LANES), lambda b, w, s: (b, s, col_fn(w)))
    prev = pl.BlockSpec((None, HALO, LANES),
                        lambda b, w, s: (b, jnp.maximum(s * rh - 1, 0), col_fn(w)))
    nxt = pl.BlockSpec((None, HALO, LANES),
                       lambda b, w, s: (b, jnp.minimum((s + 1) * rh, nbh - 1), col_fn(w)))
    return [main, prev, nxt]


def _f32(ref):
    return ref[...].astype(F32)


def _hy_pro_kernel(x0_ref, x0p_ref, x0n_ref, x1_ref, x1p_ref, x1n_ref,
                   hv_ref, hvp_ref, hvn_ref, w0_ref, w1_ref, w2_ref,
                   b0_ref, b1_ref, b2_ref, z_ref, x0u_ref):
    first = pl.program_id(2) == 0
    last = pl.program_id(2) == pl.num_programs(2) - 1
    x0 = _conv3_tile(_f32(x0_ref), _f32(x0p_ref), _f32(x0n_ref), w0_ref[...], first, last) + b0_ref[...]
    x1 = _conv3_tile(_f32(x1_ref), _f32(x1p_ref), _f32(x1n_ref), w1_ref[...], first, last) + b1_ref[...]
    hv = _conv3_tile(_f32(hv_ref), _f32(hvp_ref), _f32(hvn_ref), w2_ref[...], first, last) + b2_ref[...]
    z_ref[...] = hv * x1
    x0u_ref[...] = x0


def _hy_prologue(p3, hcw, hcb):
    B, L, _ = p3.shape
    tb = min(1024, L)
    nw = HY_WIDTH // LANES
    in_specs = []
    for piece in range(3):
        in_specs += _halo_specs(tb, L, lambda w, piece=piece: HY_BLK + nw * piece + w)
    for piece in range(3):
        in_specs.append(pl.BlockSpec((3, LANES), lambda b, w, s, piece=piece: (0, nw * piece + w)))
    for piece in range(3):
        in_specs.append(pl.BlockSpec((1, LANES), lambda b, w, s, piece=piece: (0, nw * piece + w)))
    out_spec = pl.BlockSpec((None, tb, LANES), lambda b, w, s: (b, s, w))
    out_sh = jax.ShapeDtypeStruct((B, L, HY_WIDTH), F32)
    args = [p3] * 9 + [hcw] * 3 + [hcb] * 3
    return pl.pallas_call(
        _hy_pro_kernel,
        grid=(B, nw, L // tb),
        in_specs=in_specs,
        out_specs=[out_spec, out_spec],
        out_shape=[out_sh, out_sh],
        compiler_params=_cparams(("parallel", "parallel", "parallel"), 32),
        name="hyena_prologue",
    )(*args)


def _sconv_kernel(sb_ref, sc_ref, scp_ref, scn_ref, sh_ref, shp_ref, shn_ref, w_ref, o_ref):
    first = pl.program_id(2) == 0
    last = pl.program_id(2) == pl.num_programs(2) - 1
    y = _conv3_tile(_f32(sc_ref) * _f32(sh_ref), _f32(scp_ref) * _f32(shp_ref),
                    _f32(scn_ref) * _f32(shn_ref), w_ref[...], first, last)
    o_ref[...] = (_f32(sb_ref) * y).astype(o_ref.dtype)


def _sconv(p3, scw):
    B, L, _ = p3.shape
    tb = min(1024, L)
    nw = SC_WIDTH // LANES
    in_specs = [pl.BlockSpec((None, tb, LANES), lambda b, w, s: (b, s, SC_BLK + w))]
    in_specs += _halo_specs(tb, L, lambda w: SC_BLK + nw + w)
    in_specs += _halo_specs(tb, L, lambda w: SC_BLK + 2 * nw + w)
    in_specs.append(pl.BlockSpec((3, LANES), lambda b, w, s: (0, w)))
    return pl.pallas_call(
        _sconv_kernel,
        grid=(B, nw, L // tb),
        in_specs=in_specs,
        out_specs=pl.BlockSpec((None, tb, LANES), lambda b, w, s: (b, s, w)),
        out_shape=jax.ShapeDtypeStruct((B, L, SC_WIDTH), BF16),
        compiler_params=_cparams(("parallel", "parallel", "parallel"), 32),
        name="short_conv",
    )(*([p3] * 7 + [scw]))


def _hy_filter_kernel(z_ref, w1_ref, b1_ref, w2_ref, b2_ref, w3_ref, fr_ref, dl_ref,
                      k_ref, ss_ref, *, L):
    step = pl.program_id(0)
    tr = z_ref.shape[0]
    z = z_ref[...]
    h = jnp.sin(fr_ref[0:1, :] * (_mm3(z, w1_ref[...]) + b1_ref[...]))
    h = jnp.sin(fr_ref[1:2, :] * (_mm3(h, w2_ref[...]) + b2_ref[...]))
    h3 = _mm3(h, w3_ref[...])
    row = step * tr + lax.broadcasted_iota(jnp.int32, (tr, 1), 0)
    val = jnp.where(row < L, h3[:, :HY_WIDTH], h3[:, HY_WIDTH:])
    val = val * jnp.exp(-z[:, 0:1] * dl_ref[...])
    val = jnp.where(row == L, 0.0, val)
    k_ref[...] = val

    @pl.when(step == 0)
    def _():
        ss_ref[...] = jnp.zeros_like(ss_ref)

    ss_ref[...] += jnp.sum(val * val, axis=0, keepdims=True)


def _hy_filter_raw(z2, w1p, b1, w2, b2, w3, freq, deltas, L):
    n = 2 * L
    tr = min(512, n)
    H = HY_FILTER_HIDDEN
    full = lambda shape: pl.BlockSpec(shape, lambda i: (0, 0))
    return pl.pallas_call(
        functools.partial(_hy_filter_kernel, L=L),
        grid=(n // tr,),
        in_specs=[pl.BlockSpec((tr, H), lambda i: (i, 0)),
                  full((H, H)), full((1, H)), full((H, H)), full((1, H)),
                  full((H, 2 * HY_WIDTH)), full((2, H)), full((1, HY_WIDTH))],
        out_specs=[pl.BlockSpec((tr, HY_WIDTH), lambda i: (i, 0)), full((1, HY_WIDTH))],
        out_shape=[jax.ShapeDtypeStruct((n, HY_WIDTH), F32),
                   jax.ShapeDtypeStruct((1, HY_WIDTH), F32)],
        compiler_params=_cparams(("arbitrary",), 32),
        name="hyena_filter",
    )(z2, w1p, b1, w2, b2, w3, freq, deltas)


def _fft_dims(L):
    n = 2 * L
    n2 = 128 if n >= 32 * 128 else 64
    return n // n2, n2


def _fft_consts(n1, n2, r):
    n = n1 * n2
    a1 = 2.0 * np.pi * np.outer(np.arange(n1), np.arange(n1)) / n1
    c1, s1 = np.cos(a1), np.sin(a1)
    a2 = 2.0 * np.pi * np.outer(np.arange(n2), np.arange(n2)) / n2
    c2, s2 = np.cos(a2), np.sin(a2)
    f1 = np.block([[c1[:, :r], s1[:, :r]], [-s1[:, :r], c1[:, :r]]])
    f1_full = np.concatenate([c1, -s1], axis=0)
    m2 = np.block([[c2, s2], [-s2, c2]])
    m2i = np.block([[c2, -s2], [s2, c2]])
    fi = np.block([[c1[:r, :], -s1[:r, :]], [s1[:r, :], c1[:r, :]]])
    aw = 2.0 * np.pi * np.arange(n1) / n
    off = np.repeat(np.arange(FFT_CH), LANES)[None, :] * aw[:, None]
    step = np.broadcast_to((FFT_CH * aw)[:, None], (n1, LANES))
    f32 = lambda a: jnp.asarray(np.asarray(a, np.float32))
    return dict(f1=_np_split(f1), f1_full=_np_split(f1_full), m2=_np_split(m2),
                m2i=_np_split(m2i), fi=_np_split(fi),
                tw=(f32(np.cos(off)), f32(-np.sin(off)), f32(np.cos(step)), f32(-np.sin(step))))


def _cmul(a_re, a_im, b_re, b_im):
    return a_re * b_re - a_im * b_im, a_re * b_im + a_im * b_re


def _lane_tile(x, reps):
    return jnp.concatenate([x] * reps, axis=1)


def _lane_part(x, j):
    return x[:, LANES * j:LANES * (j + 1)]


def _chunk_twiddle(t_re, t_im, tw_refs):
    d_re, d_im, s_re, s_im = (ref[...] for ref in tw_refs)
    cur = _cmul(_lane_tile(t_re, FFT_CH), _lane_tile(t_im, FFT_CH), d_re, d_im)
    return cur, _cmul(t_re, t_im, s_re, s_im)


def _twiddle_init(n1):
    return jnp.ones((n1, LANES), F32), jnp.zeros((n1, LANES), F32)


def _fft_stage1(load_rows, a_ref, f_hi, f_lo, tw_refs, n1, n2, passes):
    def body(ci, carry):
        i0s, tws = [], []
        for u in range(FFT_SETS):
            tw, carry = _chunk_twiddle(*carry, tw_refs)
            tws.append(tw)
            i0s.append((ci * FFT_SETS + u) * FFT_CH)
        prods = [_mm_const(f_hi, f_lo, jnp.concatenate(
            [load_rows(i0 + j) for j in range(FFT_CH)], axis=1), passes)
            for i0 in i0s]
        for i0, (t_re, t_im), a in zip(i0s, tws, prods):
            o_re, o_im = _cmul(a[:n1], a[n1:], t_re, t_im)
            for j in range(FFT_CH):
                base = pl.multiple_of((i0 + j) * 2 * n1, 2 * n1)
                a_ref[pl.ds(base, n1), :] = _lane_part(o_re, j)
                a_ref[pl.ds(base + n1, n1), :] = _lane_part(o_im, j)
        return carry

    lax.fori_loop(0, n2 // (FFT_CH * FFT_SETS), body, _twiddle_init(n1))


def _load_spectrum_rows(a_ref, k1, n1, n2):
    ld = lambda off: a_ref[pl.ds(off, n2, stride=2 * n1), :]
    return jnp.concatenate(
        [jnp.concatenate([ld(k1 + j) for j in range(FFT_CH)], axis=1),
         jnp.concatenate([ld(n1 + k1 + j) for j in range(FFT_CH)], axis=1)], axis=0)


def _filter_fft_kernel(k_ref, ss_ref, f_hi_ref, f_lo_ref, m_hi_ref, m_lo_ref,
                       dre_ref, dim_ref, sre_ref, sim_ref, kr_ref, ki_ref, a_ref, *, n1, n2):
    c = pl.program_id(1)

    @pl.when(c == 0)
    def _():
        _fft_stage1(lambda i: k_ref[pl.ds(i, n1, stride=n2), :], a_ref,
                    f_hi_ref[...], f_lo_ref[...], (dre_ref, dim_ref, sre_ref, sim_ref),
                    n1, n2, FFT_PASSES_FILTER)

    scale = _lane_tile(lax.rsqrt(ss_ref[...] + NORM_EPS) * (1.0 / (n1 * n2)), FFT_CH)
    xin = _load_spectrum_rows(a_ref, c * FFT_CH, n1, n2)
    x = _mm_const(m_hi_ref[...], m_lo_ref[...], xin, FFT_PASSES_FILTER)
    kr_ref[...] = x[:n2] * scale
    ki_ref[...] = x[n2:] * scale


def _filter_fft(k2raw, sumsq, cst, n1, n2):
    n = n1 * n2
    nw = HY_WIDTH // LANES
    consts = [*cst["f1_full"], *cst["m2"], *cst["tw"]]
    full = lambda a: pl.BlockSpec(a.shape, lambda w, c: (0,) * a.ndim)
    out_spec = pl.BlockSpec((None, None, n2, FFT_CH * LANES), lambda w, c: (w, c, 0, 0))
    out_sh = jax.ShapeDtypeStruct((nw, n1 // FFT_CH, n2, FFT_CH * LANES), F32)
    return pl.pallas_call(
        functools.partial(_filter_fft_kernel, n1=n1, n2=n2),
        grid=(nw, n1 // FFT_CH),
        in_specs=[pl.BlockSpec((n, LANES), lambda w, c: (0, w)),
                  pl.BlockSpec((1, LANES), lambda w, c: (0, w))] + [full(a) for a in consts],
        out_specs=[out_spec, out_spec],
        out_shape=[out_sh, out_sh],
        scratch_shapes=[pltpu.VMEM((2 * n, LANES), F32)],
        compiler_params=_cparams(("parallel", "arbitrary"), 48),
        name="hyena_filter_fft",
    )(k2raw, sumsq, *consts)


def _hy_conv_kernel(z_ref, x0_ref, hb_ref, kr_ref, ki_ref, f_hi_ref, f_lo_ref,
                    m_hi_ref, m_lo_ref, mi_hi_ref, mi_lo_ref, fi_hi_ref, fi_lo_ref,
                    dre_ref, dim_ref, sre_ref, sim_ref, o_ref, a_ref, *, n1, n2):
    c = pl.program_id(2)
    r = n1 // 2
    tw_refs = (dre_ref, dim_ref, sre_ref, sim_ref)

    def load_pair(i):
        seq = pl.ds(i, r, stride=n2)
        return jnp.concatenate([z_ref[0, seq, :], z_ref[1, seq, :]], axis=0)

    @pl.when(c == 0)
    def _():
        _fft_stage1(load_pair, a_ref, f_hi_ref[...], f_lo_ref[...], tw_refs, n1, n2,
                    FFT_PASSES_DATA)

    k1s = [(c * FFT_SETS + u) * FFT_CH for u in range(FFT_SETS)]
    xs = [_mm_const(m_hi_ref[...], m_lo_ref[...], _load_spectrum_rows(a_ref, k1, n1, n2),
                    FFT_PASSES_DATA) for k1 in k1s]
    ys = [jnp.concatenate(_cmul(x[:n2], x[n2:], kr_ref[u], ki_ref[u]), axis=0)
          for u, x in enumerate(xs)]
    bqs = [_mm_const(mi_hi_ref[...], mi_lo_ref[...], y, FFT_PASSES_DATA) for y in ys]
    for k1, bq in zip(k1s, bqs):
        for j in range(FFT_CH):
            a_ref[pl.ds(k1 + j, n2, stride=2 * n1), :] = _lane_part(bq[:n2], j)
            a_ref[pl.ds(n1 + k1 + j, n2, stride=2 * n1), :] = _lane_part(bq[n2:], j)

    @pl.when(c == pl.num_programs(2) - 1)
    def _():
        hb = hb_ref[...]

        def body(ci, carry):
            i0s, bts = [], []
            for u in range(FFT_SETS):
                (t_re, t_im), carry = _chunk_twiddle(*carry, tw_refs)
                i0 = (ci * FFT_SETS + u) * FFT_CH
                rows = [pl.multiple_of((i0 + j) * 2 * n1, 2 * n1) for j in range(FFT_CH)]
                br = jnp.concatenate([a_ref[pl.ds(b0, n1), :] for b0 in rows], axis=1)
                bi = jnp.concatenate([a_ref[pl.ds(b0 + n1, n1), :] for b0 in rows], axis=1)
                bts.append(jnp.concatenate([br * t_re + bi * t_im, bi * t_re - br * t_im],
                                           axis=0))
                i0s.append(i0)
            ys = [_mm_const(fi_hi_ref[...], fi_lo_ref[...], bt, FFT_PASSES_DATA) for bt in bts]
            for i0, y in zip(i0s, ys):
                for j in range(FFT_CH):
                    seq = pl.ds(i0 + j, r, stride=n2)
                    for row in range(2):
                        conv = _lane_part(y[row * r:(row + 1) * r], j)
                        o_ref[row, seq, :] = (x0_ref[row, seq, :]
                                              * (conv + z_ref[row, seq, :] * hb)
                                              ).astype(o_ref.dtype)
            return carry

        lax.fori_loop(0, n2 // (FFT_CH * FFT_SETS), body, _twiddle_init(n1))


def _hy_conv(z, x0u, hbias, kf_re, kf_im, cst, n1, n2):
    B, L, _ = z.shape
    assert B % 2 == 0
    nw = HY_WIDTH // LANES
    seq = pl.BlockSpec((2, L, LANES), lambda b, w, c: (b, 0, w), pipeline_mode=pl.Buffered(1))
    kf_spec = pl.BlockSpec((None, FFT_SETS, n2, FFT_CH * LANES), lambda b, w, c: (w, c, 0, 0))
    full = lambda a: pl.BlockSpec(a.shape, lambda b, w, c: (0,) * a.ndim)
    consts = [*cst["f1"], *cst["m2"], *cst["m2i"], *cst["fi"], *cst["tw"]]
    return pl.pallas_call(
        functools.partial(_hy_conv_kernel, n1=n1, n2=n2),
        grid=(B // 2, nw, n1 // (FFT_CH * FFT_SETS)),
        in_specs=[seq, seq, pl.BlockSpec((1, LANES), lambda b, w, c: (0, w)), kf_spec, kf_spec]
                 + [full(a) for a in consts],
        out_specs=seq,
        out_shape=jax.ShapeDtypeStruct((B, L, HY_WIDTH), F32),
        scratch_shapes=[pltpu.VMEM((2 * n1 * n2, LANES), F32)],
        compiler_params=_cparams(("parallel", "parallel", "arbitrary"), 56),
        name="hyena_conv",
    )(z, x0u, hbias, kf_re, kf_im, *consts)


T_D0, T_D1, T_WQF, T_WQB, T_WKF, T_WKB, T_GCF, T_GCB, T_BD = range(9)


def _ret_tables_kernel(rl_ref, rh_ref, t_ref):
    C = RET_CHUNK
    i = lax.broadcasted_iota(jnp.int32, (C, LANES), 0).astype(F32)
    jn = lax.broadcasted_iota(jnp.int32, (C, LANES), 1).astype(F32)
    log_g = lambda e: jnp.log1p(-jnp.exp2(-e))
    diff = i - jn
    for hp in range(2):
        lgf = log_g(rh_ref[hp, 0:1, :])
        lgb = log_g(rh_ref[hp, 1:2, :])
        fwd = jnp.exp(jnp.where(diff >= 0, diff, 0.0) * lgf)
        bwd = jnp.exp(jnp.where(diff < 0, -diff, 0.0) * lgb)
        t_ref[T_D0 + hp] = jnp.where(diff >= 0, fwd, bwd)
    lf = log_g(rl_ref[0:1, :])
    lb = log_g(rl_ref[1:2, :])
    t_ref[T_WQF] = jnp.exp((i + 1.0) * lf)
    t_ref[T_WQB] = jnp.exp((C - i) * lb)
    t_ref[T_WKF] = jnp.exp((C - 1.0 - i) * lf)
    t_ref[T_WKB] = jnp.exp(i * lb)
    bd = jnp.where((i < HEAD_DIM) == (jn < HEAD_DIM), 1.0, 0.0)
    t_ref[T_GCF] = jnp.exp(C * lf) * bd
    t_ref[T_GCB] = jnp.exp(C * lb) * bd
    t_ref[T_BD] = bd


def _ret_tables(rde):
    per_lane = jnp.repeat(rde.reshape(2, 2, 2), HEAD_DIM, axis=2)
    per_lane = per_lane.transpose(1, 0, 2)
    per_head = jnp.broadcast_to(rde.reshape(2, 2, 2, 1), (2, 2, 2, LANES))
    per_head = per_head.transpose(1, 2, 0, 3)
    return pl.pallas_call(
        _ret_tables_kernel,
        grid=(2,),
        in_specs=[pl.BlockSpec((None, 2, LANES), lambda j: (j, 0, 0)),
                  pl.BlockSpec((None, 2, 2, LANES), lambda j: (j, 0, 0, 0))],
        out_specs=pl.BlockSpec((None, 9, RET_CHUNK, LANES), lambda j: (j, 0, 0, 0)),
        out_shape=jax.ShapeDtypeStruct((2, 9, RET_CHUNK, LANES), F32),
        compiler_params=_cparams(("parallel",), 32),
        name="retention_tables",
    )(per_lane, per_head)


def _ret_state_kernel(rk_ref, rv_ref, c_ref, s_ref, t_ref, st_ref, r_ref):
    C = RET_CHUNK
    nch = rk_ref.shape[0] // C

    @pl.when(pl.program_id(2) == 0)
    def _():
        r_ref[...] = jnp.zeros_like(r_ref)

    wkb = t_ref[T_WKB]
    gcb = t_ref[T_GCB]
    bd = t_ref[T_BD]
    kvs = []
    for n in range(nch):
        sl = slice(n * C, (n + 1) * C)
        k = _rope(rk_ref[sl, :].astype(F32), c_ref[sl, :], s_ref[sl, :]) * (HEAD_DIM ** -0.5)
        kvs.append(lax.dot_general((k * wkb).astype(BF16), rv_ref[sl, :],
                                   (((0,), (0,)), ((), ())), preferred_element_type=F32))
    state = r_ref[...]
    for n in range(nch - 1, -1, -1):
        st_ref[n] = state
        state = gcb * state + bd * kvs[n]
    r_ref[...] = state


def _ret_main_kernel(rq_ref, rk_ref, rv_ref, rg_ref, c_ref, s_ref, t_ref, st_ref, j_ref,
                     o_ref, sf_ref):
    C = RET_CHUNK
    nch = rq_ref.shape[0] // C

    @pl.when(pl.program_id(2) == 0)
    def _():
        sf_ref[...] = jnp.zeros_like(sf_ref)

    lane = lax.broadcasted_iota(jnp.int32, (1, LANES), 1)
    lo64 = lane < HEAD_DIM
    j_bf = j_ref[...]
    chunks = [slice(n * C, (n + 1) * C) for n in range(nch)]
    nt = (((1,), (1,)), ((), ()))
    tn = (((0,), (0,)), ((), ()))
    q, k, vb = [], [], []
    for sl in chunks:
        cs, sn = c_ref[sl, :], s_ref[sl, :]
        q.append(_rope(rq_ref[sl, :].astype(F32), cs, sn))
        k.append(_rope(rk_ref[sl, :].astype(F32), cs, sn) * (HEAD_DIM ** -0.5))
        vb.append(rv_ref[sl, :])
    scores = []
    for n in range(nch):
        qb = q[n].astype(BF16)
        kb = k[n].astype(BF16)
        zero = jnp.zeros_like(qb)
        scores.append([lax.dot_general(jnp.where(lo64, qb, zero), kb, nt,
                                       preferred_element_type=F32),
                       lax.dot_general(jnp.where(lo64, zero, qb), kb, nt,
                                       preferred_element_type=F32)])
    kvs = [lax.dot_general((k[n] * t_ref[T_WKF]).astype(BF16), vb[n], tn,
                           preferred_element_type=F32) for n in range(nch)]
    cross_b = [jnp.dot((q[n] * t_ref[T_WQB]).astype(BF16), st_ref[n].astype(BF16),
                       preferred_element_type=F32) for n in range(nch)]
    intra = []
    for n in range(nch):
        parts = [jnp.dot((scores[n][hp] * t_ref[T_D0 + hp]).astype(BF16), vb[n],
                         preferred_element_type=F32) for hp in range(2)]
        intra.append(jnp.where(lo64, parts[0], parts[1]))
    state = sf_ref[...]
    states = []
    for n in range(nch):
        states.append(state.astype(BF16))
        state = t_ref[T_GCF] * state + t_ref[T_BD] * kvs[n]
    sf_ref[...] = state
    outs = [intra[n] + cross_b[n]
            + jnp.dot((q[n] * t_ref[T_WQF]).astype(BF16), states[n], preferred_element_type=F32)
            for n in range(nch)]
    mean_sq = [_head_mean_sq(o, j_bf) for o in outs]
    for n, sl in enumerate(chunks):
        ret = outs[n] * lax.rsqrt(mean_sq[n] + NORM_EPS)
        g = rg_ref[sl, :].astype(F32)
        o_ref[sl, :] = (ret * (g * _sigmoid(g))).astype(o_ref.dtype)


def _retention(p3, cos_t, sin_t, tables, j_bf):
    B, L, _ = p3.shape
    C = RET_CHUNK
    tb = min(1024, L)
    nblk = L // tb
    nch = tb // C
    nw = RET_W // LANES
    col = lambda piece: (lambda b, j, s: (b, s, RET_BLK + nw * piece + j))
    colr = lambda piece: (lambda b, j, s: (b, nblk - 1 - s, RET_BLK + nw * piece + j))
    seq = lambda fn: pl.BlockSpec((None, tb, LANES), fn)
    tab_spec = pl.BlockSpec((None, 9, C, LANES), lambda b, j, s: (j, 0, 0, 0))
    states = pl.pallas_call(
        _ret_state_kernel,
        grid=(B, nw, nblk),
        in_specs=[seq(colr(1)), seq(colr(2)),
                  pl.BlockSpec((tb, LANES), lambda b, j, s: (nblk - 1 - s, 0)),
                  pl.BlockSpec((tb, LANES), lambda b, j, s: (nblk - 1 - s, 0)),
                  tab_spec],
        out_specs=pl.BlockSpec((None, None, nch, LANES, LANES),
                               lambda b, j, s: (b, j, nblk - 1 - s, 0, 0)),
        out_shape=jax.ShapeDtypeStruct((B, nw, L // C, LANES, LANES), F32),
        scratch_shapes=[pltpu.VMEM((LANES, LANES), F32)],
        compiler_params=_cparams(("parallel", "parallel", "arbitrary"), 32),
        name="retention_state",
    )(p3, p3, cos_t, sin_t, tables)
    return pl.pallas_call(
        _ret_main_kernel,
        grid=(B, nw, nblk),
        in_specs=[seq(col(0)), seq(col(1)), seq(col(2)), seq(col(3)),
                  pl.BlockSpec((tb, LANES), lambda b, j, s: (s, 0)),
                  pl.BlockSpec((tb, LANES), lambda b, j, s: (s, 0)),
                  tab_spec,
                  pl.BlockSpec((None, None, nch, LANES, LANES), lambda b, j, s: (b, j, s, 0, 0)),
                  pl.BlockSpec((LANES, LANES), lambda b, j, s: (0, 0))],
        out_specs=pl.BlockSpec((None, tb, LANES), lambda b, j, s: (b, s, j)),
        out_shape=jax.ShapeDtypeStruct((B, L, RET_W), BF16),
        scratch_shapes=[pltpu.VMEM((LANES, LANES), F32)],
        compiler_params=_cparams(("parallel", "parallel", "arbitrary"), 32),
        name="retention_main",
    )(p3, p3, p3, p3, cos_t, sin_t, tables, states, j_bf)


def _merge_kernel(oa_ref, ob_ref, oc_ref, od_ref, g0_ref, g1_ref, g2_ref, g3_ref,
                  wb_ref, wo_ref, x_ref, gn_ref, h_ref):
    tm = x_ref.shape[0]
    nparts = 2 if tm % (2 * HALO) == 0 else 1
    parts = [slice(i * tm // nparts, (i + 1) * tm // nparts) for i in range(nparts)]
    branches = ((oa_ref, g0_ref), (ob_ref, g1_ref), (oc_ref, g2_ref), (od_ref, g3_ref))
    proj = [[jnp.dot(br[rows, :].astype(BF16), wb_ref[n], preferred_element_type=F32)
             for n, (br, _) in enumerate(branches)] for rows in parts]
    merged = []
    for pi, rows in enumerate(parts):
        acc = None
        for n, (_, gr) in enumerate(branches):
            gate = (0.5 * jnp.tanh(gr[rows, :]) + 0.5).astype(F32)
            term = gate * proj[pi][n]
            acc = term if acc is None else acc + term
        merged.append(acc.astype(BF16))
    ys = [jnp.dot(m, wo_ref[...], preferred_element_type=F32) for m in merged]
    for rows, y in zip(parts, ys):
        ms = jnp.mean(y * y, axis=-1, keepdims=True)
        h_ref[rows, :] = x_ref[rows, :] + y * lax.rsqrt(ms + NORM_EPS) * gn_ref[...]


def _merge(oa, ob, oc, od, p2, wb_bf, wo_bf, x2, gn):
    T = x2.shape[0]
    tm = min(1024, T)
    br = pl.BlockSpec((tm, BRANCH_W), lambda i: (i, 0))
    gate = lambda n: pl.BlockSpec((tm, D_MODEL), lambda i, n=n: (i, GATE_BLK + n))
    row = pl.BlockSpec((tm, D_MODEL), lambda i: (i, 0))
    return pl.pallas_call(
        _merge_kernel,
        grid=(T // tm,),
        in_specs=[br, br, br, br, gate(0), gate(1), gate(2), gate(3),
                  pl.BlockSpec((N_BRANCH, BRANCH_W, D_MODEL), lambda i: (0, 0, 0)),
                  pl.BlockSpec((D_MODEL, D_MODEL), lambda i: (0, 0)),
                  row, pl.BlockSpec((1, D_MODEL), lambda i: (0, 0))],
        out_specs=row,
        out_shape=jax.ShapeDtypeStruct((T, D_MODEL), F32),
        compiler_params=_cparams(("parallel",), 48),
        name="merge_out_proj",
    )(oa, ob, oc, od, p2, p2, p2, p2, wb_bf, wo_bf, x2, gn)


def _ffn_kernel(h_ref, g2_ref, wg_ref, wu_ref, wd_ref, g3_ref, o_ref):
    h = h_ref[...]
    ms = jnp.mean(h * h, axis=-1, keepdims=True)
    hn = (h * lax.rsqrt(ms + NORM_EPS) * g2_ref[...]).astype(BF16)
    f = None
    for lo, hi in _col_chunks(D_FF, 2):
        g = jnp.dot(hn, wg_ref[:, lo:hi], preferred_element_type=F32)
        u = jnp.dot(hn, wu_ref[:, lo:hi], preferred_element_type=F32)
        a = (g * _sigmoid(g) * u).astype(BF16)
        part = jnp.dot(a, wd_ref[lo:hi, :], preferred_element_type=F32)
        f = part if f is None else f + part
    ms = jnp.mean(f * f, axis=-1, keepdims=True)
    o_ref[...] = h + f * lax.rsqrt(ms + NORM_EPS) * g3_ref[...]


def _ffn(h2, g2, wg_bf, wu_bf, wd_bf, g3):
    T = h2.shape[0]
    tm = min(512, T)
    row = pl.BlockSpec((tm, D_MODEL), lambda i: (i, 0))
    vec = pl.BlockSpec((1, D_MODEL), lambda i: (0, 0))
    return pl.pallas_call(
        _ffn_kernel,
        grid=(T // tm,),
        in_specs=[row, vec,
                  pl.BlockSpec((D_MODEL, D_FF), lambda i: (0, 0)),
                  pl.BlockSpec((D_MODEL, D_FF), lambda i: (0, 0)),
                  pl.BlockSpec((D_FF, D_MODEL), lambda i: (0, 0)),
                  vec],
        out_specs=row,
        out_shape=jax.ShapeDtypeStruct((T, D_MODEL), F32),
        compiler_params=_cparams(("parallel",), 48),
        name="ffn",
    )(h2, g2, wg_bf, wu_bf, wd_bf, g3)


def _rope_tables(L):
    rows = L // GRID_W
    r = jnp.repeat(jnp.arange(rows, dtype=F32), GRID_W)
    c = jnp.tile(jnp.arange(GRID_W, dtype=F32), rows)
    inv = ROPE_BASE ** (-jnp.arange(ROPE_FREQS, dtype=F32) / ROPE_FREQS)
    ar = r[:, None] * inv
    ac = c[:, None] * inv
    cos64 = jnp.concatenate([jnp.cos(ar), jnp.cos(ar), jnp.cos(ac), jnp.cos(ac)], axis=1)
    sin64 = jnp.concatenate([-jnp.sin(ar), jnp.sin(ar), -jnp.sin(ac), jnp.sin(ac)], axis=1)
    return jnp.tile(cos64, (1, 2)), jnp.tile(sin64, (1, 2))


def _filter_features(L):
    t = np.linspace(0.0, 1.0, L)[:, None]
    f = np.linspace(1e-4, HY_BANDS - 1, HY_BANDS)
    ang = (2.0 * math.pi / L) * np.arange(L)[:, None] * f[None, :]
    z = np.concatenate([t, np.cos(ang), -np.sin(ang)], axis=-1)
    z = np.pad(z, ((0, 0), (0, HY_FILTER_HIDDEN - HY_EMB)))
    idx = np.concatenate([np.arange(L), [0], np.arange(L - 1, 0, -1)])
    return jnp.asarray(z[idx].astype(np.float32))


def _head_mean_matrix():
    i = np.arange(LANES)
    j = ((i[:, None] // HEAD_DIM) == (i[None, :] // HEAD_DIM)).astype(np.float32) / HEAD_DIM
    return jnp.asarray(j.astype(BF16))


def _halve_gate_columns(w_in_bf):
    scale = jnp.where(jnp.arange(IN_COLS) >= GATE_OFF, 0.5, 1.0).astype(BF16)
    return w_in_bf * scale


def _trunk(x, wts, fft_dims=None):
    B, L, _ = x.shape
    T = B * L
    n1, n2 = fft_dims or _fft_dims(L)
    cst = _fft_consts(n1, n2, n1 // 2)
    cos_t, sin_t = _rope_tables(L)
    z2 = _filter_features(L)
    deltas = jnp.abs(jnp.linspace(math.log(HY_TARGET) / HY_SLOW_DECAY,
                                  math.log(HY_TARGET) / HY_FAST_DECAY, HY_WIDTH, dtype=F32))[None, :]
    j_bf = _head_mean_matrix()
    depth = wts["w_in"].shape[0]
    x2 = x.reshape(T, D_MODEL)
    for l in range(depth):
        ng = wts["norm_gains"][l]
        w1p = jnp.pad(wts["hy_w1"][l], ((0, HY_FILTER_HIDDEN - HY_EMB), (0, 0)))
        k2raw, sumsq = _hy_filter_raw(z2, w1p, wts["hy_b1"][l][None, :], wts["hy_w2"][l],
                                      wts["hy_b2"][l][None, :], wts["hy_w3"][l],
                                      wts["hy_freq"][l], deltas, L)
        kf_re, kf_im = _filter_fft(k2raw, sumsq, cst, n1, n2)

        p2 = _in_proj(x2, ng[0][None, :], wts["w_in_bf"][l])
        p3 = p2.reshape(B, L, IN_COLS)

        qg = jnp.tile(wts["qk_norm"][l, 0], 2)[None, :]
        kg = jnp.tile(wts["qk_norm"][l, 1], 2)[None, :]
        qt, k, vt = _attn_prep(p3, cos_t, sin_t, qg, kg, j_bf)
        out_a = _flash(qt, k, vt)

        z, x0u = _hy_prologue(p3, wts["hy_conv_w"][l], wts["hy_conv_b"][l][None, :])
        out_b = _hy_conv(z, x0u, wts["hy_bias"][l][None, :], kf_re, kf_im, cst, n1, n2)

        tables = _ret_tables(wts["ret_decay_exp"][l])
        out_c = _retention(p3, cos_t, sin_t, tables, j_bf)

        out_d = _sconv(p3, wts["sc_conv_w"][l])

        h2 = _merge(out_a.reshape(T, BRANCH_W), out_b.reshape(T, BRANCH_W),
                    out_c.reshape(T, BRANCH_W), out_d.reshape(T, BRANCH_W),
                    p2, wts["w_branch_bf"][l], wts["w_out_bf"][l], x2, ng[1][None, :])
        x2 = _ffn(h2, ng[2][None, :], wts["w_gate_bf"][l], wts["w_up_bf"][l],
                  wts["w_ffn_out_bf"][l], ng[3][None, :])
    return x2.reshape(B, L, D_MODEL)


def kernel(x_prompt, x_sample, norm_gains, w_in, qk_norm, hy_conv_w, hy_conv_b, hy_w1, hy_b1, hy_w2,
           hy_b2, hy_w3, hy_freq, hy_bias, ret_decay_exp, sc_conv_w, w_branch, w_out, w_ffn_in,
           w_ffn_out):
    wts = dict(norm_gains=norm_gains, w_in=w_in, qk_norm=qk_norm, hy_conv_w=hy_conv_w,
               hy_conv_b=hy_conv_b, hy_w1=hy_w1, hy_b1=hy_b1, hy_w2=hy_w2, hy_b2=hy_b2,
               hy_w3=hy_w3, hy_freq=hy_freq, hy_bias=hy_bias, ret_decay_exp=ret_decay_exp,
               sc_conv_w=sc_conv_w,
               w_in_bf=_halve_gate_columns(w_in.astype(BF16)), w_branch_bf=w_branch.astype(BF16),
               w_out_bf=w_out.astype(BF16), w_gate_bf=w_ffn_in[..., :D_FF].astype(BF16),
               w_up_bf=w_ffn_in[..., D_FF:].astype(BF16), w_ffn_out_bf=w_ffn_out.astype(BF16))
    return _trunk(x_prompt, wts), _trunk(x_sample, wts)
```

```python
import functools
import math

import numpy as np
import jax
import jax.numpy as jnp
from jax import lax
from jax.experimental import pallas as pl
from jax.experimental.pallas import tpu as pltpu

F32 = jnp.float32
BF16 = jnp.bfloat16

D_MODEL = 1024
GRID_W = 64
N_BRANCH = 4
BRANCH_W = 256
HEAD_DIM = 64
ATT_Q_HEADS = 4
ATT_KV_HEADS = 2
ROPE_BASE = 10000.0
ROPE_FREQS = HEAD_DIM // 4
HY_WIDTH = BRANCH_W
HY_EMB = 33
HY_BANDS = (HY_EMB - 1) // 2
HY_FILTER_HIDDEN = 64
HY_FAST_DECAY = 0.3
HY_SLOW_DECAY = 1.5
HY_TARGET = 1e-2
RET_HEADS = 4
RET_W = RET_HEADS * HEAD_DIM
RET_CHUNK = 128
SC_WIDTH = BRANCH_W
D_FF = 2816
NORM_EPS = 1e-6

ATT_Q_W = ATT_Q_HEADS * HEAD_DIM
ATT_KV_W = ATT_KV_HEADS * HEAD_DIM
A_K_OFF = ATT_Q_W
A_V_OFF = A_K_OFF + ATT_KV_W
HY_OFF = A_V_OFF + ATT_KV_W
RET_OFF = HY_OFF + 3 * HY_WIDTH
SC_OFF = RET_OFF + 4 * RET_W
GATE_OFF = SC_OFF + 3 * SC_WIDTH
IN_COLS = GATE_OFF + N_BRANCH * D_MODEL

LANES = 128
SUBLANES = 8
MXU_W = 256
HALO = 2 * SUBLANES
HY_BLK = HY_OFF // LANES
RET_BLK = RET_OFF // LANES
SC_BLK = SC_OFF // LANES
GATE_BLK = GATE_OFF // D_MODEL
ATT_KB = 512
ATT_TQ = 256
VT_ROWS = HEAD_DIM + HALO
Q_SCALE = HEAD_DIM ** -0.5 * math.log2(math.e)
FFT_PASSES_FILTER = 3
FFT_PASSES_DATA = 1
FFT_CH = 8
FFT_SETS = 1
MIB = 1 << 20


def _cparams(sem, vmem_mib):
    return pltpu.CompilerParams(dimension_semantics=sem, vmem_limit_bytes=vmem_mib * MIB)


def _sigmoid(x):
    return 0.5 * jnp.tanh(0.5 * x) + 0.5


def _split(x):
    hi = x.astype(BF16)
    lo = (x - hi.astype(F32)).astype(BF16)
    return hi, lo


def _np_split(a64):
    a32 = np.asarray(a64, np.float32)
    hi = a32.astype(BF16)
    lo = (a32 - hi.astype(np.float32)).astype(BF16)
    return jnp.asarray(hi), jnp.asarray(lo)


def _mm_const(a_hi, a_lo, x, passes):
    if passes == 1:
        return jnp.dot(a_hi, x.astype(BF16), preferred_element_type=F32)
    xh, xl = _split(x)
    out = jnp.dot(a_hi, xh, preferred_element_type=F32)
    if passes >= 3:
        out = out + jnp.dot(a_lo, xh, preferred_element_type=F32)
        out = out + jnp.dot(a_hi, xl, preferred_element_type=F32)
    return out


def _mm3(a, b):
    ah, al = _split(a)
    bh, bl = _split(b)
    out = jnp.dot(ah, bh, preferred_element_type=F32)
    out = out + jnp.dot(al, bh, preferred_element_type=F32)
    return out + jnp.dot(ah, bl, preferred_element_type=F32)


def _head_mean_sq(x, j_bf):
    hi, lo = _split(x * x)
    return (jnp.dot(hi, j_bf, preferred_element_type=F32)
            + jnp.dot(lo, j_bf, preferred_element_type=F32))


def _rope(x, c, s):
    lane = lax.broadcasted_iota(jnp.int32, (1, LANES), 1)
    is_b = (lane & 16) != 0
    partner = jnp.where(is_b, pltpu.roll(x, 16, 1), pltpu.roll(x, LANES - 16, 1))
    return x * c + partner * s


def _col_chunks(n, parts):
    tiles = n // MXU_W
    bounds = [MXU_W * (tiles * i // parts) for i in range(parts + 1)]
    return list(zip(bounds[:-1], bounds[1:]))


def _in_proj_kernel(x_ref, g_ref, w_ref, o_ref):
    x = x_ref[...]
    ms = jnp.mean(x * x, axis=-1, keepdims=True)
    xn = (x * lax.rsqrt(ms + NORM_EPS) * g_ref[...]).astype(BF16)
    for lo, hi in _col_chunks(IN_COLS, 4):
        o_ref[:, lo:hi] = jnp.dot(xn, w_ref[:, lo:hi],
                                  preferred_element_type=F32).astype(o_ref.dtype)


def _in_proj(x2, g, w_bf):
    T = x2.shape[0]
    tm = min(512, T)
    return pl.pallas_call(
        _in_proj_kernel,
        grid=(T // tm,),
        in_specs=[pl.BlockSpec((tm, D_MODEL), lambda i: (i, 0)),
                  pl.BlockSpec((1, D_MODEL), lambda i: (0, 0)),
                  pl.BlockSpec((D_MODEL, IN_COLS), lambda i: (0, 0))],
        out_specs=pl.BlockSpec((tm, IN_COLS), lambda i: (i, 0)),
        out_shape=jax.ShapeDtypeStruct((T, IN_COLS), BF16),
        compiler_params=_cparams(("parallel",), 48),
        name="in_proj",
    )(x2, g, w_bf)


def _attn_prep_kernel(p_ref, c_ref, s_ref, qg_ref, kg_ref, j_ref, qt_ref, k_ref, vt_ref):
    c = c_ref[...]
    s = s_ref[...]
    j_bf = j_ref[...]
    tm = p_ref.shape[0]
    tq = qt_ref.shape[-1]
    zeros = jnp.zeros((HEAD_DIM, tm), BF16)
    for blk in range(2):
        q = p_ref[:, LANES * blk:LANES * (blk + 1)].astype(F32)
        qn = q * lax.rsqrt(_head_mean_sq(q, j_bf) + NORM_EPS) * qg_ref[...]
        qt = (_rope(qn, c, s) * Q_SCALE).astype(BF16).T
        lo, hi = qt[:HEAD_DIM], qt[HEAD_DIM:]
        if blk == 0:
            heads = (jnp.concatenate([lo, zeros], axis=0), jnp.concatenate([hi, zeros], axis=0))
        else:
            heads = (jnp.concatenate([zeros, lo], axis=0), jnp.concatenate([zeros, hi], axis=0))
        for hp, qh in enumerate(heads):
            for t in range(tm // tq):
                qt_ref[2 * blk + hp, t] = qh[:, t * tq:(t + 1) * tq]
    k = p_ref[:, A_K_OFF:A_K_OFF + LANES].astype(F32)
    kn = k * lax.rsqrt(_head_mean_sq(k, j_bf) + NORM_EPS) * kg_ref[...]
    k_ref[...] = _rope(kn, c, s).astype(BF16)
    v = p_ref[:, A_V_OFF:A_V_OFF + LANES]
    ones = jnp.ones((VT_ROWS - HEAD_DIM, ATT_KB), BF16)
    for t in range(tm // ATT_KB):
        vt = v[t * ATT_KB:(t + 1) * ATT_KB, :].T
        vt_ref[t] = jnp.concatenate([vt[:HEAD_DIM], ones, vt[HEAD_DIM:], ones], axis=0)


def _attn_prep(p3, cos_t, sin_t, qg, kg, j_bf):
    B, L, _ = p3.shape
    tm = min(1024, L)
    tq = min(ATT_TQ, L)
    tab = pl.BlockSpec((tm, LANES), lambda b, i: (i, 0))
    vec = pl.BlockSpec((1, LANES), lambda b, i: (0, 0))
    return pl.pallas_call(
        _attn_prep_kernel,
        grid=(B, L // tm),
        in_specs=[pl.BlockSpec((None, tm, HY_OFF), lambda b, i: (b, i, 0)), tab, tab, vec, vec,
                  pl.BlockSpec((LANES, LANES), lambda b, i: (0, 0))],
        out_specs=[pl.BlockSpec((None, ATT_Q_HEADS, tm // tq, LANES, tq),
                                lambda b, i: (b, 0, i, 0, 0)),
                   pl.BlockSpec((None, tm, LANES), lambda b, i: (b, i, 0)),
                   pl.BlockSpec((None, tm // ATT_KB, ATT_KV_HEADS * VT_ROWS, ATT_KB),
                                lambda b, i: (b, i, 0, 0))],
        out_shape=[jax.ShapeDtypeStruct((B, ATT_Q_HEADS, L // tq, LANES, tq), BF16),
                   jax.ShapeDtypeStruct((B, L, LANES), BF16),
                   jax.ShapeDtypeStruct((B, L // ATT_KB, ATT_KV_HEADS * VT_ROWS, ATT_KB), BF16)],
        compiler_params=_cparams(("parallel", "parallel"), 32),
        name="attn_prep",
    )(p3, cos_t, sin_t, qg, kg, j_bf)


def _flash_kernel(qt_ref, k_ref, vt_ref, o_ref, sa_ref, ma_ref, sb_ref, mb_ref, *,
                  ngrp, nsub, ntile):
    tq = ATT_TQ
    group = ATT_Q_HEADS // ATT_KV_HEADS

    def produce(slot, qi, g, t, h):
        s_ref, mx_ref = slot
        st = pl.multiple_of((g * nsub + t) * ATT_KB, ATT_KB)
        s = jnp.dot(k_ref[pl.ds(st, ATT_KB), :], qt_ref[h, qi],
                    preferred_element_type=F32)
        s_ref[t * ATT_Q_HEADS + h] = s
        mx_ref[t * ATT_Q_HEADS + h] = jnp.max(s, axis=0, keepdims=True)

    def consume(slot, g, t, h, state):
        s_ref, mx_ref = slot
        m, acc = state
        m_new = jnp.maximum(m, mx_ref[t * ATT_Q_HEADS + h])
        alpha = jnp.exp2(m - m_new)
        p = jnp.exp2((s_ref[t * ATT_Q_HEADS + h] - m_new).astype(BF16))
        j = h // group
        vt = vt_ref[g * nsub + t, VT_ROWS * j:VT_ROWS * (j + 1), :]
        return m_new, alpha * acc + jnp.dot(vt, p, preferred_element_type=F32)

    def step(slot_in, g_in, slot_out, qi_out, g_out, carry):
        carry = list(carry)
        for t in range(nsub):
            for h in range(ATT_Q_HEADS):
                if slot_out is not None:
                    produce(slot_out, qi_out, g_out, t, h)
                if slot_in is not None:
                    carry[h] = consume(slot_in, g_in, t, h, carry[h])
        return tuple(carry)

    def fresh():
        return tuple((jnp.full((1, tq), -jnp.inf, F32), jnp.zeros((VT_ROWS, tq), F32))
                     for _ in range(ATT_Q_HEADS))

    def finalize(qi, carry):
        outs = [acc[:HEAD_DIM] / acc[HEAD_DIM:HEAD_DIM + 1] for _, acc in carry]
        rows = pl.ds(pl.multiple_of(qi * tq, tq), tq)
        o_ref[rows, :] = jnp.concatenate(outs, axis=0).T.astype(o_ref.dtype)

    slot_a = (sa_ref, ma_ref)
    slot_b = (sb_ref, mb_ref)

    if ngrp == 1:
        def tile(qi, _):
            step(None, None, slot_a, qi, 0, ())
            finalize(qi, step(slot_a, 0, None, None, None, fresh()))
            return 0
    else:
        step(None, None, slot_a, 0, 0, ())

        def tile(qi, _):
            def body(i, carry):
                g = 2 * i
                carry = step(slot_a, g, slot_b, qi, g + 1, carry)
                return step(slot_b, g + 1, slot_a, qi, g + 2, carry)

            carry = lax.fori_loop(0, ngrp // 2 - 1, body, fresh())
            carry = step(slot_a, ngrp - 2, slot_b, qi, ngrp - 1, carry)
            carry = step(slot_b, ngrp - 1, slot_a, jnp.minimum(qi + 1, ntile - 1), 0, carry)
            finalize(qi, carry)
            return 0

    lax.fori_loop(0, ntile, tile, 0)


def _flash(qt, k, vt):
    B, _, nqt, _, tq = qt.shape
    L = nqt * tq
    ntile = math.gcd(nqt, 4)
    nkb = L // ATT_KB
    nsub = 2 if nkb % 4 == 0 and nkb >= 8 else 1
    ngrp = L // (ATT_KB * nsub)
    assert ngrp == 1 or ngrp % 2 == 0
    s_scratch = pltpu.VMEM((nsub * ATT_Q_HEADS, ATT_KB, tq), F32)
    m_scratch = pltpu.VMEM((nsub * ATT_Q_HEADS, 1, tq), F32)
    return pl.pallas_call(
        functools.partial(_flash_kernel, ngrp=ngrp, nsub=nsub, ntile=ntile),
        grid=(B, nqt // ntile),
        in_specs=[pl.BlockSpec((None, ATT_Q_HEADS, ntile, LANES, tq),
                               lambda b, i: (b, 0, i, 0, 0)),
                  pl.BlockSpec((None, L, LANES), lambda b, i: (b, 0, 0)),
                  pl.BlockSpec((None, L // ATT_KB, ATT_KV_HEADS * VT_ROWS, ATT_KB),
                               lambda b, i: (b, 0, 0, 0))],
        out_specs=pl.BlockSpec((None, ntile * tq, ATT_Q_W), lambda b, i: (b, i, 0)),
        out_shape=jax.ShapeDtypeStruct((B, L, ATT_Q_W), BF16),
        scratch_shapes=[s_scratch, m_scratch, s_scratch, m_scratch],
        compiler_params=_cparams(("parallel", "arbitrary"), 48),
        name="flash_attn",
    )(qt, k, vt)


def _conv3_tile(main, prev_t, next_t, w, first, last):
    tb = main.shape[0]
    w0, w1, w2 = w[0:1, :], w[1:2, :], w[2:3, :]
    body = pltpu.roll(main, 1, 0) * w0 + main * w1 + pltpu.roll(main, tb - 1, 0) * w2
    before = jnp.where(first, 0.0, prev_t[HALO - 1:HALO, :])
    after = jnp.where(last, 0.0, next_t[0:1, :])
    row_first = before * w0 + main[0:1, :] * w1 + main[1:2, :] * w2
    row_last = main[tb - 2:tb - 1, :] * w0 + main[tb - 1:tb, :] * w1 + after * w2
    row = lax.broadcasted_iota(jnp.int32, (HALO, 1), 0)
    head = jnp.where(row == 0, row_first, body[:HALO])
    tail = jnp.where(row == HALO - 1, row_last, body[tb - HALO:])
    return jnp.concatenate([head, body[HALO:tb - HALO], tail], axis=0)


def _halo_specs(tb, L, col_fn):
    nbh = L // HALO
    rh = tb // HALO
    main = pl.BlockSpec((None, tb, LANES), lambda b, w, s: (b, s, col_fn(w)))
    prev = pl.BlockSpec((None, HALO, LANES),
                        lambda b, w, s: (b, jnp.maximum(s * rh - 1, 0), col_fn(w)))
    nxt = pl.BlockSpec((None, HALO, LANES),
                       lambda b, w, s: (b, jnp.minimum((s + 1) * rh, nbh - 1), col_fn(w)))
    return [main, prev, nxt]


def _f32(ref):
    return ref[...].astype(F32)


def _hy_pro_kernel(x0_ref, x0p_ref, x0n_ref, x1_ref, x1p_ref, x1n_ref,
                   hv_ref, hvp_ref, hvn_ref, w0_ref, w1_ref, w2_ref,
                   b0_ref, b1_ref, b2_ref, z_ref, x0u_ref):
    first = pl.program_id(2) == 0
    last = pl.program_id(2) == pl.num_programs(2) - 1
    x0 = _conv3_tile(_f32(x0_ref), _f32(x0p_ref), _f32(x0n_ref), w0_ref[...], first, last) + b0_ref[...]
    x1 = _conv3_tile(_f32(x1_ref), _f32(x1p_ref), _f32(x1n_ref), w1_ref[...], first, last) + b1_ref[...]
    hv = _conv3_tile(_f32(hv_ref), _f32(hvp_ref), _f32(hvn_ref), w2_ref[...], first, last) + b2_ref[...]
    z_ref[...] = hv * x1
    x0u_ref[...] = x0


def _hy_prologue(p3, hcw, hcb):
    B, L, _ = p3.shape
    tb = min(1024, L)
    nw = HY_WIDTH // LANES
    in_specs = []
    for piece in range(3):
        in_specs += _halo_specs(tb, L, lambda w, piece=piece: HY_BLK + nw * piece + w)
    for piece in range(3):
        in_specs.append(pl.BlockSpec((3, LANES), lambda b, w, s, piece=piece: (0, nw * piece + w)))
    for piece in range(3):
        in_specs.append(pl.BlockSpec((1, LANES), lambda b, w, s, piece=piece: (0, nw * piece + w)))
    out_spec = pl.BlockSpec((None, tb, LANES), lambda b, w, s: (b, s, w))
    out_sh = jax.ShapeDtypeStruct((B, L, HY_WIDTH), F32)
    args = [p3] * 9 + [hcw] * 3 + [hcb] * 3
    return pl.pallas_call(
        _hy_pro_kernel,
        grid=(B, nw, L // tb),
        in_specs=in_specs,
        out_specs=[out_spec, out_spec],
        out_shape=[out_sh, out_sh],
        compiler_params=_cparams(("parallel", "parallel", "parallel"), 32),
        name="hyena_prologue",
    )(*args)


def _sconv_kernel(sb_ref, sc_ref, scp_ref, scn_ref, sh_ref, shp_ref, shn_ref, w_ref, o_ref):
    first = pl.program_id(2) == 0
    last = pl.program_id(2) == pl.num_programs(2) - 1
    y = _conv3_tile(_f32(sc_ref) * _f32(sh_ref), _f32(scp_ref) * _f32(shp_ref),
                    _f32(scn_ref) * _f32(shn_ref), w_ref[...], first, last)
    o_ref[...] = (_f32(sb_ref) * y).astype(o_ref.dtype)


def _sconv(p3, scw):
    B, L, _ = p3.shape
    tb = min(1024, L)
    nw = SC_WIDTH // LANES
    in_specs = [pl.BlockSpec((None, tb, LANES), lambda b, w, s: (b, s, SC_BLK + w))]
    in_specs += _halo_specs(tb, L, lambda w: SC_BLK + nw + w)
    in_specs += _halo_specs(tb, L, lambda w: SC_BLK + 2 * nw + w)
    in_specs.append(pl.BlockSpec((3, LANES), lambda b, w, s: (0, w)))
    return pl.pallas_call(
        _sconv_kernel,
        grid=(B, nw, L // tb),
        in_specs=in_specs,
        out_specs=pl.BlockSpec((None, tb, LANES), lambda b, w, s: (b, s, w)),
        out_shape=jax.ShapeDtypeStruct((B, L, SC_WIDTH), BF16),
        compiler_params=_cparams(("parallel", "parallel", "parallel"), 32),
        name="short_conv",
    )(*([p3] * 7 + [scw]))


def _hy_filter_kernel(z_ref, w1_ref, b1_ref, w2_ref, b2_ref, w3_ref, fr_ref, dl_ref,
                      k_ref, ss_ref, *, L):
    step = pl.program_id(0)
    tr = z_ref.shape[0]
    z = z_ref[...]
    h = jnp.sin(fr_ref[0:1, :] * (_mm3(z, w1_ref[...]) + b1_ref[...]))
    h = jnp.sin(fr_ref[1:2, :] * (_mm3(h, w2_ref[...]) + b2_ref[...]))
    h3 = _mm3(h, w3_ref[...])
    row = step * tr + lax.broadcasted_iota(jnp.int32, (tr, 1), 0)
    val = jnp.where(row < L, h3[:, :HY_WIDTH], h3[:, HY_WIDTH:])
    val = val * jnp.exp(-z[:, 0:1] * dl_ref[...])
    val = jnp.where(row == L, 0.0, val)
    k_ref[...] = val

    @pl.when(step == 0)
    def _():
        ss_ref[...] = jnp.zeros_like(ss_ref)

    ss_ref[...] += jnp.sum(val * val, axis=0, keepdims=True)


def _hy_filter_raw(z2, w1p, b1, w2, b2, w3, freq, deltas, L):
    n = 2 * L
    tr = min(512, n)
    H = HY_FILTER_HIDDEN
    full = lambda shape: pl.BlockSpec(shape, lambda i: (0, 0))
    return pl.pallas_call(
        functools.partial(_hy_filter_kernel, L=L),
        grid=(n // tr,),
        in_specs=[pl.BlockSpec((tr, H), lambda i: (i, 0)),
                  full((H, H)), full((1, H)), full((H, H)), full((1, H)),
                  full((H, 2 * HY_WIDTH)), full((2, H)), full((1, HY_WIDTH))],
        out_specs=[pl.BlockSpec((tr, HY_WIDTH), lambda i: (i, 0)), full((1, HY_WIDTH))],
        out_shape=[jax.ShapeDtypeStruct((n, HY_WIDTH), F32),
                   jax.ShapeDtypeStruct((1, HY_WIDTH), F32)],
        compiler_params=_cparams(("arbitrary",), 32),
        name="hyena_filter",
    )(z2, w1p, b1, w2, b2, w3, freq, deltas)


def _fft_dims(L):
    n = 2 * L
    n2 = 128 if n >= 32 * 128 else 64
    return n // n2, n2


def _fft_consts(n1, n2, r):
    n = n1 * n2
    a1 = 2.0 * np.pi * np.outer(np.arange(n1), np.arange(n1)) / n1
    c1, s1 = np.cos(a1), np.sin(a1)
    a2 = 2.0 * np.pi * np.outer(np.arange(n2), np.arange(n2)) / n2
    c2, s2 = np.cos(a2), np.sin(a2)
    f1 = np.block([[c1[:, :r], s1[:, :r]], [-s1[:, :r], c1[:, :r]]])
    f1_full = np.concatenate([c1, -s1], axis=0)
    m2 = np.block([[c2, s2], [-s2, c2]])
    m2i = np.block([[c2, -s2], [s2, c2]])
    fi = np.block([[c1[:r, :], -s1[:r, :]], [s1[:r, :], c1[:r, :]]])
    aw = 2.0 * np.pi * np.arange(n1) / n
    off = np.repeat(np.arange(FFT_CH), LANES)[None, :] * aw[:, None]
    step = np.broadcast_to((FFT_CH * aw)[:, None], (n1, LANES))
    f32 = lambda a: jnp.asarray(np.asarray(a, np.float32))
    phase = (np.arange(n2)[None, None, :]
             * (np.arange(n1)[:, None, None] + n1 * np.arange(n2)[None, :, None])) % n
    ck, sk = np.cos(2.0 * np.pi * phase / n), np.sin(2.0 * np.pi * phase / n)
    mk = np.concatenate([np.concatenate([ck, sk], axis=2),
                         np.concatenate([-sk, ck], axis=2)], axis=1)
    return dict(f1=_np_split(f1), f1_full=_np_split(f1_full), m2=_np_split(m2),
                m2i=_np_split(m2i), fi=_np_split(fi),
                mk=jnp.asarray(np.asarray(mk, np.float32).astype(BF16)),
                tw=(f32(np.cos(off)), f32(-np.sin(off)), f32(np.cos(step)), f32(-np.sin(step))))


def _cmul(a_re, a_im, b_re, b_im):
    return a_re * b_re - a_im * b_im, a_re * b_im + a_im * b_re


def _lane_tile(x, reps):
    return jnp.concatenate([x] * reps, axis=1)


def _lane_part(x, j):
    return x[:, LANES * j:LANES * (j + 1)]


def _chunk_twiddle(t_re, t_im, tw_refs):
    d_re, d_im, s_re, s_im = (ref[...] for ref in tw_refs)
    cur = _cmul(_lane_tile(t_re, FFT_CH), _lane_tile(t_im, FFT_CH), d_re, d_im)
    return cur, _cmul(t_re, t_im, s_re, s_im)


def _twiddle_init(n1):
    return jnp.ones((n1, LANES), F32), jnp.zeros((n1, LANES), F32)


def _fft_stage1(load_rows, a_ref, f_hi, f_lo, tw_refs, n1, n2, passes):
    def body(ci, carry):
        i0s, tws = [], []
        for u in range(FFT_SETS):
            tw, carry = _chunk_twiddle(*carry, tw_refs)
            tws.append(tw)
            i0s.append((ci * FFT_SETS + u) * FFT_CH)
        prods = [_mm_const(f_hi, f_lo, jnp.concatenate(
            [load_rows(i0 + j) for j in range(FFT_CH)], axis=1), passes)
            for i0 in i0s]
        for i0, (t_re, t_im), a in zip(i0s, tws, prods):
            o_re, o_im = _cmul(a[:n1], a[n1:], t_re, t_im)
            for j in range(FFT_CH):
                base = pl.multiple_of((i0 + j) * 2 * n1, 2 * n1)
                a_ref[pl.ds(base, n1), :] = _lane_part(o_re, j)
                a_ref[pl.ds(base + n1, n1), :] = _lane_part(o_im, j)
        return carry

    lax.fori_loop(0, n2 // (FFT_CH * FFT_SETS), body, _twiddle_init(n1))


def _load_spectrum_rows(a_ref, k1, n1, n2):
    ld = lambda off: a_ref[pl.ds(off, n2, stride=2 * n1), :]
    return jnp.concatenate(
        [jnp.concatenate([ld(k1 + j) for j in range(FFT_CH)], axis=1),
         jnp.concatenate([ld(n1 + k1 + j) for j in range(FFT_CH)], axis=1)], axis=0)


def _filter_fft_kernel(k_ref, ss_ref, f_hi_ref, f_lo_ref, m_hi_ref, m_lo_ref,
                       dre_ref, dim_ref, sre_ref, sim_ref, kr_ref, ki_ref, a_ref, *, n1, n2):
    c = pl.program_id(1)

    @pl.when(c == 0)
    def _():
        _fft_stage1(lambda i: k_ref[pl.ds(i, n1, stride=n2), :], a_ref,
                    f_hi_ref[...], f_lo_ref[...], (dre_ref, dim_ref, sre_ref, sim_ref),
                    n1, n2, FFT_PASSES_FILTER)

    scale = _lane_tile(lax.rsqrt(ss_ref[...] + NORM_EPS) * (1.0 / (n1 * n2)), FFT_CH)
    xin = _load_spectrum_rows(a_ref, c * FFT_CH, n1, n2)
    x = _mm_const(m_hi_ref[...], m_lo_ref[...], xin, FFT_PASSES_FILTER)
    kr_ref[...] = x[:n2] * scale
    ki_ref[...] = x[n2:] * scale


def _filter_fft(k2raw, sumsq, cst, n1, n2):
    n = n1 * n2
    nw = HY_WIDTH // LANES
    consts = [*cst["f1_full"], *cst["m2"], *cst["tw"]]
    full = lambda a: pl.BlockSpec(a.shape, lambda w, c: (0,) * a.ndim)
    out_spec = pl.BlockSpec((None, None, n2, FFT_CH * LANES), lambda w, c: (w, c, 0, 0))
    out_sh = jax.ShapeDtypeStruct((nw, n1 // FFT_CH, n2, FFT_CH * LANES), F32)
    return pl.pallas_call(
        functools.partial(_filter_fft_kernel, n1=n1, n2=n2),
        grid=(nw, n1 // FFT_CH),
        in_specs=[pl.BlockSpec((n, LANES), lambda w, c: (0, w)),
                  pl.BlockSpec((1, LANES), lambda w, c: (0, w))] + [full(a) for a in consts],
        out_specs=[out_spec, out_spec],
        out_shape=[out_sh, out_sh],
        scratch_shapes=[pltpu.VMEM((2 * n, LANES), F32)],
        compiler_params=_cparams(("parallel", "arbitrary"), 48),
        name="hyena_filter_fft",
    )(k2raw, sumsq, *consts)


def _hy_conv_kernel(z_ref, x0_ref, hb_ref, kr_ref, ki_ref, f_ref, fi_ref, mk_ref,
                    o_ref, a_ref, *, n1, n2):
    c = pl.program_id(2)
    r = n1 // 2
    tn = (((0,), (0,)), ((), ()))

    @pl.when(c == 0)
    def _():
        f = f_ref[...]

        def body(ci, _):
            i0 = ci * FFT_CH
            cols = []
            for j in range(FFT_CH):
                seq = pl.ds(i0 + j, r, stride=n2)
                cols.append(jnp.concatenate([z_ref[0, seq, :], z_ref[1, seq, :]], axis=0))
            a = jnp.dot(f, jnp.concatenate(cols, axis=1).astype(BF16),
                        preferred_element_type=F32)
            for j in range(FFT_CH):
                base = pl.multiple_of((i0 + j) * 2 * n1, 2 * n1)
                a_ref[pl.ds(base, 2 * n1), :] = _lane_part(a, j)
            return 0

        lax.fori_loop(0, n2 // FFT_CH, body, 0)

    k1s = [c * FFT_CH + j for j in range(FFT_CH)]
    spec = [jnp.dot(mk_ref[j], jnp.concatenate(
        [a_ref[pl.ds(k1, n2, stride=2 * n1), :], a_ref[pl.ds(n1 + k1, n2, stride=2 * n1), :]],
        axis=0).astype(BF16), preferred_element_type=F32) for j, k1 in enumerate(k1s)]
    prods = []
    for j, x in enumerate(spec):
        kr = kr_ref[:, LANES * j:LANES * (j + 1)]
        ki = ki_ref[:, LANES * j:LANES * (j + 1)]
        prods.append(jnp.concatenate(_cmul(x[:n2], x[n2:], kr, ki), axis=0).astype(BF16))
    back = [lax.dot_general(mk_ref[j], y, tn, preferred_element_type=F32)
            for j, y in enumerate(prods)]
    for k1, bq in zip(k1s, back):
        a_ref[pl.ds(k1, n2, stride=2 * n1), :] = bq[:n2]
        a_ref[pl.ds(n1 + k1, n2, stride=2 * n1), :] = bq[n2:]

    @pl.when(c == pl.num_programs(2) - 1)
    def _():
        hb = hb_ref[...]
        fi = fi_ref[...]

        def body(ci, _):
            i0 = ci * FFT_CH
            rows = [pl.multiple_of((i0 + j) * 2 * n1, 2 * n1) for j in range(FFT_CH)]
            bt = jnp.concatenate([a_ref[pl.ds(b0, 2 * n1), :] for b0 in rows], axis=1)
            y = jnp.dot(fi, bt.astype(BF16), preferred_element_type=F32)
            for j in range(FFT_CH):
                seq = pl.ds(i0 + j, r, stride=n2)
                for row in range(2):
                    conv = _lane_part(y[row * r:(row + 1) * r], j)
                    o_ref[row, seq, :] = (x0_ref[row, seq, :] * (conv + z_ref[row, seq, :] * hb)
                                          ).astype(o_ref.dtype)
            return 0

        lax.fori_loop(0, n2 // FFT_CH, body, 0)


def _hy_conv(z, x0u, hbias, kf_re, kf_im, cst, n1, n2):
    B, L, _ = z.shape
    assert B % 2 == 0
    nw = HY_WIDTH // LANES
    seq = pl.BlockSpec((2, L, LANES), lambda b, w, c: (b, 0, w), pipeline_mode=pl.Buffered(1))
    kf_spec = pl.BlockSpec((None, None, n2, FFT_CH * LANES), lambda b, w, c: (w, c, 0, 0))
    full = lambda a: pl.BlockSpec(a.shape, lambda b, w, c: (0,) * a.ndim)
    f_hi, fi_hi, mk = cst["f1"][0], cst["fi"][0], cst["mk"]
    return pl.pallas_call(
        functools.partial(_hy_conv_kernel, n1=n1, n2=n2),
        grid=(B // 2, nw, n1 // FFT_CH),
        in_specs=[seq, seq, pl.BlockSpec((1, LANES), lambda b, w, c: (0, w)), kf_spec, kf_spec,
                  full(f_hi), full(fi_hi),
                  pl.BlockSpec((FFT_CH, 2 * n2, 2 * n2), lambda b, w, c: (c, 0, 0))],
        out_specs=seq,
        out_shape=jax.ShapeDtypeStruct((B, L, HY_WIDTH), F32),
        scratch_shapes=[pltpu.VMEM((2 * n1 * n2, LANES), F32)],
        compiler_params=_cparams(("parallel", "parallel", "arbitrary"), 56),
        name="hyena_conv",
    )(z, x0u, hbias, kf_re, kf_im, f_hi, fi_hi, mk)


T_D0, T_D1, T_WQF, T_WQB, T_WKF, T_WKB, T_GCF, T_GCB, T_BD = range(9)


def _ret_tables_kernel(rl_ref, rh_ref, t_ref):
    C = RET_CHUNK
    i = lax.broadcasted_iota(jnp.int32, (C, LANES), 0).astype(F32)
    jn = lax.broadcasted_iota(jnp.int32, (C, LANES), 1).astype(F32)
    log_g = lambda e: jnp.log1p(-jnp.exp2(-e))
    diff = i - jn
    for hp in range(2):
        lgf = log_g(rh_ref[hp, 0:1, :])
        lgb = log_g(rh_ref[hp, 1:2, :])
        fwd = jnp.exp(jnp.where(diff >= 0, diff, 0.0) * lgf)
        bwd = jnp.exp(jnp.where(diff < 0, -diff, 0.0) * lgb)
        t_ref[T_D0 + hp] = jnp.where(diff >= 0, fwd, bwd)
    lf = log_g(rl_ref[0:1, :])
    lb = log_g(rl_ref[1:2, :])
    t_ref[T_WQF] = jnp.exp((i + 1.0) * lf)
    t_ref[T_WQB] = jnp.exp((C - i) * lb)
    t_ref[T_WKF] = jnp.exp((C - 1.0 - i) * lf)
    t_ref[T_WKB] = jnp.exp(i * lb)
    bd = jnp.where((i < HEAD_DIM) == (jn < HEAD_DIM), 1.0, 0.0)
    t_ref[T_GCF] = jnp.exp(C * lf) * bd
    t_ref[T_GCB] = jnp.exp(C * lb) * bd
    t_ref[T_BD] = bd


def _ret_tables(rde):
    per_lane = jnp.repeat(rde.reshape(2, 2, 2), HEAD_DIM, axis=2)
    per_lane = per_lane.transpose(1, 0, 2)
    per_head = jnp.broadcast_to(rde.reshape(2, 2, 2, 1), (2, 2, 2, LANES))
    per_head = per_head.transpose(1, 2, 0, 3)
    return pl.pallas_call(
        _ret_tables_kernel,
        grid=(2,),
        in_specs=[pl.BlockSpec((None, 2, LANES), lambda j: (j, 0, 0)),
                  pl.BlockSpec((None, 2, 2, LANES), lambda j: (j, 0, 0, 0))],
        out_specs=pl.BlockSpec((None, 9, RET_CHUNK, LANES), lambda j: (j, 0, 0, 0)),
        out_shape=jax.ShapeDtypeStruct((2, 9, RET_CHUNK, LANES), F32),
        compiler_params=_cparams(("parallel",), 32),
        name="retention_tables",
    )(per_lane, per_head)


def _ret_state_kernel(rk_ref, rv_ref, c_ref, s_ref, t_ref, st_ref, r_ref):
    C = RET_CHUNK
    nch = rk_ref.shape[0] // C

    @pl.when(pl.program_id(2) == 0)
    def _():
        r_ref[...] = jnp.zeros_like(r_ref)

    wkb = t_ref[T_WKB]
    gcb = t_ref[T_GCB]
    bd = t_ref[T_BD]
    kvs = []
    for n in range(nch):
        sl = slice(n * C, (n + 1) * C)
        k = _rope(rk_ref[sl, :].astype(F32), c_ref[sl, :], s_ref[sl, :]) * (HEAD_DIM ** -0.5)
        kvs.append(lax.dot_general((k * wkb).astype(BF16), rv_ref[sl, :],
                                   (((0,), (0,)), ((), ())), preferred_element_type=F32))
    state = r_ref[...]
    for n in range(nch - 1, -1, -1):
        st_ref[n] = state
        state = gcb * state + bd * kvs[n]
    r_ref[...] = state


def _ret_main_kernel(rq_ref, rk_ref, rv_ref, rg_ref, c_ref, s_ref, t_ref, st_ref, j_ref,
                     o_ref, sf_ref):
    C = RET_CHUNK
    nch = rq_ref.shape[0] // C

    @pl.when(pl.program_id(2) == 0)
    def _():
        sf_ref[...] = jnp.zeros_like(sf_ref)

    lane = lax.broadcasted_iota(jnp.int32, (1, LANES), 1)
    lo64 = lane < HEAD_DIM
    j_bf = j_ref[...]
    chunks = [slice(n * C, (n + 1) * C) for n in range(nch)]
    nt = (((1,), (1,)), ((), ()))
    tn = (((0,), (0,)), ((), ()))
    q, k, vb = [], [], []
    for sl in chunks:
        cs, sn = c_ref[sl, :], s_ref[sl, :]
        q.append(_rope(rq_ref[sl, :].astype(F32), cs, sn))
        k.append(_rope(rk_ref[sl, :].astype(F32), cs, sn) * (HEAD_DIM ** -0.5))
        vb.append(rv_ref[sl, :])
    scores = []
    for n in range(nch):
        qb = q[n].astype(BF16)
        kb = k[n].astype(BF16)
        zero = jnp.zeros_like(qb)
        scores.append([lax.dot_general(jnp.where(lo64, qb, zero), kb, nt,
                                       preferred_element_type=F32),
                       lax.dot_general(jnp.where(lo64, zero, qb), kb, nt,
                                       preferred_element_type=F32)])
    kvs = [lax.dot_general((k[n] * t_ref[T_WKF]).astype(BF16), vb[n], tn,
                           preferred_element_type=F32) for n in range(nch)]
    cross_b = [jnp.dot((q[n] * t_ref[T_WQB]).astype(BF16), st_ref[n].astype(BF16),
                       preferred_element_type=F32) for n in range(nch)]
    intra = []
    for n in range(nch):
        parts = [jnp.dot((scores[n][hp] * t_ref[T_D0 + hp]).astype(BF16), vb[n],
                         preferred_element_type=F32) for hp in range(2)]
        intra.append(jnp.where(lo64, parts[0], parts[1]))
    state = sf_ref[...]
    states = []
    for n in range(nch):
        states.append(state.astype(BF16))
        state = t_ref[T_GCF] * state + t_ref[T_BD] * kvs[n]
    sf_ref[...] = state
    outs = [intra[n] + cross_b[n]
            + jnp.dot((q[n] * t_ref[T_WQF]).astype(BF16), states[n], preferred_element_type=F32)
            for n in range(nch)]
    mean_sq = [_head_mean_sq(o, j_bf) for o in outs]
    for n, sl in enumerate(chunks):
        ret = outs[n] * lax.rsqrt(mean_sq[n] + NORM_EPS)
        g = rg_ref[sl, :].astype(F32)
        o_ref[sl, :] = (ret * (g * _sigmoid(g))).astype(o_ref.dtype)


def _retention(p3, cos_t, sin_t, tables, j_bf):
    B, L, _ = p3.shape
    C = RET_CHUNK
    tb = min(1024, L)
    nblk = L // tb
    nch = tb // C
    nw = RET_W // LANES
    col = lambda piece: (lambda b, j, s: (b, s, RET_BLK + nw * piece + j))
    colr = lambda piece: (lambda b, j, s: (b, nblk - 1 - s, RET_BLK + nw * piece + j))
    seq = lambda fn: pl.BlockSpec((None, tb, LANES), fn)
    tab_spec = pl.BlockSpec((None, 9, C, LANES), lambda b, j, s: (j, 0, 0, 0))
    states = pl.pallas_call(
        _ret_state_kernel,
        grid=(B, nw, nblk),
        in_specs=[seq(colr(1)), seq(colr(2)),
                  pl.BlockSpec((tb, LANES), lambda b, j, s: (nblk - 1 - s, 0)),
                  pl.BlockSpec((tb, LANES), lambda b, j, s: (nblk - 1 - s, 0)),
                  tab_spec],
        out_specs=pl.BlockSpec((None, None, nch, LANES, LANES),
                               lambda b, j, s: (b, j, nblk - 1 - s, 0, 0)),
        out_shape=jax.ShapeDtypeStruct((B, nw, L // C, LANES, LANES), F32),
        scratch_shapes=[pltpu.VMEM((LANES, LANES), F32)],
        compiler_params=_cparams(("parallel", "parallel", "arbitrary"), 32),
        name="retention_state",
    )(p3, p3, cos_t, sin_t, tables)
    return pl.pallas_call(
        _ret_main_kernel,
        grid=(B, nw, nblk),
        in_specs=[seq(col(0)), seq(col(1)), seq(col(2)), seq(col(3)),
                  pl.BlockSpec((tb, LANES), lambda b, j, s: (s, 0)),
                  pl.BlockSpec((tb, LANES), lambda b, j, s: (s, 0)),
                  tab_spec,
                  pl.BlockSpec((None, None, nch, LANES, LANES), lambda b, j, s: (b, j, s, 0, 0)),
                  pl.BlockSpec((LANES, LANES), lambda b, j, s: (0, 0))],
        out_specs=pl.BlockSpec((None, tb, LANES), lambda b, j, s: (b, s, j)),
        out_shape=jax.ShapeDtypeStruct((B, L, RET_W), BF16),
        scratch_shapes=[pltpu.VMEM((LANES, LANES), F32)],
        compiler_params=_cparams(("parallel", "parallel", "arbitrary"), 32),
        name="retention_main",
    )(p3, p3, p3, p3, cos_t, sin_t, tables, states, j_bf)


def _merge_kernel(oa_ref, ob_ref, oc_ref, od_ref, g0_ref, g1_ref, g2_ref, g3_ref,
                  wb_ref, wo_ref, x_ref, gn_ref, h_ref):
    tm = x_ref.shape[0]
    nparts = 2 if tm % (2 * HALO) == 0 else 1
    parts = [slice(i * tm // nparts, (i + 1) * tm // nparts) for i in range(nparts)]
    branches = ((oa_ref, g0_ref), (ob_ref, g1_ref), (oc_ref, g2_ref), (od_ref, g3_ref))
    proj = [[jnp.dot(br[rows, :].astype(BF16), wb_ref[n], preferred_element_type=F32)
             for n, (br, _) in enumerate(branches)] for rows in parts]
    merged = []
    for pi, rows in enumerate(parts):
        acc = None
        for n, (_, gr) in enumerate(branches):
            gate = (0.5 * jnp.tanh(gr[rows, :]) + 0.5).astype(F32)
            term = gate * proj[pi][n]
            acc = term if acc is None else acc + term
        merged.append(acc.astype(BF16))
    ys = [jnp.dot(m, wo_ref[...], preferred_element_type=F32) for m in merged]
    for rows, y in zip(parts, ys):
        ms = jnp.mean(y * y, axis=-1, keepdims=True)
        h_ref[rows, :] = x_ref[rows, :] + y * lax.rsqrt(ms + NORM_EPS) * gn_ref[...]


def _merge(oa, ob, oc, od, p2, wb_bf, wo_bf, x2, gn):
    T = x2.shape[0]
    tm = min(1024, T)
    br = pl.BlockSpec((tm, BRANCH_W), lambda i: (i, 0))
    gate = lambda n: pl.BlockSpec((tm, D_MODEL), lambda i, n=n: (i, GATE_BLK + n))
    row = pl.BlockSpec((tm, D_MODEL), lambda i: (i, 0))
    return pl.pallas_call(
        _merge_kernel,
        grid=(T // tm,),
        in_specs=[br, br, br, br, gate(0), gate(1), gate(2), gate(3),
                  pl.BlockSpec((N_BRANCH, BRANCH_W, D_MODEL), lambda i: (0, 0, 0)),
                  pl.BlockSpec((D_MODEL, D_MODEL), lambda i: (0, 0)),
                  row, pl.BlockSpec((1, D_MODEL), lambda i: (0, 0))],
        out_specs=row,
        out_shape=jax.ShapeDtypeStruct((T, D_MODEL), F32),
        compiler_params=_cparams(("parallel",), 48),
        name="merge_out_proj",
    )(oa, ob, oc, od, p2, p2, p2, p2, wb_bf, wo_bf, x2, gn)


def _ffn_kernel(h_ref, g2_ref, wg_ref, wu_ref, wd_ref, g3_ref, o_ref):
    h = h_ref[...]
    ms = jnp.mean(h * h, axis=-1, keepdims=True)
    hn = (h * lax.rsqrt(ms + NORM_EPS) * g2_ref[...]).astype(BF16)
    f = None
    for lo, hi in _col_chunks(D_FF, 2):
        g = jnp.dot(hn, wg_ref[:, lo:hi], preferred_element_type=F32)
        u = jnp.dot(hn, wu_ref[:, lo:hi], preferred_element_type=F32)
        a = (g * _sigmoid(g) * u).astype(BF16)
        part = jnp.dot(a, wd_ref[lo:hi, :], preferred_element_type=F32)
        f = part if f is None else f + part
    ms = jnp.mean(f * f, axis=-1, keepdims=True)
    o_ref[...] = h + f * lax.rsqrt(ms + NORM_EPS) * g3_ref[...]


def _ffn(h2, g2, wg_bf, wu_bf, wd_bf, g3):
    T = h2.shape[0]
    tm = min(512, T)
    row = pl.BlockSpec((tm, D_MODEL), lambda i: (i, 0))
    vec = pl.BlockSpec((1, D_MODEL), lambda i: (0, 0))
    return pl.pallas_call(
        _ffn_kernel,
        grid=(T // tm,),
        in_specs=[row, vec,
                  pl.BlockSpec((D_MODEL, D_FF), lambda i: (0, 0)),
                  pl.BlockSpec((D_MODEL, D_FF), lambda i: (0, 0)),
                  pl.BlockSpec((D_FF, D_MODEL), lambda i: (0, 0)),
                  vec],
        out_specs=row,
        out_shape=jax.ShapeDtypeStruct((T, D_MODEL), F32),
        compiler_params=_cparams(("parallel",), 48),
        name="ffn",
    )(h2, g2, wg_bf, wu_bf, wd_bf, g3)


def _rope_tables(L):
    rows = L // GRID_W
    r = jnp.repeat(jnp.arange(rows, dtype=F32), GRID_W)
    c = jnp.tile(jnp.arange(GRID_W, dtype=F32), rows)
    inv = ROPE_BASE ** (-jnp.arange(ROPE_FREQS, dtype=F32) / ROPE_FREQS)
    ar = r[:, None] * inv
    ac = c[:, None] * inv
    cos64 = jnp.concatenate([jnp.cos(ar), jnp.cos(ar), jnp.cos(ac), jnp.cos(ac)], axis=1)
    sin64 = jnp.concatenate([-jnp.sin(ar), jnp.sin(ar), -jnp.sin(ac), jnp.sin(ac)], axis=1)
    return jnp.tile(cos64, (1, 2)), jnp.tile(sin64, (1, 2))


def _filter_features(L):
    t = np.linspace(0.0, 1.0, L)[:, None]
    f = np.linspace(1e-4, HY_BANDS - 1, HY_BANDS)
    ang = (2.0 * math.pi / L) * np.arange(L)[:, None] * f[None, :]
    z = np.concatenate([t, np.cos(ang), -np.sin(ang)], axis=-1)
    z = np.pad(z, ((0, 0), (0, HY_FILTER_HIDDEN - HY_EMB)))
    idx = np.concatenate([np.arange(L), [0], np.arange(L - 1, 0, -1)])
    return jnp.asarray(z[idx].astype(np.float32))


def _head_mean_matrix():
    i = np.arange(LANES)
    j = ((i[:, None] // HEAD_DIM) == (i[None, :] // HEAD_DIM)).astype(np.float32) / HEAD_DIM
    return jnp.asarray(j.astype(BF16))


def _halve_gate_columns(w_in_bf):
    scale = jnp.where(jnp.arange(IN_COLS) >= GATE_OFF, 0.5, 1.0).astype(BF16)
    return w_in_bf * scale


def _trunk(x, wts, fft_dims=None):
    B, L, _ = x.shape
    T = B * L
    n1, n2 = fft_dims or _fft_dims(L)
    cst = _fft_consts(n1, n2, n1 // 2)
    cos_t, sin_t = _rope_tables(L)
    z2 = _filter_features(L)
    deltas = jnp.abs(jnp.linspace(math.log(HY_TARGET) / HY_SLOW_DECAY,
                                  math.log(HY_TARGET) / HY_FAST_DECAY, HY_WIDTH, dtype=F32))[None, :]
    j_bf = _head_mean_matrix()
    depth = wts["w_in"].shape[0]
    x2 = x.reshape(T, D_MODEL)
    for l in range(depth):
        ng = wts["norm_gains"][l]
        w1p = jnp.pad(wts["hy_w1"][l], ((0, HY_FILTER_HIDDEN - HY_EMB), (0, 0)))
        k2raw, sumsq = _hy_filter_raw(z2, w1p, wts["hy_b1"][l][None, :], wts["hy_w2"][l],
                                      wts["hy_b2"][l][None, :], wts["hy_w3"][l],
                                      wts["hy_freq"][l], deltas, L)
        kf_re, kf_im = _filter_fft(k2raw, sumsq, cst, n1, n2)

        p2 = _in_proj(x2, ng[0][None, :], wts["w_in_bf"][l])
        p3 = p2.reshape(B, L, IN_COLS)

        qg = jnp.tile(wts["qk_norm"][l, 0], 2)[None, :]
        kg = jnp.tile(wts["qk_norm"][l, 1], 2)[None, :]
        qt, k, vt = _attn_prep(p3, cos_t, sin_t, qg, kg, j_bf)
        out_a = _flash(qt, k, vt)

        z, x0u = _hy_prologue(p3, wts["hy_conv_w"][l], wts["hy_conv_b"][l][None, :])
        out_b = _hy_conv(z, x0u, wts["hy_bias"][l][None, :], kf_re, kf_im, cst, n1, n2)

        tables = _ret_tables(wts["ret_decay_exp"][l])
        out_c = _retention(p3, cos_t, sin_t, tables, j_bf)

        out_d = _sconv(p3, wts["sc_conv_w"][l])

        h2 = _merge(out_a.reshape(T, BRANCH_W), out_b.reshape(T, BRANCH_W),
                    out_c.reshape(T, BRANCH_W), out_d.reshape(T, BRANCH_W),
                    p2, wts["w_branch_bf"][l], wts["w_out_bf"][l], x2, ng[1][None, :])
        x2 = _ffn(h2, ng[2][None, :], wts["w_gate_bf"][l], wts["w_up_bf"][l],
                  wts["w_ffn_out_bf"][l], ng[3][None, :])
    return x2.reshape(B, L, D_MODEL)


def kernel(x_prompt, x_sample, norm_gains, w_in, qk_norm, hy_conv_w, hy_conv_b, hy_w1, hy_b1, hy_w2,
           hy_b2, hy_w3, hy_freq, hy_bias, ret_decay_exp, sc_conv_w, w_branch, w_out, w_ffn_in,
           w_ffn_out):
    wts = dict(norm_gains=norm_gains, w_in=w_in, qk_norm=qk_norm, hy_conv_w=hy_conv_w,
               hy_conv_b=hy_conv_b, hy_w1=hy_w1, hy_b1=hy_b1, hy_w2=hy_w2, hy_b2=hy_b2,
               hy_w3=hy_w3, hy_freq=hy_freq, hy_bias=hy_bias, ret_decay_exp=ret_decay_exp,
               sc_conv_w=sc_conv_w,
               w_in_bf=_halve_gate_columns(w_in.astype(BF16)), w_branch_bf=w_branch.astype(BF16),
               w_out_bf=w_out.astype(BF16), w_gate_bf=w_ffn_in[..., :D_FF].astype(BF16),
               w_up_bf=w_ffn_in[..., D_FF:].astype(BF16), w_ffn_out_bf=w_ffn_out.astype(BF16))
    return _trunk(x_prompt, wts), _trunk(x_sample, wts)
```

```python
import functools
import math

import numpy as np
import jax
import jax.numpy as jnp
from jax import lax
from jax.experimental import pallas as pl
from jax.experimental.pallas import tpu as pltpu

F32 = jnp.float32
BF16 = jnp.bfloat16

D_MODEL = 1024
GRID_W = 64
N_BRANCH = 4
BRANCH_W = 256
HEAD_DIM = 64
ATT_Q_HEADS = 4
ATT_KV_HEADS = 2
ROPE_BASE = 10000.0
ROPE_FREQS = HEAD_DIM // 4
HY_WIDTH = BRANCH_W
HY_EMB = 33
HY_BANDS = (HY_EMB - 1) // 2
HY_FILTER_HIDDEN = 64
HY_FAST_DECAY = 0.3
HY_SLOW_DECAY = 1.5
HY_TARGET = 1e-2
RET_HEADS = 4
RET_W = RET_HEADS * HEAD_DIM
RET_CHUNK = 128
SC_WIDTH = BRANCH_W
D_FF = 2816
NORM_EPS = 1e-6

ATT_Q_W = ATT_Q_HEADS * HEAD_DIM
ATT_KV_W = ATT_KV_HEADS * HEAD_DIM
A_K_OFF = ATT_Q_W
A_V_OFF = A_K_OFF + ATT_KV_W
HY_OFF = A_V_OFF + ATT_KV_W
RET_OFF = HY_OFF + 3 * HY_WIDTH
SC_OFF = RET_OFF + 4 * RET_W
GATE_OFF = SC_OFF + 3 * SC_WIDTH
IN_COLS = GATE_OFF + N_BRANCH * D_MODEL

LANES = 128
SUBLANES = 8
MXU_W = 256
HALO = 2 * SUBLANES
HY_BLK = HY_OFF // LANES
RET_BLK = RET_OFF // LANES
SC_BLK = SC_OFF // LANES
GATE_BLK = GATE_OFF // D_MODEL
ATT_KB = 512
ATT_TQ = 256
VT_ROWS = HEAD_DIM + HALO
Q_SCALE = HEAD_DIM ** -0.5 * math.log2(math.e)
FFT_PASSES_FILTER = 3
FFT_PASSES_DATA = 1
FFT_CH = 8
FFT_SETS = 1
MIB = 1 << 20


def _cparams(sem, vmem_mib):
    return pltpu.CompilerParams(dimension_semantics=sem, vmem_limit_bytes=vmem_mib * MIB)


def _sigmoid(x):
    return 0.5 * jnp.tanh(0.5 * x) + 0.5


def _split(x):
    hi = x.astype(BF16)
    lo = (x - hi.astype(F32)).astype(BF16)
    return hi, lo


def _np_split(a64):
    a32 = np.asarray(a64, np.float32)
    hi = a32.astype(BF16)
    lo = (a32 - hi.astype(np.float32)).astype(BF16)
    return jnp.asarray(hi), jnp.asarray(lo)


def _mm_const(a_hi, a_lo, x, passes):
    if passes == 1:
        return jnp.dot(a_hi, x.astype(BF16), preferred_element_type=F32)
    xh, xl = _split(x)
    out = jnp.dot(a_hi, xh, preferred_element_type=F32)
    if passes >= 3:
        out = out + jnp.dot(a_lo, xh, preferred_element_type=F32)
        out = out + jnp.dot(a_hi, xl, preferred_element_type=F32)
    return out


def _mm3(a, b):
    ah, al = _split(a)
    bh, bl = _split(b)
    out = jnp.dot(ah, bh, preferred_element_type=F32)
    out = out + jnp.dot(al, bh, preferred_element_type=F32)
    return out + jnp.dot(ah, bl, preferred_element_type=F32)


def _head_mean_sq(x, j_bf):
    hi, lo = _split(x * x)
    return (jnp.dot(hi, j_bf, preferred_element_type=F32)
            + jnp.dot(lo, j_bf, preferred_element_type=F32))


def _rope(x, c, s):
    lane = lax.broadcasted_iota(jnp.int32, (1, LANES), 1)
    is_b = (lane & 16) != 0
    partner = jnp.where(is_b, pltpu.roll(x, 16, 1), pltpu.roll(x, LANES - 16, 1))
    return x * c + partner * s


def _col_chunks(n, parts):
    tiles = n // MXU_W
    bounds = [MXU_W * (tiles * i // parts) for i in range(parts + 1)]
    return list(zip(bounds[:-1], bounds[1:]))


def _conv3_rows(main, before, after, w):
    tb = main.shape[0]
    w0, w1, w2 = w[0:1, :], w[1:2, :], w[2:3, :]
    body = pltpu.roll(main, 1, 0) * w0 + main * w1 + pltpu.roll(main, tb - 1, 0) * w2
    row_first = before * w0 + main[0:1, :] * w1 + main[1:2, :] * w2
    row_last = main[tb - 2:tb - 1, :] * w0 + main[tb - 1:tb, :] * w1 + after * w2
    row = lax.broadcasted_iota(jnp.int32, (SUBLANES, 1), 0)
    head = jnp.where(row == 0, row_first, body[:SUBLANES])
    tail = jnp.where(row == SUBLANES - 1, row_last, body[tb - SUBLANES:])
    return jnp.concatenate([head, body[SUBLANES:tb - SUBLANES], tail], axis=0)


def _in_proj_kernel(x_ref, xp_ref, xn_ref, g_ref, w_ref, hcw_ref, hcb_ref, scw_ref,
                    pa_ref, pr_ref, pg_ref, z_ref, x0_ref, od_ref, *, tiles_per_seq):
    i = pl.program_id(0)
    first = (i % tiles_per_seq) == 0
    last = (i % tiles_per_seq) == tiles_per_seq - 1
    g = g_ref[...]

    def normed(x):
        ms = jnp.mean(x * x, axis=-1, keepdims=True)
        return (x * lax.rsqrt(ms + NORM_EPS) * g).astype(BF16)

    dot = lambda a, lo, hi: jnp.dot(a, w_ref[:, lo:hi], preferred_element_type=F32)
    xn = normed(x_ref[...])
    halo = normed(jnp.concatenate([xp_ref[...], xn_ref[...]], axis=0))
    edge = lambda h: (jnp.where(first, 0.0, h[SUBLANES - 1:SUBLANES, :]),
                      jnp.where(last, 0.0, h[SUBLANES:SUBLANES + 1, :]))

    head = dot(xn, 0, RET_OFF)
    hy_halo = dot(halo, HY_OFF, RET_OFF)
    sc = dot(xn, SC_OFF, GATE_OFF)
    sc_halo = dot(halo, SC_OFF + SC_WIDTH, GATE_OFF)
    pa_ref[...] = head[:, :HY_OFF].astype(pa_ref.dtype)
    pr_ref[...] = dot(xn, RET_OFF, SC_OFF).astype(pr_ref.dtype)

    u = _conv3_rows(head[:, HY_OFF:], *edge(hy_halo), hcw_ref[...]) + hcb_ref[...]
    x0_ref[...] = u[:, :HY_WIDTH]
    z_ref[...] = u[:, 2 * HY_WIDTH:] * u[:, HY_WIDTH:2 * HY_WIDTH]
    m = sc[:, SC_WIDTH:2 * SC_WIDTH] * sc[:, 2 * SC_WIDTH:]
    mh = sc_halo[:, :SC_WIDTH] * sc_halo[:, SC_WIDTH:]
    od_ref[...] = (sc[:, :SC_WIDTH] * _conv3_rows(m, *edge(mh), scw_ref[...])).astype(od_ref.dtype)

    for lo, hi in _col_chunks(N_BRANCH * D_MODEL, 2):
        pg_ref[:, lo:hi] = dot(xn, GATE_OFF + lo, GATE_OFF + hi).astype(pg_ref.dtype)


def _in_proj(x2, g, w_bf, hcw, hcb, scw, L):
    T = x2.shape[0]
    tm = min(512, L)
    r8 = tm // SUBLANES
    row = lambda w: pl.BlockSpec((tm, w), lambda i: (i, 0))
    full = lambda a: pl.BlockSpec(a.shape, lambda i: (0,) * a.ndim)
    sds = lambda w, dt: jax.ShapeDtypeStruct((T, w), dt)
    return pl.pallas_call(
        functools.partial(_in_proj_kernel, tiles_per_seq=L // tm),
        grid=(T // tm,),
        in_specs=[row(D_MODEL),
                  pl.BlockSpec((SUBLANES, D_MODEL), lambda i: (jnp.maximum(i * r8 - 1, 0), 0)),
                  pl.BlockSpec((SUBLANES, D_MODEL),
                               lambda i: (jnp.minimum((i + 1) * r8, T // SUBLANES - 1), 0)),
                  full(g), full(w_bf), full(hcw), full(hcb), full(scw)],
        out_specs=[row(HY_OFF), row(RET_W * 4), row(N_BRANCH * D_MODEL),
                   row(HY_WIDTH), row(HY_WIDTH), row(SC_WIDTH)],
        out_shape=[sds(HY_OFF, BF16), sds(RET_W * 4, BF16), sds(N_BRANCH * D_MODEL, BF16),
                   sds(HY_WIDTH, F32), sds(HY_WIDTH, F32), sds(SC_WIDTH, BF16)],
        compiler_params=_cparams(("parallel",), 48),
        name="in_proj",
    )(x2, x2, x2, g, w_bf, hcw, hcb, scw)


def _attn_prep_kernel(p_ref, c_ref, s_ref, qg_ref, kg_ref, j_ref, qt_ref, k_ref, vt_ref):
    c = c_ref[...]
    s = s_ref[...]
    j_bf = j_ref[...]
    tm = p_ref.shape[0]
    tq = qt_ref.shape[-1]
    zeros = jnp.zeros((HEAD_DIM, tm), BF16)
    for blk in range(2):
        q = p_ref[:, LANES * blk:LANES * (blk + 1)].astype(F32)
        qn = q * lax.rsqrt(_head_mean_sq(q, j_bf) + NORM_EPS) * qg_ref[...]
        qt = (_rope(qn, c, s) * Q_SCALE).astype(BF16).T
        lo, hi = qt[:HEAD_DIM], qt[HEAD_DIM:]
        if blk == 0:
            heads = (jnp.concatenate([lo, zeros], axis=0), jnp.concatenate([hi, zeros], axis=0))
        else:
            heads = (jnp.concatenate([zeros, lo], axis=0), jnp.concatenate([zeros, hi], axis=0))
        for hp, qh in enumerate(heads):
            for t in range(tm // tq):
                qt_ref[2 * blk + hp, t] = qh[:, t * tq:(t + 1) * tq]
    k = p_ref[:, A_K_OFF:A_K_OFF + LANES].astype(F32)
    kn = k * lax.rsqrt(_head_mean_sq(k, j_bf) + NORM_EPS) * kg_ref[...]
    k_ref[...] = _rope(kn, c, s).astype(BF16)
    v = p_ref[:, A_V_OFF:A_V_OFF + LANES]
    ones = jnp.ones((VT_ROWS - HEAD_DIM, ATT_KB), BF16)
    for t in range(tm // ATT_KB):
        vt = v[t * ATT_KB:(t + 1) * ATT_KB, :].T
        vt_ref[t] = jnp.concatenate([vt[:HEAD_DIM], ones, vt[HEAD_DIM:], ones], axis=0)


def _attn_prep(p3, cos_t, sin_t, qg, kg, j_bf):
    B, L, _ = p3.shape
    tm = min(1024, L)
    tq = min(ATT_TQ, L)
    tab = pl.BlockSpec((tm, LANES), lambda b, i: (i, 0))
    vec = pl.BlockSpec((1, LANES), lambda b, i: (0, 0))
    return pl.pallas_call(
        _attn_prep_kernel,
        grid=(B, L // tm),
        in_specs=[pl.BlockSpec((None, tm, HY_OFF), lambda b, i: (b, i, 0)), tab, tab, vec, vec,
                  pl.BlockSpec((LANES, LANES), lambda b, i: (0, 0))],
        out_specs=[pl.BlockSpec((None, ATT_Q_HEADS, tm // tq, LANES, tq),
                                lambda b, i: (b, 0, i, 0, 0)),
                   pl.BlockSpec((None, tm, LANES), lambda b, i: (b, i, 0)),
                   pl.BlockSpec((None, tm // ATT_KB, ATT_KV_HEADS * VT_ROWS, ATT_KB),
                                lambda b, i: (b, i, 0, 0))],
        out_shape=[jax.ShapeDtypeStruct((B, ATT_Q_HEADS, L // tq, LANES, tq), BF16),
                   jax.ShapeDtypeStruct((B, L, LANES), BF16),
                   jax.ShapeDtypeStruct((B, L // ATT_KB, ATT_KV_HEADS * VT_ROWS, ATT_KB), BF16)],
        compiler_params=_cparams(("parallel", "parallel"), 32),
        name="attn_prep",
    )(p3, cos_t, sin_t, qg, kg, j_bf)


def _flash_kernel(qt_ref, k_ref, vt_ref, o_ref, sa_ref, ma_ref, sb_ref, mb_ref, *,
                  ngrp, nsub, ntile):
    tq = ATT_TQ
    group = ATT_Q_HEADS // ATT_KV_HEADS

    def produce(slot, qi, g, t, h):
        s_ref, mx_ref = slot
        st = pl.multiple_of((g * nsub + t) * ATT_KB, ATT_KB)
        s = jnp.dot(k_ref[pl.ds(st, ATT_KB), :], qt_ref[h, qi],
                    preferred_element_type=F32)
        s_ref[t * ATT_Q_HEADS + h] = s
        mx_ref[t * ATT_Q_HEADS + h] = jnp.max(s, axis=0, keepdims=True)

    def consume(slot, g, t, h, state):
        s_ref, mx_ref = slot
        m, acc = state
        m_new = jnp.maximum(m, mx_ref[t * ATT_Q_HEADS + h])
        alpha = jnp.exp2(m - m_new)
        p = jnp.exp2((s_ref[t * ATT_Q_HEADS + h] - m_new).astype(BF16))
        j = h // group
        vt = vt_ref[g * nsub + t, VT_ROWS * j:VT_ROWS * (j + 1), :]
        return m_new, alpha * acc + jnp.dot(vt, p, preferred_element_type=F32)

    def step(slot_in, g_in, slot_out, qi_out, g_out, carry):
        carry = list(carry)
        for t in range(nsub):
            for h in range(ATT_Q_HEADS):
                if slot_out is not None:
                    produce(slot_out, qi_out, g_out, t, h)
                if slot_in is not None:
                    carry[h] = consume(slot_in, g_in, t, h, carry[h])
        return tuple(carry)

    def fresh():
        return tuple((jnp.full((1, tq), -jnp.inf, F32), jnp.zeros((VT_ROWS, tq), F32))
                     for _ in range(ATT_Q_HEADS))

    def finalize(qi, carry):
        outs = [acc[:HEAD_DIM] / acc[HEAD_DIM:HEAD_DIM + 1] for _, acc in carry]
        rows = pl.ds(pl.multiple_of(qi * tq, tq), tq)
        o_ref[rows, :] = jnp.concatenate(outs, axis=0).T.astype(o_ref.dtype)

    slot_a = (sa_ref, ma_ref)
    slot_b = (sb_ref, mb_ref)

    if ngrp == 1:
        def tile(qi, _):
            step(None, None, slot_a, qi, 0, ())
            finalize(qi, step(slot_a, 0, None, None, None, fresh()))
            return 0
    else:
        step(None, None, slot_a, 0, 0, ())

        def tile(qi, _):
            def body(i, carry):
                g = 2 * i
                carry = step(slot_a, g, slot_b, qi, g + 1, carry)
                return step(slot_b, g + 1, slot_a, qi, g + 2, carry)

            carry = lax.fori_loop(0, ngrp // 2 - 1, body, fresh())
            carry = step(slot_a, ngrp - 2, slot_b, qi, ngrp - 1, carry)
            carry = step(slot_b, ngrp - 1, slot_a, jnp.minimum(qi + 1, ntile - 1), 0, carry)
            finalize(qi, carry)
            return 0

    lax.fori_loop(0, ntile, tile, 0)


def _flash(qt, k, vt):
    B, _, nqt, _, tq = qt.shape
    L = nqt * tq
    ntile = math.gcd(nqt, 4)
    nkb = L // ATT_KB
    nsub = 2 if nkb % 4 == 0 and nkb >= 8 else 1
    ngrp = L // (ATT_KB * nsub)
    assert ngrp == 1 or ngrp % 2 == 0
    s_scratch = pltpu.VMEM((nsub * ATT_Q_HEADS, ATT_KB, tq), F32)
    m_scratch = pltpu.VMEM((nsub * ATT_Q_HEADS, 1, tq), F32)
    return pl.pallas_call(
        functools.partial(_flash_kernel, ngrp=ngrp, nsub=nsub, ntile=ntile),
        grid=(B, nqt // ntile),
        in_specs=[pl.BlockSpec((None, ATT_Q_HEADS, ntile, LANES, tq),
                               lambda b, i: (b, 0, i, 0, 0)),
                  pl.BlockSpec((None, L, LANES), lambda b, i: (b, 0, 0)),
                  pl.BlockSpec((None, L // ATT_KB, ATT_KV_HEADS * VT_ROWS, ATT_KB),
                               lambda b, i: (b, 0, 0, 0))],
        out_specs=pl.BlockSpec((None, ntile * tq, ATT_Q_W), lambda b, i: (b, i, 0)),
        out_shape=jax.ShapeDtypeStruct((B, L, ATT_Q_W), BF16),
        scratch_shapes=[s_scratch, m_scratch, s_scratch, m_scratch],
        compiler_params=_cparams(("parallel", "arbitrary"), 48),
        name="flash_attn",
    )(qt, k, vt)


def _conv3_tile(main, prev_t, next_t, w, first, last):
    tb = main.shape[0]
    w0, w1, w2 = w[0:1, :], w[1:2, :], w[2:3, :]
    body = pltpu.roll(main, 1, 0) * w0 + main * w1 + pltpu.roll(main, tb - 1, 0) * w2
    before = jnp.where(first, 0.0, prev_t[HALO - 1:HALO, :])
    after = jnp.where(last, 0.0, next_t[0:1, :])
    row_first = before * w0 + main[0:1, :] * w1 + main[1:2, :] * w2
    row_last = main[tb - 2:tb - 1, :] * w0 + main[tb - 1:tb, :] * w1 + after * w2
    row = lax.broadcasted_iota(jnp.int32, (HALO, 1), 0)
    head = jnp.where(row == 0, row_first, body[:HALO])
    tail = jnp.where(row == HALO - 1, row_last, body[tb - HALO:])
    return jnp.concatenate([head, body[HALO:tb - HALO], tail], axis=0)


def _halo_specs(tb, L, col_fn):
    nbh = L // HALO
    rh = tb // HALO
    main = pl.BlockSpec((None, tb, LANES), lambda b, w, s: (b, s, col_fn(w)))
    prev = pl.BlockSpec((None, HALO, LANES),
                        lambda b, w, s: (b, jnp.maximum(s * rh - 1, 0), col_fn(w)))
    nxt = pl.BlockSpec((None, HALO, LANES),
                       lambda b, w, s: (b, jnp.minimum((s + 1) * rh, nbh - 1), col_fn(w)))
    return [main, prev, nxt]


def _f32(ref):
    return ref[...].astype(F32)


def _hy_pro_kernel(x0_ref, x0p_ref, x0n_ref, x1_ref, x1p_ref, x1n_ref,
                   hv_ref, hvp_ref, hvn_ref, w0_ref, w1_ref, w2_ref,
                   b0_ref, b1_ref, b2_ref, z_ref, x0u_ref):
    first = pl.program_id(2) == 0
    last = pl.program_id(2) == pl.num_programs(2) - 1
    x0 = _conv3_tile(_f32(x0_ref), _f32(x0p_ref), _f32(x0n_ref), w0_ref[...], first, last) + b0_ref[...]
    x1 = _conv3_tile(_f32(x1_ref), _f32(x1p_ref), _f32(x1n_ref), w1_ref[...], first, last) + b1_ref[...]
    hv = _conv3_tile(_f32(hv_ref), _f32(hvp_ref), _f32(hvn_ref), w2_ref[...], first, last) + b2_ref[...]
    z_ref[...] = hv * x1
    x0u_ref[...] = x0


def _hy_prologue(p3, hcw, hcb):
    B, L, _ = p3.shape
    tb = min(1024, L)
    nw = HY_WIDTH // LANES
    in_specs = []
    for piece in range(3):
        in_specs += _halo_specs(tb, L, lambda w, piece=piece: HY_BLK + nw * piece + w)
    for piece in range(3):
        in_specs.append(pl.BlockSpec((3, LANES), lambda b, w, s, piece=piece: (0, nw * piece + w)))
    for piece in range(3):
        in_specs.append(pl.BlockSpec((1, LANES), lambda b, w, s, piece=piece: (0, nw * piece + w)))
    out_spec = pl.BlockSpec((None, tb, LANES), lambda b, w, s: (b, s, w))
    out_sh = jax.ShapeDtypeStruct((B, L, HY_WIDTH), F32)
    args = [p3] * 9 + [hcw] * 3 + [hcb] * 3
    return pl.pallas_call(
        _hy_pro_kernel,
        grid=(B, nw, L // tb),
        in_specs=in_specs,
        out_specs=[out_spec, out_spec],
        out_shape=[out_sh, out_sh],
        compiler_params=_cparams(("parallel", "parallel", "parallel"), 32),
        name="hyena_prologue",
    )(*args)


def _sconv_kernel(sb_ref, sc_ref, scp_ref, scn_ref, sh_ref, shp_ref, shn_ref, w_ref, o_ref):
    first = pl.program_id(2) == 0
    last = pl.program_id(2) == pl.num_programs(2) - 1
    y = _conv3_tile(_f32(sc_ref) * _f32(sh_ref), _f32(scp_ref) * _f32(shp_ref),
                    _f32(scn_ref) * _f32(shn_ref), w_ref[...], first, last)
    o_ref[...] = (_f32(sb_ref) * y).astype(o_ref.dtype)


def _sconv(p3, scw):
    B, L, _ = p3.shape
    tb = min(1024, L)
    nw = SC_WIDTH // LANES
    in_specs = [pl.BlockSpec((None, tb, LANES), lambda b, w, s: (b, s, SC_BLK + w))]
    in_specs += _halo_specs(tb, L, lambda w: SC_BLK + nw + w)
    in_specs += _halo_specs(tb, L, lambda w: SC_BLK + 2 * nw + w)
    in_specs.append(pl.BlockSpec((3, LANES), lambda b, w, s: (0, w)))
    return pl.pallas_call(
        _sconv_kernel,
        grid=(B, nw, L // tb),
        in_specs=in_specs,
        out_specs=pl.BlockSpec((None, tb, LANES), lambda b, w, s: (b, s, w)),
        out_shape=jax.ShapeDtypeStruct((B, L, SC_WIDTH), BF16),
        compiler_params=_cparams(("parallel", "parallel", "parallel"), 32),
        name="short_conv",
    )(*([p3] * 7 + [scw]))


def _hy_filter_kernel(z_ref, w1_ref, b1_ref, w2_ref, b2_ref, w3_ref, fr_ref, dl_ref,
                      k_ref, ss_ref, *, L):
    step = pl.program_id(0)
    tr = z_ref.shape[0]
    z = z_ref[...]
    h = jnp.sin(fr_ref[0:1, :] * (_mm3(z, w1_ref[...]) + b1_ref[...]))
    h = jnp.sin(fr_ref[1:2, :] * (_mm3(h, w2_ref[...]) + b2_ref[...]))
    h3 = _mm3(h, w3_ref[...])
    row = step * tr + lax.broadcasted_iota(jnp.int32, (tr, 1), 0)
    val = jnp.where(row < L, h3[:, :HY_WIDTH], h3[:, HY_WIDTH:])
    val = val * jnp.exp(-z[:, 0:1] * dl_ref[...])
    val = jnp.where(row == L, 0.0, val)
    k_ref[...] = val

    @pl.when(step == 0)
    def _():
        ss_ref[...] = jnp.zeros_like(ss_ref)

    ss_ref[...] += jnp.sum(val * val, axis=0, keepdims=True)


def _hy_filter_raw(z2, w1p, b1, w2, b2, w3, freq, deltas, L):
    n = 2 * L
    tr = min(512, n)
    H = HY_FILTER_HIDDEN
    full = lambda shape: pl.BlockSpec(shape, lambda i: (0, 0))
    return pl.pallas_call(
        functools.partial(_hy_filter_kernel, L=L),
        grid=(n // tr,),
        in_specs=[pl.BlockSpec((tr, H), lambda i: (i, 0)),
                  full((H, H)), full((1, H)), full((H, H)), full((1, H)),
                  full((H, 2 * HY_WIDTH)), full((2, H)), full((1, HY_WIDTH))],
        out_specs=[pl.BlockSpec((tr, HY_WIDTH), lambda i: (i, 0)), full((1, HY_WIDTH))],
        out_shape=[jax.ShapeDtypeStruct((n, HY_WIDTH), F32),
                   jax.ShapeDtypeStruct((1, HY_WIDTH), F32)],
        compiler_params=_cparams(("arbitrary",), 32),
        name="hyena_filter",
    )(z2, w1p, b1, w2, b2, w3, freq, deltas)


def _fft_dims(L):
    n = 2 * L
    n2 = 128 if n >= 32 * 128 else 64
    return n // n2, n2


def _fft_consts(n1, n2, r):
    n = n1 * n2
    a1 = 2.0 * np.pi * np.outer(np.arange(n1), np.arange(n1)) / n1
    c1, s1 = np.cos(a1), np.sin(a1)
    a2 = 2.0 * np.pi * np.outer(np.arange(n2), np.arange(n2)) / n2
    c2, s2 = np.cos(a2), np.sin(a2)
    f1 = np.block([[c1[:, :r], s1[:, :r]], [-s1[:, :r], c1[:, :r]]])
    f1_full = np.concatenate([c1, -s1], axis=0)
    m2 = np.block([[c2, s2], [-s2, c2]])
    m2i = np.block([[c2, -s2], [s2, c2]])
    fi = np.block([[c1[:r, :], -s1[:r, :]], [s1[:r, :], c1[:r, :]]])
    aw = 2.0 * np.pi * np.arange(n1) / n
    off = np.repeat(np.arange(FFT_CH), LANES)[None, :] * aw[:, None]
    step = np.broadcast_to((FFT_CH * aw)[:, None], (n1, LANES))
    f32 = lambda a: jnp.asarray(np.asarray(a, np.float32))
    return dict(f1=_np_split(f1), f1_full=_np_split(f1_full), m2=_np_split(m2),
                m2i=_np_split(m2i), fi=_np_split(fi),
                tw=(f32(np.cos(off)), f32(-np.sin(off)), f32(np.cos(step)), f32(-np.sin(step))))


def _cmul(a_re, a_im, b_re, b_im):
    return a_re * b_re - a_im * b_im, a_re * b_im + a_im * b_re


def _lane_tile(x, reps):
    return jnp.concatenate([x] * reps, axis=1)


def _lane_part(x, j):
    return x[:, LANES * j:LANES * (j + 1)]


def _chunk_twiddle(t_re, t_im, tw_refs):
    d_re, d_im, s_re, s_im = (ref[...] for ref in tw_refs)
    cur = _cmul(_lane_tile(t_re, FFT_CH), _lane_tile(t_im, FFT_CH), d_re, d_im)
    return cur, _cmul(t_re, t_im, s_re, s_im)


def _twiddle_init(n1):
    return jnp.ones((n1, LANES), F32), jnp.zeros((n1, LANES), F32)


def _fft_stage1(load_rows, a_ref, f_hi, f_lo, tw_refs, n1, n2, passes):
    def body(ci, carry):
        i0s, tws = [], []
        for u in range(FFT_SETS):
            tw, carry = _chunk_twiddle(*carry, tw_refs)
            tws.append(tw)
            i0s.append((ci * FFT_SETS + u) * FFT_CH)
        prods = [_mm_const(f_hi, f_lo, jnp.concatenate(
            [load_rows(i0 + j) for j in range(FFT_CH)], axis=1), passes)
            for i0 in i0s]
        for i0, (t_re, t_im), a in zip(i0s, tws, prods):
            o_re, o_im = _cmul(a[:n1], a[n1:], t_re, t_im)
            for j in range(FFT_CH):
                base = pl.multiple_of((i0 + j) * 2 * n1, 2 * n1)
                a_ref[pl.ds(base, n1), :] = _lane_part(o_re, j)
                a_ref[pl.ds(base + n1, n1), :] = _lane_part(o_im, j)
        return carry

    lax.fori_loop(0, n2 // (FFT_CH * FFT_SETS), body, _twiddle_init(n1))


def _load_spectrum_rows(a_ref, k1, n1, n2):
    ld = lambda off: a_ref[pl.ds(off, n2, stride=2 * n1), :]
    return jnp.concatenate(
        [jnp.concatenate([ld(k1 + j) for j in range(FFT_CH)], axis=1),
         jnp.concatenate([ld(n1 + k1 + j) for j in range(FFT_CH)], axis=1)], axis=0)


def _filter_fft_kernel(k_ref, ss_ref, f_hi_ref, f_lo_ref, m_hi_ref, m_lo_ref,
                       dre_ref, dim_ref, sre_ref, sim_ref, kr_ref, ki_ref, a_ref, *, n1, n2):
    c = pl.program_id(1)

    @pl.when(c == 0)
    def _():
        _fft_stage1(lambda i: k_ref[pl.ds(i, n1, stride=n2), :], a_ref,
                    f_hi_ref[...], f_lo_ref[...], (dre_ref, dim_ref, sre_ref, sim_ref),
                    n1, n2, FFT_PASSES_FILTER)

    scale = _lane_tile(lax.rsqrt(ss_ref[...] + NORM_EPS) * (1.0 / (n1 * n2)), FFT_CH)
    xin = _load_spectrum_rows(a_ref, c * FFT_CH, n1, n2)
    x = _mm_const(m_hi_ref[...], m_lo_ref[...], xin, FFT_PASSES_FILTER)
    kr_ref[...] = x[:n2] * scale
    ki_ref[...] = x[n2:] * scale


def _filter_fft(k2raw, sumsq, cst, n1, n2):
    n = n1 * n2
    nw = HY_WIDTH // LANES
    consts = [*cst["f1_full"], *cst["m2"], *cst["tw"]]
    full = lambda a: pl.BlockSpec(a.shape, lambda w, c: (0,) * a.ndim)
    out_spec = pl.BlockSpec((None, None, n2, FFT_CH * LANES), lambda w, c: (w, c, 0, 0))
    out_sh = jax.ShapeDtypeStruct((nw, n1 // FFT_CH, n2, FFT_CH * LANES), F32)
    return pl.pallas_call(
        functools.partial(_filter_fft_kernel, n1=n1, n2=n2),
        grid=(nw, n1 // FFT_CH),
        in_specs=[pl.BlockSpec((n, LANES), lambda w, c: (0, w)),
                  pl.BlockSpec((1, LANES), lambda w, c: (0, w))] + [full(a) for a in consts],
        out_specs=[out_spec, out_spec],
        out_shape=[out_sh, out_sh],
        scratch_shapes=[pltpu.VMEM((2 * n, LANES), F32)],
        compiler_params=_cparams(("parallel", "arbitrary"), 48),
        name="hyena_filter_fft",
    )(k2raw, sumsq, *consts)


def _hy_conv_kernel(z_ref, x0_ref, hb_ref, kr_ref, ki_ref, f_hi_ref, f_lo_ref,
                    m_hi_ref, m_lo_ref, mi_hi_ref, mi_lo_ref, fi_hi_ref, fi_lo_ref,
                    dre_ref, dim_ref, sre_ref, sim_ref, o_ref, a_ref, *, n1, n2):
    c = pl.program_id(2)
    r = n1 // 2
    tw_refs = (dre_ref, dim_ref, sre_ref, sim_ref)

    def load_pair(i):
        seq = pl.ds(i, r, stride=n2)
        return jnp.concatenate([z_ref[0, seq, :], z_ref[1, seq, :]], axis=0)

    @pl.when(c == 0)
    def _():
        _fft_stage1(load_pair, a_ref, f_hi_ref[...], f_lo_ref[...], tw_refs, n1, n2,
                    FFT_PASSES_DATA)

    k1s = [(c * FFT_SETS + u) * FFT_CH for u in range(FFT_SETS)]
    xs = [_mm_const(m_hi_ref[...], m_lo_ref[...], _load_spectrum_rows(a_ref, k1, n1, n2),
                    FFT_PASSES_DATA) for k1 in k1s]
    ys = [jnp.concatenate(_cmul(x[:n2], x[n2:], kr_ref[u], ki_ref[u]), axis=0)
          for u, x in enumerate(xs)]
    bqs = [_mm_const(mi_hi_ref[...], mi_lo_ref[...], y, FFT_PASSES_DATA) for y in ys]
    for k1, bq in zip(k1s, bqs):
        for j in range(FFT_CH):
            a_ref[pl.ds(k1 + j, n2, stride=2 * n1), :] = _lane_part(bq[:n2], j)
            a_ref[pl.ds(n1 + k1 + j, n2, stride=2 * n1), :] = _lane_part(bq[n2:], j)

    @pl.when(c == pl.num_programs(2) - 1)
    def _():
        hb = hb_ref[...]

        def body(ci, carry):
            i0s, bts = [], []
            for u in range(FFT_SETS):
                (t_re, t_im), carry = _chunk_twiddle(*carry, tw_refs)
                i0 = (ci * FFT_SETS + u) * FFT_CH
                rows = [pl.multiple_of((i0 + j) * 2 * n1, 2 * n1) for j in range(FFT_CH)]
                br = jnp.concatenate([a_ref[pl.ds(b0, n1), :] for b0 in rows], axis=1)
                bi = jnp.concatenate([a_ref[pl.ds(b0 + n1, n1), :] for b0 in rows], axis=1)
                bts.append(jnp.concatenate([br * t_re + bi * t_im, bi * t_re - br * t_im],
                                           axis=0))
                i0s.append(i0)
            ys = [_mm_const(fi_hi_ref[...], fi_lo_ref[...], bt, FFT_PASSES_DATA) for bt in bts]
            for i0, y in zip(i0s, ys):
                for j in range(FFT_CH):
                    seq = pl.ds(i0 + j, r, stride=n2)
                    for row in range(2):
                        conv = _lane_part(y[row * r:(row + 1) * r], j)
                        o_ref[row, seq, :] = (x0_ref[row, seq, :]
                                              * (conv + z_ref[row, seq, :] * hb)
                                              ).astype(o_ref.dtype)
            return carry

        lax.fori_loop(0, n2 // (FFT_CH * FFT_SETS), body, _twiddle_init(n1))


def _hy_conv(z, x0u, hbias, kf_re, kf_im, cst, n1, n2):
    B, L, _ = z.shape
    assert B % 2 == 0
    nw = HY_WIDTH // LANES
    seq = pl.BlockSpec((2, L, LANES), lambda b, w, c: (b, 0, w), pipeline_mode=pl.Buffered(1))
    kf_spec = pl.BlockSpec((None, FFT_SETS, n2, FFT_CH * LANES), lambda b, w, c: (w, c, 0, 0))
    full = lambda a: pl.BlockSpec(a.shape, lambda b, w, c: (0,) * a.ndim)
    consts = [*cst["f1"], *cst["m2"], *cst["m2i"], *cst["fi"], *cst["tw"]]
    return pl.pallas_call(
        functools.partial(_hy_conv_kernel, n1=n1, n2=n2),
        grid=(B // 2, nw, n1 // (FFT_CH * FFT_SETS)),
        in_specs=[seq, seq, pl.BlockSpec((1, LANES), lambda b, w, c: (0, w)), kf_spec, kf_spec]
                 + [full(a) for a in consts],
        out_specs=seq,
        out_shape=jax.ShapeDtypeStruct((B, L, HY_WIDTH), F32),
        scratch_shapes=[pltpu.VMEM((2 * n1 * n2, LANES), F32)],
        compiler_params=_cparams(("parallel", "parallel", "arbitrary"), 56),
        name="hyena_conv",
    )(z, x0u, hbias, kf_re, kf_im, *consts)


T_D0, T_D1, T_WQF, T_WQB, T_WKF, T_WKB, T_GCF, T_GCB, T_BD = range(9)


def _ret_tables_kernel(rl_ref, rh_ref, t_ref):
    C = RET_CHUNK
    i = lax.broadcasted_iota(jnp.int32, (C, LANES), 0).astype(F32)
    jn = lax.broadcasted_iota(jnp.int32, (C, LANES), 1).astype(F32)
    log_g = lambda e: jnp.log1p(-jnp.exp2(-e))
    diff = i - jn
    for hp in range(2):
        lgf = log_g(rh_ref[hp, 0:1, :])
        lgb = log_g(rh_ref[hp, 1:2, :])
        fwd = jnp.exp(jnp.where(diff >= 0, diff, 0.0) * lgf)
        bwd = jnp.exp(jnp.where(diff < 0, -diff, 0.0) * lgb)
        t_ref[T_D0 + hp] = jnp.where(diff >= 0, fwd, bwd)
    lf = log_g(rl_ref[0:1, :])
    lb = log_g(rl_ref[1:2, :])
    t_ref[T_WQF] = jnp.exp((i + 1.0) * lf)
    t_ref[T_WQB] = jnp.exp((C - i) * lb)
    t_ref[T_WKF] = jnp.exp((C - 1.0 - i) * lf)
    t_ref[T_WKB] = jnp.exp(i * lb)
    bd = jnp.where((i < HEAD_DIM) == (jn < HEAD_DIM), 1.0, 0.0)
    t_ref[T_GCF] = jnp.exp(C * lf) * bd
    t_ref[T_GCB] = jnp.exp(C * lb) * bd
    t_ref[T_BD] = bd


def _ret_tables(rde):
    per_lane = jnp.repeat(rde.reshape(2, 2, 2), HEAD_DIM, axis=2)
    per_lane = per_lane.transpose(1, 0, 2)
    per_head = jnp.broadcast_to(rde.reshape(2, 2, 2, 1), (2, 2, 2, LANES))
    per_head = per_head.transpose(1, 2, 0, 3)
    return pl.pallas_call(
        _ret_tables_kernel,
        grid=(2,),
        in_specs=[pl.BlockSpec((None, 2, LANES), lambda j: (j, 0, 0)),
                  pl.BlockSpec((None, 2, 2, LANES), lambda j: (j, 0, 0, 0))],
        out_specs=pl.BlockSpec((None, 9, RET_CHUNK, LANES), lambda j: (j, 0, 0, 0)),
        out_shape=jax.ShapeDtypeStruct((2, 9, RET_CHUNK, LANES), F32),
        compiler_params=_cparams(("parallel",), 32),
        name="retention_tables",
    )(per_lane, per_head)


def _ret_state_kernel(rk_ref, rv_ref, c_ref, s_ref, t_ref, st_ref, r_ref):
    C = RET_CHUNK
    nch = rk_ref.shape[0] // C

    @pl.when(pl.program_id(2) == 0)
    def _():
        r_ref[...] = jnp.zeros_like(r_ref)

    wkb = t_ref[T_WKB]
    gcb = t_ref[T_GCB]
    bd = t_ref[T_BD]
    kvs = []
    for n in range(nch):
        sl = slice(n * C, (n + 1) * C)
        k = _rope(rk_ref[sl, :].astype(F32), c_ref[sl, :], s_ref[sl, :]) * (HEAD_DIM ** -0.5)
        kvs.append(lax.dot_general((k * wkb).astype(BF16), rv_ref[sl, :],
                                   (((0,), (0,)), ((), ())), preferred_element_type=F32))
    state = r_ref[...]
    for n in range(nch - 1, -1, -1):
        st_ref[n] = state
        state = gcb * state + bd * kvs[n]
    r_ref[...] = state


def _ret_main_kernel(rq_ref, rk_ref, rv_ref, rg_ref, c_ref, s_ref, t_ref, st_ref, j_ref,
                     o_ref, sf_ref):
    C = RET_CHUNK
    nch = rq_ref.shape[0] // C

    @pl.when(pl.program_id(2) == 0)
    def _():
        sf_ref[...] = jnp.zeros_like(sf_ref)

    lane = lax.broadcasted_iota(jnp.int32, (1, LANES), 1)
    lo64 = lane < HEAD_DIM
    j_bf = j_ref[...]
    chunks = [slice(n * C, (n + 1) * C) for n in range(nch)]
    nt = (((1,), (1,)), ((), ()))
    tn = (((0,), (0,)), ((), ()))
    q, k, vb = [], [], []
    for sl in chunks:
        cs, sn = c_ref[sl, :], s_ref[sl, :]
        q.append(_rope(rq_ref[sl, :].astype(F32), cs, sn))
        k.append(_rope(rk_ref[sl, :].astype(F32), cs, sn) * (HEAD_DIM ** -0.5))
        vb.append(rv_ref[sl, :])
    scores = []
    for n in range(nch):
        qb = q[n].astype(BF16)
        kb = k[n].astype(BF16)
        zero = jnp.zeros_like(qb)
        scores.append([lax.dot_general(jnp.where(lo64, qb, zero), kb, nt,
                                       preferred_element_type=F32),
                       lax.dot_general(jnp.where(lo64, zero, qb), kb, nt,
                                       preferred_element_type=F32)])
    kvs = [lax.dot_general((k[n] * t_ref[T_WKF]).astype(BF16), vb[n], tn,
                           preferred_element_type=F32) for n in range(nch)]
    cross_b = [jnp.dot((q[n] * t_ref[T_WQB]).astype(BF16), st_ref[n].astype(BF16),
                       preferred_element_type=F32) for n in range(nch)]
    intra = []
    for n in range(nch):
        parts = [jnp.dot((scores[n][hp] * t_ref[T_D0 + hp]).astype(BF16), vb[n],
                         preferred_element_type=F32) for hp in range(2)]
        intra.append(jnp.where(lo64, parts[0], parts[1]))
    state = sf_ref[...]
    states = []
    for n in range(nch):
        states.append(state.astype(BF16))
        state = t_ref[T_GCF] * state + t_ref[T_BD] * kvs[n]
    sf_ref[...] = state
    outs = [intra[n] + cross_b[n]
            + jnp.dot((q[n] * t_ref[T_WQF]).astype(BF16), states[n], preferred_element_type=F32)
            for n in range(nch)]
    mean_sq = [_head_mean_sq(o, j_bf) for o in outs]
    for n, sl in enumerate(chunks):
        ret = outs[n] * lax.rsqrt(mean_sq[n] + NORM_EPS)
        g = rg_ref[sl, :].astype(F32)
        o_ref[sl, :] = (ret * (g * _sigmoid(g))).astype(o_ref.dtype)


def _retention(p3, cos_t, sin_t, tables, j_bf):
    B, L, _ = p3.shape
    C = RET_CHUNK
    tb = min(1024, L)
    nblk = L // tb
    nch = tb // C
    nw = RET_W // LANES
    col = lambda piece: (lambda b, j, s: (b, s, nw * piece + j))
    colr = lambda piece: (lambda b, j, s: (b, nblk - 1 - s, nw * piece + j))
    seq = lambda fn: pl.BlockSpec((None, tb, LANES), fn)
    tab_spec = pl.BlockSpec((None, 9, C, LANES), lambda b, j, s: (j, 0, 0, 0))
    states = pl.pallas_call(
        _ret_state_kernel,
        grid=(B, nw, nblk),
        in_specs=[seq(colr(1)), seq(colr(2)),
                  pl.BlockSpec((tb, LANES), lambda b, j, s: (nblk - 1 - s, 0)),
                  pl.BlockSpec((tb, LANES), lambda b, j, s: (nblk - 1 - s, 0)),
                  tab_spec],
        out_specs=pl.BlockSpec((None, None, nch, LANES, LANES),
                               lambda b, j, s: (b, j, nblk - 1 - s, 0, 0)),
        out_shape=jax.ShapeDtypeStruct((B, nw, L // C, LANES, LANES), F32),
        scratch_shapes=[pltpu.VMEM((LANES, LANES), F32)],
        compiler_params=_cparams(("parallel", "parallel", "arbitrary"), 32),
        name="retention_state",
    )(p3, p3, cos_t, sin_t, tables)
    return pl.pallas_call(
        _ret_main_kernel,
        grid=(B, nw, nblk),
        in_specs=[seq(col(0)), seq(col(1)), seq(col(2)), seq(col(3)),
                  pl.BlockSpec((tb, LANES), lambda b, j, s: (s, 0)),
                  pl.BlockSpec((tb, LANES), lambda b, j, s: (s, 0)),
                  tab_spec,
                  pl.BlockSpec((None, None, nch, LANES, LANES), lambda b, j, s: (b, j, s, 0, 0)),
                  pl.BlockSpec((LANES, LANES), lambda b, j, s: (0, 0))],
        out_specs=pl.BlockSpec((None, tb, LANES), lambda b, j, s: (b, s, j)),
        out_shape=jax.ShapeDtypeStruct((B, L, RET_W), BF16),
        scratch_shapes=[pltpu.VMEM((LANES, LANES), F32)],
        compiler_params=_cparams(("parallel", "parallel", "arbitrary"), 32),
        name="retention_main",
    )(p3, p3, p3, p3, cos_t, sin_t, tables, states, j_bf)


def _merge_kernel(oa_ref, ob_ref, oc_ref, od_ref, g0_ref, g1_ref, g2_ref, g3_ref,
                  wb_ref, wo_ref, x_ref, gn_ref, h_ref):
    tm = x_ref.shape[0]
    nparts = 2 if tm % (2 * HALO) == 0 else 1
    parts = [slice(i * tm // nparts, (i + 1) * tm // nparts) for i in range(nparts)]
    branches = ((oa_ref, g0_ref), (ob_ref, g1_ref), (oc_ref, g2_ref), (od_ref, g3_ref))
    proj = [[jnp.dot(br[rows, :].astype(BF16), wb_ref[n], preferred_element_type=F32)
             for n, (br, _) in enumerate(branches)] for rows in parts]
    merged = []
    for pi, rows in enumerate(parts):
        acc = None
        for n, (_, gr) in enumerate(branches):
            gate = (0.5 * jnp.tanh(gr[rows, :]) + 0.5).astype(F32)
            term = gate * proj[pi][n]
            acc = term if acc is None else acc + term
        merged.append(acc.astype(BF16))
    ys = [jnp.dot(m, wo_ref[...], preferred_element_type=F32) for m in merged]
    for rows, y in zip(parts, ys):
        ms = jnp.mean(y * y, axis=-1, keepdims=True)
        h_ref[rows, :] = x_ref[rows, :] + y * lax.rsqrt(ms + NORM_EPS) * gn_ref[...]


def _merge(oa, ob, oc, od, p2, wb_bf, wo_bf, x2, gn):
    T = x2.shape[0]
    tm = min(1024, T)
    br = pl.BlockSpec((tm, BRANCH_W), lambda i: (i, 0))
    gate = lambda n: pl.BlockSpec((tm, D_MODEL), lambda i, n=n: (i, n))
    row = pl.BlockSpec((tm, D_MODEL), lambda i: (i, 0))
    return pl.pallas_call(
        _merge_kernel,
        grid=(T // tm,),
        in_specs=[br, br, br, br, gate(0), gate(1), gate(2), gate(3),
                  pl.BlockSpec((N_BRANCH, BRANCH_W, D_MODEL), lambda i: (0, 0, 0)),
                  pl.BlockSpec((D_MODEL, D_MODEL), lambda i: (0, 0)),
                  row, pl.BlockSpec((1, D_MODEL), lambda i: (0, 0))],
        out_specs=row,
        out_shape=jax.ShapeDtypeStruct((T, D_MODEL), F32),
        compiler_params=_cparams(("parallel",), 48),
        name="merge_out_proj",
    )(oa, ob, oc, od, p2, p2, p2, p2, wb_bf, wo_bf, x2, gn)


def _ffn_kernel(h_ref, g2_ref, wg_ref, wu_ref, wd_ref, g3_ref, o_ref):
    h = h_ref[...]
    ms = jnp.mean(h * h, axis=-1, keepdims=True)
    hn = (h * lax.rsqrt(ms + NORM_EPS) * g2_ref[...]).astype(BF16)
    f = None
    for lo, hi in _col_chunks(D_FF, 2):
        g = jnp.dot(hn, wg_ref[:, lo:hi], preferred_element_type=F32)
        u = jnp.dot(hn, wu_ref[:, lo:hi], preferred_element_type=F32)
        a = (g * _sigmoid(g) * u).astype(BF16)
        part = jnp.dot(a, wd_ref[lo:hi, :], preferred_element_type=F32)
        f = part if f is None else f + part
    ms = jnp.mean(f * f, axis=-1, keepdims=True)
    o_ref[...] = h + f * lax.rsqrt(ms + NORM_EPS) * g3_ref[...]


def _ffn(h2, g2, wg_bf, wu_bf, wd_bf, g3):
    T = h2.shape[0]
    tm = min(512, T)
    row = pl.BlockSpec((tm, D_MODEL), lambda i: (i, 0))
    vec = pl.BlockSpec((1, D_MODEL), lambda i: (0, 0))
    return pl.pallas_call(
        _ffn_kernel,
        grid=(T // tm,),
        in_specs=[row, vec,
                  pl.BlockSpec((D_MODEL, D_FF), lambda i: (0, 0)),
                  pl.BlockSpec((D_MODEL, D_FF), lambda i: (0, 0)),
                  pl.BlockSpec((D_FF, D_MODEL), lambda i: (0, 0)),
                  vec],
        out_specs=row,
        out_shape=jax.ShapeDtypeStruct((T, D_MODEL), F32),
        compiler_params=_cparams(("parallel",), 48),
        name="ffn",
    )(h2, g2, wg_bf, wu_bf, wd_bf, g3)


def _rope_tables(L):
    rows = L // GRID_W
    r = jnp.repeat(jnp.arange(rows, dtype=F32), GRID_W)
    c = jnp.tile(jnp.arange(GRID_W, dtype=F32), rows)
    inv = ROPE_BASE ** (-jnp.arange(ROPE_FREQS, dtype=F32) / ROPE_FREQS)
    ar = r[:, None] * inv
    ac = c[:, None] * inv
    cos64 = jnp.concatenate([jnp.cos(ar), jnp.cos(ar), jnp.cos(ac), jnp.cos(ac)], axis=1)
    sin64 = jnp.concatenate([-jnp.sin(ar), jnp.sin(ar), -jnp.sin(ac), jnp.sin(ac)], axis=1)
    return jnp.tile(cos64, (1, 2)), jnp.tile(sin64, (1, 2))


def _filter_features(L):
    t = np.linspace(0.0, 1.0, L)[:, None]
    f = np.linspace(1e-4, HY_BANDS - 1, HY_BANDS)
    ang = (2.0 * math.pi / L) * np.arange(L)[:, None] * f[None, :]
    z = np.concatenate([t, np.cos(ang), -np.sin(ang)], axis=-1)
    z = np.pad(z, ((0, 0), (0, HY_FILTER_HIDDEN - HY_EMB)))
    idx = np.concatenate([np.arange(L), [0], np.arange(L - 1, 0, -1)])
    return jnp.asarray(z[idx].astype(np.float32))


def _head_mean_matrix():
    i = np.arange(LANES)
    j = ((i[:, None] // HEAD_DIM) == (i[None, :] // HEAD_DIM)).astype(np.float32) / HEAD_DIM
    return jnp.asarray(j.astype(BF16))


def _halve_gate_columns(w_in_bf):
    scale = jnp.where(jnp.arange(IN_COLS) >= GATE_OFF, 0.5, 1.0).astype(BF16)
    return w_in_bf * scale


def _trunk(x, wts, fft_dims=None):
    B, L, _ = x.shape
    T = B * L
    n1, n2 = fft_dims or _fft_dims(L)
    cst = _fft_consts(n1, n2, n1 // 2)
    cos_t, sin_t = _rope_tables(L)
    z2 = _filter_features(L)
    deltas = jnp.abs(jnp.linspace(math.log(HY_TARGET) / HY_SLOW_DECAY,
                                  math.log(HY_TARGET) / HY_FAST_DECAY, HY_WIDTH, dtype=F32))[None, :]
    j_bf = _head_mean_matrix()
    depth = wts["w_in"].shape[0]
    x2 = x.reshape(T, D_MODEL)
    for l in range(depth):
        ng = wts["norm_gains"][l]
        w1p = jnp.pad(wts["hy_w1"][l], ((0, HY_FILTER_HIDDEN - HY_EMB), (0, 0)))
        k2raw, sumsq = _hy_filter_raw(z2, w1p, wts["hy_b1"][l][None, :], wts["hy_w2"][l],
                                      wts["hy_b2"][l][None, :], wts["hy_w3"][l],
                                      wts["hy_freq"][l], deltas, L)
        kf_re, kf_im = _filter_fft(k2raw, sumsq, cst, n1, n2)

        p_att, p_ret, p_gate, z, x0u, out_d = _in_proj(
            x2, ng[0][None, :], wts["w_in_bf"][l], wts["hy_conv_w"][l],
            wts["hy_conv_b"][l][None, :], wts["sc_conv_w"][l], L)

        qg = jnp.tile(wts["qk_norm"][l, 0], 2)[None, :]
        kg = jnp.tile(wts["qk_norm"][l, 1], 2)[None, :]
        qt, k, vt = _attn_prep(p_att.reshape(B, L, HY_OFF), cos_t, sin_t, qg, kg, j_bf)
        out_a = _flash(qt, k, vt)

        out_b = _hy_conv(z.reshape(B, L, HY_WIDTH), x0u.reshape(B, L, HY_WIDTH),
                         wts["hy_bias"][l][None, :], kf_re, kf_im, cst, n1, n2)

        tables = _ret_tables(wts["ret_decay_exp"][l])
        out_c = _retention(p_ret.reshape(B, L, 4 * RET_W), cos_t, sin_t, tables, j_bf)

        h2 = _merge(out_a.reshape(T, BRANCH_W), out_b.reshape(T, BRANCH_W),
                    out_c.reshape(T, BRANCH_W), out_d,
                    p_gate, wts["w_branch_bf"][l], wts["w_out_bf"][l], x2, ng[1][None, :])
        x2 = _ffn(h2, ng[2][None, :], wts["w_gate_bf"][l], wts["w_up_bf"][l],
                  wts["w_ffn_out_bf"][l], ng[3][None, :])
    return x2.reshape(B, L, D_MODEL)


def kernel(x_prompt, x_sample, norm_gains, w_in, qk_norm, hy_conv_w, hy_conv_b, hy_w1, hy_b1, hy_w2,
           hy_b2, hy_w3, hy_freq, hy_bias, ret_decay_exp, sc_conv_w, w_branch, w_out, w_ffn_in,
           w_ffn_out):
    wts = dict(norm_gains=norm_gains, w_in=w_in, qk_norm=qk_norm, hy_conv_w=hy_conv_w,
               hy_conv_b=hy_conv_b, hy_w1=hy_w1, hy_b1=hy_b1, hy_w2=hy_w2, hy_b2=hy_b2,
               hy_w3=hy_w3, hy_freq=hy_freq, hy_bias=hy_bias, ret_decay_exp=ret_decay_exp,
               sc_conv_w=sc_conv_w,
               w_in_bf=_halve_gate_columns(w_in.astype(BF16)), w_branch_bf=w_branch.astype(BF16),
               w_out_bf=w_out.astype(BF16), w_gate_bf=w_ffn_in[..., :D_FF].astype(BF16),
               w_up_bf=w_ffn_in[..., D_FF:].astype(BF16), w_ffn_out_bf=w_ffn_out.astype(BF16))
    return _trunk(x_prompt, wts), _trunk(x_sample, wts)
```

```python
import functools
import math

import numpy as np
import jax
import jax.numpy as jnp
from jax import lax
from jax.experimental import pallas as pl
from jax.experimental.pallas import tpu as pltpu

F32 = jnp.float32
BF16 = jnp.bfloat16

D_MODEL = 1024
GRID_W = 64
N_BRANCH = 4
BRANCH_W = 256
HEAD_DIM = 64
ATT_Q_HEADS = 4
ATT_KV_HEADS = 2
ROPE_BASE = 10000.0
ROPE_FREQS = HEAD_DIM // 4
HY_WIDTH = BRANCH_W
HY_EMB = 33
HY_BANDS = (HY_EMB - 1) // 2
HY_FILTER_HIDDEN = 64
HY_FAST_DECAY = 0.3
HY_SLOW_DECAY = 1.5
HY_TARGET = 1e-2
RET_HEADS = 4
RET_W = RET_HEADS * HEAD_DIM
RET_CHUNK = 128
SC_WIDTH = BRANCH_W
D_FF = 2816
NORM_EPS = 1e-6

ATT_Q_W = ATT_Q_HEADS * HEAD_DIM
ATT_KV_W = ATT_KV_HEADS * HEAD_DIM
A_K_OFF = ATT_Q_W
A_V_OFF = A_K_OFF + ATT_KV_W
HY_OFF = A_V_OFF + ATT_KV_W
RET_OFF = HY_OFF + 3 * HY_WIDTH
SC_OFF = RET_OFF + 4 * RET_W
GATE_OFF = SC_OFF + 3 * SC_WIDTH
IN_COLS = GATE_OFF + N_BRANCH * D_MODEL

LANES = 128
SUBLANES = 8
MXU_W = 256
HALO = 2 * SUBLANES
ATT_KB = 512
ATT_TQ = 256
VT_ROWS = HEAD_DIM + HALO
Q_SCALE = HEAD_DIM ** -0.5 * math.log2(math.e)
FFT_PASSES_FILTER = 3
FFT_PASSES_DATA = 1
FFT_CH = 8
FFT_SETS = 1
MIB = 1 << 20


def _cparams(sem, vmem_mib):
    return pltpu.CompilerParams(dimension_semantics=sem, vmem_limit_bytes=vmem_mib * MIB)


def _sigmoid(x):
    return 0.5 * jnp.tanh(0.5 * x) + 0.5


def _split(x):
    hi = x.astype(BF16)
    lo = (x - hi.astype(F32)).astype(BF16)
    return hi, lo


def _np_split(a64):
    a32 = np.asarray(a64, np.float32)
    hi = a32.astype(BF16)
    lo = (a32 - hi.astype(np.float32)).astype(BF16)
    return jnp.asarray(hi), jnp.asarray(lo)


def _mm_const(a_hi, a_lo, x, passes):
    if passes == 1:
        return jnp.dot(a_hi, x.astype(BF16), preferred_element_type=F32)
    xh, xl = _split(x)
    out = jnp.dot(a_hi, xh, preferred_element_type=F32)
    if passes >= 3:
        out = out + jnp.dot(a_lo, xh, preferred_element_type=F32)
        out = out + jnp.dot(a_hi, xl, preferred_element_type=F32)
    return out


def _mm3(a, b):
    ah, al = _split(a)
    bh, bl = _split(b)
    out = jnp.dot(ah, bh, preferred_element_type=F32)
    out = out + jnp.dot(al, bh, preferred_element_type=F32)
    return out + jnp.dot(ah, bl, preferred_element_type=F32)


def _head_mean_sq(x, j_bf):
    hi, lo = _split(x * x)
    return (jnp.dot(hi, j_bf, preferred_element_type=F32)
            + jnp.dot(lo, j_bf, preferred_element_type=F32))


def _rope(x, c, s):
    lane = lax.broadcasted_iota(jnp.int32, (1, LANES), 1)
    is_b = (lane & 16) != 0
    partner = jnp.where(is_b, pltpu.roll(x, 16, 1), pltpu.roll(x, LANES - 16, 1))
    return x * c + partner * s


def _col_chunks(n, parts):
    tiles = n // MXU_W
    bounds = [MXU_W * (tiles * i // parts) for i in range(parts + 1)]
    return list(zip(bounds[:-1], bounds[1:]))


def _conv3_rows(main, before, after, w):
    tb = main.shape[0]
    w0, w1, w2 = w[0:1, :], w[1:2, :], w[2:3, :]
    body = pltpu.roll(main, 1, 0) * w0 + main * w1 + pltpu.roll(main, tb - 1, 0) * w2
    row_first = before * w0 + main[0:1, :] * w1 + main[1:2, :] * w2
    row_last = main[tb - 2:tb - 1, :] * w0 + main[tb - 1:tb, :] * w1 + after * w2
    row = lax.broadcasted_iota(jnp.int32, (SUBLANES, 1), 0)
    head = jnp.where(row == 0, row_first, body[:SUBLANES])
    tail = jnp.where(row == SUBLANES - 1, row_last, body[tb - SUBLANES:])
    return jnp.concatenate([head, body[SUBLANES:tb - SUBLANES], tail], axis=0)


def _in_proj_kernel(x_ref, xp_ref, xn_ref, g_ref, w_ref, hcw_ref, hcb_ref, scw_ref,
                    c_ref, s_ref, qg_ref, kg_ref, j_ref,
                    qt_ref, k_ref, vt_ref, pr_ref, pg_ref, z_ref, x0_ref, od_ref, *,
                    tiles_per_seq):
    i = pl.program_id(0)
    first = (i % tiles_per_seq) == 0
    last = (i % tiles_per_seq) == tiles_per_seq - 1
    g = g_ref[...]

    def normed(x):
        ms = jnp.mean(x * x, axis=-1, keepdims=True)
        return (x * lax.rsqrt(ms + NORM_EPS) * g).astype(BF16)

    dot = lambda a, lo, hi: jnp.dot(a, w_ref[:, lo:hi], preferred_element_type=F32)
    xn = normed(x_ref[...])
    halo = normed(jnp.concatenate([xp_ref[...], xn_ref[...]], axis=0))
    edge = lambda h: (jnp.where(first, 0.0, h[SUBLANES - 1:SUBLANES, :]),
                      jnp.where(last, 0.0, h[SUBLANES:SUBLANES + 1, :]))

    head = dot(xn, 0, RET_OFF)
    hy_halo = dot(halo, HY_OFF, RET_OFF)
    sc = dot(xn, SC_OFF, GATE_OFF)
    sc_halo = dot(halo, SC_OFF + SC_WIDTH, GATE_OFF)
    pr_ref[...] = dot(xn, RET_OFF, SC_OFF).astype(pr_ref.dtype)
    _attn_prep_tile(head[:, :HY_OFF], c_ref, s_ref, qg_ref, kg_ref, j_ref, qt_ref, k_ref, vt_ref)

    u = _conv3_rows(head[:, HY_OFF:], *edge(hy_halo), hcw_ref[...]) + hcb_ref[...]
    x0_ref[...] = u[:, :HY_WIDTH]
    z_ref[...] = u[:, 2 * HY_WIDTH:] * u[:, HY_WIDTH:2 * HY_WIDTH]
    m = sc[:, SC_WIDTH:2 * SC_WIDTH] * sc[:, 2 * SC_WIDTH:]
    mh = sc_halo[:, :SC_WIDTH] * sc_halo[:, SC_WIDTH:]
    od_ref[...] = (sc[:, :SC_WIDTH] * _conv3_rows(m, *edge(mh), scw_ref[...])).astype(od_ref.dtype)

    for lo, hi in _col_chunks(N_BRANCH * D_MODEL, 2):
        g_half = dot(xn, GATE_OFF + lo, GATE_OFF + hi).astype(pg_ref.dtype)
        pg_ref[:, lo:hi] = 0.5 * jnp.tanh(g_half) + 0.5


def _in_proj(x2, g, w_bf, hcw, hcb, scw, cos_t, sin_t, qg, kg, j_bf, L):
    T = x2.shape[0]
    B = T // L
    tm = min(ATT_KB, L)
    tq = min(ATT_TQ, L)
    tps = L // tm
    r8 = tm // SUBLANES
    row = lambda w: pl.BlockSpec((tm, w), lambda i: (i, 0))
    full = lambda a: pl.BlockSpec(a.shape, lambda i: (0,) * a.ndim)
    tab = pl.BlockSpec((tm, LANES), lambda i: (i % tps, 0))
    sds = lambda w, dt: jax.ShapeDtypeStruct((T, w), dt)
    return pl.pallas_call(
        functools.partial(_in_proj_kernel, tiles_per_seq=tps),
        grid=(T // tm,),
        in_specs=[row(D_MODEL),
                  pl.BlockSpec((SUBLANES, D_MODEL), lambda i: (jnp.maximum(i * r8 - 1, 0), 0)),
                  pl.BlockSpec((SUBLANES, D_MODEL),
                               lambda i: (jnp.minimum((i + 1) * r8, T // SUBLANES - 1), 0)),
                  full(g), full(w_bf), full(hcw), full(hcb), full(scw),
                  tab, tab, full(qg), full(kg), full(j_bf)],
        out_specs=[pl.BlockSpec((None, ATT_Q_HEADS, tm // tq, LANES, tq),
                                lambda i: (i // tps, 0, i % tps, 0, 0)),
                   pl.BlockSpec((None, tm, LANES), lambda i: (i // tps, i % tps, 0)),
                   pl.BlockSpec((None, tm // ATT_KB, ATT_KV_HEADS * VT_ROWS, ATT_KB),
                                lambda i: (i // tps, i % tps, 0, 0)),
                   row(RET_W * 4), row(N_BRANCH * D_MODEL),
                   row(HY_WIDTH), row(HY_WIDTH), row(SC_WIDTH)],
        out_shape=[jax.ShapeDtypeStruct((B, ATT_Q_HEADS, L // tq, LANES, tq), BF16),
                   jax.ShapeDtypeStruct((B, L, LANES), BF16),
                   jax.ShapeDtypeStruct((B, L // ATT_KB, ATT_KV_HEADS * VT_ROWS, ATT_KB), BF16),
                   sds(RET_W * 4, BF16), sds(N_BRANCH * D_MODEL, BF16),
                   sds(HY_WIDTH, F32), sds(HY_WIDTH, F32), sds(SC_WIDTH, BF16)],
        compiler_params=_cparams(("parallel",), 48),
        name="in_proj",
    )(x2, x2, x2, g, w_bf, hcw, hcb, scw, cos_t, sin_t, qg, kg, j_bf)


def _attn_prep_tile(qkv, c_ref, s_ref, qg_ref, kg_ref, j_ref, qt_ref, k_ref, vt_ref):
    c = c_ref[...]
    s = s_ref[...]
    j_bf = j_ref[...]
    tm = qkv.shape[0]
    tq = qt_ref.shape[-1]
    zeros = jnp.zeros((HEAD_DIM, tm), BF16)
    for blk in range(2):
        q = qkv[:, LANES * blk:LANES * (blk + 1)]
        qn = q * lax.rsqrt(_head_mean_sq(q, j_bf) + NORM_EPS) * qg_ref[...]
        qt = (_rope(qn, c, s) * Q_SCALE).astype(BF16).T
        lo, hi = qt[:HEAD_DIM], qt[HEAD_DIM:]
        if blk == 0:
            heads = (jnp.concatenate([lo, zeros], axis=0), jnp.concatenate([hi, zeros], axis=0))
        else:
            heads = (jnp.concatenate([zeros, lo], axis=0), jnp.concatenate([zeros, hi], axis=0))
        for hp, qh in enumerate(heads):
            for t in range(tm // tq):
                qt_ref[2 * blk + hp, t] = qh[:, t * tq:(t + 1) * tq]
    k = qkv[:, A_K_OFF:A_K_OFF + LANES]
    kn = k * lax.rsqrt(_head_mean_sq(k, j_bf) + NORM_EPS) * kg_ref[...]
    k_ref[...] = _rope(kn, c, s).astype(BF16)
    v = qkv[:, A_V_OFF:A_V_OFF + LANES].astype(BF16)
    ones = jnp.ones((VT_ROWS - HEAD_DIM, ATT_KB), BF16)
    for t in range(tm // ATT_KB):
        vt = v[t * ATT_KB:(t + 1) * ATT_KB, :].T
        vt_ref[t] = jnp.concatenate([vt[:HEAD_DIM], ones, vt[HEAD_DIM:], ones], axis=0)


def _flash_kernel(qt_ref, k_ref, vt_ref, o_ref, sa_ref, ma_ref, sb_ref, mb_ref, *,
                  ngrp, nsub, ntile):
    tq = ATT_TQ
    group = ATT_Q_HEADS // ATT_KV_HEADS

    def produce(slot, qi, g, t, h):
        s_ref, mx_ref = slot
        st = pl.multiple_of((g * nsub + t) * ATT_KB, ATT_KB)
        s = jnp.dot(k_ref[pl.ds(st, ATT_KB), :], qt_ref[h, qi],
                    preferred_element_type=F32)
        s_ref[t * ATT_Q_HEADS + h] = s
        mx_ref[t * ATT_Q_HEADS + h] = jnp.max(s, axis=0, keepdims=True)

    def consume(slot, g, t, h, state):
        s_ref, mx_ref = slot
        m, acc = state
        m_new = jnp.maximum(m, mx_ref[t * ATT_Q_HEADS + h])
        alpha = jnp.exp2(m - m_new)
        p = jnp.exp2((s_ref[t * ATT_Q_HEADS + h] - m_new).astype(BF16))
        j = h // group
        vt = vt_ref[g * nsub + t, VT_ROWS * j:VT_ROWS * (j + 1), :]
        return m_new, alpha * acc + jnp.dot(vt, p, preferred_element_type=F32)

    def step(slot_in, g_in, slot_out, qi_out, g_out, carry):
        carry = list(carry)
        for t in range(nsub):
            for h in range(ATT_Q_HEADS):
                if slot_out is not None:
                    produce(slot_out, qi_out, g_out, t, h)
                if slot_in is not None:
                    carry[h] = consume(slot_in, g_in, t, h, carry[h])
        return tuple(carry)

    def fresh():
        return tuple((jnp.full((1, tq), -jnp.inf, F32), jnp.zeros((VT_ROWS, tq), F32))
                     for _ in range(ATT_Q_HEADS))

    def finalize(qi, carry):
        outs = [acc[:HEAD_DIM] / acc[HEAD_DIM:HEAD_DIM + 1] for _, acc in carry]
        rows = pl.ds(pl.multiple_of(qi * tq, tq), tq)
        o_ref[rows, :] = jnp.concatenate(outs, axis=0).T.astype(o_ref.dtype)

    slot_a = (sa_ref, ma_ref)
    slot_b = (sb_ref, mb_ref)

    if ngrp == 1:
        def tile(qi, _):
            step(None, None, slot_a, qi, 0, ())
            finalize(qi, step(slot_a, 0, None, None, None, fresh()))
            return 0
    else:
        step(None, None, slot_a, 0, 0, ())

        def tile(qi, _):
            def body(i, carry):
                g = 2 * i
                carry = step(slot_a, g, slot_b, qi, g + 1, carry)
                return step(slot_b, g + 1, slot_a, qi, g + 2, carry)

            carry = lax.fori_loop(0, ngrp // 2 - 1, body, fresh())
            carry = step(slot_a, ngrp - 2, slot_b, qi, ngrp - 1, carry)
            carry = step(slot_b, ngrp - 1, slot_a, jnp.minimum(qi + 1, ntile - 1), 0, carry)
            finalize(qi, carry)
            return 0

    lax.fori_loop(0, ntile, tile, 0)


def _flash(qt, k, vt):
    B, _, nqt, _, tq = qt.shape
    L = nqt * tq
    ntile = math.gcd(nqt, 4)
    nkb = L // ATT_KB
    nsub = 2 if nkb % 4 == 0 and nkb >= 8 else 1
    ngrp = L // (ATT_KB * nsub)
    assert ngrp == 1 or ngrp % 2 == 0
    s_scratch = pltpu.VMEM((nsub * ATT_Q_HEADS, ATT_KB, tq), F32)
    m_scratch = pltpu.VMEM((nsub * ATT_Q_HEADS, 1, tq), F32)
    return pl.pallas_call(
        functools.partial(_flash_kernel, ngrp=ngrp, nsub=nsub, ntile=ntile),
        grid=(B, nqt // ntile),
        in_specs=[pl.BlockSpec((None, ATT_Q_HEADS, ntile, LANES, tq),
                               lambda b, i: (b, 0, i, 0, 0)),
                  pl.BlockSpec((None, L, LANES), lambda b, i: (b, 0, 0)),
                  pl.BlockSpec((None, L // ATT_KB, ATT_KV_HEADS * VT_ROWS, ATT_KB),
                               lambda b, i: (b, 0, 0, 0))],
        out_specs=pl.BlockSpec((None, ntile * tq, ATT_Q_W), lambda b, i: (b, i, 0)),
        out_shape=jax.ShapeDtypeStruct((B, L, ATT_Q_W), BF16),
        scratch_shapes=[s_scratch, m_scratch, s_scratch, m_scratch],
        compiler_params=_cparams(("parallel", "arbitrary"), 48),
        name="flash_attn",
    )(qt, k, vt)


def _hy_filter_kernel(z_ref, w1_ref, b1_ref, w2_ref, b2_ref, w3_ref, fr_ref, dl_ref,
                      k_ref, ss_ref, *, L):
    step = pl.program_id(0)
    tr = z_ref.shape[0]
    z = z_ref[...]
    h = jnp.sin(fr_ref[0:1, :] * (_mm3(z, w1_ref[...]) + b1_ref[...]))
    h = jnp.sin(fr_ref[1:2, :] * (_mm3(h, w2_ref[...]) + b2_ref[...]))
    h3 = _mm3(h, w3_ref[...])
    row = step * tr + lax.broadcasted_iota(jnp.int32, (tr, 1), 0)
    val = jnp.where(row < L, h3[:, :HY_WIDTH], h3[:, HY_WIDTH:])
    val = val * jnp.exp(-z[:, 0:1] * dl_ref[...])
    val = jnp.where(row == L, 0.0, val)
    k_ref[...] = val

    @pl.when(step == 0)
    def _():
        ss_ref[...] = jnp.zeros_like(ss_ref)

    ss_ref[...] += jnp.sum(val * val, axis=0, keepdims=True)


def _hy_filter_raw(z2, w1p, b1, w2, b2, w3, freq, deltas, L):
    n = 2 * L
    tr = min(512, n)
    H = HY_FILTER_HIDDEN
    full = lambda shape: pl.BlockSpec(shape, lambda i: (0, 0))
    return pl.pallas_call(
        functools.partial(_hy_filter_kernel, L=L),
        grid=(n // tr,),
        in_specs=[pl.BlockSpec((tr, H), lambda i: (i, 0)),
                  full((H, H)), full((1, H)), full((H, H)), full((1, H)),
                  full((H, 2 * HY_WIDTH)), full((2, H)), full((1, HY_WIDTH))],
        out_specs=[pl.BlockSpec((tr, HY_WIDTH), lambda i: (i, 0)), full((1, HY_WIDTH))],
        out_shape=[jax.ShapeDtypeStruct((n, HY_WIDTH), F32),
                   jax.ShapeDtypeStruct((1, HY_WIDTH), F32)],
        compiler_params=_cparams(("arbitrary",), 32),
        name="hyena_filter",
    )(z2, w1p, b1, w2, b2, w3, freq, deltas)


def _fft_dims(L):
    n = 2 * L
    n2 = 128 if n >= 32 * 128 else 64
    return n // n2, n2


def _fft_consts(n1, n2, r):
    n = n1 * n2
    a1 = 2.0 * np.pi * np.outer(np.arange(n1), np.arange(n1)) / n1
    c1, s1 = np.cos(a1), np.sin(a1)
    a2 = 2.0 * np.pi * np.outer(np.arange(n2), np.arange(n2)) / n2
    c2, s2 = np.cos(a2), np.sin(a2)
    f1 = np.block([[c1[:, :r], s1[:, :r]], [-s1[:, :r], c1[:, :r]]])
    f1_full = np.concatenate([c1, -s1], axis=0)
    m2 = np.block([[c2, s2], [-s2, c2]])
    m2i = np.block([[c2, -s2], [s2, c2]])
    fi = np.block([[c1[:r, :], -s1[:r, :]], [s1[:r, :], c1[:r, :]]])
    aw = 2.0 * np.pi * np.arange(n1) / n
    off = np.repeat(np.arange(FFT_CH), LANES)[None, :] * aw[:, None]
    step = np.broadcast_to((FFT_CH * aw)[:, None], (n1, LANES))
    f32 = lambda a: jnp.asarray(np.asarray(a, np.float32))
    return dict(f1=_np_split(f1), f1_full=_np_split(f1_full), m2=_np_split(m2),
                m2i=_np_split(m2i), fi=_np_split(fi),
                tw=(f32(np.cos(off)), f32(-np.sin(off)), f32(np.cos(step)), f32(-np.sin(step))))


def _cmul(a_re, a_im, b_re, b_im):
    return a_re * b_re - a_im * b_im, a_re * b_im + a_im * b_re


def _lane_tile(x, reps):
    return jnp.concatenate([x] * reps, axis=1)


def _lane_part(x, j):
    return x[:, LANES * j:LANES * (j + 1)]


def _chunk_twiddle(t_re, t_im, tw_refs):
    d_re, d_im, s_re, s_im = (ref[...] for ref in tw_refs)
    cur = _cmul(_lane_tile(t_re, FFT_CH), _lane_tile(t_im, FFT_CH), d_re, d_im)
    return cur, _cmul(t_re, t_im, s_re, s_im)


def _twiddle_init(n1):
    return jnp.ones((n1, LANES), F32), jnp.zeros((n1, LANES), F32)


def _fft_stage1(load_rows, a_ref, f_hi, f_lo, tw_refs, n1, n2, passes):
    def body(ci, carry):
        i0s, tws = [], []
        for u in range(FFT_SETS):
            tw, carry = _chunk_twiddle(*carry, tw_refs)
            tws.append(tw)
            i0s.append((ci * FFT_SETS + u) * FFT_CH)
        prods = [_mm_const(f_hi, f_lo, jnp.concatenate(
            [load_rows(i0 + j) for j in range(FFT_CH)], axis=1), passes)
            for i0 in i0s]
        for i0, (t_re, t_im), a in zip(i0s, tws, prods):
            o_re, o_im = _cmul(a[:n1], a[n1:], t_re, t_im)
            for j in range(FFT_CH):
                base = pl.multiple_of((i0 + j) * 2 * n1, 2 * n1)
                a_ref[pl.ds(base, n1), :] = _lane_part(o_re, j)
                a_ref[pl.ds(base + n1, n1), :] = _lane_part(o_im, j)
        return carry

    lax.fori_loop(0, n2 // (FFT_CH * FFT_SETS), body, _twiddle_init(n1))


def _load_spectrum_rows(a_ref, k1, n1, n2):
    ld = lambda off: a_ref[pl.ds(off, n2, stride=2 * n1), :]
    return jnp.concatenate(
        [jnp.concatenate([ld(k1 + j) for j in range(FFT_CH)], axis=1),
         jnp.concatenate([ld(n1 + k1 + j) for j in range(FFT_CH)], axis=1)], axis=0)


def _filter_fft_kernel(k_ref, ss_ref, f_hi_ref, f_lo_ref, m_hi_ref, m_lo_ref,
                       dre_ref, dim_ref, sre_ref, sim_ref, kr_ref, ki_ref, a_ref, *, n1, n2):
    c = pl.program_id(1)

    @pl.when(c == 0)
    def _():
        _fft_stage1(lambda i: k_ref[pl.ds(i, n1, stride=n2), :], a_ref,
                    f_hi_ref[...], f_lo_ref[...], (dre_ref, dim_ref, sre_ref, sim_ref),
                    n1, n2, FFT_PASSES_FILTER)

    scale = _lane_tile(lax.rsqrt(ss_ref[...] + NORM_EPS) * (1.0 / (n1 * n2)), FFT_CH)
    xin = _load_spectrum_rows(a_ref, c * FFT_CH, n1, n2)
    x = _mm_const(m_hi_ref[...], m_lo_ref[...], xin, FFT_PASSES_FILTER)
    kr_ref[...] = x[:n2] * scale
    ki_ref[...] = x[n2:] * scale


def _filter_fft(k2raw, sumsq, cst, n1, n2):
    n = n1 * n2
    nw = HY_WIDTH // LANES
    consts = [*cst["f1_full"], *cst["m2"], *cst["tw"]]
    full = lambda a: pl.BlockSpec(a.shape, lambda w, c: (0,) * a.ndim)
    out_spec = pl.BlockSpec((None, None, n2, FFT_CH * LANES), lambda w, c: (w, c, 0, 0))
    out_sh = jax.ShapeDtypeStruct((nw, n1 // FFT_CH, n2, FFT_CH * LANES), F32)
    return pl.pallas_call(
        functools.partial(_filter_fft_kernel, n1=n1, n2=n2),
        grid=(nw, n1 // FFT_CH),
        in_specs=[pl.BlockSpec((n, LANES), lambda w, c: (0, w)),
                  pl.BlockSpec((1, LANES), lambda w, c: (0, w))] + [full(a) for a in consts],
        out_specs=[out_spec, out_spec],
        out_shape=[out_sh, out_sh],
        scratch_shapes=[pltpu.VMEM((2 * n, LANES), F32)],
        compiler_params=_cparams(("parallel", "arbitrary"), 48),
        name="hyena_filter_fft",
    )(k2raw, sumsq, *consts)


def _hy_conv_kernel(z_ref, x0_ref, hb_ref, kr_ref, ki_ref, f_hi_ref, f_lo_ref,
                    m_hi_ref, m_lo_ref, mi_hi_ref, mi_lo_ref, fi_hi_ref, fi_lo_ref,
                    dre_ref, dim_ref, sre_ref, sim_ref, o_ref, a_ref, *, n1, n2):
    c = pl.program_id(2)
    r = n1 // 2
    tw_refs = (dre_ref, dim_ref, sre_ref, sim_ref)

    def load_pair(i):
        seq = pl.ds(i, r, stride=n2)
        return jnp.concatenate([z_ref[0, seq, :], z_ref[1, seq, :]], axis=0)

    @pl.when(c == 0)
    def _():
        _fft_stage1(load_pair, a_ref, f_hi_ref[...], f_lo_ref[...], tw_refs, n1, n2,
                    FFT_PASSES_DATA)

    k1s = [(c * FFT_SETS + u) * FFT_CH for u in range(FFT_SETS)]
    xs = [_mm_const(m_hi_ref[...], m_lo_ref[...], _load_spectrum_rows(a_ref, k1, n1, n2),
                    FFT_PASSES_DATA) for k1 in k1s]
    ys = [jnp.concatenate(_cmul(x[:n2], x[n2:], kr_ref[u], ki_ref[u]), axis=0)
          for u, x in enumerate(xs)]
    bqs = [_mm_const(mi_hi_ref[...], mi_lo_ref[...], y, FFT_PASSES_DATA) for y in ys]
    for k1, bq in zip(k1s, bqs):
        for j in range(FFT_CH):
            a_ref[pl.ds(k1 + j, n2, stride=2 * n1), :] = _lane_part(bq[:n2], j)
            a_ref[pl.ds(n1 + k1 + j, n2, stride=2 * n1), :] = _lane_part(bq[n2:], j)

    @pl.when(c == pl.num_programs(2) - 1)
    def _():
        hb = hb_ref[...]

        def body(ci, carry):
            i0s, bts = [], []
            for u in range(FFT_SETS):
                (t_re, t_im), carry = _chunk_twiddle(*carry, tw_refs)
                i0 = (ci * FFT_SETS + u) * FFT_CH
                rows = [pl.multiple_of((i0 + j) * 2 * n1, 2 * n1) for j in range(FFT_CH)]
                br = jnp.concatenate([a_ref[pl.ds(b0, n1), :] for b0 in rows], axis=1)
                bi = jnp.concatenate([a_ref[pl.ds(b0 + n1, n1), :] for b0 in rows], axis=1)
                bts.append(jnp.concatenate([br * t_re + bi * t_im, bi * t_re - br * t_im],
                                           axis=0))
                i0s.append(i0)
            ys = [_mm_const(fi_hi_ref[...], fi_lo_ref[...], bt, FFT_PASSES_DATA) for bt in bts]
            for i0, y in zip(i0s, ys):
                for j in range(FFT_CH):
                    seq = pl.ds(i0 + j, r, stride=n2)
                    for row in range(2):
                        conv = _lane_part(y[row * r:(row + 1) * r], j)
                        o_ref[row, seq, :] = (x0_ref[row, seq, :]
                                              * (conv + z_ref[row, seq, :] * hb)
                                              ).astype(o_ref.dtype)
            return carry

        lax.fori_loop(0, n2 // (FFT_CH * FFT_SETS), body, _twiddle_init(n1))


def _hy_conv(z, x0u, hbias, kf_re, kf_im, cst, n1, n2):
    B, L, _ = z.shape
    assert B % 2 == 0
    nw = HY_WIDTH // LANES
    seq = pl.BlockSpec((2, L, LANES), lambda b, w, c: (b, 0, w), pipeline_mode=pl.Buffered(1))
    kf_spec = pl.BlockSpec((None, FFT_SETS, n2, FFT_CH * LANES), lambda b, w, c: (w, c, 0, 0))
    full = lambda a: pl.BlockSpec(a.shape, lambda b, w, c: (0,) * a.ndim)
    consts = [*cst["f1"], *cst["m2"], *cst["m2i"], *cst["fi"], *cst["tw"]]
    return pl.pallas_call(
        functools.partial(_hy_conv_kernel, n1=n1, n2=n2),
        grid=(B // 2, nw, n1 // (FFT_CH * FFT_SETS)),
        in_specs=[seq, seq, pl.BlockSpec((1, LANES), lambda b, w, c: (0, w)), kf_spec, kf_spec]
                 + [full(a) for a in consts],
        out_specs=seq,
        out_shape=jax.ShapeDtypeStruct((B, L, HY_WIDTH), F32),
        scratch_shapes=[pltpu.VMEM((2 * n1 * n2, LANES), F32)],
        compiler_params=_cparams(("parallel", "parallel", "arbitrary"), 56),
        name="hyena_conv",
    )(z, x0u, hbias, kf_re, kf_im, *consts)


T_D0, T_D1, T_WQF, T_WQB, T_WKF, T_WKB, T_GCF, T_GCB, T_BD = range(9)


def _ret_tables_kernel(rl_ref, rh_ref, t_ref):
    C = RET_CHUNK
    i = lax.broadcasted_iota(jnp.int32, (C, LANES), 0).astype(F32)
    jn = lax.broadcasted_iota(jnp.int32, (C, LANES), 1).astype(F32)
    log_g = lambda e: jnp.log1p(-jnp.exp2(-e))
    diff = i - jn
    for hp in range(2):
        lgf = log_g(rh_ref[hp, 0:1, :])
        lgb = log_g(rh_ref[hp, 1:2, :])
        fwd = jnp.exp(jnp.where(diff >= 0, diff, 0.0) * lgf)
        bwd = jnp.exp(jnp.where(diff < 0, -diff, 0.0) * lgb)
        t_ref[T_D0 + hp] = jnp.where(diff >= 0, fwd, bwd)
    lf = log_g(rl_ref[0:1, :])
    lb = log_g(rl_ref[1:2, :])
    t_ref[T_WQF] = jnp.exp((i + 1.0) * lf)
    t_ref[T_WQB] = jnp.exp((C - i) * lb)
    t_ref[T_WKF] = jnp.exp((C - 1.0 - i) * lf)
    t_ref[T_WKB] = jnp.exp(i * lb)
    bd = jnp.where((i < HEAD_DIM) == (jn < HEAD_DIM), 1.0, 0.0)
    t_ref[T_GCF] = jnp.exp(C * lf) * bd
    t_ref[T_GCB] = jnp.exp(C * lb) * bd
    t_ref[T_BD] = bd


def _ret_tables(rde):
    per_lane = jnp.repeat(rde.reshape(2, 2, 2), HEAD_DIM, axis=2)
    per_lane = per_lane.transpose(1, 0, 2)
    per_head = jnp.broadcast_to(rde.reshape(2, 2, 2, 1), (2, 2, 2, LANES))
    per_head = per_head.transpose(1, 2, 0, 3)
    return pl.pallas_call(
        _ret_tables_kernel,
        grid=(2,),
        in_specs=[pl.BlockSpec((None, 2, LANES), lambda j: (j, 0, 0)),
                  pl.BlockSpec((None, 2, 2, LANES), lambda j: (j, 0, 0, 0))],
        out_specs=pl.BlockSpec((None, 9, RET_CHUNK, LANES), lambda j: (j, 0, 0, 0)),
        out_shape=jax.ShapeDtypeStruct((2, 9, RET_CHUNK, LANES), F32),
        compiler_params=_cparams(("parallel",), 32),
        name="retention_tables",
    )(per_lane, per_head)


def _ret_state_kernel(rk_ref, rv_ref, c_ref, s_ref, t_ref, st_ref, r_ref):
    C = RET_CHUNK
    nch = rk_ref.shape[0] // C

    @pl.when(pl.program_id(2) == 0)
    def _():
        r_ref[...] = jnp.zeros_like(r_ref)

    wkb = t_ref[T_WKB]
    gcb = t_ref[T_GCB]
    bd = t_ref[T_BD]
    kvs = []
    for n in range(nch):
        sl = slice(n * C, (n + 1) * C)
        k = _rope(rk_ref[sl, :].astype(F32), c_ref[sl, :], s_ref[sl, :]) * (HEAD_DIM ** -0.5)
        kvs.append(lax.dot_general((k * wkb).astype(BF16), rv_ref[sl, :],
                                   (((0,), (0,)), ((), ())), preferred_element_type=F32))
    state = r_ref[...]
    for n in range(nch - 1, -1, -1):
        st_ref[n] = state
        state = gcb * state + bd * kvs[n]
    r_ref[...] = state


def _ret_main_kernel(rq_ref, rk_ref, rv_ref, rg_ref, c_ref, s_ref, t_ref, st_ref, j_ref,
                     o_ref, sf_ref):
    C = RET_CHUNK
    nch = rq_ref.shape[0] // C

    @pl.when(pl.program_id(2) == 0)
    def _():
        sf_ref[...] = jnp.zeros_like(sf_ref)

    lane = lax.broadcasted_iota(jnp.int32, (1, LANES), 1)
    lo64 = lane < HEAD_DIM
    j_bf = j_ref[...]
    chunks = [slice(n * C, (n + 1) * C) for n in range(nch)]
    nt = (((1,), (1,)), ((), ()))
    tn = (((0,), (0,)), ((), ()))
    q, k, vb = [], [], []
    for sl in chunks:
        cs, sn = c_ref[sl, :], s_ref[sl, :]
        q.append(_rope(rq_ref[sl, :].astype(F32), cs, sn))
        k.append(_rope(rk_ref[sl, :].astype(F32), cs, sn) * (HEAD_DIM ** -0.5))
        vb.append(rv_ref[sl, :])
    scores = []
    for n in range(nch):
        qb = q[n].astype(BF16)
        kb = k[n].astype(BF16)
        zero = jnp.zeros_like(qb)
        scores.append([lax.dot_general(jnp.where(lo64, qb, zero), kb, nt,
                                       preferred_element_type=F32),
                       lax.dot_general(jnp.where(lo64, zero, qb), kb, nt,
                                       preferred_element_type=F32)])
    kvs = [lax.dot_general((k[n] * t_ref[T_WKF]).astype(BF16), vb[n], tn,
                           preferred_element_type=F32) for n in range(nch)]
    cross_b = [jnp.dot((q[n] * t_ref[T_WQB]).astype(BF16), st_ref[n].astype(BF16),
                       preferred_element_type=F32) for n in range(nch)]
    intra = []
    for n in range(nch):
        parts = [jnp.dot((scores[n][hp] * t_ref[T_D0 + hp]).astype(BF16), vb[n],
                         preferred_element_type=F32) for hp in range(2)]
        intra.append(jnp.where(lo64, parts[0], parts[1]))
    state = sf_ref[...]
    states = []
    for n in range(nch):
        states.append(state.astype(BF16))
        state = t_ref[T_GCF] * state + t_ref[T_BD] * kvs[n]
    sf_ref[...] = state
    outs = [intra[n] + cross_b[n]
            + jnp.dot((q[n] * t_ref[T_WQF]).astype(BF16), states[n], preferred_element_type=F32)
            for n in range(nch)]
    mean_sq = [_head_mean_sq(o, j_bf) for o in outs]
    for n, sl in enumerate(chunks):
        ret = outs[n] * lax.rsqrt(mean_sq[n] + NORM_EPS)
        g = rg_ref[sl, :].astype(F32)
        o_ref[sl, :] = (ret * (g * _sigmoid(g))).astype(o_ref.dtype)


def _retention(p3, cos_t, sin_t, tables, j_bf):
    B, L, _ = p3.shape
    C = RET_CHUNK
    tb = min(1024, L)
    nblk = L // tb
    nch = tb // C
    nw = RET_W // LANES
    col = lambda piece: (lambda b, j, s: (b, s, nw * piece + j))
    colr = lambda piece: (lambda b, j, s: (b, nblk - 1 - s, nw * piece + j))
    seq = lambda fn: pl.BlockSpec((None, tb, LANES), fn)
    tab_spec = pl.BlockSpec((None, 9, C, LANES), lambda b, j, s: (j, 0, 0, 0))
    states = pl.pallas_call(
        _ret_state_kernel,
        grid=(B, nw, nblk),
        in_specs=[seq(colr(1)), seq(colr(2)),
                  pl.BlockSpec((tb, LANES), lambda b, j, s: (nblk - 1 - s, 0)),
                  pl.BlockSpec((tb, LANES), lambda b, j, s: (nblk - 1 - s, 0)),
                  tab_spec],
        out_specs=pl.BlockSpec((None, None, nch, LANES, LANES),
                               lambda b, j, s: (b, j, nblk - 1 - s, 0, 0)),
        out_shape=jax.ShapeDtypeStruct((B, nw, L // C, LANES, LANES), F32),
        scratch_shapes=[pltpu.VMEM((LANES, LANES), F32)],
        compiler_params=_cparams(("parallel", "parallel", "arbitrary"), 32),
        name="retention_state",
    )(p3, p3, cos_t, sin_t, tables)
    return pl.pallas_call(
        _ret_main_kernel,
        grid=(B, nw, nblk),
        in_specs=[seq(col(0)), seq(col(1)), seq(col(2)), seq(col(3)),
                  pl.BlockSpec((tb, LANES), lambda b, j, s: (s, 0)),
                  pl.BlockSpec((tb, LANES), lambda b, j, s: (s, 0)),
                  tab_spec,
                  pl.BlockSpec((None, None, nch, LANES, LANES), lambda b, j, s: (b, j, s, 0, 0)),
                  pl.BlockSpec((LANES, LANES), lambda b, j, s: (0, 0))],
        out_specs=pl.BlockSpec((None, tb, LANES), lambda b, j, s: (b, s, j)),
        out_shape=jax.ShapeDtypeStruct((B, L, RET_W), BF16),
        scratch_shapes=[pltpu.VMEM((LANES, LANES), F32)],
        compiler_params=_cparams(("parallel", "parallel", "arbitrary"), 32),
        name="retention_main",
    )(p3, p3, p3, p3, cos_t, sin_t, tables, states, j_bf)


def _merge_kernel(oa_ref, ob_ref, oc_ref, od_ref, g0_ref, g1_ref, g2_ref, g3_ref,
                  wb_ref, wo_ref, x_ref, gn_ref, h_ref):
    tm = x_ref.shape[0]
    nparts = 2 if tm % (2 * HALO) == 0 else 1
    parts = [slice(i * tm // nparts, (i + 1) * tm // nparts) for i in range(nparts)]
    branches = ((oa_ref, g0_ref), (ob_ref, g1_ref), (oc_ref, g2_ref), (od_ref, g3_ref))
    proj = [[jnp.dot(br[rows, :].astype(BF16), wb_ref[n], preferred_element_type=F32)
             for n, (br, _) in enumerate(branches)] for rows in parts]
    merged = []
    for pi, rows in enumerate(parts):
        acc = None
        for n, (_, gr) in enumerate(branches):
            term = gr[rows, :].astype(F32) * proj[pi][n]
            acc = term if acc is None else acc + term
        merged.append(acc.astype(BF16))
    ys = [jnp.dot(m, wo_ref[...], preferred_element_type=F32) for m in merged]
    for rows, y in zip(parts, ys):
        ms = jnp.mean(y * y, axis=-1, keepdims=True)
        h_ref[rows, :] = x_ref[rows, :] + y * lax.rsqrt(ms + NORM_EPS) * gn_ref[...]


def _merge(oa, ob, oc, od, p2, wb_bf, wo_bf, x2, gn):
    T = x2.shape[0]
    tm = min(1024, T)
    br = pl.BlockSpec((tm, BRANCH_W), lambda i: (i, 0))
    gate = lambda n: pl.BlockSpec((tm, D_MODEL), lambda i, n=n: (i, n))
    row = pl.BlockSpec((tm, D_MODEL), lambda i: (i, 0))
    return pl.pallas_call(
        _merge_kernel,
        grid=(T // tm,),
        in_specs=[br, br, br, br, gate(0), gate(1), gate(2), gate(3),
                  pl.BlockSpec((N_BRANCH, BRANCH_W, D_MODEL), lambda i: (0, 0, 0)),
                  pl.BlockSpec((D_MODEL, D_MODEL), lambda i: (0, 0)),
                  row, pl.BlockSpec((1, D_MODEL), lambda i: (0, 0))],
        out_specs=row,
        out_shape=jax.ShapeDtypeStruct((T, D_MODEL), F32),
        compiler_params=_cparams(("parallel",), 48),
        name="merge_out_proj",
    )(oa, ob, oc, od, p2, p2, p2, p2, wb_bf, wo_bf, x2, gn)


def _ffn_kernel(h_ref, g2_ref, wg_ref, wu_ref, wd_ref, g3_ref, o_ref):
    tm = h_ref.shape[0]
    halves = [slice(0, tm // 2), slice(tm // 2, tm)] if tm % (2 * HALO) == 0 else [slice(0, tm)]
    hns = []
    for rows in halves:
        h = h_ref[rows, :]
        ms = jnp.mean(h * h, axis=-1, keepdims=True)
        hns.append((h * lax.rsqrt(ms + NORM_EPS) * g2_ref[...]).astype(BF16))
    fs = [None] * len(halves)
    for lo, hi in _col_chunks(D_FF, 2):
        gus = [(jnp.dot(hn, wg_ref[:, lo:hi], preferred_element_type=F32),
                jnp.dot(hn, wu_ref[:, lo:hi], preferred_element_type=F32)) for hn in hns]
        acts = [(g * _sigmoid(g) * u).astype(BF16) for g, u in gus]
        for i, a in enumerate(acts):
            part = jnp.dot(a, wd_ref[lo:hi, :], preferred_element_type=F32)
            fs[i] = part if fs[i] is None else fs[i] + part
    for rows, f in zip(halves, fs):
        ms = jnp.mean(f * f, axis=-1, keepdims=True)
        o_ref[rows, :] = h_ref[rows, :] + f * lax.rsqrt(ms + NORM_EPS) * g3_ref[...]


def _ffn(h2, g2, wg_bf, wu_bf, wd_bf, g3):
    T = h2.shape[0]
    tm = min(512, T)
    row = pl.BlockSpec((tm, D_MODEL), lambda i: (i, 0))
    vec = pl.BlockSpec((1, D_MODEL), lambda i: (0, 0))
    return pl.pallas_call(
        _ffn_kernel,
        grid=(T // tm,),
        in_specs=[row, vec,
                  pl.BlockSpec((D_MODEL, D_FF), lambda i: (0, 0)),
                  pl.BlockSpec((D_MODEL, D_FF), lambda i: (0, 0)),
                  pl.BlockSpec((D_FF, D_MODEL), lambda i: (0, 0)),
                  vec],
        out_specs=row,
        out_shape=jax.ShapeDtypeStruct((T, D_MODEL), F32),
        compiler_params=_cparams(("parallel",), 48),
        name="ffn",
    )(h2, g2, wg_bf, wu_bf, wd_bf, g3)


def _rope_tables(L):
    rows = L // GRID_W
    r = jnp.repeat(jnp.arange(rows, dtype=F32), GRID_W)
    c = jnp.tile(jnp.arange(GRID_W, dtype=F32), rows)
    inv = ROPE_BASE ** (-jnp.arange(ROPE_FREQS, dtype=F32) / ROPE_FREQS)
    ar = r[:, None] * inv
    ac = c[:, None] * inv
    cos64 = jnp.concatenate([jnp.cos(ar), jnp.cos(ar), jnp.cos(ac), jnp.cos(ac)], axis=1)
    sin64 = jnp.concatenate([-jnp.sin(ar), jnp.sin(ar), -jnp.sin(ac), jnp.sin(ac)], axis=1)
    return jnp.tile(cos64, (1, 2)), jnp.tile(sin64, (1, 2))


def _filter_features(L):
    t = np.linspace(0.0, 1.0, L)[:, None]
    f = np.linspace(1e-4, HY_BANDS - 1, HY_BANDS)
    ang = (2.0 * math.pi / L) * np.arange(L)[:, None] * f[None, :]
    z = np.concatenate([t, np.cos(ang), -np.sin(ang)], axis=-1)
    z = np.pad(z, ((0, 0), (0, HY_FILTER_HIDDEN - HY_EMB)))
    idx = np.concatenate([np.arange(L), [0], np.arange(L - 1, 0, -1)])
    return jnp.asarray(z[idx].astype(np.float32))


def _head_mean_matrix():
    i = np.arange(LANES)
    j = ((i[:, None] // HEAD_DIM) == (i[None, :] // HEAD_DIM)).astype(np.float32) / HEAD_DIM
    return jnp.asarray(j.astype(BF16))


def _halve_gate_columns(w_in_bf):
    scale = jnp.where(jnp.arange(IN_COLS) >= GATE_OFF, 0.5, 1.0).astype(BF16)
    return w_in_bf * scale


def _trunk(x, wts, fft_dims=None):
    B, L, _ = x.shape
    T = B * L
    n1, n2 = fft_dims or _fft_dims(L)
    cst = _fft_consts(n1, n2, n1 // 2)
    cos_t, sin_t = _rope_tables(L)
    z2 = _filter_features(L)
    deltas = jnp.abs(jnp.linspace(math.log(HY_TARGET) / HY_SLOW_DECAY,
                                  math.log(HY_TARGET) / HY_FAST_DECAY, HY_WIDTH, dtype=F32))[None, :]
    j_bf = _head_mean_matrix()
    depth = wts["w_in"].shape[0]
    x2 = x.reshape(T, D_MODEL)
    for l in range(depth):
        ng = wts["norm_gains"][l]
        w1p = jnp.pad(wts["hy_w1"][l], ((0, HY_FILTER_HIDDEN - HY_EMB), (0, 0)))
        k2raw, sumsq = _hy_filter_raw(z2, w1p, wts["hy_b1"][l][None, :], wts["hy_w2"][l],
                                      wts["hy_b2"][l][None, :], wts["hy_w3"][l],
                                      wts["hy_freq"][l], deltas, L)
        kf_re, kf_im = _filter_fft(k2raw, sumsq, cst, n1, n2)

        qg = jnp.tile(wts["qk_norm"][l, 0], 2)[None, :]
        kg = jnp.tile(wts["qk_norm"][l, 1], 2)[None, :]
        qt, k, vt, p_ret, p_gate, z, x0u, out_d = _in_proj(
            x2, ng[0][None, :], wts["w_in_bf"][l], wts["hy_conv_w"][l],
            wts["hy_conv_b"][l][None, :], wts["sc_conv_w"][l], cos_t, sin_t, qg, kg, j_bf, L)
        out_a = _flash(qt, k, vt)

        out_b = _hy_conv(z.reshape(B, L, HY_WIDTH), x0u.reshape(B, L, HY_WIDTH),
                         wts["hy_bias"][l][None, :], kf_re, kf_im, cst, n1, n2)

        tables = _ret_tables(wts["ret_decay_exp"][l])
        out_c = _retention(p_ret.reshape(B, L, 4 * RET_W), cos_t, sin_t, tables, j_bf)

        h2 = _merge(out_a.reshape(T, BRANCH_W), out_b.reshape(T, BRANCH_W),
                    out_c.reshape(T, BRANCH_W), out_d,
                    p_gate, wts["w_branch_bf"][l], wts["w_out_bf"][l], x2, ng[1][None, :])
        x2 = _ffn(h2, ng[2][None, :], wts["w_gate_bf"][l], wts["w_up_bf"][l],
                  wts["w_ffn_out_bf"][l], ng[3][None, :])
    return x2.reshape(B, L, D_MODEL)


def kernel(x_prompt, x_sample, norm_gains, w_in, qk_norm, hy_conv_w, hy_conv_b, hy_w1, hy_b1, hy_w2,
           hy_b2, hy_w3, hy_freq, hy_bias, ret_decay_exp, sc_conv_w, w_branch, w_out, w_ffn_in,
           w_ffn_out):
    wts = dict(norm_gains=norm_gains, w_in=w_in, qk_norm=qk_norm, hy_conv_w=hy_conv_w,
               hy_conv_b=hy_conv_b, hy_w1=hy_w1, hy_b1=hy_b1, hy_w2=hy_w2, hy_b2=hy_b2,
               hy_w3=hy_w3, hy_freq=hy_freq, hy_bias=hy_bias, ret_decay_exp=ret_decay_exp,
               sc_conv_w=sc_conv_w,
               w_in_bf=_halve_gate_columns(w_in.astype(BF16)), w_branch_bf=w_branch.astype(BF16),
               w_out_bf=w_out.astype(BF16), w_gate_bf=w_ffn_in[..., :D_FF].astype(BF16),
               w_up_bf=w_ffn_in[..., D_FF:].astype(BF16), w_ffn_out_bf=w_ffn_out.astype(BF16))
    return _trunk(x_prompt, wts), _trunk(x_sample, wts)
```

```python
import functools
import math

import numpy as np
import jax
import jax.numpy as jnp
from jax import lax
from jax.experimental import pallas as pl
from jax.experimental.pallas import tpu as pltpu

F32 = jnp.float32
BF16 = jnp.bfloat16

D_MODEL = 1024
GRID_W = 64
N_BRANCH = 4
BRANCH_W = 256
HEAD_DIM = 64
ATT_Q_HEADS = 4
ATT_KV_HEADS = 2
ROPE_BASE = 10000.0
ROPE_FREQS = HEAD_DIM // 4
HY_WIDTH = BRANCH_W
HY_EMB = 33
HY_BANDS = (HY_EMB - 1) // 2
HY_FILTER_HIDDEN = 64
HY_FAST_DECAY = 0.3
HY_SLOW_DECAY = 1.5
HY_TARGET = 1e-2
RET_HEADS = 4
RET_W = RET_HEADS * HEAD_DIM
RET_CHUNK = 128
SC_WIDTH = BRANCH_W
D_FF = 2816
NORM_EPS = 1e-6

ATT_Q_W = ATT_Q_HEADS * HEAD_DIM
ATT_KV_W = ATT_KV_HEADS * HEAD_DIM
A_K_OFF = ATT_Q_W
A_V_OFF = A_K_OFF + ATT_KV_W
HY_OFF = A_V_OFF + ATT_KV_W
RET_OFF = HY_OFF + 3 * HY_WIDTH
SC_OFF = RET_OFF + 4 * RET_W
GATE_OFF = SC_OFF + 3 * SC_WIDTH
IN_COLS = GATE_OFF + N_BRANCH * D_MODEL

LANES = 128
SUBLANES = 8
MXU_W = 256
HALO = 2 * SUBLANES
ATT_KB = 512
ATT_TQ = 256
VT_ROWS = HEAD_DIM + HALO
Q_SCALE = HEAD_DIM ** -0.5 * math.log2(math.e)
FFT_PASSES_FILTER = 3
FFT_PASSES_DATA = 1
FFT_CH = 8
FFT_SETS = 1
MIB = 1 << 20


def _cparams(sem, vmem_mib):
    return pltpu.CompilerParams(dimension_semantics=sem, vmem_limit_bytes=vmem_mib * MIB)


def _sigmoid(x):
    return 0.5 * jnp.tanh(0.5 * x) + 0.5


def _split(x):
    hi = x.astype(BF16)
    lo = (x - hi.astype(F32)).astype(BF16)
    return hi, lo


def _np_split(a64):
    a32 = np.asarray(a64, np.float32)
    hi = a32.astype(BF16)
    lo = (a32 - hi.astype(np.float32)).astype(BF16)
    return jnp.asarray(hi), jnp.asarray(lo)


def _mm_const(a_hi, a_lo, x, passes):
    if passes == 1:
        return jnp.dot(a_hi, x.astype(BF16), preferred_element_type=F32)
    xh, xl = _split(x)
    out = jnp.dot(a_hi, xh, preferred_element_type=F32)
    if passes >= 3:
        out = out + jnp.dot(a_lo, xh, preferred_element_type=F32)
        out = out + jnp.dot(a_hi, xl, preferred_element_type=F32)
    return out


def _mm3(a, b):
    ah, al = _split(a)
    bh, bl = _split(b)
    out = jnp.dot(ah, bh, preferred_element_type=F32)
    out = out + jnp.dot(al, bh, preferred_element_type=F32)
    return out + jnp.dot(ah, bl, preferred_element_type=F32)


def _head_mean_sq(x, j_bf):
    hi, lo = _split(x * x)
    return (jnp.dot(hi, j_bf, preferred_element_type=F32)
            + jnp.dot(lo, j_bf, preferred_element_type=F32))


def _rope(x, c, s):
    lane = lax.broadcasted_iota(jnp.int32, (1, LANES), 1)
    is_b = (lane & 16) != 0
    partner = jnp.where(is_b, pltpu.roll(x, 16, 1), pltpu.roll(x, LANES - 16, 1))
    return x * c + partner * s


def _col_chunks(n, parts):
    tiles = n // MXU_W
    bounds = [MXU_W * (tiles * i // parts) for i in range(parts + 1)]
    return list(zip(bounds[:-1], bounds[1:]))


def _conv3_rows(main, before, after, w):
    tb = main.shape[0]
    w0, w1, w2 = w[0:1, :], w[1:2, :], w[2:3, :]
    body = pltpu.roll(main, 1, 0) * w0 + main * w1 + pltpu.roll(main, tb - 1, 0) * w2
    row_first = before * w0 + main[0:1, :] * w1 + main[1:2, :] * w2
    row_last = main[tb - 2:tb - 1, :] * w0 + main[tb - 1:tb, :] * w1 + after * w2
    row = lax.broadcasted_iota(jnp.int32, (SUBLANES, 1), 0)
    head = jnp.where(row == 0, row_first, body[:SUBLANES])
    tail = jnp.where(row == SUBLANES - 1, row_last, body[tb - SUBLANES:])
    return jnp.concatenate([head, body[SUBLANES:tb - SUBLANES], tail], axis=0)


def _in_proj_kernel(x_ref, xp_ref, xn_ref, g_ref, w_ref, hcw_ref, hcb_ref, scw_ref,
                    c_ref, s_ref, qg_ref, kg_ref, j_ref,
                    qt_ref, k_ref, vt_ref, pr_ref, pg_ref, z_ref, x0_ref, od_ref, *,
                    tiles_per_seq):
    i = pl.program_id(0)
    first = (i % tiles_per_seq) == 0
    last = (i % tiles_per_seq) == tiles_per_seq - 1
    g = g_ref[...]

    def normed(x):
        ms = jnp.mean(x * x, axis=-1, keepdims=True)
        return (x * lax.rsqrt(ms + NORM_EPS) * g).astype(BF16)

    dot = lambda a, lo, hi: jnp.dot(a, w_ref[:, lo:hi], preferred_element_type=F32)
    xn = normed(x_ref[...])
    halo = normed(jnp.concatenate([xp_ref[...], xn_ref[...]], axis=0))
    edge = lambda h: (jnp.where(first, 0.0, h[SUBLANES - 1:SUBLANES, :]),
                      jnp.where(last, 0.0, h[SUBLANES:SUBLANES + 1, :]))

    head = dot(xn, 0, RET_OFF)
    hy_halo = dot(halo, HY_OFF, RET_OFF)
    sc = dot(xn, SC_OFF, GATE_OFF)
    sc_halo = dot(halo, SC_OFF + SC_WIDTH, GATE_OFF)
    pr_ref[...] = dot(xn, RET_OFF, SC_OFF).astype(pr_ref.dtype)
    _attn_prep_tile(head[:, :HY_OFF], c_ref, s_ref, qg_ref, kg_ref, j_ref, qt_ref, k_ref, vt_ref)

    u = _conv3_rows(head[:, HY_OFF:], *edge(hy_halo), hcw_ref[...]) + hcb_ref[...]
    x0_ref[...] = u[:, :HY_WIDTH]
    z_ref[...] = u[:, 2 * HY_WIDTH:] * u[:, HY_WIDTH:2 * HY_WIDTH]
    m = sc[:, SC_WIDTH:2 * SC_WIDTH] * sc[:, 2 * SC_WIDTH:]
    mh = sc_halo[:, :SC_WIDTH] * sc_halo[:, SC_WIDTH:]
    od_ref[...] = (sc[:, :SC_WIDTH] * _conv3_rows(m, *edge(mh), scw_ref[...])).astype(od_ref.dtype)

    for lo, hi in _col_chunks(N_BRANCH * D_MODEL, 2):
        g_half = dot(xn, GATE_OFF + lo, GATE_OFF + hi).astype(pg_ref.dtype)
        pg_ref[:, lo:hi] = 0.5 * jnp.tanh(g_half) + 0.5


def _in_proj(x2, g, w_bf, hcw, hcb, scw, cos_t, sin_t, qg, kg, j_bf, L):
    T = x2.shape[0]
    B = T // L
    tm = min(ATT_KB, L)
    tq = min(ATT_TQ, L)
    tps = L // tm
    r8 = tm // SUBLANES
    row = lambda w: pl.BlockSpec((tm, w), lambda i: (i, 0))
    full = lambda a: pl.BlockSpec(a.shape, lambda i: (0,) * a.ndim)
    tab = pl.BlockSpec((tm, LANES), lambda i: (i % tps, 0))
    sds = lambda w, dt: jax.ShapeDtypeStruct((T, w), dt)
    return pl.pallas_call(
        functools.partial(_in_proj_kernel, tiles_per_seq=tps),
        grid=(T // tm,),
        in_specs=[row(D_MODEL),
                  pl.BlockSpec((SUBLANES, D_MODEL), lambda i: (jnp.maximum(i * r8 - 1, 0), 0)),
                  pl.BlockSpec((SUBLANES, D_MODEL),
                               lambda i: (jnp.minimum((i + 1) * r8, T // SUBLANES - 1), 0)),
                  full(g), full(w_bf), full(hcw), full(hcb), full(scw),
                  tab, tab, full(qg), full(kg), full(j_bf)],
        out_specs=[pl.BlockSpec((None, ATT_Q_HEADS, tm // tq, LANES, tq),
                                lambda i: (i // tps, 0, i % tps, 0, 0)),
                   pl.BlockSpec((None, tm, LANES), lambda i: (i // tps, i % tps, 0)),
                   pl.BlockSpec((None, tm // ATT_KB, ATT_KV_HEADS * VT_ROWS, ATT_KB),
                                lambda i: (i // tps, i % tps, 0, 0)),
                   row(RET_W * 4), row(N_BRANCH * D_MODEL),
                   row(HY_WIDTH), row(HY_WIDTH), row(SC_WIDTH)],
        out_shape=[jax.ShapeDtypeStruct((B, ATT_Q_HEADS, L // tq, LANES, tq), BF16),
                   jax.ShapeDtypeStruct((B, L, LANES), BF16),
                   jax.ShapeDtypeStruct((B, L // ATT_KB, ATT_KV_HEADS * VT_ROWS, ATT_KB), BF16),
                   sds(RET_W * 4, BF16), sds(N_BRANCH * D_MODEL, BF16),
                   sds(HY_WIDTH, F32), sds(HY_WIDTH, F32), sds(SC_WIDTH, BF16)],
        compiler_params=_cparams(("parallel",), 48),
        name="in_proj",
    )(x2, x2, x2, g, w_bf, hcw, hcb, scw, cos_t, sin_t, qg, kg, j_bf)


def _attn_prep_tile(qkv, c_ref, s_ref, qg_ref, kg_ref, j_ref, qt_ref, k_ref, vt_ref):
    c = c_ref[...]
    s = s_ref[...]
    j_bf = j_ref[...]
    tm = qkv.shape[0]
    tq = qt_ref.shape[-1]
    zeros = jnp.zeros((HEAD_DIM, tm), BF16)
    for blk in range(2):
        q = qkv[:, LANES * blk:LANES * (blk + 1)]
        qn = q * lax.rsqrt(_head_mean_sq(q, j_bf) + NORM_EPS) * qg_ref[...]
        qt = (_rope(qn, c, s) * Q_SCALE).astype(BF16).T
        lo, hi = qt[:HEAD_DIM], qt[HEAD_DIM:]
        if blk == 0:
            heads = (jnp.concatenate([lo, zeros], axis=0), jnp.concatenate([hi, zeros], axis=0))
        else:
            heads = (jnp.concatenate([zeros, lo], axis=0), jnp.concatenate([zeros, hi], axis=0))
        for hp, qh in enumerate(heads):
            for t in range(tm // tq):
                qt_ref[2 * blk + hp, t] = qh[:, t * tq:(t + 1) * tq]
    k = qkv[:, A_K_OFF:A_K_OFF + LANES]
    kn = k * lax.rsqrt(_head_mean_sq(k, j_bf) + NORM_EPS) * kg_ref[...]
    k_ref[...] = _rope(kn, c, s).astype(BF16)
    v = qkv[:, A_V_OFF:A_V_OFF + LANES].astype(BF16)
    ones = jnp.ones((VT_ROWS - HEAD_DIM, ATT_KB), BF16)
    for t in range(tm // ATT_KB):
        vt = v[t * ATT_KB:(t + 1) * ATT_KB, :].T
        vt_ref[t] = jnp.concatenate([vt[:HEAD_DIM], ones, vt[HEAD_DIM:], ones], axis=0)


def _flash_kernel(qt_ref, k_ref, vt_ref, o_ref, sa_ref, ma_ref, sb_ref, mb_ref, *,
                  ngrp, nsub, ntile):
    tq = ATT_TQ
    group = ATT_Q_HEADS // ATT_KV_HEADS

    def produce(slot, qi, g, t, h):
        s_ref, mx_ref = slot
        st = pl.multiple_of((g * nsub + t) * ATT_KB, ATT_KB)
        s = jnp.dot(k_ref[pl.ds(st, ATT_KB), :], qt_ref[h, qi],
                    preferred_element_type=F32)
        s_ref[t * ATT_Q_HEADS + h] = s
        mx_ref[t * ATT_Q_HEADS + h] = jnp.max(s, axis=0, keepdims=True)

    def consume(slot, g, t, h, state):
        s_ref, mx_ref = slot
        m, acc = state
        m_new = jnp.maximum(m, mx_ref[t * ATT_Q_HEADS + h])
        alpha = jnp.exp2(m - m_new)
        p = jnp.exp2((s_ref[t * ATT_Q_HEADS + h] - m_new).astype(BF16))
        j = h // group
        vt = vt_ref[g * nsub + t, VT_ROWS * j:VT_ROWS * (j + 1), :]
        return m_new, alpha * acc + jnp.dot(vt, p, preferred_element_type=F32)

    def step(slot_in, g_in, slot_out, qi_out, g_out, carry):
        carry = list(carry)
        for t in range(nsub):
            for h in range(ATT_Q_HEADS):
                if slot_out is not None:
                    produce(slot_out, qi_out, g_out, t, h)
                if slot_in is not None:
                    carry[h] = consume(slot_in, g_in, t, h, carry[h])
        return tuple(carry)

    def fresh():
        return tuple((jnp.full((1, tq), -jnp.inf, F32), jnp.zeros((VT_ROWS, tq), F32))
                     for _ in range(ATT_Q_HEADS))

    def finalize(qi, carry):
        outs = [acc[:HEAD_DIM] / acc[HEAD_DIM:HEAD_DIM + 1] for _, acc in carry]
        rows = pl.ds(pl.multiple_of(qi * tq, tq), tq)
        o_ref[rows, :] = jnp.concatenate(outs, axis=0).T.astype(o_ref.dtype)

    slot_a = (sa_ref, ma_ref)
    slot_b = (sb_ref, mb_ref)

    if ngrp == 1:
        def tile(qi, _):
            step(None, None, slot_a, qi, 0, ())
            finalize(qi, step(slot_a, 0, None, None, None, fresh()))
            return 0
    else:
        step(None, None, slot_a, 0, 0, ())

        def tile(qi, _):
            def body(i, carry):
                g = 2 * i
                carry = step(slot_a, g, slot_b, qi, g + 1, carry)
                return step(slot_b, g + 1, slot_a, qi, g + 2, carry)

            carry = lax.fori_loop(0, ngrp // 2 - 1, body, fresh())
            carry = step(slot_a, ngrp - 2, slot_b, qi, ngrp - 1, carry)
            carry = step(slot_b, ngrp - 1, slot_a, jnp.minimum(qi + 1, ntile - 1), 0, carry)
            finalize(qi, carry)
            return 0

    lax.fori_loop(0, ntile, tile, 0)


def _flash(qt, k, vt):
    B, _, nqt, _, tq = qt.shape
    L = nqt * tq
    ntile = math.gcd(nqt, 4)
    nkb = L // ATT_KB
    nsub = 2 if nkb % 4 == 0 and nkb >= 8 else 1
    ngrp = L // (ATT_KB * nsub)
    assert ngrp == 1 or ngrp % 2 == 0
    s_scratch = pltpu.VMEM((nsub * ATT_Q_HEADS, ATT_KB, tq), F32)
    m_scratch = pltpu.VMEM((nsub * ATT_Q_HEADS, 1, tq), F32)
    return pl.pallas_call(
        functools.partial(_flash_kernel, ngrp=ngrp, nsub=nsub, ntile=ntile),
        grid=(B, nqt // ntile),
        in_specs=[pl.BlockSpec((None, ATT_Q_HEADS, ntile, LANES, tq),
                               lambda b, i: (b, 0, i, 0, 0)),
                  pl.BlockSpec((None, L, LANES), lambda b, i: (b, 0, 0)),
                  pl.BlockSpec((None, L // ATT_KB, ATT_KV_HEADS * VT_ROWS, ATT_KB),
                               lambda b, i: (b, 0, 0, 0))],
        out_specs=pl.BlockSpec((None, ntile * tq, ATT_Q_W), lambda b, i: (b, i, 0)),
        out_shape=jax.ShapeDtypeStruct((B, L, ATT_Q_W), BF16),
        scratch_shapes=[s_scratch, m_scratch, s_scratch, m_scratch],
        compiler_params=_cparams(("parallel", "arbitrary"), 48),
        name="flash_attn",
    )(qt, k, vt)


def _hy_filter_kernel(z_ref, w1_ref, b1_ref, w2_ref, b2_ref, w3_ref, fr_ref, dl_ref,
                      k_ref, ss_ref, *, L):
    step = pl.program_id(0)
    tr = z_ref.shape[0]
    z = z_ref[...]
    h = jnp.sin(fr_ref[0:1, :] * (_mm3(z, w1_ref[...]) + b1_ref[...]))
    h = jnp.sin(fr_ref[1:2, :] * (_mm3(h, w2_ref[...]) + b2_ref[...]))
    h3 = _mm3(h, w3_ref[...])
    row = step * tr + lax.broadcasted_iota(jnp.int32, (tr, 1), 0)
    val = jnp.where(row < L, h3[:, :HY_WIDTH], h3[:, HY_WIDTH:])
    val = val * jnp.exp(-z[:, 0:1] * dl_ref[...])
    val = jnp.where(row == L, 0.0, val)
    k_ref[...] = val

    @pl.when(step == 0)
    def _():
        ss_ref[...] = jnp.zeros_like(ss_ref)

    ss_ref[...] += jnp.sum(val * val, axis=0, keepdims=True)


def _hy_filter_raw(z2, w1p, b1, w2, b2, w3, freq, deltas, L):
    n = 2 * L
    tr = min(512, n)
    H = HY_FILTER_HIDDEN
    full = lambda shape: pl.BlockSpec(shape, lambda i: (0, 0))
    return pl.pallas_call(
        functools.partial(_hy_filter_kernel, L=L),
        grid=(n // tr,),
        in_specs=[pl.BlockSpec((tr, H), lambda i: (i, 0)),
                  full((H, H)), full((1, H)), full((H, H)), full((1, H)),
                  full((H, 2 * HY_WIDTH)), full((2, H)), full((1, HY_WIDTH))],
        out_specs=[pl.BlockSpec((tr, HY_WIDTH), lambda i: (i, 0)), full((1, HY_WIDTH))],
        out_shape=[jax.ShapeDtypeStruct((n, HY_WIDTH), F32),
                   jax.ShapeDtypeStruct((1, HY_WIDTH), F32)],
        compiler_params=_cparams(("arbitrary",), 32),
        name="hyena_filter",
    )(z2, w1p, b1, w2, b2, w3, freq, deltas)


def _fft_dims(L):
    n = 2 * L
    n2 = 128 if n >= 32 * 128 else 64
    return n // n2, n2


def _fft_consts(n1, n2, r):
    n = n1 * n2
    a1 = 2.0 * np.pi * np.outer(np.arange(n1), np.arange(n1)) / n1
    c1, s1 = np.cos(a1), np.sin(a1)
    a2 = 2.0 * np.pi * np.outer(np.arange(n2), np.arange(n2)) / n2
    c2, s2 = np.cos(a2), np.sin(a2)
    f1 = np.block([[c1[:, :r], s1[:, :r]], [-s1[:, :r], c1[:, :r]]])
    f1_full = np.concatenate([c1, -s1], axis=0)
    m2 = np.block([[c2, s2], [-s2, c2]])
    m2i = np.block([[c2, -s2], [s2, c2]])
    fi = np.block([[c1[:r, :], -s1[:r, :]], [s1[:r, :], c1[:r, :]]])
    aw = 2.0 * np.pi * np.arange(n1) / n
    off = np.repeat(np.arange(FFT_CH), LANES)[None, :] * aw[:, None]
    step = np.broadcast_to((FFT_CH * aw)[:, None], (n1, LANES))
    f32 = lambda a: jnp.asarray(np.asarray(a, np.float32))
    return dict(f1=_np_split(f1), f1_full=_np_split(f1_full), m2=_np_split(m2),
                m2i=_np_split(m2i), fi=_np_split(fi),
                tw=(f32(np.cos(off)), f32(-np.sin(off)), f32(np.cos(step)), f32(-np.sin(step))))


def _cmul(a_re, a_im, b_re, b_im):
    return a_re * b_re - a_im * b_im, a_re * b_im + a_im * b_re


def _lane_tile(x, reps):
    return jnp.concatenate([x] * reps, axis=1)


def _lane_part(x, j):
    return x[:, LANES * j:LANES * (j + 1)]


def _chunk_twiddle(t_re, t_im, tw_refs):
    d_re, d_im, s_re, s_im = (ref[...] for ref in tw_refs)
    cur = _cmul(_lane_tile(t_re, FFT_CH), _lane_tile(t_im, FFT_CH), d_re, d_im)
    return cur, _cmul(t_re, t_im, s_re, s_im)


def _twiddle_init(n1):
    return jnp.ones((n1, LANES), F32), jnp.zeros((n1, LANES), F32)


def _fft_stage1(load_rows, a_ref, f_hi, f_lo, tw_refs, n1, n2, passes):
    def body(ci, carry):
        i0s, tws = [], []
        for u in range(FFT_SETS):
            tw, carry = _chunk_twiddle(*carry, tw_refs)
            tws.append(tw)
            i0s.append((ci * FFT_SETS + u) * FFT_CH)
        prods = [_mm_const(f_hi, f_lo, jnp.concatenate(
            [load_rows(i0 + j) for j in range(FFT_CH)], axis=1), passes)
            for i0 in i0s]
        for i0, (t_re, t_im), a in zip(i0s, tws, prods):
            o_re, o_im = _cmul(a[:n1], a[n1:], t_re, t_im)
            for j in range(FFT_CH):
                base = pl.multiple_of((i0 + j) * 2 * n1, 2 * n1)
                a_ref[pl.ds(base, n1), :] = _lane_part(o_re, j)
                a_ref[pl.ds(base + n1, n1), :] = _lane_part(o_im, j)
        return carry

    lax.fori_loop(0, n2 // (FFT_CH * FFT_SETS), body, _twiddle_init(n1))


def _load_spectrum_rows(a_ref, k1, n1, n2):
    ld = lambda off: a_ref[pl.ds(off, n2, stride=2 * n1), :]
    return jnp.concatenate(
        [jnp.concatenate([ld(k1 + j) for j in range(FFT_CH)], axis=1),
         jnp.concatenate([ld(n1 + k1 + j) for j in range(FFT_CH)], axis=1)], axis=0)


def _filter_fft_kernel(k_ref, ss_ref, f_hi_ref, f_lo_ref, m_hi_ref, m_lo_ref,
                       dre_ref, dim_ref, sre_ref, sim_ref, kr_ref, ki_ref, a_ref, *, n1, n2):
    c = pl.program_id(1)

    @pl.when(c == 0)
    def _():
        _fft_stage1(lambda i: k_ref[pl.ds(i, n1, stride=n2), :], a_ref,
                    f_hi_ref[...], f_lo_ref[...], (dre_ref, dim_ref, sre_ref, sim_ref),
                    n1, n2, FFT_PASSES_FILTER)

    scale = _lane_tile(lax.rsqrt(ss_ref[...] + NORM_EPS) * (1.0 / (n1 * n2)), FFT_CH)
    xin = _load_spectrum_rows(a_ref, c * FFT_CH, n1, n2)
    x = _mm_const(m_hi_ref[...], m_lo_ref[...], xin, FFT_PASSES_FILTER)
    kr_ref[...] = x[:n2] * scale
    ki_ref[...] = x[n2:] * scale


def _filter_fft(k2raw, sumsq, cst, n1, n2):
    n = n1 * n2
    nw = HY_WIDTH // LANES
    consts = [*cst["f1_full"], *cst["m2"], *cst["tw"]]
    full = lambda a: pl.BlockSpec(a.shape, lambda w, c: (0,) * a.ndim)
    out_spec = pl.BlockSpec((None, None, n2, FFT_CH * LANES), lambda w, c: (w, c, 0, 0))
    out_sh = jax.ShapeDtypeStruct((nw, n1 // FFT_CH, n2, FFT_CH * LANES), F32)
    return pl.pallas_call(
        functools.partial(_filter_fft_kernel, n1=n1, n2=n2),
        grid=(nw, n1 // FFT_CH),
        in_specs=[pl.BlockSpec((n, LANES), lambda w, c: (0, w)),
                  pl.BlockSpec((1, LANES), lambda w, c: (0, w))] + [full(a) for a in consts],
        out_specs=[out_spec, out_spec],
        out_shape=[out_sh, out_sh],
        scratch_shapes=[pltpu.VMEM((2 * n, LANES), F32)],
        compiler_params=_cparams(("parallel", "arbitrary"), 48),
        name="hyena_filter_fft",
    )(k2raw, sumsq, *consts)


def _hy_conv_kernel(z_ref, x0_ref, hb_ref, kr_ref, ki_ref, f_hi_ref, f_lo_ref,
                    m_hi_ref, m_lo_ref, mi_hi_ref, mi_lo_ref, fi_hi_ref, fi_lo_ref,
                    dre_ref, dim_ref, sre_ref, sim_ref, o_ref, a_ref, *, n1, n2):
    c = pl.program_id(2)
    r = n1 // 2
    tw_refs = (dre_ref, dim_ref, sre_ref, sim_ref)

    def load_pair(i):
        seq = pl.ds(i, r, stride=n2)
        return jnp.concatenate([z_ref[0, seq, :], z_ref[1, seq, :]], axis=0)

    @pl.when(c == 0)
    def _():
        _fft_stage1(load_pair, a_ref, f_hi_ref[...], f_lo_ref[...], tw_refs, n1, n2,
                    FFT_PASSES_DATA)

    k1s = [(c * FFT_SETS + u) * FFT_CH for u in range(FFT_SETS)]
    xs = [_mm_const(m_hi_ref[...], m_lo_ref[...], _load_spectrum_rows(a_ref, k1, n1, n2),
                    FFT_PASSES_DATA) for k1 in k1s]
    ys = [jnp.concatenate(_cmul(x[:n2], x[n2:], kr_ref[u], ki_ref[u]), axis=0)
          for u, x in enumerate(xs)]
    bqs = [_mm_const(mi_hi_ref[...], mi_lo_ref[...], y, FFT_PASSES_DATA) for y in ys]
    for k1, bq in zip(k1s, bqs):
        for j in range(FFT_CH):
            a_ref[pl.ds(k1 + j, n2, stride=2 * n1), :] = _lane_part(bq[:n2], j)
            a_ref[pl.ds(n1 + k1 + j, n2, stride=2 * n1), :] = _lane_part(bq[n2:], j)

    @pl.when(c == pl.num_programs(2) - 1)
    def _():
        hb = hb_ref[...]

        def body(ci, carry):
            i0s, bts = [], []
            for u in range(FFT_SETS):
                (t_re, t_im), carry = _chunk_twiddle(*carry, tw_refs)
                i0 = (ci * FFT_SETS + u) * FFT_CH
                rows = [pl.multiple_of((i0 + j) * 2 * n1, 2 * n1) for j in range(FFT_CH)]
                br = jnp.concatenate([a_ref[pl.ds(b0, n1), :] for b0 in rows], axis=1)
                bi = jnp.concatenate([a_ref[pl.ds(b0 + n1, n1), :] for b0 in rows], axis=1)
                bts.append(jnp.concatenate([br * t_re + bi * t_im, bi * t_re - br * t_im],
                                           axis=0))
                i0s.append(i0)
            ys = [_mm_const(fi_hi_ref[...], fi_lo_ref[...], bt, FFT_PASSES_DATA) for bt in bts]
            for i0, y in zip(i0s, ys):
                for j in range(FFT_CH):
                    seq = pl.ds(i0 + j, r, stride=n2)
                    for row in range(2):
                        conv = _lane_part(y[row * r:(row + 1) * r], j)
                        o_ref[row, seq, :] = (x0_ref[row, seq, :]
                                              * (conv + z_ref[row, seq, :] * hb)
                                              ).astype(o_ref.dtype)
            return carry

        lax.fori_loop(0, n2 // (FFT_CH * FFT_SETS), body, _twiddle_init(n1))


def _hy_conv(z, x0u, hbias, kf_re, kf_im, cst, n1, n2):
    B, L, _ = z.shape
    assert B % 2 == 0
    nw = HY_WIDTH // LANES
    seq = pl.BlockSpec((2, L, LANES), lambda b, w, c: (b, 0, w), pipeline_mode=pl.Buffered(1))
    seq_z = pl.BlockSpec((2, L, LANES), lambda b, w, c: (b, 0, w))
    kf_spec = pl.BlockSpec((None, FFT_SETS, n2, FFT_CH * LANES), lambda b, w, c: (w, c, 0, 0))
    full = lambda a: pl.BlockSpec(a.shape, lambda b, w, c: (0,) * a.ndim)
    consts = [*cst["f1"], *cst["m2"], *cst["m2i"], *cst["fi"], *cst["tw"]]
    return pl.pallas_call(
        functools.partial(_hy_conv_kernel, n1=n1, n2=n2),
        grid=(B // 2, nw, n1 // (FFT_CH * FFT_SETS)),
        in_specs=[seq_z, seq, pl.BlockSpec((1, LANES), lambda b, w, c: (0, w)), kf_spec, kf_spec]
                 + [full(a) for a in consts],
        out_specs=seq,
        out_shape=jax.ShapeDtypeStruct((B, L, HY_WIDTH), F32),
        scratch_shapes=[pltpu.VMEM((2 * n1 * n2, LANES), F32)],
        compiler_params=_cparams(("parallel", "parallel", "arbitrary"), 58),
        name="hyena_conv",
    )(z, x0u, hbias, kf_re, kf_im, *consts)


T_D0, T_D1, T_WQF, T_WQB, T_WKF, T_WKB, T_GCF, T_GCB, T_BD = range(9)


def _ret_tables_kernel(rl_ref, rh_ref, t_ref):
    C = RET_CHUNK
    i = lax.broadcasted_iota(jnp.int32, (C, LANES), 0).astype(F32)
    jn = lax.broadcasted_iota(jnp.int32, (C, LANES), 1).astype(F32)
    log_g = lambda e: jnp.log1p(-jnp.exp2(-e))
    diff = i - jn
    for hp in range(2):
        lgf = log_g(rh_ref[hp, 0:1, :])
        lgb = log_g(rh_ref[hp, 1:2, :])
        fwd = jnp.exp(jnp.where(diff >= 0, diff, 0.0) * lgf)
        bwd = jnp.exp(jnp.where(diff < 0, -diff, 0.0) * lgb)
        t_ref[T_D0 + hp] = jnp.where(diff >= 0, fwd, bwd)
    lf = log_g(rl_ref[0:1, :])
    lb = log_g(rl_ref[1:2, :])
    t_ref[T_WQF] = jnp.exp((i + 1.0) * lf)
    t_ref[T_WQB] = jnp.exp((C - i) * lb)
    t_ref[T_WKF] = jnp.exp((C - 1.0 - i) * lf)
    t_ref[T_WKB] = jnp.exp(i * lb)
    bd = jnp.where((i < HEAD_DIM) == (jn < HEAD_DIM), 1.0, 0.0)
    t_ref[T_GCF] = jnp.exp(C * lf) * bd
    t_ref[T_GCB] = jnp.exp(C * lb) * bd
    t_ref[T_BD] = bd


def _ret_tables(rde):
    per_lane = jnp.repeat(rde.reshape(2, 2, 2), HEAD_DIM, axis=2)
    per_lane = per_lane.transpose(1, 0, 2)
    per_head = jnp.broadcast_to(rde.reshape(2, 2, 2, 1), (2, 2, 2, LANES))
    per_head = per_head.transpose(1, 2, 0, 3)
    return pl.pallas_call(
        _ret_tables_kernel,
        grid=(2,),
        in_specs=[pl.BlockSpec((None, 2, LANES), lambda j: (j, 0, 0)),
                  pl.BlockSpec((None, 2, 2, LANES), lambda j: (j, 0, 0, 0))],
        out_specs=pl.BlockSpec((None, 9, RET_CHUNK, LANES), lambda j: (j, 0, 0, 0)),
        out_shape=jax.ShapeDtypeStruct((2, 9, RET_CHUNK, LANES), F32),
        compiler_params=_cparams(("parallel",), 32),
        name="retention_tables",
    )(per_lane, per_head)


def _ret_state_kernel(rk_ref, rv_ref, c_ref, s_ref, t_ref, st_ref, r_ref):
    C = RET_CHUNK
    nch = rk_ref.shape[0] // C

    @pl.when(pl.program_id(2) == 0)
    def _():
        r_ref[...] = jnp.zeros_like(r_ref)

    wkb = t_ref[T_WKB]
    gcb = t_ref[T_GCB]
    bd = t_ref[T_BD]
    kvs = []
    for n in range(nch):
        sl = slice(n * C, (n + 1) * C)
        k = _rope(rk_ref[sl, :].astype(F32), c_ref[sl, :], s_ref[sl, :]) * (HEAD_DIM ** -0.5)
        kvs.append(lax.dot_general((k * wkb).astype(BF16), rv_ref[sl, :],
                                   (((0,), (0,)), ((), ())), preferred_element_type=F32))
    state = r_ref[...]
    for n in range(nch - 1, -1, -1):
        st_ref[n] = state
        state = gcb * state + bd * kvs[n]
    r_ref[...] = state


def _ret_main_kernel(rq_ref, rk_ref, rv_ref, rg_ref, c_ref, s_ref, t_ref, st_ref, j_ref,
                     o_ref, sf_ref):
    C = RET_CHUNK
    nch = rq_ref.shape[0] // C

    @pl.when(pl.program_id(2) == 0)
    def _():
        sf_ref[...] = jnp.zeros_like(sf_ref)

    lane = lax.broadcasted_iota(jnp.int32, (1, LANES), 1)
    lo64 = lane < HEAD_DIM
    j_bf = j_ref[...]
    chunks = [slice(n * C, (n + 1) * C) for n in range(nch)]
    nt = (((1,), (1,)), ((), ()))
    tn = (((0,), (0,)), ((), ()))
    q, k, vb = [], [], []
    for sl in chunks:
        cs, sn = c_ref[sl, :], s_ref[sl, :]
        q.append(_rope(rq_ref[sl, :].astype(F32), cs, sn))
        k.append(_rope(rk_ref[sl, :].astype(F32), cs, sn) * (HEAD_DIM ** -0.5))
        vb.append(rv_ref[sl, :])
    scores = []
    for n in range(nch):
        qb = q[n].astype(BF16)
        kb = k[n].astype(BF16)
        zero = jnp.zeros_like(qb)
        scores.append([lax.dot_general(jnp.where(lo64, qb, zero), kb, nt,
                                       preferred_element_type=F32),
                       lax.dot_general(jnp.where(lo64, zero, qb), kb, nt,
                                       preferred_element_type=F32)])
    kvs = [lax.dot_general((k[n] * t_ref[T_WKF]).astype(BF16), vb[n], tn,
                           preferred_element_type=F32) for n in range(nch)]
    cross_b = [jnp.dot((q[n] * t_ref[T_WQB]).astype(BF16), st_ref[n].astype(BF16),
                       preferred_element_type=F32) for n in range(nch)]
    intra = []
    for n in range(nch):
        parts = [jnp.dot((scores[n][hp] * t_ref[T_D0 + hp]).astype(BF16), vb[n],
                         preferred_element_type=F32) for hp in range(2)]
        intra.append(jnp.where(lo64, parts[0], parts[1]))
    state = sf_ref[...]
    states = []
    for n in range(nch):
        states.append(state.astype(BF16))
        state = t_ref[T_GCF] * state + t_ref[T_BD] * kvs[n]
    sf_ref[...] = state
    outs = [intra[n] + cross_b[n]
            + jnp.dot((q[n] * t_ref[T_WQF]).astype(BF16), states[n], preferred_element_type=F32)
            for n in range(nch)]
    mean_sq = [_head_mean_sq(o, j_bf) for o in outs]
    for n, sl in enumerate(chunks):
        ret = outs[n] * lax.rsqrt(mean_sq[n] + NORM_EPS)
        g = rg_ref[sl, :].astype(F32)
        o_ref[sl, :] = (ret * (g * _sigmoid(g))).astype(o_ref.dtype)


def _retention(p3, cos_t, sin_t, tables, j_bf):
    B, L, _ = p3.shape
    C = RET_CHUNK
    tb = min(2048, L)
    nblk = L // tb
    nch = tb // C
    nw = RET_W // LANES
    col = lambda piece: (lambda b, j, s: (b, s, nw * piece + j))
    colr = lambda piece: (lambda b, j, s: (b, nblk - 1 - s, nw * piece + j))
    seq = lambda fn: pl.BlockSpec((None, tb, LANES), fn)
    tab_spec = pl.BlockSpec((None, 9, C, LANES), lambda b, j, s: (j, 0, 0, 0))
    states = pl.pallas_call(
        _ret_state_kernel,
        grid=(B, nw, nblk),
        in_specs=[seq(colr(1)), seq(colr(2)),
                  pl.BlockSpec((tb, LANES), lambda b, j, s: (nblk - 1 - s, 0)),
                  pl.BlockSpec((tb, LANES), lambda b, j, s: (nblk - 1 - s, 0)),
                  tab_spec],
        out_specs=pl.BlockSpec((None, None, nch, LANES, LANES),
                               lambda b, j, s: (b, j, nblk - 1 - s, 0, 0)),
        out_shape=jax.ShapeDtypeStruct((B, nw, L // C, LANES, LANES), F32),
        scratch_shapes=[pltpu.VMEM((LANES, LANES), F32)],
        compiler_params=_cparams(("parallel", "parallel", "arbitrary"), 32),
        name="retention_state",
    )(p3, p3, cos_t, sin_t, tables)
    return pl.pallas_call(
        _ret_main_kernel,
        grid=(B, nw, nblk),
        in_specs=[seq(col(0)), seq(col(1)), seq(col(2)), seq(col(3)),
                  pl.BlockSpec((tb, LANES), lambda b, j, s: (s, 0)),
                  pl.BlockSpec((tb, LANES), lambda b, j, s: (s, 0)),
                  tab_spec,
                  pl.BlockSpec((None, None, nch, LANES, LANES), lambda b, j, s: (b, j, s, 0, 0)),
                  pl.BlockSpec((LANES, LANES), lambda b, j, s: (0, 0))],
        out_specs=pl.BlockSpec((None, tb, LANES), lambda b, j, s: (b, s, j)),
        out_shape=jax.ShapeDtypeStruct((B, L, RET_W), BF16),
        scratch_shapes=[pltpu.VMEM((LANES, LANES), F32)],
        compiler_params=_cparams(("parallel", "parallel", "arbitrary"), 32),
        name="retention_main",
    )(p3, p3, p3, p3, cos_t, sin_t, tables, states, j_bf)


def _merge_kernel(oa_ref, ob_ref, oc_ref, od_ref, g0_ref, g1_ref, g2_ref, g3_ref,
                  wb_ref, wo_ref, x_ref, gn_ref, h_ref):
    tm = x_ref.shape[0]
    nparts = 2 if tm % (2 * HALO) == 0 else 1
    parts = [slice(i * tm // nparts, (i + 1) * tm // nparts) for i in range(nparts)]
    branches = ((oa_ref, g0_ref), (ob_ref, g1_ref), (oc_ref, g2_ref), (od_ref, g3_ref))
    proj = [[jnp.dot(br[rows, :].astype(BF16), wb_ref[n], preferred_element_type=F32)
             for n, (br, _) in enumerate(branches)] for rows in parts]
    merged = []
    for pi, rows in enumerate(parts):
        acc = None
        for n, (_, gr) in enumerate(branches):
            term = gr[rows, :].astype(F32) * proj[pi][n]
            acc = term if acc is None else acc + term
        merged.append(acc.astype(BF16))
    ys = [jnp.dot(m, wo_ref[...], preferred_element_type=F32) for m in merged]
    for rows, y in zip(parts, ys):
        ms = jnp.mean(y * y, axis=-1, keepdims=True)
        h_ref[rows, :] = x_ref[rows, :] + y * lax.rsqrt(ms + NORM_EPS) * gn_ref[...]


def _merge(oa, ob, oc, od, p2, wb_bf, wo_bf, x2, gn):
    T = x2.shape[0]
    tm = min(1024, T)
    br = pl.BlockSpec((tm, BRANCH_W), lambda i: (i, 0))
    gate = lambda n: pl.BlockSpec((tm, D_MODEL), lambda i, n=n: (i, n))
    row = pl.BlockSpec((tm, D_MODEL), lambda i: (i, 0))
    return pl.pallas_call(
        _merge_kernel,
        grid=(T // tm,),
        in_specs=[br, br, br, br, gate(0), gate(1), gate(2), gate(3),
                  pl.BlockSpec((N_BRANCH, BRANCH_W, D_MODEL), lambda i: (0, 0, 0)),
                  pl.BlockSpec((D_MODEL, D_MODEL), lambda i: (0, 0)),
                  row, pl.BlockSpec((1, D_MODEL), lambda i: (0, 0))],
        out_specs=row,
        out_shape=jax.ShapeDtypeStruct((T, D_MODEL), F32),
        compiler_params=_cparams(("parallel",), 48),
        name="merge_out_proj",
    )(oa, ob, oc, od, p2, p2, p2, p2, wb_bf, wo_bf, x2, gn)


def _ffn_kernel(h_ref, g2_ref, wg_ref, wu_ref, wd_ref, g3_ref, o_ref):
    tm = h_ref.shape[0]
    halves = [slice(0, tm // 2), slice(tm // 2, tm)] if tm % (2 * HALO) == 0 else [slice(0, tm)]
    hns = []
    for rows in halves:
        h = h_ref[rows, :]
        ms = jnp.mean(h * h, axis=-1, keepdims=True)
        hns.append((h * lax.rsqrt(ms + NORM_EPS) * g2_ref[...]).astype(BF16))
    fs = [None] * len(halves)
    for lo, hi in _col_chunks(D_FF, 2):
        gus = [(jnp.dot(hn, wg_ref[:, lo:hi], preferred_element_type=F32),
                jnp.dot(hn, wu_ref[:, lo:hi], preferred_element_type=F32)) for hn in hns]
        acts = [(g * _sigmoid(g) * u).astype(BF16) for g, u in gus]
        for i, a in enumerate(acts):
            part = jnp.dot(a, wd_ref[lo:hi, :], preferred_element_type=F32)
            fs[i] = part if fs[i] is None else fs[i] + part
    for rows, f in zip(halves, fs):
        ms = jnp.mean(f * f, axis=-1, keepdims=True)
        o_ref[rows, :] = h_ref[rows, :] + f * lax.rsqrt(ms + NORM_EPS) * g3_ref[...]


def _ffn(h2, g2, wg_bf, wu_bf, wd_bf, g3):
    T = h2.shape[0]
    tm = min(512, T)
    row = pl.BlockSpec((tm, D_MODEL), lambda i: (i, 0))
    vec = pl.BlockSpec((1, D_MODEL), lambda i: (0, 0))
    return pl.pallas_call(
        _ffn_kernel,
        grid=(T // tm,),
        in_specs=[row, vec,
                  pl.BlockSpec((D_MODEL, D_FF), lambda i: (0, 0)),
                  pl.BlockSpec((D_MODEL, D_FF), lambda i: (0, 0)),
                  pl.BlockSpec((D_FF, D_MODEL), lambda i: (0, 0)),
                  vec],
        out_specs=row,
        out_shape=jax.ShapeDtypeStruct((T, D_MODEL), F32),
        compiler_params=_cparams(("parallel",), 48),
        name="ffn",
    )(h2, g2, wg_bf, wu_bf, wd_bf, g3)


def _rope_tables(L):
    rows = L // GRID_W
    r = jnp.repeat(jnp.arange(rows, dtype=F32), GRID_W)
    c = jnp.tile(jnp.arange(GRID_W, dtype=F32), rows)
    inv = ROPE_BASE ** (-jnp.arange(ROPE_FREQS, dtype=F32) / ROPE_FREQS)
    ar = r[:, None] * inv
    ac = c[:, None] * inv
    cos64 = jnp.concatenate([jnp.cos(ar), jnp.cos(ar), jnp.cos(ac), jnp.cos(ac)], axis=1)
    sin64 = jnp.concatenate([-jnp.sin(ar), jnp.sin(ar), -jnp.sin(ac), jnp.sin(ac)], axis=1)
    return jnp.tile(cos64, (1, 2)), jnp.tile(sin64, (1, 2))


def _filter_features(L):
    t = np.linspace(0.0, 1.0, L)[:, None]
    f = np.linspace(1e-4, HY_BANDS - 1, HY_BANDS)
    ang = (2.0 * math.pi / L) * np.arange(L)[:, None] * f[None, :]
    z = np.concatenate([t, np.cos(ang), -np.sin(ang)], axis=-1)
    z = np.pad(z, ((0, 0), (0, HY_FILTER_HIDDEN - HY_EMB)))
    idx = np.concatenate([np.arange(L), [0], np.arange(L - 1, 0, -1)])
    return jnp.asarray(z[idx].astype(np.float32))


def _head_mean_matrix():
    i = np.arange(LANES)
    j = ((i[:, None] // HEAD_DIM) == (i[None, :] // HEAD_DIM)).astype(np.float32) / HEAD_DIM
    return jnp.asarray(j.astype(BF16))


def _halve_gate_columns(w_in_bf):
    scale = jnp.where(jnp.arange(IN_COLS) >= GATE_OFF, 0.5, 1.0).astype(BF16)
    return w_in_bf * scale


def _trunk(x, wts, fft_dims=None):
    B, L, _ = x.shape
    T = B * L
    n1, n2 = fft_dims or _fft_dims(L)
    cst = _fft_consts(n1, n2, n1 // 2)
    cos_t, sin_t = _rope_tables(L)
    z2 = _filter_features(L)
    deltas = jnp.abs(jnp.linspace(math.log(HY_TARGET) / HY_SLOW_DECAY,
                                  math.log(HY_TARGET) / HY_FAST_DECAY, HY_WIDTH, dtype=F32))[None, :]
    j_bf = _head_mean_matrix()
    depth = wts["w_in"].shape[0]
    x2 = x.reshape(T, D_MODEL)
    for l in range(depth):
        ng = wts["norm_gains"][l]
        w1p = jnp.pad(wts["hy_w1"][l], ((0, HY_FILTER_HIDDEN - HY_EMB), (0, 0)))
        k2raw, sumsq = _hy_filter_raw(z2, w1p, wts["hy_b1"][l][None, :], wts["hy_w2"][l],
                                      wts["hy_b2"][l][None, :], wts["hy_w3"][l],
                                      wts["hy_freq"][l], deltas, L)
        kf_re, kf_im = _filter_fft(k2raw, sumsq, cst, n1, n2)

        qg = jnp.tile(wts["qk_norm"][l, 0], 2)[None, :]
        kg = jnp.tile(wts["qk_norm"][l, 1], 2)[None, :]
        qt, k, vt, p_ret, p_gate, z, x0u, out_d = _in_proj(
            x2, ng[0][None, :], wts["w_in_bf"][l], wts["hy_conv_w"][l],
            wts["hy_conv_b"][l][None, :], wts["sc_conv_w"][l], cos_t, sin_t, qg, kg, j_bf, L)
        out_a = _flash(qt, k, vt)

        out_b = _hy_conv(z.reshape(B, L, HY_WIDTH), x0u.reshape(B, L, HY_WIDTH),
                         wts["hy_bias"][l][None, :], kf_re, kf_im, cst, n1, n2)

        tables = _ret_tables(wts["ret_decay_exp"][l])
        out_c = _retention(p_ret.reshape(B, L, 4 * RET_W), cos_t, sin_t, tables, j_bf)

        h2 = _merge(out_a.reshape(T, BRANCH_W), out_b.reshape(T, BRANCH_W),
                    out_c.reshape(T, BRANCH_W), out_d,
                    p_gate, wts["w_branch_bf"][l], wts["w_out_bf"][l], x2, ng[1][None, :])
        x2 = _ffn(h2, ng[2][None, :], wts["w_gate_bf"][l], wts["w_up_bf"][l],
                  wts["w_ffn_out_bf"][l], ng[3][None, :])
    return x2.reshape(B, L, D_MODEL)


def kernel(x_prompt, x_sample, norm_gains, w_in, qk_norm, hy_conv_w, hy_conv_b, hy_w1, hy_b1, hy_w2,
           hy_b2, hy_w3, hy_freq, hy_bias, ret_decay_exp, sc_conv_w, w_branch, w_out, w_ffn_in,
           w_ffn_out):
    wts = dict(norm_gains=norm_gains, w_in=w_in, qk_norm=qk_norm, hy_conv_w=hy_conv_w,
               hy_conv_b=hy_conv_b, hy_w1=hy_w1, hy_b1=hy_b1, hy_w2=hy_w2, hy_b2=hy_b2,
               hy_w3=hy_w3, hy_freq=hy_freq, hy_bias=hy_bias, ret_decay_exp=ret_decay_exp,
               sc_conv_w=sc_conv_w,
               w_in_bf=_halve_gate_columns(w_in.astype(BF16)), w_branch_bf=w_branch.astype(BF16),
               w_out_bf=w_out.astype(BF16), w_gate_bf=w_ffn_in[..., :D_FF].astype(BF16),
               w_up_bf=w_ffn_in[..., D_FF:].astype(BF16), w_ffn_out_bf=w_ffn_out.astype(BF16))
    return _trunk(x_prompt, wts), _trunk(x_sample, wts)
```

```python
import functools
import math

import numpy as np
import jax
import jax.numpy as jnp
from jax import lax
from jax.experimental import pallas as pl
from jax.experimental.pallas import tpu as pltpu

F32 = jnp.float32
BF16 = jnp.bfloat16

D_MODEL = 1024
GRID_W = 64
N_BRANCH = 4
BRANCH_W = 256
HEAD_DIM = 64
ATT_Q_HEADS = 4
ATT_KV_HEADS = 2
ROPE_BASE = 10000.0
ROPE_FREQS = HEAD_DIM // 4
HY_WIDTH = BRANCH_W
HY_EMB = 33
HY_BANDS = (HY_EMB - 1) // 2
HY_FILTER_HIDDEN = 64
HY_FAST_DECAY = 0.3
HY_SLOW_DECAY = 1.5
HY_TARGET = 1e-2
RET_HEADS = 4
RET_W = RET_HEADS * HEAD_DIM
RET_CHUNK = 128
SC_WIDTH = BRANCH_W
D_FF = 2816
NORM_EPS = 1e-6

ATT_Q_W = ATT_Q_HEADS * HEAD_DIM
ATT_KV_W = ATT_KV_HEADS * HEAD_DIM
A_K_OFF = ATT_Q_W
A_V_OFF = A_K_OFF + ATT_KV_W
HY_OFF = A_V_OFF + ATT_KV_W
RET_OFF = HY_OFF + 3 * HY_WIDTH
SC_OFF = RET_OFF + 4 * RET_W
GATE_OFF = SC_OFF + 3 * SC_WIDTH
IN_COLS = GATE_OFF + N_BRANCH * D_MODEL

LANES = 128
SUBLANES = 8
MXU_W = 256
HALO = 2 * SUBLANES
ATT_KB = 512
ATT_TQ = 256
VT_ROWS = HEAD_DIM + HALO
Q_SCALE = HEAD_DIM ** -0.5 * math.log2(math.e)
FFT_PASSES_FILTER = 3
FFT_PASSES_DATA = 1
FFT_CH = 8
FFT_SETS = 1
MIB = 1 << 20


def _cparams(sem, vmem_mib):
    return pltpu.CompilerParams(dimension_semantics=sem, vmem_limit_bytes=vmem_mib * MIB)


def _sigmoid(x):
    return 0.5 * jnp.tanh(0.5 * x) + 0.5


def _split(x):
    hi = x.astype(BF16)
    lo = (x - hi.astype(F32)).astype(BF16)
    return hi, lo


def _np_split(a64):
    a32 = np.asarray(a64, np.float32)
    hi = a32.astype(BF16)
    lo = (a32 - hi.astype(np.float32)).astype(BF16)
    return jnp.asarray(hi), jnp.asarray(lo)


def _mm_const(a_hi, a_lo, x, passes):
    if passes == 1:
        return jnp.dot(a_hi, x.astype(BF16), preferred_element_type=F32)
    xh, xl = _split(x)
    out = jnp.dot(a_hi, xh, preferred_element_type=F32)
    if passes >= 3:
        out = out + jnp.dot(a_lo, xh, preferred_element_type=F32)
        out = out + jnp.dot(a_hi, xl, preferred_element_type=F32)
    return out


def _mm3(a, b):
    ah, al = _split(a)
    bh, bl = _split(b)
    out = jnp.dot(ah, bh, preferred_element_type=F32)
    out = out + jnp.dot(al, bh, preferred_element_type=F32)
    return out + jnp.dot(ah, bl, preferred_element_type=F32)


def _head_mean_sq(x, j_bf):
    hi, lo = _split(x * x)
    return (jnp.dot(hi, j_bf, preferred_element_type=F32)
            + jnp.dot(lo, j_bf, preferred_element_type=F32))


def _rope(x, c, s):
    lane = lax.broadcasted_iota(jnp.int32, (1, LANES), 1)
    is_b = (lane & 16) != 0
    partner = jnp.where(is_b, pltpu.roll(x, 16, 1), pltpu.roll(x, LANES - 16, 1))
    return x * c + partner * s


def _col_chunks(n, parts):
    tiles = n // MXU_W
    bounds = [MXU_W * (tiles * i // parts) for i in range(parts + 1)]
    return list(zip(bounds[:-1], bounds[1:]))


def _conv3_rows(main, before, after, w):
    tb = main.shape[0]
    w0, w1, w2 = w[0:1, :], w[1:2, :], w[2:3, :]
    body = pltpu.roll(main, 1, 0) * w0 + main * w1 + pltpu.roll(main, tb - 1, 0) * w2
    row_first = before * w0 + main[0:1, :] * w1 + main[1:2, :] * w2
    row_last = main[tb - 2:tb - 1, :] * w0 + main[tb - 1:tb, :] * w1 + after * w2
    row = lax.broadcasted_iota(jnp.int32, (SUBLANES, 1), 0)
    head = jnp.where(row == 0, row_first, body[:SUBLANES])
    tail = jnp.where(row == SUBLANES - 1, row_last, body[tb - SUBLANES:])
    return jnp.concatenate([head, body[SUBLANES:tb - SUBLANES], tail], axis=0)


def _in_proj_kernel(x_ref, xp_ref, xn_ref, g_ref, w_ref, hcw_ref, hcb_ref, scw_ref,
                    c_ref, s_ref, qg_ref, kg_ref, j_ref,
                    qt_ref, k_ref, vt_ref, pr_ref, pg_ref, z_ref, x0_ref, od_ref, *,
                    tiles_per_seq):
    i = pl.program_id(0)
    first = (i % tiles_per_seq) == 0
    last = (i % tiles_per_seq) == tiles_per_seq - 1
    g = g_ref[...]

    def normed(x):
        ms = jnp.mean(x * x, axis=-1, keepdims=True)
        return (x * lax.rsqrt(ms + NORM_EPS) * g).astype(BF16)

    dot = lambda a, lo, hi: jnp.dot(a, w_ref[:, lo:hi], preferred_element_type=F32)
    xn = normed(x_ref[...])
    halo = normed(jnp.concatenate([xp_ref[...], xn_ref[...]], axis=0))
    edge = lambda h: (jnp.where(first, 0.0, h[SUBLANES - 1:SUBLANES, :]),
                      jnp.where(last, 0.0, h[SUBLANES:SUBLANES + 1, :]))

    head = dot(xn, 0, RET_OFF)
    hy_halo = dot(halo, HY_OFF, RET_OFF)
    sc = dot(xn, SC_OFF, GATE_OFF)
    sc_halo = dot(halo, SC_OFF + SC_WIDTH, GATE_OFF)
    pr_ref[...] = dot(xn, RET_OFF, SC_OFF).astype(pr_ref.dtype)
    _attn_prep_tile(head[:, :HY_OFF], c_ref, s_ref, qg_ref, kg_ref, j_ref, qt_ref, k_ref, vt_ref)

    u = _conv3_rows(head[:, HY_OFF:], *edge(hy_halo), hcw_ref[...]) + hcb_ref[...]
    x0_ref[...] = u[:, :HY_WIDTH]
    z_ref[...] = u[:, 2 * HY_WIDTH:] * u[:, HY_WIDTH:2 * HY_WIDTH]
    m = sc[:, SC_WIDTH:2 * SC_WIDTH] * sc[:, 2 * SC_WIDTH:]
    mh = sc_halo[:, :SC_WIDTH] * sc_halo[:, SC_WIDTH:]
    od_ref[...] = (sc[:, :SC_WIDTH] * _conv3_rows(m, *edge(mh), scw_ref[...])).astype(od_ref.dtype)

    for lo, hi in _col_chunks(N_BRANCH * D_MODEL, 2):
        g_half = dot(xn, GATE_OFF + lo, GATE_OFF + hi).astype(pg_ref.dtype)
        pg_ref[:, lo:hi] = 0.5 * jnp.tanh(g_half) + 0.5


def _in_proj(x2, g, w_bf, hcw, hcb, scw, cos_t, sin_t, qg, kg, j_bf, L):
    T = x2.shape[0]
    B = T // L
    tm = min(ATT_KB, L)
    tq = min(ATT_TQ, L)
    tps = L // tm
    r8 = tm // SUBLANES
    row = lambda w: pl.BlockSpec((tm, w), lambda i: (i, 0))
    full = lambda a: pl.BlockSpec(a.shape, lambda i: (0,) * a.ndim)
    tab = pl.BlockSpec((tm, LANES), lambda i: (i % tps, 0))
    sds = lambda w, dt: jax.ShapeDtypeStruct((T, w), dt)
    return pl.pallas_call(
        functools.partial(_in_proj_kernel, tiles_per_seq=tps),
        grid=(T // tm,),
        in_specs=[row(D_MODEL),
                  pl.BlockSpec((SUBLANES, D_MODEL), lambda i: (jnp.maximum(i * r8 - 1, 0), 0)),
                  pl.BlockSpec((SUBLANES, D_MODEL),
                               lambda i: (jnp.minimum((i + 1) * r8, T // SUBLANES - 1), 0)),
                  full(g), full(w_bf), full(hcw), full(hcb), full(scw),
                  tab, tab, full(qg), full(kg), full(j_bf)],
        out_specs=[pl.BlockSpec((None, ATT_Q_HEADS, tm // tq, LANES, tq),
                                lambda i: (i // tps, 0, i % tps, 0, 0)),
                   pl.BlockSpec((None, tm, LANES), lambda i: (i // tps, i % tps, 0)),
                   pl.BlockSpec((None, tm // ATT_KB, ATT_KV_HEADS * VT_ROWS, ATT_KB),
                                lambda i: (i // tps, i % tps, 0, 0)),
                   row(RET_W * 4), row(N_BRANCH * D_MODEL),
                   row(HY_WIDTH), row(HY_WIDTH), row(SC_WIDTH)],
        out_shape=[jax.ShapeDtypeStruct((B, ATT_Q_HEADS, L // tq, LANES, tq), BF16),
                   jax.ShapeDtypeStruct((B, L, LANES), BF16),
                   jax.ShapeDtypeStruct((B, L // ATT_KB, ATT_KV_HEADS * VT_ROWS, ATT_KB), BF16),
                   sds(RET_W * 4, BF16), sds(N_BRANCH * D_MODEL, BF16),
                   sds(HY_WIDTH, F32), sds(HY_WIDTH, F32), sds(SC_WIDTH, BF16)],
        compiler_params=_cparams(("parallel",), 48),
        name="in_proj",
    )(x2, x2, x2, g, w_bf, hcw, hcb, scw, cos_t, sin_t, qg, kg, j_bf)


def _attn_prep_tile(qkv, c_ref, s_ref, qg_ref, kg_ref, j_ref, qt_ref, k_ref, vt_ref):
    c = c_ref[...]
    s = s_ref[...]
    j_bf = j_ref[...]
    tm = qkv.shape[0]
    tq = qt_ref.shape[-1]
    zeros = jnp.zeros((HEAD_DIM, tm), BF16)
    for blk in range(2):
        q = qkv[:, LANES * blk:LANES * (blk + 1)]
        qn = q * lax.rsqrt(_head_mean_sq(q, j_bf) + NORM_EPS) * qg_ref[...]
        qt = (_rope(qn, c, s) * Q_SCALE).astype(BF16).T
        lo, hi = qt[:HEAD_DIM], qt[HEAD_DIM:]
        if blk == 0:
            heads = (jnp.concatenate([lo, zeros], axis=0), jnp.concatenate([hi, zeros], axis=0))
        else:
            heads = (jnp.concatenate([zeros, lo], axis=0), jnp.concatenate([zeros, hi], axis=0))
        for hp, qh in enumerate(heads):
            for t in range(tm // tq):
                qt_ref[2 * blk + hp, t] = qh[:, t * tq:(t + 1) * tq]
    k = qkv[:, A_K_OFF:A_K_OFF + LANES]
    kn = k * lax.rsqrt(_head_mean_sq(k, j_bf) + NORM_EPS) * kg_ref[...]
    k_ref[...] = _rope(kn, c, s).astype(BF16)
    v = qkv[:, A_V_OFF:A_V_OFF + LANES].astype(BF16)
    ones = jnp.ones((VT_ROWS - HEAD_DIM, ATT_KB), BF16)
    for t in range(tm // ATT_KB):
        vt = v[t * ATT_KB:(t + 1) * ATT_KB, :].T
        vt_ref[t] = jnp.concatenate([vt[:HEAD_DIM], ones, vt[HEAD_DIM:], ones], axis=0)


def _flash_kernel(qt_ref, k_ref, vt_ref, o_ref, sa_ref, ma_ref, sb_ref, mb_ref, *,
                  ngrp, nsub, ntile):
    tq = ATT_TQ
    group = ATT_Q_HEADS // ATT_KV_HEADS

    def produce(slot, qi, g, t, h):
        s_ref, mx_ref = slot
        st = pl.multiple_of((g * nsub + t) * ATT_KB, ATT_KB)
        s = jnp.dot(k_ref[pl.ds(st, ATT_KB), :], qt_ref[h, qi],
                    preferred_element_type=F32)
        s_ref[t * ATT_Q_HEADS + h] = s
        mx_ref[t * ATT_Q_HEADS + h] = jnp.max(s, axis=0, keepdims=True)

    def consume(slot, g, t, h, state):
        s_ref, mx_ref = slot
        m, acc = state
        m_new = jnp.maximum(m, mx_ref[t * ATT_Q_HEADS + h])
        alpha = jnp.exp2(m - m_new)
        p = jnp.exp2((s_ref[t * ATT_Q_HEADS + h] - m_new).astype(BF16))
        j = h // group
        vt = vt_ref[g * nsub + t, VT_ROWS * j:VT_ROWS * (j + 1), :]
        return m_new, alpha * acc + jnp.dot(vt, p, preferred_element_type=F32)

    def step(slot_in, g_in, slot_out, qi_out, g_out, carry):
        carry = list(carry)
        for t in range(nsub):
            for h in range(ATT_Q_HEADS):
                if slot_out is not None:
                    produce(slot_out, qi_out, g_out, t, h)
                if slot_in is not None:
                    carry[h] = consume(slot_in, g_in, t, h, carry[h])
        return tuple(carry)

    def fresh():
        return tuple((jnp.full((1, tq), -jnp.inf, F32), jnp.zeros((VT_ROWS, tq), F32))
                     for _ in range(ATT_Q_HEADS))

    def finalize(qi, carry):
        outs = [acc[:HEAD_DIM] / acc[HEAD_DIM:HEAD_DIM + 1] for _, acc in carry]
        rows = pl.ds(pl.multiple_of(qi * tq, tq), tq)
        o_ref[rows, :] = jnp.concatenate(outs, axis=0).T.astype(o_ref.dtype)

    slot_a = (sa_ref, ma_ref)
    slot_b = (sb_ref, mb_ref)

    if ngrp == 1:
        def tile(qi, _):
            step(None, None, slot_a, qi, 0, ())
            finalize(qi, step(slot_a, 0, None, None, None, fresh()))
            return 0
    else:
        step(None, None, slot_a, 0, 0, ())

        def tile(qi, _):
            def body(i, carry):
                g = 2 * i
                carry = step(slot_a, g, slot_b, qi, g + 1, carry)
                return step(slot_b, g + 1, slot_a, qi, g + 2, carry)

            carry = lax.fori_loop(0, ngrp // 2 - 1, body, fresh())
            carry = step(slot_a, ngrp - 2, slot_b, qi, ngrp - 1, carry)
            carry = step(slot_b, ngrp - 1, slot_a, jnp.minimum(qi + 1, ntile - 1), 0, carry)
            finalize(qi, carry)
            return 0

    lax.fori_loop(0, ntile, tile, 0)


def _flash(qt, k, vt):
    B, _, nqt, _, tq = qt.shape
    L = nqt * tq
    ntile = math.gcd(nqt, 8)
    nkb = L // ATT_KB
    nsub = 2 if nkb % 4 == 0 and nkb >= 8 else 1
    ngrp = L // (ATT_KB * nsub)
    assert ngrp == 1 or ngrp % 2 == 0
    s_scratch = pltpu.VMEM((nsub * ATT_Q_HEADS, ATT_KB, tq), F32)
    m_scratch = pltpu.VMEM((nsub * ATT_Q_HEADS, 1, tq), F32)
    return pl.pallas_call(
        functools.partial(_flash_kernel, ngrp=ngrp, nsub=nsub, ntile=ntile),
        grid=(B, nqt // ntile),
        in_specs=[pl.BlockSpec((None, ATT_Q_HEADS, ntile, LANES, tq),
                               lambda b, i: (b, 0, i, 0, 0)),
                  pl.BlockSpec((None, L, LANES), lambda b, i: (b, 0, 0)),
                  pl.BlockSpec((None, L // ATT_KB, ATT_KV_HEADS * VT_ROWS, ATT_KB),
                               lambda b, i: (b, 0, 0, 0))],
        out_specs=pl.BlockSpec((None, ntile * tq, ATT_Q_W), lambda b, i: (b, i, 0)),
        out_shape=jax.ShapeDtypeStruct((B, L, ATT_Q_W), BF16),
        scratch_shapes=[s_scratch, m_scratch, s_scratch, m_scratch],
        compiler_params=_cparams(("parallel", "arbitrary"), 48),
        name="flash_attn",
    )(qt, k, vt)


def _hy_filter_kernel(z_ref, w1_ref, b1_ref, w2_ref, b2_ref, w3_ref, fr_ref, dl_ref,
                      k_ref, ss_ref, *, L):
    step = pl.program_id(0)
    tr = z_ref.shape[0]
    z = z_ref[...]
    h = jnp.sin(fr_ref[0:1, :] * (_mm3(z, w1_ref[...]) + b1_ref[...]))
    h = jnp.sin(fr_ref[1:2, :] * (_mm3(h, w2_ref[...]) + b2_ref[...]))
    h3 = _mm3(h, w3_ref[...])
    row = step * tr + lax.broadcasted_iota(jnp.int32, (tr, 1), 0)
    val = jnp.where(row < L, h3[:, :HY_WIDTH], h3[:, HY_WIDTH:])
    val = val * jnp.exp(-z[:, 0:1] * dl_ref[...])
    val = jnp.where(row == L, 0.0, val)
    k_ref[...] = val

    @pl.when(step == 0)
    def _():
        ss_ref[...] = jnp.zeros_like(ss_ref)

    ss_ref[...] += jnp.sum(val * val, axis=0, keepdims=True)


def _hy_filter_raw(z2, w1p, b1, w2, b2, w3, freq, deltas, L):
    n = 2 * L
    tr = min(512, n)
    H = HY_FILTER_HIDDEN
    full = lambda shape: pl.BlockSpec(shape, lambda i: (0, 0))
    return pl.pallas_call(
        functools.partial(_hy_filter_kernel, L=L),
        grid=(n // tr,),
        in_specs=[pl.BlockSpec((tr, H), lambda i: (i, 0)),
                  full((H, H)), full((1, H)), full((H, H)), full((1, H)),
                  full((H, 2 * HY_WIDTH)), full((2, H)), full((1, HY_WIDTH))],
        out_specs=[pl.BlockSpec((tr, HY_WIDTH), lambda i: (i, 0)), full((1, HY_WIDTH))],
        out_shape=[jax.ShapeDtypeStruct((n, HY_WIDTH), F32),
                   jax.ShapeDtypeStruct((1, HY_WIDTH), F32)],
        compiler_params=_cparams(("arbitrary",), 32),
        name="hyena_filter",
    )(z2, w1p, b1, w2, b2, w3, freq, deltas)


def _fft_dims(L):
    n = 2 * L
    n2 = 128 if n >= 32 * 128 else 64
    return n // n2, n2


def _fft_consts(n1, n2, r):
    n = n1 * n2
    a1 = 2.0 * np.pi * np.outer(np.arange(n1), np.arange(n1)) / n1
    c1, s1 = np.cos(a1), np.sin(a1)
    a2 = 2.0 * np.pi * np.outer(np.arange(n2), np.arange(n2)) / n2
    c2, s2 = np.cos(a2), np.sin(a2)
    f1 = np.block([[c1[:, :r], s1[:, :r]], [-s1[:, :r], c1[:, :r]]])
    f1_full = np.concatenate([c1, -s1], axis=0)
    m2 = np.block([[c2, s2], [-s2, c2]])
    m2i = np.block([[c2, -s2], [s2, c2]])
    fi = np.block([[c1[:r, :], -s1[:r, :]], [s1[:r, :], c1[:r, :]]])
    aw = 2.0 * np.pi * np.arange(n1) / n
    off = np.repeat(np.arange(FFT_CH), LANES)[None, :] * aw[:, None]
    step = np.broadcast_to((FFT_CH * aw)[:, None], (n1, LANES))
    f32 = lambda a: jnp.asarray(np.asarray(a, np.float32))
    return dict(f1=_np_split(f1), f1_full=_np_split(f1_full), m2=_np_split(m2),
                m2i=_np_split(m2i), fi=_np_split(fi),
                tw=(f32(np.cos(off)), f32(-np.sin(off)), f32(np.cos(step)), f32(-np.sin(step))))


def _cmul(a_re, a_im, b_re, b_im):
    return a_re * b_re - a_im * b_im, a_re * b_im + a_im * b_re


def _lane_tile(x, reps):
    return jnp.concatenate([x] * reps, axis=1)


def _lane_part(x, j):
    return x[:, LANES * j:LANES * (j + 1)]


def _chunk_twiddle(t_re, t_im, tw_refs):
    d_re, d_im, s_re, s_im = (ref[...] for ref in tw_refs)
    cur = _cmul(_lane_tile(t_re, FFT_CH), _lane_tile(t_im, FFT_CH), d_re, d_im)
    return cur, _cmul(t_re, t_im, s_re, s_im)


def _twiddle_init(n1):
    return jnp.ones((n1, LANES), F32), jnp.zeros((n1, LANES), F32)


def _fft_stage1(load_rows, a_ref, f_hi, f_lo, tw_refs, n1, n2, passes):
    def body(ci, carry):
        i0s, tws = [], []
        for u in range(FFT_SETS):
            tw, carry = _chunk_twiddle(*carry, tw_refs)
            tws.append(tw)
            i0s.append((ci * FFT_SETS + u) * FFT_CH)
        prods = [_mm_const(f_hi, f_lo, jnp.concatenate(
            [load_rows(i0 + j) for j in range(FFT_CH)], axis=1), passes)
            for i0 in i0s]
        for i0, (t_re, t_im), a in zip(i0s, tws, prods):
            o_re, o_im = _cmul(a[:n1], a[n1:], t_re, t_im)
            for j in range(FFT_CH):
                base = pl.multiple_of((i0 + j) * 2 * n1, 2 * n1)
                a_ref[pl.ds(base, n1), :] = _lane_part(o_re, j)
                a_ref[pl.ds(base + n1, n1), :] = _lane_part(o_im, j)
        return carry

    lax.fori_loop(0, n2 // (FFT_CH * FFT_SETS), body, _twiddle_init(n1))


def _load_spectrum_rows(a_ref, k1, n1, n2):
    ld = lambda off: a_ref[pl.ds(off, n2, stride=2 * n1), :]
    return jnp.concatenate(
        [jnp.concatenate([ld(k1 + j) for j in range(FFT_CH)], axis=1),
         jnp.concatenate([ld(n1 + k1 + j) for j in range(FFT_CH)], axis=1)], axis=0)


def _filter_fft_kernel(k_ref, ss_ref, f_hi_ref, f_lo_ref, m_hi_ref, m_lo_ref,
                       dre_ref, dim_ref, sre_ref, sim_ref, kr_ref, ki_ref, a_ref, *, n1, n2):
    c = pl.program_id(1)

    @pl.when(c == 0)
    def _():
        _fft_stage1(lambda i: k_ref[pl.ds(i, n1, stride=n2), :], a_ref,
                    f_hi_ref[...], f_lo_ref[...], (dre_ref, dim_ref, sre_ref, sim_ref),
                    n1, n2, FFT_PASSES_FILTER)

    scale = _lane_tile(lax.rsqrt(ss_ref[...] + NORM_EPS) * (1.0 / (n1 * n2)), FFT_CH)
    xin = _load_spectrum_rows(a_ref, c * FFT_CH, n1, n2)
    x = _mm_const(m_hi_ref[...], m_lo_ref[...], xin, FFT_PASSES_FILTER)
    kr_ref[...] = x[:n2] * scale
    ki_ref[...] = x[n2:] * scale


def _filter_fft(k2raw, sumsq, cst, n1, n2):
    n = n1 * n2
    nw = HY_WIDTH // LANES
    consts = [*cst["f1_full"], *cst["m2"], *cst["tw"]]
    full = lambda a: pl.BlockSpec(a.shape, lambda w, c: (0,) * a.ndim)
    out_spec = pl.BlockSpec((None, None, n2, FFT_CH * LANES), lambda w, c: (w, c, 0, 0))
    out_sh = jax.ShapeDtypeStruct((nw, n1 // FFT_CH, n2, FFT_CH * LANES), F32)
    return pl.pallas_call(
        functools.partial(_filter_fft_kernel, n1=n1, n2=n2),
        grid=(nw, n1 // FFT_CH),
        in_specs=[pl.BlockSpec((n, LANES), lambda w, c: (0, w)),
                  pl.BlockSpec((1, LANES), lambda w, c: (0, w))] + [full(a) for a in consts],
        out_specs=[out_spec, out_spec],
        out_shape=[out_sh, out_sh],
        scratch_shapes=[pltpu.VMEM((2 * n, LANES), F32)],
        compiler_params=_cparams(("parallel", "arbitrary"), 48),
        name="hyena_filter_fft",
    )(k2raw, sumsq, *consts)


def _hy_conv_kernel(z_ref, x0_ref, hb_ref, kr_ref, ki_ref, f_hi_ref, f_lo_ref,
                    m_hi_ref, m_lo_ref, mi_hi_ref, mi_lo_ref, fi_hi_ref, fi_lo_ref,
                    dre_ref, dim_ref, sre_ref, sim_ref, o_ref, a_ref, *, n1, n2):
    c = pl.program_id(2)
    r = n1 // 2
    tw_refs = (dre_ref, dim_ref, sre_ref, sim_ref)

    def load_pair(i):
        seq = pl.ds(i, r, stride=n2)
        return jnp.concatenate([z_ref[0, seq, :], z_ref[1, seq, :]], axis=0)

    @pl.when(c == 0)
    def _():
        _fft_stage1(load_pair, a_ref, f_hi_ref[...], f_lo_ref[...], tw_refs, n1, n2,
                    FFT_PASSES_DATA)

    k1s = [(c * FFT_SETS + u) * FFT_CH for u in range(FFT_SETS)]
    xs = [_mm_const(m_hi_ref[...], m_lo_ref[...], _load_spectrum_rows(a_ref, k1, n1, n2),
                    FFT_PASSES_DATA) for k1 in k1s]
    ys = [jnp.concatenate(_cmul(x[:n2], x[n2:], kr_ref[u], ki_ref[u]), axis=0)
          for u, x in enumerate(xs)]
    bqs = [_mm_const(mi_hi_ref[...], mi_lo_ref[...], y, FFT_PASSES_DATA) for y in ys]
    for k1, bq in zip(k1s, bqs):
        for j in range(FFT_CH):
            a_ref[pl.ds(k1 + j, n2, stride=2 * n1), :] = _lane_part(bq[:n2], j)
            a_ref[pl.ds(n1 + k1 + j, n2, stride=2 * n1), :] = _lane_part(bq[n2:], j)

    @pl.when(c == pl.num_programs(2) - 1)
    def _():
        hb = hb_ref[...]

        def body(ci, carry):
            i0s, bts = [], []
            for u in range(FFT_SETS):
                (t_re, t_im), carry = _chunk_twiddle(*carry, tw_refs)
                i0 = (ci * FFT_SETS + u) * FFT_CH
                rows = [pl.multiple_of((i0 + j) * 2 * n1, 2 * n1) for j in range(FFT_CH)]
                br = jnp.concatenate([a_ref[pl.ds(b0, n1), :] for b0 in rows], axis=1)
                bi = jnp.concatenate([a_ref[pl.ds(b0 + n1, n1), :] for b0 in rows], axis=1)
                bts.append(jnp.concatenate([br * t_re + bi * t_im, bi * t_re - br * t_im],
                                           axis=0))
                i0s.append(i0)
            ys = [_mm_const(fi_hi_ref[...], fi_lo_ref[...], bt, FFT_PASSES_DATA) for bt in bts]
            for i0, y in zip(i0s, ys):
                for j in range(FFT_CH):
                    seq = pl.ds(i0 + j, r, stride=n2)
                    for row in range(2):
                        conv = _lane_part(y[row * r:(row + 1) * r], j)
                        o_ref[row, seq, :] = (x0_ref[row, seq, :]
                                              * (conv + z_ref[row, seq, :] * hb)
                                              ).astype(o_ref.dtype)
            return carry

        lax.fori_loop(0, n2 // (FFT_CH * FFT_SETS), body, _twiddle_init(n1))


def _hy_conv(z, x0u, hbias, kf_re, kf_im, cst, n1, n2):
    B, L, _ = z.shape
    assert B % 2 == 0
    nw = HY_WIDTH // LANES
    seq = pl.BlockSpec((2, L, LANES), lambda b, w, c: (b, 0, w), pipeline_mode=pl.Buffered(1))
    seq_z = pl.BlockSpec((2, L, LANES), lambda b, w, c: (b, 0, w))
    kf_spec = pl.BlockSpec((None, FFT_SETS, n2, FFT_CH * LANES), lambda b, w, c: (w, c, 0, 0))
    full = lambda a: pl.BlockSpec(a.shape, lambda b, w, c: (0,) * a.ndim)
    consts = [*cst["f1"], *cst["m2"], *cst["m2i"], *cst["fi"], *cst["tw"]]
    return pl.pallas_call(
        functools.partial(_hy_conv_kernel, n1=n1, n2=n2),
        grid=(B // 2, nw, n1 // (FFT_CH * FFT_SETS)),
        in_specs=[seq_z, seq, pl.BlockSpec((1, LANES), lambda b, w, c: (0, w)), kf_spec, kf_spec]
                 + [full(a) for a in consts],
        out_specs=seq,
        out_shape=jax.ShapeDtypeStruct((B, L, HY_WIDTH), F32),
        scratch_shapes=[pltpu.VMEM((2 * n1 * n2, LANES), F32)],
        compiler_params=_cparams(("parallel", "parallel", "arbitrary"), 58),
        name="hyena_conv",
    )(z, x0u, hbias, kf_re, kf_im, *consts)


T_D0, T_D1, T_WQF, T_WQB, T_WKF, T_WKB, T_GCF, T_GCB, T_BD = range(9)


def _ret_tables_kernel(rl_ref, rh_ref, t_ref):
    C = RET_CHUNK
    i = lax.broadcasted_iota(jnp.int32, (C, LANES), 0).astype(F32)
    jn = lax.broadcasted_iota(jnp.int32, (C, LANES), 1).astype(F32)
    log_g = lambda e: jnp.log1p(-jnp.exp2(-e))
    diff = i - jn
    for hp in range(2):
        lgf = log_g(rh_ref[hp, 0:1, :])
        lgb = log_g(rh_ref[hp, 1:2, :])
        fwd = jnp.exp(jnp.where(diff >= 0, diff, 0.0) * lgf)
        bwd = jnp.exp(jnp.where(diff < 0, -diff, 0.0) * lgb)
        t_ref[T_D0 + hp] = jnp.where(diff >= 0, fwd, bwd)
    lf = log_g(rl_ref[0:1, :])
    lb = log_g(rl_ref[1:2, :])
    t_ref[T_WQF] = jnp.exp((i + 1.0) * lf)
    t_ref[T_WQB] = jnp.exp((C - i) * lb)
    t_ref[T_WKF] = jnp.exp((C - 1.0 - i) * lf)
    t_ref[T_WKB] = jnp.exp(i * lb)
    bd = jnp.where((i < HEAD_DIM) == (jn < HEAD_DIM), 1.0, 0.0)
    t_ref[T_GCF] = jnp.exp(C * lf) * bd
    t_ref[T_GCB] = jnp.exp(C * lb) * bd
    t_ref[T_BD] = bd


def _ret_tables(rde):
    per_lane = jnp.repeat(rde.reshape(2, 2, 2), HEAD_DIM, axis=2)
    per_lane = per_lane.transpose(1, 0, 2)
    per_head = jnp.broadcast_to(rde.reshape(2, 2, 2, 1), (2, 2, 2, LANES))
    per_head = per_head.transpose(1, 2, 0, 3)
    return pl.pallas_call(
        _ret_tables_kernel,
        grid=(2,),
        in_specs=[pl.BlockSpec((None, 2, LANES), lambda j: (j, 0, 0)),
                  pl.BlockSpec((None, 2, 2, LANES), lambda j: (j, 0, 0, 0))],
        out_specs=pl.BlockSpec((None, 9, RET_CHUNK, LANES), lambda j: (j, 0, 0, 0)),
        out_shape=jax.ShapeDtypeStruct((2, 9, RET_CHUNK, LANES), F32),
        compiler_params=_cparams(("parallel",), 32),
        name="retention_tables",
    )(per_lane, per_head)


def _ret_state_kernel(rk_ref, rv_ref, c_ref, s_ref, t_ref, st_ref, r_ref):
    C = RET_CHUNK
    nch = rk_ref.shape[0] // C

    @pl.when(pl.program_id(2) == 0)
    def _():
        r_ref[...] = jnp.zeros_like(r_ref)

    wkb = t_ref[T_WKB]
    gcb = t_ref[T_GCB]
    bd = t_ref[T_BD]
    kvs = []
    for n in range(nch):
        sl = slice(n * C, (n + 1) * C)
        k = _rope(rk_ref[sl, :].astype(F32), c_ref[sl, :], s_ref[sl, :]) * (HEAD_DIM ** -0.5)
        kvs.append(lax.dot_general((k * wkb).astype(BF16), rv_ref[sl, :],
                                   (((0,), (0,)), ((), ())), preferred_element_type=F32))
    state = r_ref[...]
    for n in range(nch - 1, -1, -1):
        st_ref[n] = state
        state = gcb * state + bd * kvs[n]
    r_ref[...] = state


def _ret_main_kernel(rq_ref, rk_ref, rv_ref, rg_ref, c_ref, s_ref, t_ref, st_ref, j_ref,
                     o_ref, sf_ref):
    C = RET_CHUNK
    nch = rq_ref.shape[0] // C

    @pl.when(pl.program_id(2) == 0)
    def _():
        sf_ref[...] = jnp.zeros_like(sf_ref)

    lane = lax.broadcasted_iota(jnp.int32, (1, LANES), 1)
    lo64 = lane < HEAD_DIM
    j_bf = j_ref[...]
    chunks = [slice(n * C, (n + 1) * C) for n in range(nch)]
    nt = (((1,), (1,)), ((), ()))
    tn = (((0,), (0,)), ((), ()))
    q, k, vb = [], [], []
    for sl in chunks:
        cs, sn = c_ref[sl, :], s_ref[sl, :]
        q.append(_rope(rq_ref[sl, :].astype(F32), cs, sn))
        k.append(_rope(rk_ref[sl, :].astype(F32), cs, sn) * (HEAD_DIM ** -0.5))
        vb.append(rv_ref[sl, :])
    scores = []
    for n in range(nch):
        qb = q[n].astype(BF16)
        kb = k[n].astype(BF16)
        zero = jnp.zeros_like(qb)
        scores.append([lax.dot_general(jnp.where(lo64, qb, zero), kb, nt,
                                       preferred_element_type=F32),
                       lax.dot_general(jnp.where(lo64, zero, qb), kb, nt,
                                       preferred_element_type=F32)])
    kvs = [lax.dot_general((k[n] * t_ref[T_WKF]).astype(BF16), vb[n], tn,
                           preferred_element_type=F32) for n in range(nch)]
    cross_b = [jnp.dot((q[n] * t_ref[T_WQB]).astype(BF16), st_ref[n].astype(BF16),
                       preferred_element_type=F32) for n in range(nch)]
    intra = []
    for n in range(nch):
        parts = [jnp.dot((scores[n][hp] * t_ref[T_D0 + hp]).astype(BF16), vb[n],
                         preferred_element_type=F32) for hp in range(2)]
        intra.append(jnp.where(lo64, parts[0], parts[1]))
    state = sf_ref[...]
    states = []
    for n in range(nch):
        states.append(state.astype(BF16))
        state = t_ref[T_GCF] * state + t_ref[T_BD] * kvs[n]
    sf_ref[...] = state
    outs = [intra[n] + cross_b[n]
            + jnp.dot((q[n] * t_ref[T_WQF]).astype(BF16), states[n], preferred_element_type=F32)
            for n in range(nch)]
    mean_sq = [_head_mean_sq(o, j_bf) for o in outs]
    for n, sl in enumerate(chunks):
        ret = outs[n] * lax.rsqrt(mean_sq[n] + NORM_EPS)
        g = rg_ref[sl, :].astype(F32)
        o_ref[sl, :] = (ret * (g * _sigmoid(g))).astype(o_ref.dtype)


def _retention(p3, cos_t, sin_t, tables, j_bf):
    B, L, _ = p3.shape
    C = RET_CHUNK
    tb = min(4096, L)
    nblk = L // tb
    nch = tb // C
    nw = RET_W // LANES
    col = lambda piece: (lambda b, j, s: (b, s, nw * piece + j))
    colr = lambda piece: (lambda b, j, s: (b, nblk - 1 - s, nw * piece + j))
    seq = lambda fn: pl.BlockSpec((None, tb, LANES), fn)
    tab_spec = pl.BlockSpec((None, 9, C, LANES), lambda b, j, s: (j, 0, 0, 0))
    states = pl.pallas_call(
        _ret_state_kernel,
        grid=(B, nw, nblk),
        in_specs=[seq(colr(1)), seq(colr(2)),
                  pl.BlockSpec((tb, LANES), lambda b, j, s: (nblk - 1 - s, 0)),
                  pl.BlockSpec((tb, LANES), lambda b, j, s: (nblk - 1 - s, 0)),
                  tab_spec],
        out_specs=pl.BlockSpec((None, None, nch, LANES, LANES),
                               lambda b, j, s: (b, j, nblk - 1 - s, 0, 0)),
        out_shape=jax.ShapeDtypeStruct((B, nw, L // C, LANES, LANES), F32),
        scratch_shapes=[pltpu.VMEM((LANES, LANES), F32)],
        compiler_params=_cparams(("parallel", "parallel", "arbitrary"), 32),
        name="retention_state",
    )(p3, p3, cos_t, sin_t, tables)
    return pl.pallas_call(
        _ret_main_kernel,
        grid=(B, nw, nblk),
        in_specs=[seq(col(0)), seq(col(1)), seq(col(2)), seq(col(3)),
                  pl.BlockSpec((tb, LANES), lambda b, j, s: (s, 0)),
                  pl.BlockSpec((tb, LANES), lambda b, j, s: (s, 0)),
                  tab_spec,
                  pl.BlockSpec((None, None, nch, LANES, LANES), lambda b, j, s: (b, j, s, 0, 0)),
                  pl.BlockSpec((LANES, LANES), lambda b, j, s: (0, 0))],
        out_specs=pl.BlockSpec((None, tb, LANES), lambda b, j, s: (b, s, j)),
        out_shape=jax.ShapeDtypeStruct((B, L, RET_W), BF16),
        scratch_shapes=[pltpu.VMEM((LANES, LANES), F32)],
        compiler_params=_cparams(("parallel", "parallel", "arbitrary"), 32),
        name="retention_main",
    )(p3, p3, p3, p3, cos_t, sin_t, tables, states, j_bf)


def _merge_kernel(oa_ref, ob_ref, oc_ref, od_ref, g0_ref, g1_ref, g2_ref, g3_ref,
                  wb_ref, wo_ref, x_ref, gn_ref, h_ref):
    tm = x_ref.shape[0]
    nparts = 2 if tm % (2 * HALO) == 0 else 1
    parts = [slice(i * tm // nparts, (i + 1) * tm // nparts) for i in range(nparts)]
    branches = ((oa_ref, g0_ref), (ob_ref, g1_ref), (oc_ref, g2_ref), (od_ref, g3_ref))
    proj = [[jnp.dot(br[rows, :].astype(BF16), wb_ref[n], preferred_element_type=F32)
             for n, (br, _) in enumerate(branches)] for rows in parts]
    merged = []
    for pi, rows in enumerate(parts):
        acc = None
        for n, (_, gr) in enumerate(branches):
            term = gr[rows, :].astype(F32) * proj[pi][n]
            acc = term if acc is None else acc + term
        merged.append(acc.astype(BF16))
    ys = [jnp.dot(m, wo_ref[...], preferred_element_type=F32) for m in merged]
    for rows, y in zip(parts, ys):
        ms = jnp.mean(y * y, axis=-1, keepdims=True)
        h_ref[rows, :] = x_ref[rows, :] + y * lax.rsqrt(ms + NORM_EPS) * gn_ref[...]


def _merge(oa, ob, oc, od, p2, wb_bf, wo_bf, x2, gn):
    T = x2.shape[0]
    tm = min(1024, T)
    br = pl.BlockSpec((tm, BRANCH_W), lambda i: (i, 0))
    gate = lambda n: pl.BlockSpec((tm, D_MODEL), lambda i, n=n: (i, n))
    row = pl.BlockSpec((tm, D_MODEL), lambda i: (i, 0))
    return pl.pallas_call(
        _merge_kernel,
        grid=(T // tm,),
        in_specs=[br, br, br, br, gate(0), gate(1), gate(2), gate(3),
                  pl.BlockSpec((N_BRANCH, BRANCH_W, D_MODEL), lambda i: (0, 0, 0)),
                  pl.BlockSpec((D_MODEL, D_MODEL), lambda i: (0, 0)),
                  row, pl.BlockSpec((1, D_MODEL), lambda i: (0, 0))],
        out_specs=row,
        out_shape=jax.ShapeDtypeStruct((T, D_MODEL), F32),
        compiler_params=_cparams(("parallel",), 48),
        name="merge_out_proj",
    )(oa, ob, oc, od, p2, p2, p2, p2, wb_bf, wo_bf, x2, gn)


def _ffn_kernel(h_ref, g2_ref, wg_ref, wu_ref, wd_ref, g3_ref, o_ref):
    tm = h_ref.shape[0]
    halves = [slice(0, tm // 2), slice(tm // 2, tm)] if tm % (2 * HALO) == 0 else [slice(0, tm)]
    hns = []
    for rows in halves:
        h = h_ref[rows, :]
        ms = jnp.mean(h * h, axis=-1, keepdims=True)
        hns.append((h * lax.rsqrt(ms + NORM_EPS) * g2_ref[...]).astype(BF16))
    fs = [None] * len(halves)
    for lo, hi in _col_chunks(D_FF, 2):
        gus = [(jnp.dot(hn, wg_ref[:, lo:hi], preferred_element_type=F32),
                jnp.dot(hn, wu_ref[:, lo:hi], preferred_element_type=F32)) for hn in hns]
        acts = [(g * _sigmoid(g) * u).astype(BF16) for g, u in gus]
        for i, a in enumerate(acts):
            part = jnp.dot(a, wd_ref[lo:hi, :], preferred_element_type=F32)
            fs[i] = part if fs[i] is None else fs[i] + part
    for rows, f in zip(halves, fs):
        ms = jnp.mean(f * f, axis=-1, keepdims=True)
        o_ref[rows, :] = h_ref[rows, :] + f * lax.rsqrt(ms + NORM_EPS) * g3_ref[...]


def _ffn(h2, g2, wg_bf, wu_bf, wd_bf, g3):
    T = h2.shape[0]
    tm = min(512, T)
    row = pl.BlockSpec((tm, D_MODEL), lambda i: (i, 0))
    vec = pl.BlockSpec((1, D_MODEL), lambda i: (0, 0))
    return pl.pallas_call(
        _ffn_kernel,
        grid=(T // tm,),
        in_specs=[row, vec,
                  pl.BlockSpec((D_MODEL, D_FF), lambda i: (0, 0)),
                  pl.BlockSpec((D_MODEL, D_FF), lambda i: (0, 0)),
                  pl.BlockSpec((D_FF, D_MODEL), lambda i: (0, 0)),
                  vec],
        out_specs=row,
        out_shape=jax.ShapeDtypeStruct((T, D_MODEL), F32),
        compiler_params=_cparams(("parallel",), 48),
        name="ffn",
    )(h2, g2, wg_bf, wu_bf, wd_bf, g3)


def _rope_tables(L):
    rows = L // GRID_W
    r = jnp.repeat(jnp.arange(rows, dtype=F32), GRID_W)
    c = jnp.tile(jnp.arange(GRID_W, dtype=F32), rows)
    inv = ROPE_BASE ** (-jnp.arange(ROPE_FREQS, dtype=F32) / ROPE_FREQS)
    ar = r[:, None] * inv
    ac = c[:, None] * inv
    cos64 = jnp.concatenate([jnp.cos(ar), jnp.cos(ar), jnp.cos(ac), jnp.cos(ac)], axis=1)
    sin64 = jnp.concatenate([-jnp.sin(ar), jnp.sin(ar), -jnp.sin(ac), jnp.sin(ac)], axis=1)
    return jnp.tile(cos64, (1, 2)), jnp.tile(sin64, (1, 2))


def _filter_features(L):
    t = np.linspace(0.0, 1.0, L)[:, None]
    f = np.linspace(1e-4, HY_BANDS - 1, HY_BANDS)
    ang = (2.0 * math.pi / L) * np.arange(L)[:, None] * f[None, :]
    z = np.concatenate([t, np.cos(ang), -np.sin(ang)], axis=-1)
    z = np.pad(z, ((0, 0), (0, HY_FILTER_HIDDEN - HY_EMB)))
    idx = np.concatenate([np.arange(L), [0], np.arange(L - 1, 0, -1)])
    return jnp.asarray(z[idx].astype(np.float32))


def _head_mean_matrix():
    i = np.arange(LANES)
    j = ((i[:, None] // HEAD_DIM) == (i[None, :] // HEAD_DIM)).astype(np.float32) / HEAD_DIM
    return jnp.asarray(j.astype(BF16))


def _halve_gate_columns(w_in_bf):
    scale = jnp.where(jnp.arange(IN_COLS) >= GATE_OFF, 0.5, 1.0).astype(BF16)
    return w_in_bf * scale


def _trunk(x, wts, fft_dims=None):
    B, L, _ = x.shape
    T = B * L
    n1, n2 = fft_dims or _fft_dims(L)
    cst = _fft_consts(n1, n2, n1 // 2)
    cos_t, sin_t = _rope_tables(L)
    z2 = _filter_features(L)
    deltas = jnp.abs(jnp.linspace(math.log(HY_TARGET) / HY_SLOW_DECAY,
                                  math.log(HY_TARGET) / HY_FAST_DECAY, HY_WIDTH, dtype=F32))[None, :]
    j_bf = _head_mean_matrix()
    depth = wts["w_in"].shape[0]
    x2 = x.reshape(T, D_MODEL)
    for l in range(depth):
        ng = wts["norm_gains"][l]
        w1p = jnp.pad(wts["hy_w1"][l], ((0, HY_FILTER_HIDDEN - HY_EMB), (0, 0)))
        k2raw, sumsq = _hy_filter_raw(z2, w1p, wts["hy_b1"][l][None, :], wts["hy_w2"][l],
                                      wts["hy_b2"][l][None, :], wts["hy_w3"][l],
                                      wts["hy_freq"][l], deltas, L)
        kf_re, kf_im = _filter_fft(k2raw, sumsq, cst, n1, n2)

        qg = jnp.tile(wts["qk_norm"][l, 0], 2)[None, :]
        kg = jnp.tile(wts["qk_norm"][l, 1], 2)[None, :]
        qt, k, vt, p_ret, p_gate, z, x0u, out_d = _in_proj(
            x2, ng[0][None, :], wts["w_in_bf"][l], wts["hy_conv_w"][l],
            wts["hy_conv_b"][l][None, :], wts["sc_conv_w"][l], cos_t, sin_t, qg, kg, j_bf, L)
        out_a = _flash(qt, k, vt)

        out_b = _hy_conv(z.reshape(B, L, HY_WIDTH), x0u.reshape(B, L, HY_WIDTH),
                         wts["hy_bias"][l][None, :], kf_re, kf_im, cst, n1, n2)

        tables = _ret_tables(wts["ret_decay_exp"][l])
        out_c = _retention(p_ret.reshape(B, L, 4 * RET_W), cos_t, sin_t, tables, j_bf)

        h2 = _merge(out_a.reshape(T, BRANCH_W), out_b.reshape(T, BRANCH_W),
                    out_c.reshape(T, BRANCH_W), out_d,
                    p_gate, wts["w_branch_bf"][l], wts["w_out_bf"][l], x2, ng[1][None, :])
        x2 = _ffn(h2, ng[2][None, :], wts["w_gate_bf"][l], wts["w_up_bf"][l],
                  wts["w_ffn_out_bf"][l], ng[3][None, :])
    return x2.reshape(B, L, D_MODEL)


def kernel(x_prompt, x_sample, norm_gains, w_in, qk_norm, hy_conv_w, hy_conv_b, hy_w1, hy_b1, hy_w2,
           hy_b2, hy_w3, hy_freq, hy_bias, ret_decay_exp, sc_conv_w, w_branch, w_out, w_ffn_in,
           w_ffn_out):
    wts = dict(norm_gains=norm_gains, w_in=w_in, qk_norm=qk_norm, hy_conv_w=hy_conv_w,
               hy_conv_b=hy_conv_b, hy_w1=hy_w1, hy_b1=hy_b1, hy_w2=hy_w2, hy_b2=hy_b2,
               hy_w3=hy_w3, hy_freq=hy_freq, hy_bias=hy_bias, ret_decay_exp=ret_decay_exp,
               sc_conv_w=sc_conv_w,
               w_in_bf=_halve_gate_columns(w_in.astype(BF16)), w_branch_bf=w_branch.astype(BF16),
               w_out_bf=w_out.astype(BF16), w_gate_bf=w_ffn_in[..., :D_FF].astype(BF16),
               w_up_bf=w_ffn_in[..., D_FF:].astype(BF16), w_ffn_out_bf=w_ffn_out.astype(BF16))
    return _trunk(x_prompt, wts), _trunk(x_sample, wts)
```

```python
import functools
import math

import numpy as np
import jax
import jax.numpy as jnp
from jax import lax
from jax.experimental import pallas as pl
from jax.experimental.pallas import tpu as pltpu

F32 = jnp.float32
BF16 = jnp.bfloat16

D_MODEL = 1024
GRID_W = 64
N_BRANCH = 4
BRANCH_W = 256
HEAD_DIM = 64
ATT_Q_HEADS = 4
ATT_KV_HEADS = 2
ROPE_BASE = 10000.0
ROPE_FREQS = HEAD_DIM // 4
HY_WIDTH = BRANCH_W
HY_EMB = 33
HY_BANDS = (HY_EMB - 1) // 2
HY_FILTER_HIDDEN = 64
HY_FAST_DECAY = 0.3
HY_SLOW_DECAY = 1.5
HY_TARGET = 1e-2
RET_HEADS = 4
RET_W = RET_HEADS * HEAD_DIM
RET_CHUNK = 128
SC_WIDTH = BRANCH_W
D_FF = 2816
NORM_EPS = 1e-6

ATT_Q_W = ATT_Q_HEADS * HEAD_DIM
ATT_KV_W = ATT_KV_HEADS * HEAD_DIM
A_K_OFF = ATT_Q_W
A_V_OFF = A_K_OFF + ATT_KV_W
HY_OFF = A_V_OFF + ATT_KV_W
RET_OFF = HY_OFF + 3 * HY_WIDTH
SC_OFF = RET_OFF + 4 * RET_W
GATE_OFF = SC_OFF + 3 * SC_WIDTH
IN_COLS = GATE_OFF + N_BRANCH * D_MODEL

LANES = 128
SUBLANES = 8
MXU_W = 256
HALO = 2 * SUBLANES
ATT_KB = 512
ATT_TQ = 256
VT_ROWS = HEAD_DIM + HALO
Q_SCALE = HEAD_DIM ** -0.5 * math.log2(math.e)
FFT_PASSES_FILTER = 3
FFT_PASSES_DATA = 1
FFT_CH = 8
FFT_SETS = 1
MIB = 1 << 20


def _cparams(sem, vmem_mib):
    return pltpu.CompilerParams(dimension_semantics=sem, vmem_limit_bytes=vmem_mib * MIB)


def _sigmoid(x):
    return 0.5 * jnp.tanh(0.5 * x) + 0.5


def _split(x):
    hi = x.astype(BF16)
    lo = (x - hi.astype(F32)).astype(BF16)
    return hi, lo


def _np_split(a64):
    a32 = np.asarray(a64, np.float32)
    hi = a32.astype(BF16)
    lo = (a32 - hi.astype(np.float32)).astype(BF16)
    return jnp.asarray(hi), jnp.asarray(lo)


def _mm_const(a_hi, a_lo, x, passes):
    if passes == 1:
        return jnp.dot(a_hi, x.astype(BF16), preferred_element_type=F32)
    xh, xl = _split(x)
    out = jnp.dot(a_hi, xh, preferred_element_type=F32)
    if passes >= 3:
        out = out + jnp.dot(a_lo, xh, preferred_element_type=F32)
        out = out + jnp.dot(a_hi, xl, preferred_element_type=F32)
    return out


def _mm3(a, b):
    ah, al = _split(a)
    bh, bl = _split(b)
    out = jnp.dot(ah, bh, preferred_element_type=F32)
    out = out + jnp.dot(al, bh, preferred_element_type=F32)
    return out + jnp.dot(ah, bl, preferred_element_type=F32)


def _head_mean_sq(x, j_bf):
    hi, lo = _split(x * x)
    return (jnp.dot(hi, j_bf, preferred_element_type=F32)
            + jnp.dot(lo, j_bf, preferred_element_type=F32))


def _rope(x, c, s):
    lane = lax.broadcasted_iota(jnp.int32, (1, LANES), 1)
    is_b = (lane & 16) != 0
    partner = jnp.where(is_b, pltpu.roll(x, 16, 1), pltpu.roll(x, LANES - 16, 1))
    return x * c + partner * s


def _col_chunks(n, parts):
    tiles = n // MXU_W
    bounds = [MXU_W * (tiles * i // parts) for i in range(parts + 1)]
    return list(zip(bounds[:-1], bounds[1:]))


def _conv3_rows(main, before, after, w):
    tb = main.shape[0]
    w0, w1, w2 = w[0:1, :], w[1:2, :], w[2:3, :]
    body = pltpu.roll(main, 1, 0) * w0 + main * w1 + pltpu.roll(main, tb - 1, 0) * w2
    row_first = before * w0 + main[0:1, :] * w1 + main[1:2, :] * w2
    row_last = main[tb - 2:tb - 1, :] * w0 + main[tb - 1:tb, :] * w1 + after * w2
    row = lax.broadcasted_iota(jnp.int32, (SUBLANES, 1), 0)
    head = jnp.where(row == 0, row_first, body[:SUBLANES])
    tail = jnp.where(row == SUBLANES - 1, row_last, body[tb - SUBLANES:])
    return jnp.concatenate([head, body[SUBLANES:tb - SUBLANES], tail], axis=0)


def _in_proj_kernel(x_ref, xp_ref, xn_ref, g_ref, w_ref, hcw_ref, hcb_ref, scw_ref,
                    c_ref, s_ref, qg_ref, kg_ref, j_ref,
                    qt_ref, k_ref, vt_ref, pr_ref, pg_ref, z_ref, x0_ref, od_ref, *,
                    tiles_per_seq):
    i = pl.program_id(0)
    first = (i % tiles_per_seq) == 0
    last = (i % tiles_per_seq) == tiles_per_seq - 1
    g = g_ref[...]

    def normed(x):
        ms = jnp.mean(x * x, axis=-1, keepdims=True)
        return (x * lax.rsqrt(ms + NORM_EPS) * g).astype(BF16)

    dot = lambda a, lo, hi: jnp.dot(a, w_ref[:, lo:hi], preferred_element_type=F32)
    xn = normed(x_ref[...])
    halo = normed(jnp.concatenate([xp_ref[...], xn_ref[...]], axis=0))
    edge = lambda h: (jnp.where(first, 0.0, h[SUBLANES - 1:SUBLANES, :]),
                      jnp.where(last, 0.0, h[SUBLANES:SUBLANES + 1, :]))

    head = dot(xn, 0, RET_OFF)
    hy_halo = dot(halo, HY_OFF, RET_OFF)
    sc = dot(xn, SC_OFF, GATE_OFF)
    sc_halo = dot(halo, SC_OFF + SC_WIDTH, GATE_OFF)
    pr_ref[...] = dot(xn, RET_OFF, SC_OFF).astype(pr_ref.dtype)
    _attn_prep_tile(head[:, :HY_OFF], c_ref, s_ref, qg_ref, kg_ref, j_ref, qt_ref, k_ref, vt_ref)

    u = _conv3_rows(head[:, HY_OFF:], *edge(hy_halo), hcw_ref[...]) + hcb_ref[...]
    x0_ref[...] = u[:, :HY_WIDTH]
    z_ref[...] = u[:, 2 * HY_WIDTH:] * u[:, HY_WIDTH:2 * HY_WIDTH]
    m = sc[:, SC_WIDTH:2 * SC_WIDTH] * sc[:, 2 * SC_WIDTH:]
    mh = sc_halo[:, :SC_WIDTH] * sc_halo[:, SC_WIDTH:]
    od_ref[...] = (sc[:, :SC_WIDTH] * _conv3_rows(m, *edge(mh), scw_ref[...])).astype(od_ref.dtype)

    for lo, hi in _col_chunks(N_BRANCH * D_MODEL, 2):
        g_half = dot(xn, GATE_OFF + lo, GATE_OFF + hi).astype(pg_ref.dtype)
        pg_ref[:, lo:hi] = 0.5 * jnp.tanh(g_half) + 0.5


def _in_proj(x2, g, w_bf, hcw, hcb, scw, cos_t, sin_t, qg, kg, j_bf, L):
    T = x2.shape[0]
    B = T // L
    tm = min(ATT_KB, L)
    tq = min(ATT_TQ, L)
    tps = L // tm
    r8 = tm // SUBLANES
    row = lambda w: pl.BlockSpec((tm, w), lambda i: (i, 0))
    full = lambda a: pl.BlockSpec(a.shape, lambda i: (0,) * a.ndim)
    tab = pl.BlockSpec((tm, LANES), lambda i: (i % tps, 0))
    sds = lambda w, dt: jax.ShapeDtypeStruct((T, w), dt)
    return pl.pallas_call(
        functools.partial(_in_proj_kernel, tiles_per_seq=tps),
        grid=(T // tm,),
        in_specs=[row(D_MODEL),
                  pl.BlockSpec((SUBLANES, D_MODEL), lambda i: (jnp.maximum(i * r8 - 1, 0), 0)),
                  pl.BlockSpec((SUBLANES, D_MODEL),
                               lambda i: (jnp.minimum((i + 1) * r8, T // SUBLANES - 1), 0)),
                  full(g), full(w_bf), full(hcw), full(hcb), full(scw),
                  tab, tab, full(qg), full(kg), full(j_bf)],
        out_specs=[pl.BlockSpec((None, ATT_Q_HEADS, tm // tq, LANES, tq),
                                lambda i: (i // tps, 0, i % tps, 0, 0)),
                   pl.BlockSpec((None, tm, LANES), lambda i: (i // tps, i % tps, 0)),
                   pl.BlockSpec((None, tm // ATT_KB, ATT_KV_HEADS * VT_ROWS, ATT_KB),
                                lambda i: (i // tps, i % tps, 0, 0)),
                   row(RET_W * 4), row(N_BRANCH * D_MODEL),
                   row(HY_WIDTH), row(HY_WIDTH), row(SC_WIDTH)],
        out_shape=[jax.ShapeDtypeStruct((B, ATT_Q_HEADS, L // tq, LANES, tq), BF16),
                   jax.ShapeDtypeStruct((B, L, LANES), BF16),
                   jax.ShapeDtypeStruct((B, L // ATT_KB, ATT_KV_HEADS * VT_ROWS, ATT_KB), BF16),
                   sds(RET_W * 4, BF16), sds(N_BRANCH * D_MODEL, BF16),
                   sds(HY_WIDTH, F32), sds(HY_WIDTH, F32), sds(SC_WIDTH, BF16)],
        compiler_params=_cparams(("parallel",), 48),
        name="in_proj",
    )(x2, x2, x2, g, w_bf, hcw, hcb, scw, cos_t, sin_t, qg, kg, j_bf)


def _attn_prep_tile(qkv, c_ref, s_ref, qg_ref, kg_ref, j_ref, qt_ref, k_ref, vt_ref):
    c = c_ref[...]
    s = s_ref[...]
    j_bf = j_ref[...]
    tm = qkv.shape[0]
    tq = qt_ref.shape[-1]
    zeros = jnp.zeros((HEAD_DIM, tm), BF16)
    for blk in range(2):
        q = qkv[:, LANES * blk:LANES * (blk + 1)]
        qn = q * lax.rsqrt(_head_mean_sq(q, j_bf) + NORM_EPS) * qg_ref[...]
        qt = (_rope(qn, c, s) * Q_SCALE).astype(BF16).T
        lo, hi = qt[:HEAD_DIM], qt[HEAD_DIM:]
        if blk == 0:
            heads = (jnp.concatenate([lo, zeros], axis=0), jnp.concatenate([hi, zeros], axis=0))
        else:
            heads = (jnp.concatenate([zeros, lo], axis=0), jnp.concatenate([zeros, hi], axis=0))
        for hp, qh in enumerate(heads):
            for t in range(tm // tq):
                qt_ref[2 * blk + hp, t] = qh[:, t * tq:(t + 1) * tq]
    k = qkv[:, A_K_OFF:A_K_OFF + LANES]
    kn = k * lax.rsqrt(_head_mean_sq(k, j_bf) + NORM_EPS) * kg_ref[...]
    k_ref[...] = _rope(kn, c, s).astype(BF16)
    v = qkv[:, A_V_OFF:A_V_OFF + LANES].astype(BF16)
    ones = jnp.ones((VT_ROWS - HEAD_DIM, ATT_KB), BF16)
    for t in range(tm // ATT_KB):
        vt = v[t * ATT_KB:(t + 1) * ATT_KB, :].T
        vt_ref[t] = jnp.concatenate([vt[:HEAD_DIM], ones, vt[HEAD_DIM:], ones], axis=0)


def _flash_kernel(qt_ref, k_ref, vt_ref, o_ref, sa_ref, ma_ref, sb_ref, mb_ref, *,
                  ngrp, nsub, ntile):
    tq = ATT_TQ
    group = ATT_Q_HEADS // ATT_KV_HEADS

    def produce(slot, qi, g, t, h):
        s_ref, mx_ref = slot
        st = pl.multiple_of((g * nsub + t) * ATT_KB, ATT_KB)
        s = jnp.dot(k_ref[pl.ds(st, ATT_KB), :], qt_ref[h, qi],
                    preferred_element_type=F32)
        s_ref[t * ATT_Q_HEADS + h] = s
        mx_ref[t * ATT_Q_HEADS + h] = jnp.max(s, axis=0, keepdims=True)

    def consume(slot, g, t, h, state):
        s_ref, mx_ref = slot
        m, acc = state
        m_new = jnp.maximum(m, mx_ref[t * ATT_Q_HEADS + h])
        alpha = jnp.exp2(m - m_new)
        p = jnp.exp2((s_ref[t * ATT_Q_HEADS + h] - m_new).astype(BF16))
        j = h // group
        vt = vt_ref[g * nsub + t, VT_ROWS * j:VT_ROWS * (j + 1), :]
        return m_new, alpha * acc + jnp.dot(vt, p, preferred_element_type=F32)

    def step(slot_in, g_in, slot_out, qi_out, g_out, carry):
        carry = list(carry)
        for t in range(nsub):
            for h in range(ATT_Q_HEADS):
                if slot_out is not None:
                    produce(slot_out, qi_out, g_out, t, h)
                if slot_in is not None:
                    carry[h] = consume(slot_in, g_in, t, h, carry[h])
        return tuple(carry)

    def fresh():
        return tuple((jnp.full((1, tq), -jnp.inf, F32), jnp.zeros((VT_ROWS, tq), F32))
                     for _ in range(ATT_Q_HEADS))

    def finalize(qi, carry):
        outs = [acc[:HEAD_DIM] / acc[HEAD_DIM:HEAD_DIM + 1] for _, acc in carry]
        rows = pl.ds(pl.multiple_of(qi * tq, tq), tq)
        o_ref[rows, :] = jnp.concatenate(outs, axis=0).T.astype(o_ref.dtype)

    slot_a = (sa_ref, ma_ref)
    slot_b = (sb_ref, mb_ref)

    if ngrp == 1:
        def tile(qi, _):
            step(None, None, slot_a, qi, 0, ())
            finalize(qi, step(slot_a, 0, None, None, None, fresh()))
            return 0
    else:
        step(None, None, slot_a, 0, 0, ())

        def tile(qi, _):
            def body(i, carry):
                g = 2 * i
                carry = step(slot_a, g, slot_b, qi, g + 1, carry)
                return step(slot_b, g + 1, slot_a, qi, g + 2, carry)

            carry = lax.fori_loop(0, ngrp // 2 - 1, body, fresh())
            carry = step(slot_a, ngrp - 2, slot_b, qi, ngrp - 1, carry)
            carry = step(slot_b, ngrp - 1, slot_a, jnp.minimum(qi + 1, ntile - 1), 0, carry)
            finalize(qi, carry)
            return 0

    lax.fori_loop(0, ntile, tile, 0)


def _flash(qt, k, vt):
    B, _, nqt, _, tq = qt.shape
    L = nqt * tq
    ntile = math.gcd(nqt, 8)
    nkb = L // ATT_KB
    nsub = 2 if nkb % 4 == 0 and nkb >= 8 else 1
    ngrp = L // (ATT_KB * nsub)
    assert ngrp == 1 or ngrp % 2 == 0
    s_scratch = pltpu.VMEM((nsub * ATT_Q_HEADS, ATT_KB, tq), F32)
    m_scratch = pltpu.VMEM((nsub * ATT_Q_HEADS, 1, tq), F32)
    return pl.pallas_call(
        functools.partial(_flash_kernel, ngrp=ngrp, nsub=nsub, ntile=ntile),
        grid=(B, nqt // ntile),
        in_specs=[pl.BlockSpec((None, ATT_Q_HEADS, ntile, LANES, tq),
                               lambda b, i: (b, 0, i, 0, 0)),
                  pl.BlockSpec((None, L, LANES), lambda b, i: (b, 0, 0)),
                  pl.BlockSpec((None, L // ATT_KB, ATT_KV_HEADS * VT_ROWS, ATT_KB),
                               lambda b, i: (b, 0, 0, 0))],
        out_specs=pl.BlockSpec((None, ntile * tq, ATT_Q_W), lambda b, i: (b, i, 0)),
        out_shape=jax.ShapeDtypeStruct((B, L, ATT_Q_W), BF16),
        scratch_shapes=[s_scratch, m_scratch, s_scratch, m_scratch],
        compiler_params=_cparams(("parallel", "arbitrary"), 48),
        name="flash_attn",
    )(qt, k, vt)


def _hy_filter_kernel(z_ref, w1_ref, b1_ref, w2_ref, b2_ref, w3_ref, fr_ref, dl_ref,
                      k_ref, ss_ref):
    step = pl.program_id(0)
    tr = z_ref.shape[0]
    H = HY_FILTER_HIDDEN
    z = z_ref[...]
    h = jnp.sin(fr_ref[0:1, :] * (_mm3(z, w1_ref[...]) + b1_ref[...]))
    h = jnp.sin(fr_ref[1:2, :] * (_mm3(h, w2_ref[...]) + b2_ref[...]))
    h3 = _mm3(h, w3_ref[...])
    fwd = h3[:, :HY_WIDTH] * jnp.exp(-z[:, 0:1] * dl_ref[...])
    bwd = h3[:, HY_WIDTH:] * jnp.exp(-z[:, H:H + 1] * dl_ref[...])
    row = step * tr + lax.broadcasted_iota(jnp.int32, (tr, 1), 0)
    bwd = jnp.where(row == 0, 0.0, bwd)
    k_ref[0] = fwd
    k_ref[1] = bwd

    @pl.when(step == 0)
    def _():
        ss_ref[...] = jnp.zeros_like(ss_ref)

    ss_ref[...] += jnp.sum(fwd * fwd + bwd * bwd, axis=0, keepdims=True)


def _block_diag2(a, b):
    za = jnp.zeros((a.shape[0], b.shape[1]), a.dtype)
    zb = jnp.zeros((b.shape[0], a.shape[1]), a.dtype)
    return jnp.concatenate([jnp.concatenate([a, za], axis=1),
                            jnp.concatenate([zb, b], axis=1)], axis=0)


def _hy_filter_raw(z2, w1p, b1, w2, b2, w3, freq, deltas, L):
    tr = min(512, L)
    H2 = 2 * HY_FILTER_HIDDEN
    two = lambda v: jnp.tile(v, (1, 2))
    args = (z2, _block_diag2(w1p, w1p), two(b1), _block_diag2(w2, w2), two(b2),
            _block_diag2(w3[:, :HY_WIDTH], w3[:, HY_WIDTH:]), two(freq), deltas)
    full = lambda a: pl.BlockSpec(a.shape, lambda i: (0, 0))
    k2, sumsq = pl.pallas_call(
        _hy_filter_kernel,
        grid=(L // tr,),
        in_specs=[pl.BlockSpec((tr, H2), lambda i: (i, 0))] + [full(a) for a in args[1:]],
        out_specs=[pl.BlockSpec((2, tr, HY_WIDTH), lambda i: (0, i, 0)),
                   pl.BlockSpec((1, HY_WIDTH), lambda i: (0, 0))],
        out_shape=[jax.ShapeDtypeStruct((2, L, HY_WIDTH), F32),
                   jax.ShapeDtypeStruct((1, HY_WIDTH), F32)],
        compiler_params=_cparams(("arbitrary",), 32),
        name="hyena_filter",
    )(*args)
    return k2.reshape(2 * L, HY_WIDTH), sumsq


def _fft_dims(L):
    n = 2 * L
    n2 = 128 if n >= 32 * 128 else 64
    return n // n2, n2


def _fft_consts(n1, n2, r):
    n = n1 * n2
    a1 = 2.0 * np.pi * np.outer(np.arange(n1), np.arange(n1)) / n1
    c1, s1 = np.cos(a1), np.sin(a1)
    a2 = 2.0 * np.pi * np.outer(np.arange(n2), np.arange(n2)) / n2
    c2, s2 = np.cos(a2), np.sin(a2)
    f1 = np.block([[c1[:, :r], s1[:, :r]], [-s1[:, :r], c1[:, :r]]])
    f1_full = np.concatenate([c1, -s1], axis=0)
    m2 = np.block([[c2, s2], [-s2, c2]])
    m2i = np.block([[c2, -s2], [s2, c2]])
    fi = np.block([[c1[:r, :], -s1[:r, :]], [s1[:r, :], c1[:r, :]]])
    aw = 2.0 * np.pi * np.arange(n1) / n
    off = np.repeat(np.arange(FFT_CH), LANES)[None, :] * aw[:, None]
    step = np.broadcast_to((FFT_CH * aw)[:, None], (n1, LANES))
    f32 = lambda a: jnp.asarray(np.asarray(a, np.float32))
    return dict(f1=_np_split(f1), f1_full=_np_split(f1_full), m2=_np_split(m2),
                m2i=_np_split(m2i), fi=_np_split(fi),
                tw=(f32(np.cos(off)), f32(-np.sin(off)), f32(np.cos(step)), f32(-np.sin(step))))


def _cmul(a_re, a_im, b_re, b_im):
    return a_re * b_re - a_im * b_im, a_re * b_im + a_im * b_re


def _lane_tile(x, reps):
    return jnp.concatenate([x] * reps, axis=1)


def _lane_part(x, j):
    return x[:, LANES * j:LANES * (j + 1)]


def _chunk_twiddle(t_re, t_im, tw_refs):
    d_re, d_im, s_re, s_im = (ref[...] for ref in tw_refs)
    cur = _cmul(_lane_tile(t_re, FFT_CH), _lane_tile(t_im, FFT_CH), d_re, d_im)
    return cur, _cmul(t_re, t_im, s_re, s_im)


def _twiddle_init(n1):
    return jnp.ones((n1, LANES), F32), jnp.zeros((n1, LANES), F32)


def _fft_stage1(load_rows, a_ref, f_hi, f_lo, tw_refs, n1, n2, passes):
    def body(ci, carry):
        i0s, tws = [], []
        for u in range(FFT_SETS):
            tw, carry = _chunk_twiddle(*carry, tw_refs)
            tws.append(tw)
            i0s.append((ci * FFT_SETS + u) * FFT_CH)
        prods = [_mm_const(f_hi, f_lo, jnp.concatenate(
            [load_rows(i0 + j) for j in range(FFT_CH)], axis=1), passes)
            for i0 in i0s]
        for i0, (t_re, t_im), a in zip(i0s, tws, prods):
            o_re, o_im = _cmul(a[:n1], a[n1:], t_re, t_im)
            for j in range(FFT_CH):
                base = pl.multiple_of((i0 + j) * 2 * n1, 2 * n1)
                a_ref[pl.ds(base, n1), :] = _lane_part(o_re, j)
                a_ref[pl.ds(base + n1, n1), :] = _lane_part(o_im, j)
        return carry

    lax.fori_loop(0, n2 // (FFT_CH * FFT_SETS), body, _twiddle_init(n1))


def _load_spectrum_rows(a_ref, k1, n1, n2):
    ld = lambda off: a_ref[pl.ds(off, n2, stride=2 * n1), :]
    return jnp.concatenate(
        [jnp.concatenate([ld(k1 + j) for j in range(FFT_CH)], axis=1),
         jnp.concatenate([ld(n1 + k1 + j) for j in range(FFT_CH)], axis=1)], axis=0)


def _filter_fft_kernel(k_ref, ss_ref, f_hi_ref, f_lo_ref, m_hi_ref, m_lo_ref,
                       dre_ref, dim_ref, sre_ref, sim_ref, kr_ref, ki_ref, a_ref, *, n1, n2):
    c = pl.program_id(1)

    @pl.when(c == 0)
    def _():
        _fft_stage1(lambda i: k_ref[pl.ds(i, n1, stride=n2), :], a_ref,
                    f_hi_ref[...], f_lo_ref[...], (dre_ref, dim_ref, sre_ref, sim_ref),
                    n1, n2, FFT_PASSES_FILTER)

    scale = _lane_tile(lax.rsqrt(ss_ref[...] + NORM_EPS) * (1.0 / (n1 * n2)), FFT_CH)
    xin = _load_spectrum_rows(a_ref, c * FFT_CH, n1, n2)
    x = _mm_const(m_hi_ref[...], m_lo_ref[...], xin, FFT_PASSES_FILTER)
    kr_ref[...] = x[:n2] * scale
    ki_ref[...] = x[n2:] * scale


def _filter_fft(k2raw, sumsq, cst, n1, n2):
    n = n1 * n2
    nw = HY_WIDTH // LANES
    consts = [*cst["f1_full"], *cst["m2"], *cst["tw"]]
    full = lambda a: pl.BlockSpec(a.shape, lambda w, c: (0,) * a.ndim)
    out_spec = pl.BlockSpec((None, None, n2, FFT_CH * LANES), lambda w, c: (w, c, 0, 0))
    out_sh = jax.ShapeDtypeStruct((nw, n1 // FFT_CH, n2, FFT_CH * LANES), F32)
    return pl.pallas_call(
        functools.partial(_filter_fft_kernel, n1=n1, n2=n2),
        grid=(nw, n1 // FFT_CH),
        in_specs=[pl.BlockSpec((n, LANES), lambda w, c: (0, w)),
                  pl.BlockSpec((1, LANES), lambda w, c: (0, w))] + [full(a) for a in consts],
        out_specs=[out_spec, out_spec],
        out_shape=[out_sh, out_sh],
        scratch_shapes=[pltpu.VMEM((2 * n, LANES), F32)],
        compiler_params=_cparams(("parallel", "arbitrary"), 48),
        name="hyena_filter_fft",
    )(k2raw, sumsq, *consts)


def _hy_conv_kernel(z_ref, x0_ref, hb_ref, kr_ref, ki_ref, f_hi_ref, f_lo_ref,
                    m_hi_ref, m_lo_ref, mi_hi_ref, mi_lo_ref, fi_hi_ref, fi_lo_ref,
                    dre_ref, dim_ref, sre_ref, sim_ref, o_ref, a_ref, *, n1, n2):
    c = pl.program_id(2)
    r = n1 // 2
    tw_refs = (dre_ref, dim_ref, sre_ref, sim_ref)

    def load_pair(i):
        seq = pl.ds(i, r, stride=n2)
        return jnp.concatenate([z_ref[0, seq, :], z_ref[1, seq, :]], axis=0)

    @pl.when(c == 0)
    def _():
        _fft_stage1(load_pair, a_ref, f_hi_ref[...], f_lo_ref[...], tw_refs, n1, n2,
                    FFT_PASSES_DATA)

    k1s = [(c * FFT_SETS + u) * FFT_CH for u in range(FFT_SETS)]
    xs = [_mm_const(m_hi_ref[...], m_lo_ref[...], _load_spectrum_rows(a_ref, k1, n1, n2),
                    FFT_PASSES_DATA) for k1 in k1s]
    ys = [jnp.concatenate(_cmul(x[:n2], x[n2:], kr_ref[u], ki_ref[u]), axis=0)
          for u, x in enumerate(xs)]
    bqs = [_mm_const(mi_hi_ref[...], mi_lo_ref[...], y, FFT_PASSES_DATA) for y in ys]
    for k1, bq in zip(k1s, bqs):
        for j in range(FFT_CH):
            a_ref[pl.ds(k1 + j, n2, stride=2 * n1), :] = _lane_part(bq[:n2], j)
            a_ref[pl.ds(n1 + k1 + j, n2, stride=2 * n1), :] = _lane_part(bq[n2:], j)

    @pl.when(c == pl.num_programs(2) - 1)
    def _():
        hb = hb_ref[...]

        def body(ci, carry):
            i0s, bts = [], []
            for u in range(FFT_SETS):
                (t_re, t_im), carry = _chunk_twiddle(*carry, tw_refs)
                i0 = (ci * FFT_SETS + u) * FFT_CH
                rows = [pl.multiple_of((i0 + j) * 2 * n1, 2 * n1) for j in range(FFT_CH)]
                br = jnp.concatenate([a_ref[pl.ds(b0, n1), :] for b0 in rows], axis=1)
                bi = jnp.concatenate([a_ref[pl.ds(b0 + n1, n1), :] for b0 in rows], axis=1)
                bts.append(jnp.concatenate([br * t_re + bi * t_im, bi * t_re - br * t_im],
                                           axis=0))
                i0s.append(i0)
            ys = [_mm_const(fi_hi_ref[...], fi_lo_ref[...], bt, FFT_PASSES_DATA) for bt in bts]
            for i0, y in zip(i0s, ys):
                for j in range(FFT_CH):
                    seq = pl.ds(i0 + j, r, stride=n2)
                    for row in range(2):
                        conv = _lane_part(y[row * r:(row + 1) * r], j)
                        o_ref[row, seq, :] = (x0_ref[row, seq, :]
                                              * (conv + z_ref[row, seq, :] * hb)
                                              ).astype(o_ref.dtype)
            return carry

        lax.fori_loop(0, n2 // (FFT_CH * FFT_SETS), body, _twiddle_init(n1))


def _hy_conv(z, x0u, hbias, kf_re, kf_im, cst, n1, n2):
    B, L, _ = z.shape
    assert B % 2 == 0
    nw = HY_WIDTH // LANES
    seq = pl.BlockSpec((2, L, LANES), lambda b, w, c: (b, 0, w), pipeline_mode=pl.Buffered(1))
    seq_z = pl.BlockSpec((2, L, LANES), lambda b, w, c: (b, 0, w))
    kf_spec = pl.BlockSpec((None, FFT_SETS, n2, FFT_CH * LANES), lambda b, w, c: (w, c, 0, 0))
    full = lambda a: pl.BlockSpec(a.shape, lambda b, w, c: (0,) * a.ndim)
    consts = [*cst["f1"], *cst["m2"], *cst["m2i"], *cst["fi"], *cst["tw"]]
    return pl.pallas_call(
        functools.partial(_hy_conv_kernel, n1=n1, n2=n2),
        grid=(B // 2, nw, n1 // (FFT_CH * FFT_SETS)),
        in_specs=[seq_z, seq, pl.BlockSpec((1, LANES), lambda b, w, c: (0, w)), kf_spec, kf_spec]
                 + [full(a) for a in consts],
        out_specs=seq,
        out_shape=jax.ShapeDtypeStruct((B, L, HY_WIDTH), F32),
        scratch_shapes=[pltpu.VMEM((2 * n1 * n2, LANES), F32)],
        compiler_params=_cparams(("parallel", "parallel", "arbitrary"), 58),
        name="hyena_conv",
    )(z, x0u, hbias, kf_re, kf_im, *consts)


T_D0, T_D1, T_WQF, T_WQB, T_WKF, T_WKB, T_GCF, T_GCB, T_BD = range(9)


def _ret_tables_kernel(rl_ref, rh_ref, t_ref):
    C = RET_CHUNK
    i = lax.broadcasted_iota(jnp.int32, (C, LANES), 0).astype(F32)
    jn = lax.broadcasted_iota(jnp.int32, (C, LANES), 1).astype(F32)
    log_g = lambda e: jnp.log1p(-jnp.exp2(-e))
    diff = i - jn
    for hp in range(2):
        lgf = log_g(rh_ref[hp, 0:1, :])
        lgb = log_g(rh_ref[hp, 1:2, :])
        fwd = jnp.exp(jnp.where(diff >= 0, diff, 0.0) * lgf)
        bwd = jnp.exp(jnp.where(diff < 0, -diff, 0.0) * lgb)
        t_ref[T_D0 + hp] = jnp.where(diff >= 0, fwd, bwd)
    lf = log_g(rl_ref[0:1, :])
    lb = log_g(rl_ref[1:2, :])
    t_ref[T_WQF] = jnp.exp((i + 1.0) * lf)
    t_ref[T_WQB] = jnp.exp((C - i) * lb)
    t_ref[T_WKF] = jnp.exp((C - 1.0 - i) * lf)
    t_ref[T_WKB] = jnp.exp(i * lb)
    bd = jnp.where((i < HEAD_DIM) == (jn < HEAD_DIM), 1.0, 0.0)
    t_ref[T_GCF] = jnp.exp(C * lf) * bd
    t_ref[T_GCB] = jnp.exp(C * lb) * bd
    t_ref[T_BD] = bd


def _ret_tables(rde):
    per_lane = jnp.repeat(rde.reshape(2, 2, 2), HEAD_DIM, axis=2)
    per_lane = per_lane.transpose(1, 0, 2)
    per_head = jnp.broadcast_to(rde.reshape(2, 2, 2, 1), (2, 2, 2, LANES))
    per_head = per_head.transpose(1, 2, 0, 3)
    return pl.pallas_call(
        _ret_tables_kernel,
        grid=(2,),
        in_specs=[pl.BlockSpec((None, 2, LANES), lambda j: (j, 0, 0)),
                  pl.BlockSpec((None, 2, 2, LANES), lambda j: (j, 0, 0, 0))],
        out_specs=pl.BlockSpec((None, 9, RET_CHUNK, LANES), lambda j: (j, 0, 0, 0)),
        out_shape=jax.ShapeDtypeStruct((2, 9, RET_CHUNK, LANES), F32),
        compiler_params=_cparams(("parallel",), 32),
        name="retention_tables",
    )(per_lane, per_head)


def _ret_state_kernel(rk_ref, rv_ref, c_ref, s_ref, t_ref, st_ref, r_ref):
    C = RET_CHUNK
    nch = rk_ref.shape[0] // C

    @pl.when(pl.program_id(2) == 0)
    def _():
        r_ref[...] = jnp.zeros_like(r_ref)

    wkb = t_ref[T_WKB]
    gcb = t_ref[T_GCB]
    bd = t_ref[T_BD]
    kvs = []
    for n in range(nch):
        sl = slice(n * C, (n + 1) * C)
        k = _rope(rk_ref[sl, :].astype(F32), c_ref[sl, :], s_ref[sl, :]) * (HEAD_DIM ** -0.5)
        kvs.append(lax.dot_general((k * wkb).astype(BF16), rv_ref[sl, :],
                                   (((0,), (0,)), ((), ())), preferred_element_type=F32))
    state = r_ref[...]
    for n in range(nch - 1, -1, -1):
        st_ref[n] = state
        state = gcb * state + bd * kvs[n]
    r_ref[...] = state


def _ret_main_kernel(rq_ref, rk_ref, rv_ref, rg_ref, c_ref, s_ref, t_ref, st_ref, j_ref,
                     o_ref, sf_ref):
    C = RET_CHUNK
    nch = rq_ref.shape[0] // C

    @pl.when(pl.program_id(2) == 0)
    def _():
        sf_ref[...] = jnp.zeros_like(sf_ref)

    lane = lax.broadcasted_iota(jnp.int32, (1, LANES), 1)
    lo64 = lane < HEAD_DIM
    j_bf = j_ref[...]
    chunks = [slice(n * C, (n + 1) * C) for n in range(nch)]
    nt = (((1,), (1,)), ((), ()))
    tn = (((0,), (0,)), ((), ()))
    q, k, vb = [], [], []
    for sl in chunks:
        cs, sn = c_ref[sl, :], s_ref[sl, :]
        q.append(_rope(rq_ref[sl, :].astype(F32), cs, sn))
        k.append(_rope(rk_ref[sl, :].astype(F32), cs, sn) * (HEAD_DIM ** -0.5))
        vb.append(rv_ref[sl, :])
    scores = []
    for n in range(nch):
        qb = q[n].astype(BF16)
        kb = k[n].astype(BF16)
        zero = jnp.zeros_like(qb)
        scores.append([lax.dot_general(jnp.where(lo64, qb, zero), kb, nt,
                                       preferred_element_type=F32),
                       lax.dot_general(jnp.where(lo64, zero, qb), kb, nt,
                                       preferred_element_type=F32)])
    kvs = [lax.dot_general((k[n] * t_ref[T_WKF]).astype(BF16), vb[n], tn,
                           preferred_element_type=F32) for n in range(nch)]
    cross_b = [jnp.dot((q[n] * t_ref[T_WQB]).astype(BF16), st_ref[n].astype(BF16),
                       preferred_element_type=F32) for n in range(nch)]
    intra = []
    for n in range(nch):
        parts = [jnp.dot((scores[n][hp] * t_ref[T_D0 + hp]).astype(BF16), vb[n],
                         preferred_element_type=F32) for hp in range(2)]
        intra.append(jnp.where(lo64, parts[0], parts[1]))
    state = sf_ref[...]
    states = []
    for n in range(nch):
        states.append(state.astype(BF16))
        state = t_ref[T_GCF] * state + t_ref[T_BD] * kvs[n]
    sf_ref[...] = state
    outs = [intra[n] + cross_b[n]
            + jnp.dot((q[n] * t_ref[T_WQF]).astype(BF16), states[n], preferred_element_type=F32)
            for n in range(nch)]
    mean_sq = [_head_mean_sq(o, j_bf) for o in outs]
    for n, sl in enumerate(chunks):
        ret = outs[n] * lax.rsqrt(mean_sq[n] + NORM_EPS)
        g = rg_ref[sl, :].astype(F32)
        o_ref[sl, :] = (ret * (g * _sigmoid(g))).astype(o_ref.dtype)


def _retention(p3, cos_t, sin_t, tables, j_bf):
    B, L, _ = p3.shape
    C = RET_CHUNK
    tb = min(4096, L)
    nblk = L // tb
    nch = tb // C
    nw = RET_W // LANES
    col = lambda piece: (lambda b, j, s: (b, s, nw * piece + j))
    colr = lambda piece: (lambda b, j, s: (b, nblk - 1 - s, nw * piece + j))
    seq = lambda fn: pl.BlockSpec((None, tb, LANES), fn)
    tab_spec = pl.BlockSpec((None, 9, C, LANES), lambda b, j, s: (j, 0, 0, 0))
    states = pl.pallas_call(
        _ret_state_kernel,
        grid=(B, nw, nblk),
        in_specs=[seq(colr(1)), seq(colr(2)),
                  pl.BlockSpec((tb, LANES), lambda b, j, s: (nblk - 1 - s, 0)),
                  pl.BlockSpec((tb, LANES), lambda b, j, s: (nblk - 1 - s, 0)),
                  tab_spec],
        out_specs=pl.BlockSpec((None, None, nch, LANES, LANES),
                               lambda b, j, s: (b, j, nblk - 1 - s, 0, 0)),
        out_shape=jax.ShapeDtypeStruct((B, nw, L // C, LANES, LANES), F32),
        scratch_shapes=[pltpu.VMEM((LANES, LANES), F32)],
        compiler_params=_cparams(("parallel", "parallel", "arbitrary"), 32),
        name="retention_state",
    )(p3, p3, cos_t, sin_t, tables)
    return pl.pallas_call(
        _ret_main_kernel,
        grid=(B, nw, nblk),
        in_specs=[seq(col(0)), seq(col(1)), seq(col(2)), seq(col(3)),
                  pl.BlockSpec((tb, LANES), lambda b, j, s: (s, 0)),
                  pl.BlockSpec((tb, LANES), lambda b, j, s: (s, 0)),
                  tab_spec,
                  pl.BlockSpec((None, None, nch, LANES, LANES), lambda b, j, s: (b, j, s, 0, 0)),
                  pl.BlockSpec((LANES, LANES), lambda b, j, s: (0, 0))],
        out_specs=pl.BlockSpec((None, tb, LANES), lambda b, j, s: (b, s, j)),
        out_shape=jax.ShapeDtypeStruct((B, L, RET_W), BF16),
        scratch_shapes=[pltpu.VMEM((LANES, LANES), F32)],
        compiler_params=_cparams(("parallel", "parallel", "arbitrary"), 32),
        name="retention_main",
    )(p3, p3, p3, p3, cos_t, sin_t, tables, states, j_bf)


def _merge_kernel(oa_ref, ob_ref, oc_ref, od_ref, g0_ref, g1_ref, g2_ref, g3_ref,
                  wb_ref, wo_ref, x_ref, gn_ref, h_ref):
    tm = x_ref.shape[0]
    nparts = 2 if tm % (2 * HALO) == 0 else 1
    parts = [slice(i * tm // nparts, (i + 1) * tm // nparts) for i in range(nparts)]
    branches = ((oa_ref, g0_ref), (ob_ref, g1_ref), (oc_ref, g2_ref), (od_ref, g3_ref))
    proj = [[jnp.dot(br[rows, :].astype(BF16), wb_ref[n], preferred_element_type=F32)
             for n, (br, _) in enumerate(branches)] for rows in parts]
    merged = []
    for pi, rows in enumerate(parts):
        acc = None
        for n, (_, gr) in enumerate(branches):
            term = gr[rows, :].astype(F32) * proj[pi][n]
            acc = term if acc is None else acc + term
        merged.append(acc.astype(BF16))
    ys = [jnp.dot(m, wo_ref[...], preferred_element_type=F32) for m in merged]
    for rows, y in zip(parts, ys):
        ms = jnp.mean(y * y, axis=-1, keepdims=True)
        h_ref[rows, :] = x_ref[rows, :] + y * lax.rsqrt(ms + NORM_EPS) * gn_ref[...]


def _merge(oa, ob, oc, od, p2, wb_bf, wo_bf, x2, gn):
    T = x2.shape[0]
    tm = min(1024, T)
    br = pl.BlockSpec((tm, BRANCH_W), lambda i: (i, 0))
    gate = lambda n: pl.BlockSpec((tm, D_MODEL), lambda i, n=n: (i, n))
    row = pl.BlockSpec((tm, D_MODEL), lambda i: (i, 0))
    return pl.pallas_call(
        _merge_kernel,
        grid=(T // tm,),
        in_specs=[br, br, br, br, gate(0), gate(1), gate(2), gate(3),
                  pl.BlockSpec((N_BRANCH, BRANCH_W, D_MODEL), lambda i: (0, 0, 0)),
                  pl.BlockSpec((D_MODEL, D_MODEL), lambda i: (0, 0)),
                  row, pl.BlockSpec((1, D_MODEL), lambda i: (0, 0))],
        out_specs=row,
        out_shape=jax.ShapeDtypeStruct((T, D_MODEL), F32),
        compiler_params=_cparams(("parallel",), 48),
        name="merge_out_proj",
    )(oa, ob, oc, od, p2, p2, p2, p2, wb_bf, wo_bf, x2, gn)


def _ffn_kernel(h_ref, g2_ref, wg_ref, wu_ref, wd_ref, g3_ref, o_ref):
    tm = h_ref.shape[0]
    halves = [slice(0, tm // 2), slice(tm // 2, tm)] if tm % (2 * HALO) == 0 else [slice(0, tm)]
    hns = []
    for rows in halves:
        h = h_ref[rows, :]
        ms = jnp.mean(h * h, axis=-1, keepdims=True)
        hns.append((h * lax.rsqrt(ms + NORM_EPS) * g2_ref[...]).astype(BF16))
    fs = [None] * len(halves)
    for lo, hi in _col_chunks(D_FF, 2):
        gus = [(jnp.dot(hn, wg_ref[:, lo:hi], preferred_element_type=F32),
                jnp.dot(hn, wu_ref[:, lo:hi], preferred_element_type=F32)) for hn in hns]
        acts = [(g * _sigmoid(g) * u).astype(BF16) for g, u in gus]
        for i, a in enumerate(acts):
            part = jnp.dot(a, wd_ref[lo:hi, :], preferred_element_type=F32)
            fs[i] = part if fs[i] is None else fs[i] + part
    for rows, f in zip(halves, fs):
        ms = jnp.mean(f * f, axis=-1, keepdims=True)
        o_ref[rows, :] = h_ref[rows, :] + f * lax.rsqrt(ms + NORM_EPS) * g3_ref[...]


def _ffn(h2, g2, wg_bf, wu_bf, wd_bf, g3):
    T = h2.shape[0]
    tm = min(512, T)
    row = pl.BlockSpec((tm, D_MODEL), lambda i: (i, 0))
    vec = pl.BlockSpec((1, D_MODEL), lambda i: (0, 0))
    return pl.pallas_call(
        _ffn_kernel,
        grid=(T // tm,),
        in_specs=[row, vec,
                  pl.BlockSpec((D_MODEL, D_FF), lambda i: (0, 0)),
                  pl.BlockSpec((D_MODEL, D_FF), lambda i: (0, 0)),
                  pl.BlockSpec((D_FF, D_MODEL), lambda i: (0, 0)),
                  vec],
        out_specs=row,
        out_shape=jax.ShapeDtypeStruct((T, D_MODEL), F32),
        compiler_params=_cparams(("parallel",), 48),
        name="ffn",
    )(h2, g2, wg_bf, wu_bf, wd_bf, g3)


def _rope_tables(L):
    rows = L // GRID_W
    r = jnp.repeat(jnp.arange(rows, dtype=F32), GRID_W)
    c = jnp.tile(jnp.arange(GRID_W, dtype=F32), rows)
    inv = ROPE_BASE ** (-jnp.arange(ROPE_FREQS, dtype=F32) / ROPE_FREQS)
    ar = r[:, None] * inv
    ac = c[:, None] * inv
    cos64 = jnp.concatenate([jnp.cos(ar), jnp.cos(ar), jnp.cos(ac), jnp.cos(ac)], axis=1)
    sin64 = jnp.concatenate([-jnp.sin(ar), jnp.sin(ar), -jnp.sin(ac), jnp.sin(ac)], axis=1)
    return jnp.tile(cos64, (1, 2)), jnp.tile(sin64, (1, 2))


def _filter_features(L):
    t = np.linspace(0.0, 1.0, L)[:, None]
    f = np.linspace(1e-4, HY_BANDS - 1, HY_BANDS)
    ang = (2.0 * math.pi / L) * np.arange(L)[:, None] * f[None, :]
    z = np.concatenate([t, np.cos(ang), -np.sin(ang)], axis=-1)
    z = np.pad(z, ((0, 0), (0, HY_FILTER_HIDDEN - HY_EMB)))
    back = np.concatenate([[0], np.arange(L - 1, 0, -1)])
    return jnp.asarray(np.concatenate([z, z[back]], axis=1).astype(np.float32))


def _head_mean_matrix():
    i = np.arange(LANES)
    j = ((i[:, None] // HEAD_DIM) == (i[None, :] // HEAD_DIM)).astype(np.float32) / HEAD_DIM
    return jnp.asarray(j.astype(BF16))


def _halve_gate_columns(w_in_bf):
    scale = jnp.where(jnp.arange(IN_COLS) >= GATE_OFF, 0.5, 1.0).astype(BF16)
    return w_in_bf * scale


def _trunk(x, wts, fft_dims=None):
    B, L, _ = x.shape
    T = B * L
    n1, n2 = fft_dims or _fft_dims(L)
    cst = _fft_consts(n1, n2, n1 // 2)
    cos_t, sin_t = _rope_tables(L)
    z2 = _filter_features(L)
    deltas = jnp.abs(jnp.linspace(math.log(HY_TARGET) / HY_SLOW_DECAY,
                                  math.log(HY_TARGET) / HY_FAST_DECAY, HY_WIDTH, dtype=F32))[None, :]
    j_bf = _head_mean_matrix()
    depth = wts["w_in"].shape[0]
    x2 = x.reshape(T, D_MODEL)
    for l in range(depth):
        ng = wts["norm_gains"][l]
        w1p = jnp.pad(wts["hy_w1"][l], ((0, HY_FILTER_HIDDEN - HY_EMB), (0, 0)))
        k2raw, sumsq = _hy_filter_raw(z2, w1p, wts["hy_b1"][l][None, :], wts["hy_w2"][l],
                                      wts["hy_b2"][l][None, :], wts["hy_w3"][l],
                                      wts["hy_freq"][l], deltas, L)
        kf_re, kf_im = _filter_fft(k2raw, sumsq, cst, n1, n2)

        qg = jnp.tile(wts["qk_norm"][l, 0], 2)[None, :]
        kg = jnp.tile(wts["qk_norm"][l, 1], 2)[None, :]
        qt, k, vt, p_ret, p_gate, z, x0u, out_d = _in_proj(
            x2, ng[0][None, :], wts["w_in_bf"][l], wts["hy_conv_w"][l],
            wts["hy_conv_b"][l][None, :], wts["sc_conv_w"][l], cos_t, sin_t, qg, kg, j_bf, L)
        out_a = _flash(qt, k, vt)

        out_b = _hy_conv(z.reshape(B, L, HY_WIDTH), x0u.reshape(B, L, HY_WIDTH),
                         wts["hy_bias"][l][None, :], kf_re, kf_im, cst, n1, n2)

        tables = _ret_tables(wts["ret_decay_exp"][l])
        out_c = _retention(p_ret.reshape(B, L, 4 * RET_W), cos_t, sin_t, tables, j_bf)

        h2 = _merge(out_a.reshape(T, BRANCH_W), out_b.reshape(T, BRANCH_W),
                    out_c.reshape(T, BRANCH_W), out_d,
                    p_gate, wts["w_branch_bf"][l], wts["w_out_bf"][l], x2, ng[1][None, :])
        x2 = _ffn(h2, ng[2][None, :], wts["w_gate_bf"][l], wts["w_up_bf"][l],
                  wts["w_ffn_out_bf"][l], ng[3][None, :])
    return x2.reshape(B, L, D_MODEL)


def kernel(x_prompt, x_sample, norm_gains, w_in, qk_norm, hy_conv_w, hy_conv_b, hy_w1, hy_b1, hy_w2,
           hy_b2, hy_w3, hy_freq, hy_bias, ret_decay_exp, sc_conv_w, w_branch, w_out, w_ffn_in,
           w_ffn_out):
    wts = dict(norm_gains=norm_gains, w_in=w_in, qk_norm=qk_norm, hy_conv_w=hy_conv_w,
               hy_conv_b=hy_conv_b, hy_w1=hy_w1, hy_b1=hy_b1, hy_w2=hy_w2, hy_b2=hy_b2,
               hy_w3=hy_w3, hy_freq=hy_freq, hy_bias=hy_bias, ret_decay_exp=ret_decay_exp,
               sc_conv_w=sc_conv_w,
               w_in_bf=_halve_gate_columns(w_in.astype(BF16)), w_branch_bf=w_branch.astype(BF16),
               w_out_bf=w_out.astype(BF16), w_gate_bf=w_ffn_in[..., :D_FF].astype(BF16),
               w_up_bf=w_ffn_in[..., D_FF:].astype(BF16), w_ffn_out_bf=w_ffn_out.astype(BF16))
    return _trunk(x_prompt, wts), _trunk(x_sample, wts)
```

```python
import functools
import math

import numpy as np
import jax
import jax.numpy as jnp
from jax import lax
from jax.experimental import pallas as pl
from jax.experimental.pallas import tpu as pltpu

F32 = jnp.float32
BF16 = jnp.bfloat16

D_MODEL = 1024
GRID_W = 64
N_BRANCH = 4
BRANCH_W = 256
HEAD_DIM = 64
ATT_Q_HEADS = 4
ATT_KV_HEADS = 2
ROPE_BASE = 10000.0
ROPE_FREQS = HEAD_DIM // 4
HY_WIDTH = BRANCH_W
HY_EMB = 33
HY_BANDS = (HY_EMB - 1) // 2
HY_FILTER_HIDDEN = 64
HY_FAST_DECAY = 0.3
HY_SLOW_DECAY = 1.5
HY_TARGET = 1e-2
RET_HEADS = 4
RET_W = RET_HEADS * HEAD_DIM
RET_CHUNK = 128
SC_WIDTH = BRANCH_W
D_FF = 2816
NORM_EPS = 1e-6

ATT_Q_W = ATT_Q_HEADS * HEAD_DIM
ATT_KV_W = ATT_KV_HEADS * HEAD_DIM
A_K_OFF = ATT_Q_W
A_V_OFF = A_K_OFF + ATT_KV_W
HY_OFF = A_V_OFF + ATT_KV_W
RET_OFF = HY_OFF + 3 * HY_WIDTH
SC_OFF = RET_OFF + 4 * RET_W
GATE_OFF = SC_OFF + 3 * SC_WIDTH
IN_COLS = GATE_OFF + N_BRANCH * D_MODEL

LANES = 128
SUBLANES = 8
MXU_W = 256
HALO = 2 * SUBLANES
IN_TM = 512
ATT_KB = 512
ATT_TQ = 256
VT_ROWS = HEAD_DIM + HALO
Q_SCALE = HEAD_DIM ** -0.5 * math.log2(math.e)
FFT_PASSES_FILTER = 3
FFT_PASSES_DATA = 1
FFT_CH = 8
FFT_SETS = 1
MIB = 1 << 20


def _cparams(sem, vmem_mib):
    return pltpu.CompilerParams(dimension_semantics=sem, vmem_limit_bytes=vmem_mib * MIB)


def _sigmoid(x):
    return 0.5 * jnp.tanh(0.5 * x) + 0.5


def _split(x):
    hi = x.astype(BF16)
    lo = (x - hi.astype(F32)).astype(BF16)
    return hi, lo


def _np_split(a64):
    a32 = np.asarray(a64, np.float32)
    hi = a32.astype(BF16)
    lo = (a32 - hi.astype(np.float32)).astype(BF16)
    return jnp.asarray(hi), jnp.asarray(lo)


def _mm_const(a_hi, a_lo, x, passes):
    if passes == 1:
        return jnp.dot(a_hi, x.astype(BF16), preferred_element_type=F32)
    xh, xl = _split(x)
    out = jnp.dot(a_hi, xh, preferred_element_type=F32)
    if passes >= 3:
        out = out + jnp.dot(a_lo, xh, preferred_element_type=F32)
        out = out + jnp.dot(a_hi, xl, preferred_element_type=F32)
    return out


def _mm3(a, b):
    ah, al = _split(a)
    bh, bl = _split(b)
    out = jnp.dot(ah, bh, preferred_element_type=F32)
    out = out + jnp.dot(al, bh, preferred_element_type=F32)
    return out + jnp.dot(ah, bl, preferred_element_type=F32)


def _head_mean_sq(x, j_bf):
    hi, lo = _split(x * x)
    return (jnp.dot(hi, j_bf, preferred_element_type=F32)
            + jnp.dot(lo, j_bf, preferred_element_type=F32))


def _rope(x, c, s):
    lane = lax.broadcasted_iota(jnp.int32, (1, LANES), 1)
    is_b = (lane & 16) != 0
    partner = jnp.where(is_b, pltpu.roll(x, 16, 1), pltpu.roll(x, LANES - 16, 1))
    return x * c + partner * s


def _col_chunks(n, parts):
    tiles = n // MXU_W
    bounds = [MXU_W * (tiles * i // parts) for i in range(parts + 1)]
    return list(zip(bounds[:-1], bounds[1:]))


def _conv3_rows(main, before, after, w):
    tb = main.shape[0]
    w0, w1, w2 = w[0:1, :], w[1:2, :], w[2:3, :]
    body = pltpu.roll(main, 1, 0) * w0 + main * w1 + pltpu.roll(main, tb - 1, 0) * w2
    row_first = before * w0 + main[0:1, :] * w1 + main[1:2, :] * w2
    row_last = main[tb - 2:tb - 1, :] * w0 + main[tb - 1:tb, :] * w1 + after * w2
    row = lax.broadcasted_iota(jnp.int32, (SUBLANES, 1), 0)
    head = jnp.where(row == 0, row_first, body[:SUBLANES])
    tail = jnp.where(row == SUBLANES - 1, row_last, body[tb - SUBLANES:])
    return jnp.concatenate([head, body[SUBLANES:tb - SUBLANES], tail], axis=0)


def _in_proj_kernel(x_ref, xp_ref, xn_ref, g_ref, w_ref, hcw_ref, hcb_ref, scw_ref,
                    c_ref, s_ref, qg_ref, kg_ref, j_ref,
                    qt_ref, k_ref, vt_ref, pr_ref, pg_ref, z_ref, x0_ref, od_ref, *,
                    tiles_per_seq):
    i = pl.program_id(0)
    first = (i % tiles_per_seq) == 0
    last = (i % tiles_per_seq) == tiles_per_seq - 1
    g = g_ref[...]

    def normed(x):
        ms = jnp.mean(x * x, axis=-1, keepdims=True)
        return (x * lax.rsqrt(ms + NORM_EPS) * g).astype(BF16)

    dot = lambda a, lo, hi: jnp.dot(a, w_ref[:, lo:hi], preferred_element_type=F32)
    xn = normed(x_ref[...])
    halo = normed(jnp.concatenate([xp_ref[...], xn_ref[...]], axis=0))
    edge = lambda h: (jnp.where(first, 0.0, h[SUBLANES - 1:SUBLANES, :]),
                      jnp.where(last, 0.0, h[SUBLANES:SUBLANES + 1, :]))

    head = dot(xn, 0, RET_OFF)
    hy_halo = dot(halo, HY_OFF, RET_OFF)
    sc = dot(xn, SC_OFF, GATE_OFF)
    sc_halo = dot(halo, SC_OFF + SC_WIDTH, GATE_OFF)
    pr_ref[...] = dot(xn, RET_OFF, SC_OFF).astype(pr_ref.dtype)
    _attn_prep_tile(head[:, :HY_OFF], c_ref, s_ref, qg_ref, kg_ref, j_ref, qt_ref, k_ref, vt_ref)

    u = _conv3_rows(head[:, HY_OFF:], *edge(hy_halo), hcw_ref[...]) + hcb_ref[...]
    x0_ref[...] = u[:, :HY_WIDTH]
    z_ref[...] = u[:, 2 * HY_WIDTH:] * u[:, HY_WIDTH:2 * HY_WIDTH]
    m = sc[:, SC_WIDTH:2 * SC_WIDTH] * sc[:, 2 * SC_WIDTH:]
    mh = sc_halo[:, :SC_WIDTH] * sc_halo[:, SC_WIDTH:]
    od_ref[...] = (sc[:, :SC_WIDTH] * _conv3_rows(m, *edge(mh), scw_ref[...])).astype(od_ref.dtype)

    for lo, hi in _col_chunks(N_BRANCH * D_MODEL, 2):
        g_half = dot(xn, GATE_OFF + lo, GATE_OFF + hi).astype(pg_ref.dtype)
        pg_ref[:, lo:hi] = 0.5 * jnp.tanh(g_half) + 0.5


def _in_proj(x2, g, w_bf, hcw, hcb, scw, cos_t, sin_t, qg, kg, j_bf, L):
    T = x2.shape[0]
    B = T // L
    tm = min(IN_TM, L)
    tq = min(ATT_TQ, L)
    tps = L // tm
    kb = min(ATT_KB, L)
    per_kb = kb // tm
    r8 = tm // SUBLANES
    row = lambda w: pl.BlockSpec((tm, w), lambda i: (i, 0))
    full = lambda a: pl.BlockSpec(a.shape, lambda i: (0,) * a.ndim)
    tab = pl.BlockSpec((tm, LANES), lambda i: (i % tps, 0))
    sds = lambda w, dt: jax.ShapeDtypeStruct((T, w), dt)
    return pl.pallas_call(
        functools.partial(_in_proj_kernel, tiles_per_seq=tps),
        grid=(T // tm,),
        in_specs=[row(D_MODEL),
                  pl.BlockSpec((SUBLANES, D_MODEL), lambda i: (jnp.maximum(i * r8 - 1, 0), 0)),
                  pl.BlockSpec((SUBLANES, D_MODEL),
                               lambda i: (jnp.minimum((i + 1) * r8, T // SUBLANES - 1), 0)),
                  full(g), full(w_bf), full(hcw), full(hcb), full(scw),
                  tab, tab, full(qg), full(kg), full(j_bf)],
        out_specs=[pl.BlockSpec((None, ATT_Q_HEADS, tm // tq, LANES, tq),
                                lambda i: (i // tps, 0, i % tps, 0, 0)),
                   pl.BlockSpec((None, tm, LANES), lambda i: (i // tps, i % tps, 0)),
                   pl.BlockSpec((None, None, ATT_KV_HEADS * VT_ROWS, tm),
                                lambda i: (i // tps, (i % tps) // per_kb, 0, (i % tps) % per_kb)),
                   row(RET_W * 4), row(N_BRANCH * D_MODEL),
                   row(HY_WIDTH), row(HY_WIDTH), row(SC_WIDTH)],
        out_shape=[jax.ShapeDtypeStruct((B, ATT_Q_HEADS, L // tq, LANES, tq), BF16),
                   jax.ShapeDtypeStruct((B, L, LANES), BF16),
                   jax.ShapeDtypeStruct((B, L // kb, ATT_KV_HEADS * VT_ROWS, kb), BF16),
                   sds(RET_W * 4, BF16), sds(N_BRANCH * D_MODEL, BF16),
                   sds(HY_WIDTH, F32), sds(HY_WIDTH, F32), sds(SC_WIDTH, BF16)],
        compiler_params=_cparams(("parallel",), 48),
        name="in_proj",
    )(x2, x2, x2, g, w_bf, hcw, hcb, scw, cos_t, sin_t, qg, kg, j_bf)


def _attn_prep_tile(qkv, c_ref, s_ref, qg_ref, kg_ref, j_ref, qt_ref, k_ref, vt_ref):
    c = c_ref[...]
    s = s_ref[...]
    j_bf = j_ref[...]
    tm = qkv.shape[0]
    tq = qt_ref.shape[-1]
    zeros = jnp.zeros((HEAD_DIM, tm), BF16)
    for blk in range(2):
        q = qkv[:, LANES * blk:LANES * (blk + 1)]
        qn = q * lax.rsqrt(_head_mean_sq(q, j_bf) + NORM_EPS) * qg_ref[...]
        qt = (_rope(qn, c, s) * Q_SCALE).astype(BF16).T
        lo, hi = qt[:HEAD_DIM], qt[HEAD_DIM:]
        if blk == 0:
            heads = (jnp.concatenate([lo, zeros], axis=0), jnp.concatenate([hi, zeros], axis=0))
        else:
            heads = (jnp.concatenate([zeros, lo], axis=0), jnp.concatenate([zeros, hi], axis=0))
        for hp, qh in enumerate(heads):
            for t in range(tm // tq):
                qt_ref[2 * blk + hp, t] = qh[:, t * tq:(t + 1) * tq]
    k = qkv[:, A_K_OFF:A_K_OFF + LANES]
    kn = k * lax.rsqrt(_head_mean_sq(k, j_bf) + NORM_EPS) * kg_ref[...]
    k_ref[...] = _rope(kn, c, s).astype(BF16)
    v = qkv[:, A_V_OFF:A_V_OFF + LANES].astype(BF16)
    ones = jnp.ones((VT_ROWS - HEAD_DIM, tm), BF16)
    vt = v.T
    vt_ref[...] = jnp.concatenate([vt[:HEAD_DIM], ones, vt[HEAD_DIM:], ones], axis=0)


def _flash_kernel(qt_ref, k_ref, vt_ref, o_ref, sa_ref, ma_ref, sb_ref, mb_ref, *,
                  ngrp, nsub, ntile):
    tq = ATT_TQ
    kb = vt_ref.shape[-1]
    group = ATT_Q_HEADS // ATT_KV_HEADS

    def produce(slot, qi, g, t, h):
        s_ref, mx_ref = slot
        st = pl.multiple_of((g * nsub + t) * kb, kb)
        s = jnp.dot(k_ref[pl.ds(st, kb), :], qt_ref[h, qi],
                    preferred_element_type=F32)
        s_ref[t * ATT_Q_HEADS + h] = s
        mx_ref[t * ATT_Q_HEADS + h] = jnp.max(s, axis=0, keepdims=True)

    def consume(slot, g, t, h, state):
        s_ref, mx_ref = slot
        m, acc = state
        m_new = jnp.maximum(m, mx_ref[t * ATT_Q_HEADS + h])
        alpha = jnp.exp2(m - m_new)
        p = jnp.exp2((s_ref[t * ATT_Q_HEADS + h] - m_new).astype(BF16))
        j = h // group
        vt = vt_ref[g * nsub + t, VT_ROWS * j:VT_ROWS * (j + 1), :]
        return m_new, alpha * acc + jnp.dot(vt, p, preferred_element_type=F32)

    def step(slot_in, g_in, slot_out, qi_out, g_out, carry):
        carry = list(carry)
        for t in range(nsub):
            for h in range(ATT_Q_HEADS):
                if slot_out is not None:
                    produce(slot_out, qi_out, g_out, t, h)
                if slot_in is not None:
                    carry[h] = consume(slot_in, g_in, t, h, carry[h])
        return tuple(carry)

    def fresh():
        return tuple((jnp.full((1, tq), -jnp.inf, F32), jnp.zeros((VT_ROWS, tq), F32))
                     for _ in range(ATT_Q_HEADS))

    def finalize(qi, carry):
        outs = [acc[:HEAD_DIM] / acc[HEAD_DIM:HEAD_DIM + 1] for _, acc in carry]
        rows = pl.ds(pl.multiple_of(qi * tq, tq), tq)
        o_ref[rows, :] = jnp.concatenate(outs, axis=0).T.astype(o_ref.dtype)

    slot_a = (sa_ref, ma_ref)
    slot_b = (sb_ref, mb_ref)

    if ngrp == 1:
        def tile(qi, _):
            step(None, None, slot_a, qi, 0, ())
            finalize(qi, step(slot_a, 0, None, None, None, fresh()))
            return 0
    else:
        step(None, None, slot_a, 0, 0, ())

        def tile(qi, _):
            def body(i, carry):
                g = 2 * i
                carry = step(slot_a, g, slot_b, qi, g + 1, carry)
                return step(slot_b, g + 1, slot_a, qi, g + 2, carry)

            carry = lax.fori_loop(0, ngrp // 2 - 1, body, fresh())
            carry = step(slot_a, ngrp - 2, slot_b, qi, ngrp - 1, carry)
            carry = step(slot_b, ngrp - 1, slot_a, jnp.minimum(qi + 1, ntile - 1), 0, carry)
            finalize(qi, carry)
            return 0

    lax.fori_loop(0, ntile, tile, 0)


def _flash(qt, k, vt):
    B, _, nqt, _, tq = qt.shape
    L = nqt * tq
    ntile = math.gcd(nqt, 8)
    kb = vt.shape[-1]
    nkb = L // kb
    nsub = 2 if nkb % 4 == 0 and nkb >= 8 else 1
    ngrp = L // (kb * nsub)
    assert ngrp == 1 or ngrp % 2 == 0
    s_scratch = pltpu.VMEM((nsub * ATT_Q_HEADS, kb, tq), F32)
    m_scratch = pltpu.VMEM((nsub * ATT_Q_HEADS, 1, tq), F32)
    return pl.pallas_call(
        functools.partial(_flash_kernel, ngrp=ngrp, nsub=nsub, ntile=ntile),
        grid=(B, nqt // ntile),
        in_specs=[pl.BlockSpec((None, ATT_Q_HEADS, ntile, LANES, tq),
                               lambda b, i: (b, 0, i, 0, 0)),
                  pl.BlockSpec((None, L, LANES), lambda b, i: (b, 0, 0)),
                  pl.BlockSpec((None, nkb, ATT_KV_HEADS * VT_ROWS, kb),
                               lambda b, i: (b, 0, 0, 0))],
        out_specs=pl.BlockSpec((None, ntile * tq, ATT_Q_W), lambda b, i: (b, i, 0)),
        out_shape=jax.ShapeDtypeStruct((B, L, ATT_Q_W), BF16),
        scratch_shapes=[s_scratch, m_scratch, s_scratch, m_scratch],
        compiler_params=_cparams(("parallel", "arbitrary"), 48),
        name="flash_attn",
    )(qt, k, vt)


def _hy_filter_kernel(z_ref, w1_ref, b1_ref, w2_ref, b2_ref, w3_ref, fr_ref, dl_ref,
                      k_ref, ss_ref):
    step = pl.program_id(0)
    tr = z_ref.shape[0]
    H = HY_FILTER_HIDDEN
    z = z_ref[...]
    h = jnp.sin(fr_ref[0:1, :] * (_mm3(z, w1_ref[...]) + b1_ref[...]))
    h = jnp.sin(fr_ref[1:2, :] * (_mm3(h, w2_ref[...]) + b2_ref[...]))
    h3 = _mm3(h, w3_ref[...])
    fwd = h3[:, :HY_WIDTH] * jnp.exp(-z[:, 0:1] * dl_ref[...])
    bwd = h3[:, HY_WIDTH:] * jnp.exp(-z[:, H:H + 1] * dl_ref[...])
    row = step * tr + lax.broadcasted_iota(jnp.int32, (tr, 1), 0)
    bwd = jnp.where(row == 0, 0.0, bwd)
    k_ref[0] = fwd
    k_ref[1] = bwd

    @pl.when(step == 0)
    def _():
        ss_ref[...] = jnp.zeros_like(ss_ref)

    ss_ref[...] += jnp.sum(fwd * fwd + bwd * bwd, axis=0, keepdims=True)


def _block_diag2(a, b):
    za = jnp.zeros((a.shape[0], b.shape[1]), a.dtype)
    zb = jnp.zeros((b.shape[0], a.shape[1]), a.dtype)
    return jnp.concatenate([jnp.concatenate([a, za], axis=1),
                            jnp.concatenate([zb, b], axis=1)], axis=0)


def _hy_filter_raw(z2, w1p, b1, w2, b2, w3, freq, deltas, L):
    tr = min(512, L)
    H2 = 2 * HY_FILTER_HIDDEN
    two = lambda v: jnp.tile(v, (1, 2))
    args = (z2, _block_diag2(w1p, w1p), two(b1), _block_diag2(w2, w2), two(b2),
            _block_diag2(w3[:, :HY_WIDTH], w3[:, HY_WIDTH:]), two(freq), deltas)
    full = lambda a: pl.BlockSpec(a.shape, lambda i: (0, 0))
    k2, sumsq = pl.pallas_call(
        _hy_filter_kernel,
        grid=(L // tr,),
        in_specs=[pl.BlockSpec((tr, H2), lambda i: (i, 0))] + [full(a) for a in args[1:]],
        out_specs=[pl.BlockSpec((2, tr, HY_WIDTH), lambda i: (0, i, 0)),
                   pl.BlockSpec((1, HY_WIDTH), lambda i: (0, 0))],
        out_shape=[jax.ShapeDtypeStruct((2, L, HY_WIDTH), F32),
                   jax.ShapeDtypeStruct((1, HY_WIDTH), F32)],
        compiler_params=_cparams(("arbitrary",), 32),
        name="hyena_filter",
    )(*args)
    return k2.reshape(2 * L, HY_WIDTH), sumsq


def _fft_dims(L):
    n = 2 * L
    n2 = 128 if n >= 32 * 128 else 64
    return n // n2, n2


def _fft_consts(n1, n2, r):
    n = n1 * n2
    a1 = 2.0 * np.pi * np.outer(np.arange(n1), np.arange(n1)) / n1
    c1, s1 = np.cos(a1), np.sin(a1)
    a2 = 2.0 * np.pi * np.outer(np.arange(n2), np.arange(n2)) / n2
    c2, s2 = np.cos(a2), np.sin(a2)
    f1 = np.block([[c1[:, :r], s1[:, :r]], [-s1[:, :r], c1[:, :r]]])
    f1_full = np.concatenate([c1, -s1], axis=0)
    m2 = np.block([[c2, s2], [-s2, c2]])
    m2i = np.block([[c2, -s2], [s2, c2]])
    fi = np.block([[c1[:r, :], -s1[:r, :]], [s1[:r, :], c1[:r, :]]])
    aw = 2.0 * np.pi * np.arange(n1) / n
    off = np.repeat(np.arange(FFT_CH), LANES)[None, :] * aw[:, None]
    step = np.broadcast_to((FFT_CH * aw)[:, None], (n1, LANES))
    f32 = lambda a: jnp.asarray(np.asarray(a, np.float32))
    return dict(f1=_np_split(f1), f1_full=_np_split(f1_full), m2=_np_split(m2),
                m2i=_np_split(m2i), fi=_np_split(fi),
                tw=(f32(np.cos(off)), f32(-np.sin(off)), f32(np.cos(step)), f32(-np.sin(step))))


def _cmul(a_re, a_im, b_re, b_im):
    return a_re * b_re - a_im * b_im, a_re * b_im + a_im * b_re


def _lane_tile(x, reps):
    return jnp.concatenate([x] * reps, axis=1)


def _lane_part(x, j):
    return x[:, LANES * j:LANES * (j + 1)]


def _chunk_twiddle(t_re, t_im, tw_refs):
    d_re, d_im, s_re, s_im = (ref[...] for ref in tw_refs)
    cur = _cmul(_lane_tile(t_re, FFT_CH), _lane_tile(t_im, FFT_CH), d_re, d_im)
    return cur, _cmul(t_re, t_im, s_re, s_im)


def _twiddle_init(n1):
    return jnp.ones((n1, LANES), F32), jnp.zeros((n1, LANES), F32)


def _fft_stage1(load_rows, a_ref, f_hi, f_lo, tw_refs, n1, n2, passes):
    def body(ci, carry):
        i0s, tws = [], []
        for u in range(FFT_SETS):
            tw, carry = _chunk_twiddle(*carry, tw_refs)
            tws.append(tw)
            i0s.append((ci * FFT_SETS + u) * FFT_CH)
        prods = [_mm_const(f_hi, f_lo, jnp.concatenate(
            [load_rows(i0 + j) for j in range(FFT_CH)], axis=1), passes)
            for i0 in i0s]
        for i0, (t_re, t_im), a in zip(i0s, tws, prods):
            o_re, o_im = _cmul(a[:n1], a[n1:], t_re, t_im)
            for j in range(FFT_CH):
                base = pl.multiple_of((i0 + j) * 2 * n1, 2 * n1)
                a_ref[pl.ds(base, n1), :] = _lane_part(o_re, j)
                a_ref[pl.ds(base + n1, n1), :] = _lane_part(o_im, j)
        return carry

    lax.fori_loop(0, n2 // (FFT_CH * FFT_SETS), body, _twiddle_init(n1))


def _load_spectrum_rows(a_ref, k1, n1, n2):
    ld = lambda off: a_ref[pl.ds(off, n2, stride=2 * n1), :]
    return jnp.concatenate(
        [jnp.concatenate([ld(k1 + j) for j in range(FFT_CH)], axis=1),
         jnp.concatenate([ld(n1 + k1 + j) for j in range(FFT_CH)], axis=1)], axis=0)


def _filter_fft_kernel(k_ref, ss_ref, f_hi_ref, f_lo_ref, m_hi_ref, m_lo_ref,
                       dre_ref, dim_ref, sre_ref, sim_ref, kr_ref, ki_ref, a_ref, *, n1, n2):
    c = pl.program_id(1)

    @pl.when(c == 0)
    def _():
        _fft_stage1(lambda i: k_ref[pl.ds(i, n1, stride=n2), :], a_ref,
                    f_hi_ref[...], f_lo_ref[...], (dre_ref, dim_ref, sre_ref, sim_ref),
                    n1, n2, FFT_PASSES_FILTER)

    scale = _lane_tile(lax.rsqrt(ss_ref[...] + NORM_EPS) * (1.0 / (n1 * n2)), FFT_CH)
    xin = _load_spectrum_rows(a_ref, c * FFT_CH, n1, n2)
    x = _mm_const(m_hi_ref[...], m_lo_ref[...], xin, FFT_PASSES_FILTER)
    kr_ref[...] = x[:n2] * scale
    ki_ref[...] = x[n2:] * scale


def _filter_fft(k2raw, sumsq, cst, n1, n2):
    n = n1 * n2
    nw = HY_WIDTH // LANES
    consts = [*cst["f1_full"], *cst["m2"], *cst["tw"]]
    full = lambda a: pl.BlockSpec(a.shape, lambda w, c: (0,) * a.ndim)
    out_spec = pl.BlockSpec((None, None, n2, FFT_CH * LANES), lambda w, c: (w, c, 0, 0))
    out_sh = jax.ShapeDtypeStruct((nw, n1 // FFT_CH, n2, FFT_CH * LANES), F32)
    return pl.pallas_call(
        functools.partial(_filter_fft_kernel, n1=n1, n2=n2),
        grid=(nw, n1 // FFT_CH),
        in_specs=[pl.BlockSpec((n, LANES), lambda w, c: (0, w)),
                  pl.BlockSpec((1, LANES), lambda w, c: (0, w))] + [full(a) for a in consts],
        out_specs=[out_spec, out_spec],
        out_shape=[out_sh, out_sh],
        scratch_shapes=[pltpu.VMEM((2 * n, LANES), F32)],
        compiler_params=_cparams(("parallel", "arbitrary"), 48),
        name="hyena_filter_fft",
    )(k2raw, sumsq, *consts)


def _hy_conv_kernel(z_ref, x0_ref, hb_ref, kr_ref, ki_ref, f_hi_ref, f_lo_ref,
                    m_hi_ref, m_lo_ref, mi_hi_ref, mi_lo_ref, fi_hi_ref, fi_lo_ref,
                    dre_ref, dim_ref, sre_ref, sim_ref, o_ref, a_ref, *, n1, n2):
    c = pl.program_id(2)
    r = n1 // 2
    tw_refs = (dre_ref, dim_ref, sre_ref, sim_ref)

    def load_pair(i):
        seq = pl.ds(i, r, stride=n2)
        return jnp.concatenate([z_ref[0, seq, :], z_ref[1, seq, :]], axis=0)

    @pl.when(c == 0)
    def _():
        _fft_stage1(load_pair, a_ref, f_hi_ref[...], f_lo_ref[...], tw_refs, n1, n2,
                    FFT_PASSES_DATA)

    k1s = [(c * FFT_SETS + u) * FFT_CH for u in range(FFT_SETS)]
    xs = [_mm_const(m_hi_ref[...], m_lo_ref[...], _load_spectrum_rows(a_ref, k1, n1, n2),
                    FFT_PASSES_DATA) for k1 in k1s]
    ys = [jnp.concatenate(_cmul(x[:n2], x[n2:], kr_ref[u], ki_ref[u]), axis=0)
          for u, x in enumerate(xs)]
    bqs = [_mm_const(mi_hi_ref[...], mi_lo_ref[...], y, FFT_PASSES_DATA) for y in ys]
    for k1, bq in zip(k1s, bqs):
        for j in range(FFT_CH):
            a_ref[pl.ds(k1 + j, n2, stride=2 * n1), :] = _lane_part(bq[:n2], j)
            a_ref[pl.ds(n1 + k1 + j, n2, stride=2 * n1), :] = _lane_part(bq[n2:], j)

    @pl.when(c == pl.num_programs(2) - 1)
    def _():
        hb = hb_ref[...]

        def body(ci, carry):
            i0s, bts = [], []
            for u in range(FFT_SETS):
                (t_re, t_im), carry = _chunk_twiddle(*carry, tw_refs)
                i0 = (ci * FFT_SETS + u) * FFT_CH
                rows = [pl.multiple_of((i0 + j) * 2 * n1, 2 * n1) for j in range(FFT_CH)]
                br = jnp.concatenate([a_ref[pl.ds(b0, n1), :] for b0 in rows], axis=1)
                bi = jnp.concatenate([a_ref[pl.ds(b0 + n1, n1), :] for b0 in rows], axis=1)
                bts.append(jnp.concatenate([br * t_re + bi * t_im, bi * t_re - br * t_im],
                                           axis=0))
                i0s.append(i0)
            ys = [_mm_const(fi_hi_ref[...], fi_lo_ref[...], bt, FFT_PASSES_DATA) for bt in bts]
            for i0, y in zip(i0s, ys):
                for j in range(FFT_CH):
                    seq = pl.ds(i0 + j, r, stride=n2)
                    for row in range(2):
                        conv = _lane_part(y[row * r:(row + 1) * r], j)
                        o_ref[row, seq, :] = (x0_ref[row, seq, :]
                                              * (conv + z_ref[row, seq, :] * hb)
                                              ).astype(o_ref.dtype)
            return carry

        lax.fori_loop(0, n2 // (FFT_CH * FFT_SETS), body, _twiddle_init(n1))


def _hy_conv(z, x0u, hbias, kf_re, kf_im, cst, n1, n2):
    B, L, _ = z.shape
    assert B % 2 == 0
    nw = HY_WIDTH // LANES
    seq = pl.BlockSpec((2, L, LANES), lambda b, w, c: (b, 0, w), pipeline_mode=pl.Buffered(1))
    seq_z = pl.BlockSpec((2, L, LANES), lambda b, w, c: (b, 0, w))
    kf_spec = pl.BlockSpec((None, FFT_SETS, n2, FFT_CH * LANES), lambda b, w, c: (w, c, 0, 0))
    full = lambda a: pl.BlockSpec(a.shape, lambda b, w, c: (0,) * a.ndim)
    consts = [*cst["f1"], *cst["m2"], *cst["m2i"], *cst["fi"], *cst["tw"]]
    return pl.pallas_call(
        functools.partial(_hy_conv_kernel, n1=n1, n2=n2),
        grid=(B // 2, nw, n1 // (FFT_CH * FFT_SETS)),
        in_specs=[seq_z, seq, pl.BlockSpec((1, LANES), lambda b, w, c: (0, w)), kf_spec, kf_spec]
                 + [full(a) for a in consts],
        out_specs=seq,
        out_shape=jax.ShapeDtypeStruct((B, L, HY_WIDTH), F32),
        scratch_shapes=[pltpu.VMEM((2 * n1 * n2, LANES), F32)],
        compiler_params=_cparams(("parallel", "parallel", "arbitrary"), 58),
        name="hyena_conv",
    )(z, x0u, hbias, kf_re, kf_im, *consts)


T_D0, T_D1, T_WQF, T_WQB, T_WKF, T_WKB, T_GCF, T_GCB, T_BD = range(9)


def _ret_tables_kernel(rl_ref, rh_ref, t_ref):
    C = RET_CHUNK
    i = lax.broadcasted_iota(jnp.int32, (C, LANES), 0).astype(F32)
    jn = lax.broadcasted_iota(jnp.int32, (C, LANES), 1).astype(F32)
    log_g = lambda e: jnp.log1p(-jnp.exp2(-e))
    diff = i - jn
    for hp in range(2):
        lgf = log_g(rh_ref[hp, 0:1, :])
        lgb = log_g(rh_ref[hp, 1:2, :])
        fwd = jnp.exp(jnp.where(diff >= 0, diff, 0.0) * lgf)
        bwd = jnp.exp(jnp.where(diff < 0, -diff, 0.0) * lgb)
        t_ref[T_D0 + hp] = jnp.where(diff >= 0, fwd, bwd)
    lf = log_g(rl_ref[0:1, :])
    lb = log_g(rl_ref[1:2, :])
    t_ref[T_WQF] = jnp.exp((i + 1.0) * lf)
    t_ref[T_WQB] = jnp.exp((C - i) * lb)
    t_ref[T_WKF] = jnp.exp((C - 1.0 - i) * lf)
    t_ref[T_WKB] = jnp.exp(i * lb)
    bd = jnp.where((i < HEAD_DIM) == (jn < HEAD_DIM), 1.0, 0.0)
    t_ref[T_GCF] = jnp.exp(C * lf) * bd
    t_ref[T_GCB] = jnp.exp(C * lb) * bd
    t_ref[T_BD] = bd


def _ret_tables(rde):
    per_lane = jnp.repeat(rde.reshape(2, 2, 2), HEAD_DIM, axis=2)
    per_lane = per_lane.transpose(1, 0, 2)
    per_head = jnp.broadcast_to(rde.reshape(2, 2, 2, 1), (2, 2, 2, LANES))
    per_head = per_head.transpose(1, 2, 0, 3)
    return pl.pallas_call(
        _ret_tables_kernel,
        grid=(2,),
        in_specs=[pl.BlockSpec((None, 2, LANES), lambda j: (j, 0, 0)),
                  pl.BlockSpec((None, 2, 2, LANES), lambda j: (j, 0, 0, 0))],
        out_specs=pl.BlockSpec((None, 9, RET_CHUNK, LANES), lambda j: (j, 0, 0, 0)),
        out_shape=jax.ShapeDtypeStruct((2, 9, RET_CHUNK, LANES), F32),
        compiler_params=_cparams(("parallel",), 32),
        name="retention_tables",
    )(per_lane, per_head)


def _ret_state_kernel(rk_ref, rv_ref, c_ref, s_ref, t_ref, st_ref, r_ref):
    C = RET_CHUNK
    nch = rk_ref.shape[0] // C

    @pl.when(pl.program_id(2) == 0)
    def _():
        r_ref[...] = jnp.zeros_like(r_ref)

    wkb = t_ref[T_WKB]
    gcb = t_ref[T_GCB]
    bd = t_ref[T_BD]
    kvs = []
    for n in range(nch):
        sl = slice(n * C, (n + 1) * C)
        k = _rope(rk_ref[sl, :].astype(F32), c_ref[sl, :], s_ref[sl, :]) * (HEAD_DIM ** -0.5)
        kvs.append(lax.dot_general((k * wkb).astype(BF16), rv_ref[sl, :],
                                   (((0,), (0,)), ((), ())), preferred_element_type=F32))
    state = r_ref[...]
    for n in range(nch - 1, -1, -1):
        st_ref[n] = state
        state = gcb * state + bd * kvs[n]
    r_ref[...] = state


def _ret_main_kernel(rq_ref, rk_ref, rv_ref, rg_ref, c_ref, s_ref, t_ref, st_ref, j_ref,
                     o_ref, sf_ref):
    C = RET_CHUNK
    nch = rq_ref.shape[0] // C

    @pl.when(pl.program_id(2) == 0)
    def _():
        sf_ref[...] = jnp.zeros_like(sf_ref)

    lane = lax.broadcasted_iota(jnp.int32, (1, LANES), 1)
    lo64 = lane < HEAD_DIM
    j_bf = j_ref[...]
    chunks = [slice(n * C, (n + 1) * C) for n in range(nch)]
    nt = (((1,), (1,)), ((), ()))
    tn = (((0,), (0,)), ((), ()))
    q, k, vb = [], [], []
    for sl in chunks:
        cs, sn = c_ref[sl, :], s_ref[sl, :]
        q.append(_rope(rq_ref[sl, :].astype(F32), cs, sn))
        k.append(_rope(rk_ref[sl, :].astype(F32), cs, sn) * (HEAD_DIM ** -0.5))
        vb.append(rv_ref[sl, :])
    scores = []
    for n in range(nch):
        qb = q[n].astype(BF16)
        kb = k[n].astype(BF16)
        zero = jnp.zeros_like(qb)
        scores.append([lax.dot_general(jnp.where(lo64, qb, zero), kb, nt,
                                       preferred_element_type=F32),
                       lax.dot_general(jnp.where(lo64, zero, qb), kb, nt,
                                       preferred_element_type=F32)])
    kvs = [lax.dot_general((k[n] * t_ref[T_WKF]).astype(BF16), vb[n], tn,
                           preferred_element_type=F32) for n in range(nch)]
    cross_b = [jnp.dot((q[n] * t_ref[T_WQB]).astype(BF16), st_ref[n].astype(BF16),
                       preferred_element_type=F32) for n in range(nch)]
    intra = []
    for n in range(nch):
        parts = [jnp.dot((scores[n][hp] * t_ref[T_D0 + hp]).astype(BF16), vb[n],
                         preferred_element_type=F32) for hp in range(2)]
        intra.append(jnp.where(lo64, parts[0], parts[1]))
    state = sf_ref[...]
    states = []
    for n in range(nch):
        states.append(state.astype(BF16))
        state = t_ref[T_GCF] * state + t_ref[T_BD] * kvs[n]
    sf_ref[...] = state
    outs = [intra[n] + cross_b[n]
            + jnp.dot((q[n] * t_ref[T_WQF]).astype(BF16), states[n], preferred_element_type=F32)
            for n in range(nch)]
    mean_sq = [_head_mean_sq(o, j_bf) for o in outs]
    for n, sl in enumerate(chunks):
        ret = outs[n] * lax.rsqrt(mean_sq[n] + NORM_EPS)
        g = rg_ref[sl, :].astype(F32)
        o_ref[sl, :] = (ret * (g * _sigmoid(g))).astype(o_ref.dtype)


def _retention(p3, cos_t, sin_t, tables, j_bf):
    B, L, _ = p3.shape
    C = RET_CHUNK
    tb = min(4096, L)
    nblk = L // tb
    nch = tb // C
    nw = RET_W // LANES
    col = lambda piece: (lambda b, j, s: (b, s, nw * piece + j))
    colr = lambda piece: (lambda b, j, s: (b, nblk - 1 - s, nw * piece + j))
    seq = lambda fn: pl.BlockSpec((None, tb, LANES), fn)
    tab_spec = pl.BlockSpec((None, 9, C, LANES), lambda b, j, s: (j, 0, 0, 0))
    states = pl.pallas_call(
        _ret_state_kernel,
        grid=(B, nw, nblk),
        in_specs=[seq(colr(1)), seq(colr(2)),
                  pl.BlockSpec((tb, LANES), lambda b, j, s: (nblk - 1 - s, 0)),
                  pl.BlockSpec((tb, LANES), lambda b, j, s: (nblk - 1 - s, 0)),
                  tab_spec],
        out_specs=pl.BlockSpec((None, None, nch, LANES, LANES),
                               lambda b, j, s: (b, j, nblk - 1 - s, 0, 0)),
        out_shape=jax.ShapeDtypeStruct((B, nw, L // C, LANES, LANES), F32),
        scratch_shapes=[pltpu.VMEM((LANES, LANES), F32)],
        compiler_params=_cparams(("parallel", "parallel", "arbitrary"), 32),
        name="retention_state",
    )(p3, p3, cos_t, sin_t, tables)
    return pl.pallas_call(
        _ret_main_kernel,
        grid=(B, nw, nblk),
        in_specs=[seq(col(0)), seq(col(1)), seq(col(2)), seq(col(3)),
                  pl.BlockSpec((tb, LANES), lambda b, j, s: (s, 0)),
                  pl.BlockSpec((tb, LANES), lambda b, j, s: (s, 0)),
                  tab_spec,
                  pl.BlockSpec((None, None, nch, LANES, LANES), lambda b, j, s: (b, j, s, 0, 0)),
                  pl.BlockSpec((LANES, LANES), lambda b, j, s: (0, 0))],
        out_specs=pl.BlockSpec((None, tb, LANES), lambda b, j, s: (b, s, j)),
        out_shape=jax.ShapeDtypeStruct((B, L, RET_W), BF16),
        scratch_shapes=[pltpu.VMEM((LANES, LANES), F32)],
        compiler_params=_cparams(("parallel", "parallel", "arbitrary"), 32),
        name="retention_main",
    )(p3, p3, p3, p3, cos_t, sin_t, tables, states, j_bf)


def _merge_kernel(oa_ref, ob_ref, oc_ref, od_ref, g0_ref, g1_ref, g2_ref, g3_ref,
                  wb_ref, wo_ref, x_ref, gn_ref, h_ref):
    tm = x_ref.shape[0]
    nparts = 2 if tm % (2 * HALO) == 0 else 1
    parts = [slice(i * tm // nparts, (i + 1) * tm // nparts) for i in range(nparts)]
    branches = ((oa_ref, g0_ref), (ob_ref, g1_ref), (oc_ref, g2_ref), (od_ref, g3_ref))
    proj = [[jnp.dot(br[rows, :].astype(BF16), wb_ref[n], preferred_element_type=F32)
             for n, (br, _) in enumerate(branches)] for rows in parts]
    merged = []
    for pi, rows in enumerate(parts):
        acc = None
        for n, (_, gr) in enumerate(branches):
            term = gr[rows, :].astype(F32) * proj[pi][n]
            acc = term if acc is None else acc + term
        merged.append(acc.astype(BF16))
    ys = [jnp.dot(m, wo_ref[...], preferred_element_type=F32) for m in merged]
    for rows, y in zip(parts, ys):
        ms = jnp.mean(y * y, axis=-1, keepdims=True)
        h_ref[rows, :] = x_ref[rows, :] + y * lax.rsqrt(ms + NORM_EPS) * gn_ref[...]


def _merge(oa, ob, oc, od, p2, wb_bf, wo_bf, x2, gn):
    T = x2.shape[0]
    tm = min(1024, T)
    br = pl.BlockSpec((tm, BRANCH_W), lambda i: (i, 0))
    gate = lambda n: pl.BlockSpec((tm, D_MODEL), lambda i, n=n: (i, n))
    row = pl.BlockSpec((tm, D_MODEL), lambda i: (i, 0))
    return pl.pallas_call(
        _merge_kernel,
        grid=(T // tm,),
        in_specs=[br, br, br, br, gate(0), gate(1), gate(2), gate(3),
                  pl.BlockSpec((N_BRANCH, BRANCH_W, D_MODEL), lambda i: (0, 0, 0)),
                  pl.BlockSpec((D_MODEL, D_MODEL), lambda i: (0, 0)),
                  row, pl.BlockSpec((1, D_MODEL), lambda i: (0, 0))],
        out_specs=row,
        out_shape=jax.ShapeDtypeStruct((T, D_MODEL), F32),
        compiler_params=_cparams(("parallel",), 48),
        name="merge_out_proj",
    )(oa, ob, oc, od, p2, p2, p2, p2, wb_bf, wo_bf, x2, gn)


def _ffn_kernel(h_ref, g2_ref, wg_ref, wu_ref, wd_ref, g3_ref, o_ref):
    tm = h_ref.shape[0]
    halves = [slice(0, tm // 2), slice(tm // 2, tm)] if tm % (2 * HALO) == 0 else [slice(0, tm)]
    hns = []
    for rows in halves:
        h = h_ref[rows, :]
        ms = jnp.mean(h * h, axis=-1, keepdims=True)
        hns.append((h * lax.rsqrt(ms + NORM_EPS) * g2_ref[...]).astype(BF16))
    fs = [None] * len(halves)
    for lo, hi in _col_chunks(D_FF, 2):
        gus = [(jnp.dot(hn, wg_ref[:, lo:hi], preferred_element_type=F32),
                jnp.dot(hn, wu_ref[:, lo:hi], preferred_element_type=F32)) for hn in hns]
        acts = [(g * _sigmoid(g) * u).astype(BF16) for g, u in gus]
        for i, a in enumerate(acts):
            part = jnp.dot(a, wd_ref[lo:hi, :], preferred_element_type=F32)
            fs[i] = part if fs[i] is None else fs[i] + part
    for rows, f in zip(halves, fs):
        ms = jnp.mean(f * f, axis=-1, keepdims=True)
        o_ref[rows, :] = h_ref[rows, :] + f * lax.rsqrt(ms + NORM_EPS) * g3_ref[...]


def _ffn(h2, g2, wg_bf, wu_bf, wd_bf, g3):
    T = h2.shape[0]
    tm = min(512, T)
    row = pl.BlockSpec((tm, D_MODEL), lambda i: (i, 0))
    vec = pl.BlockSpec((1, D_MODEL), lambda i: (0, 0))
    return pl.pallas_call(
        _ffn_kernel,
        grid=(T // tm,),
        in_specs=[row, vec,
                  pl.BlockSpec((D_MODEL, D_FF), lambda i: (0, 0)),
                  pl.BlockSpec((D_MODEL, D_FF), lambda i: (0, 0)),
                  pl.BlockSpec((D_FF, D_MODEL), lambda i: (0, 0)),
                  vec],
        out_specs=row,
        out_shape=jax.ShapeDtypeStruct((T, D_MODEL), F32),
        compiler_params=_cparams(("parallel",), 48),
        name="ffn",
    )(h2, g2, wg_bf, wu_bf, wd_bf, g3)


def _rope_tables(L):
    rows = L // GRID_W
    r = jnp.repeat(jnp.arange(rows, dtype=F32), GRID_W)
    c = jnp.tile(jnp.arange(GRID_W, dtype=F32), rows)
    inv = ROPE_BASE ** (-jnp.arange(ROPE_FREQS, dtype=F32) / ROPE_FREQS)
    ar = r[:, None] * inv
    ac = c[:, None] * inv
    cos64 = jnp.concatenate([jnp.cos(ar), jnp.cos(ar), jnp.cos(ac), jnp.cos(ac)], axis=1)
    sin64 = jnp.concatenate([-jnp.sin(ar), jnp.sin(ar), -jnp.sin(ac), jnp.sin(ac)], axis=1)
    return jnp.tile(cos64, (1, 2)), jnp.tile(sin64, (1, 2))


def _filter_features(L):
    t = np.linspace(0.0, 1.0, L)[:, None]
    f = np.linspace(1e-4, HY_BANDS - 1, HY_BANDS)
    ang = (2.0 * math.pi / L) * np.arange(L)[:, None] * f[None, :]
    z = np.concatenate([t, np.cos(ang), -np.sin(ang)], axis=-1)
    z = np.pad(z, ((0, 0), (0, HY_FILTER_HIDDEN - HY_EMB)))
    back = np.concatenate([[0], np.arange(L - 1, 0, -1)])
    return jnp.asarray(np.concatenate([z, z[back]], axis=1).astype(np.float32))


def _head_mean_matrix():
    i = np.arange(LANES)
    j = ((i[:, None] // HEAD_DIM) == (i[None, :] // HEAD_DIM)).astype(np.float32) / HEAD_DIM
    return jnp.asarray(j.astype(BF16))


def _halve_gate_columns(w_in_bf):
    scale = jnp.where(jnp.arange(IN_COLS) >= GATE_OFF, 0.5, 1.0).astype(BF16)
    return w_in_bf * scale


def _trunk(x, wts, fft_dims=None):
    B, L, _ = x.shape
    T = B * L
    n1, n2 = fft_dims or _fft_dims(L)
    cst = _fft_consts(n1, n2, n1 // 2)
    cos_t, sin_t = _rope_tables(L)
    z2 = _filter_features(L)
    deltas = jnp.abs(jnp.linspace(math.log(HY_TARGET) / HY_SLOW_DECAY,
                                  math.log(HY_TARGET) / HY_FAST_DECAY, HY_WIDTH, dtype=F32))[None, :]
    j_bf = _head_mean_matrix()
    depth = wts["w_in"].shape[0]
    x2 = x.reshape(T, D_MODEL)
    for l in range(depth):
        ng = wts["norm_gains"][l]
        w1p = jnp.pad(wts["hy_w1"][l], ((0, HY_FILTER_HIDDEN - HY_EMB), (0, 0)))
        k2raw, sumsq = _hy_filter_raw(z2, w1p, wts["hy_b1"][l][None, :], wts["hy_w2"][l],
                                      wts["hy_b2"][l][None, :], wts["hy_w3"][l],
                                      wts["hy_freq"][l], deltas, L)
        kf_re, kf_im = _filter_fft(k2raw, sumsq, cst, n1, n2)

        qg = jnp.tile(wts["qk_norm"][l, 0], 2)[None, :]
        kg = jnp.tile(wts["qk_norm"][l, 1], 2)[None, :]
        qt, k, vt, p_ret, p_gate, z, x0u, out_d = _in_proj(
            x2, ng[0][None, :], wts["w_in_bf"][l], wts["hy_conv_w"][l],
            wts["hy_conv_b"][l][None, :], wts["sc_conv_w"][l], cos_t, sin_t, qg, kg, j_bf, L)
        out_a = _flash(qt, k, vt)

        out_b = _hy_conv(z.reshape(B, L, HY_WIDTH), x0u.reshape(B, L, HY_WIDTH),
                         wts["hy_bias"][l][None, :], kf_re, kf_im, cst, n1, n2)

        tables = _ret_tables(wts["ret_decay_exp"][l])
        out_c = _retention(p_ret.reshape(B, L, 4 * RET_W), cos_t, sin_t, tables, j_bf)

        h2 = _merge(out_a.reshape(T, BRANCH_W), out_b.reshape(T, BRANCH_W),
                    out_c.reshape(T, BRANCH_W), out_d,
                    p_gate, wts["w_branch_bf"][l], wts["w_out_bf"][l], x2, ng[1][None, :])
        x2 = _ffn(h2, ng[2][None, :], wts["w_gate_bf"][l], wts["w_up_bf"][l],
                  wts["w_ffn_out_bf"][l], ng[3][None, :])
    return x2.reshape(B, L, D_MODEL)


def kernel(x_prompt, x_sample, norm_gains, w_in, qk_norm, hy_conv_w, hy_conv_b, hy_w1, hy_b1, hy_w2,
           hy_b2, hy_w3, hy_freq, hy_bias, ret_decay_exp, sc_conv_w, w_branch, w_out, w_ffn_in,
           w_ffn_out):
    wts = dict(norm_gains=norm_gains, w_in=w_in, qk_norm=qk_norm, hy_conv_w=hy_conv_w,
               hy_conv_b=hy_conv_b, hy_w1=hy_w1, hy_b1=hy_b1, hy_w2=hy_w2, hy_b2=hy_b2,
               hy_w3=hy_w3, hy_freq=hy_freq, hy_bias=hy_bias, ret_decay_exp=ret_decay_exp,
               sc_conv_w=sc_conv_w,
               w_in_bf=_halve_gate_columns(w_in.astype(BF16)), w_branch_bf=w_branch.astype(BF16),
               w_out_bf=w_out.astype(BF16), w_gate_bf=w_ffn_in[..., :D_FF].astype(BF16),
               w_up_bf=w_ffn_in[..., D_FF:].astype(BF16), w_ffn_out_bf=w_ffn_out.astype(BF16))
    return _trunk(x_prompt, wts), _trunk(x_sample, wts)
```

```python
import functools
import math

import numpy as np
import jax
import jax.numpy as jnp
from jax import lax
from jax.experimental import pallas as pl
from jax.experimental.pallas import tpu as pltpu

F32 = jnp.float32
BF16 = jnp.bfloat16

D_MODEL = 1024
GRID_W = 64
N_BRANCH = 4
BRANCH_W = 256
HEAD_DIM = 64
ATT_Q_HEADS = 4
ATT_KV_HEADS = 2
ROPE_BASE = 10000.0
ROPE_FREQS = HEAD_DIM // 4
HY_WIDTH = BRANCH_W
HY_EMB = 33
HY_BANDS = (HY_EMB - 1) // 2
HY_FILTER_HIDDEN = 64
HY_FAST_DECAY = 0.3
HY_SLOW_DECAY = 1.5
HY_TARGET = 1e-2
RET_HEADS = 4
RET_W = RET_HEADS * HEAD_DIM
RET_CHUNK = 128
SC_WIDTH = BRANCH_W
D_FF = 2816
NORM_EPS = 1e-6

ATT_Q_W = ATT_Q_HEADS * HEAD_DIM
ATT_KV_W = ATT_KV_HEADS * HEAD_DIM
A_K_OFF = ATT_Q_W
A_V_OFF = A_K_OFF + ATT_KV_W
HY_OFF = A_V_OFF + ATT_KV_W
RET_OFF = HY_OFF + 3 * HY_WIDTH
SC_OFF = RET_OFF + 4 * RET_W
GATE_OFF = SC_OFF + 3 * SC_WIDTH
IN_COLS = GATE_OFF + N_BRANCH * D_MODEL

LANES = 128
SUBLANES = 8
MXU_W = 256
HALO = 2 * SUBLANES
IN_TM = 512
ATT_KB = 512
ATT_TQ = 256
VT_ROWS = HEAD_DIM + HALO
Q_SCALE = HEAD_DIM ** -0.5 * math.log2(math.e)
FFT_PASSES_FILTER = 3
FFT_PASSES_DATA = 1
FFT_CH = 8
FFT_SETS = 1
MIB = 1 << 20


def _cparams(sem, vmem_mib):
    return pltpu.CompilerParams(dimension_semantics=sem, vmem_limit_bytes=vmem_mib * MIB)


def _sigmoid(x):
    return 0.5 * jnp.tanh(0.5 * x) + 0.5


def _split(x):
    hi = x.astype(BF16)
    lo = (x - hi.astype(F32)).astype(BF16)
    return hi, lo


def _np_split(a64):
    a32 = np.asarray(a64, np.float32)
    hi = a32.astype(BF16)
    lo = (a32 - hi.astype(np.float32)).astype(BF16)
    return jnp.asarray(hi), jnp.asarray(lo)


def _mm_const(a_hi, a_lo, x, passes):
    if passes == 1:
        return jnp.dot(a_hi, x.astype(BF16), preferred_element_type=F32)
    xh, xl = _split(x)
    out = jnp.dot(a_hi, xh, preferred_element_type=F32)
    if passes >= 3:
        out = out + jnp.dot(a_lo, xh, preferred_element_type=F32)
        out = out + jnp.dot(a_hi, xl, preferred_element_type=F32)
    return out


def _mm3(a, b):
    ah, al = _split(a)
    bh, bl = _split(b)
    out = jnp.dot(ah, bh, preferred_element_type=F32)
    out = out + jnp.dot(al, bh, preferred_element_type=F32)
    return out + jnp.dot(ah, bl, preferred_element_type=F32)


def _head_mean_sq(x, j_bf):
    hi, lo = _split(x * x)
    return (jnp.dot(hi, j_bf, preferred_element_type=F32)
            + jnp.dot(lo, j_bf, preferred_element_type=F32))


def _rope(x, c, s):
    lane = lax.broadcasted_iota(jnp.int32, (1, LANES), 1)
    is_b = (lane & 16) != 0
    partner = jnp.where(is_b, pltpu.roll(x, 16, 1), pltpu.roll(x, LANES - 16, 1))
    return x * c + partner * s


def _col_chunks(n, parts):
    tiles = n // MXU_W
    bounds = [MXU_W * (tiles * i // parts) for i in range(parts + 1)]
    return list(zip(bounds[:-1], bounds[1:]))


def _conv3_rows(main, before, after, w):
    tb = main.shape[0]
    w0, w1, w2 = w[0:1, :], w[1:2, :], w[2:3, :]
    body = pltpu.roll(main, 1, 0) * w0 + main * w1 + pltpu.roll(main, tb - 1, 0) * w2
    row_first = before * w0 + main[0:1, :] * w1 + main[1:2, :] * w2
    row_last = main[tb - 2:tb - 1, :] * w0 + main[tb - 1:tb, :] * w1 + after * w2
    row = lax.broadcasted_iota(jnp.int32, (SUBLANES, 1), 0)
    head = jnp.where(row == 0, row_first, body[:SUBLANES])
    tail = jnp.where(row == SUBLANES - 1, row_last, body[tb - SUBLANES:])
    return jnp.concatenate([head, body[SUBLANES:tb - SUBLANES], tail], axis=0)


def _in_proj_kernel(x_ref, xp_ref, xn_ref, g_ref, w_ref, hcw_ref, hcb_ref, scw_ref,
                    c_ref, s_ref, qg_ref, kg_ref, j_ref,
                    qt_ref, k_ref, vt_ref, pr_ref, pg_ref, z_ref, x0_ref, od_ref, *,
                    tiles_per_seq):
    i = pl.program_id(0)
    first = (i % tiles_per_seq) == 0
    last = (i % tiles_per_seq) == tiles_per_seq - 1
    g = g_ref[...]

    def normed(x):
        ms = jnp.mean(x * x, axis=-1, keepdims=True)
        return (x * lax.rsqrt(ms + NORM_EPS) * g).astype(BF16)

    dot = lambda a, lo, hi: jnp.dot(a, w_ref[:, lo:hi], preferred_element_type=F32)
    xn = normed(x_ref[...])
    halo = normed(jnp.concatenate([xp_ref[...], xn_ref[...]], axis=0))
    edge = lambda h: (jnp.where(first, 0.0, h[SUBLANES - 1:SUBLANES, :]),
                      jnp.where(last, 0.0, h[SUBLANES:SUBLANES + 1, :]))

    head = dot(xn, 0, RET_OFF)
    hy_halo = dot(halo, HY_OFF, RET_OFF)
    sc = dot(xn, SC_OFF, GATE_OFF)
    sc_halo = dot(halo, SC_OFF + SC_WIDTH, GATE_OFF)
    ret = dot(xn, RET_OFF, SC_OFF)
    cs, sn = c_ref[...], s_ref[...]
    for blk in range(2 * RET_W // LANES):
        cols = slice(LANES * blk, LANES * (blk + 1))
        scale = 1.0 if blk < RET_W // LANES else HEAD_DIM ** -0.5
        pr_ref[:, cols] = (_rope(ret[:, cols], cs, sn) * scale).astype(pr_ref.dtype)
    pr_ref[:, 2 * RET_W:] = ret[:, 2 * RET_W:].astype(pr_ref.dtype)
    _attn_prep_tile(head[:, :HY_OFF], c_ref, s_ref, qg_ref, kg_ref, j_ref, qt_ref, k_ref, vt_ref)

    u = _conv3_rows(head[:, HY_OFF:], *edge(hy_halo), hcw_ref[...]) + hcb_ref[...]
    x0_ref[...] = u[:, :HY_WIDTH]
    z_ref[...] = u[:, 2 * HY_WIDTH:] * u[:, HY_WIDTH:2 * HY_WIDTH]
    m = sc[:, SC_WIDTH:2 * SC_WIDTH] * sc[:, 2 * SC_WIDTH:]
    mh = sc_halo[:, :SC_WIDTH] * sc_halo[:, SC_WIDTH:]
    od_ref[...] = (sc[:, :SC_WIDTH] * _conv3_rows(m, *edge(mh), scw_ref[...])).astype(od_ref.dtype)

    for lo, hi in _col_chunks(N_BRANCH * D_MODEL, 2):
        g_half = dot(xn, GATE_OFF + lo, GATE_OFF + hi).astype(pg_ref.dtype)
        pg_ref[:, lo:hi] = 0.5 * jnp.tanh(g_half) + 0.5


def _in_proj(x2, g, w_bf, hcw, hcb, scw, cos_t, sin_t, qg, kg, j_bf, L):
    T = x2.shape[0]
    B = T // L
    tm = min(IN_TM, L)
    tq = min(ATT_TQ, L)
    tps = L // tm
    kb = min(ATT_KB, L)
    per_kb = kb // tm
    r8 = tm // SUBLANES
    row = lambda w: pl.BlockSpec((tm, w), lambda i: (i, 0))
    full = lambda a: pl.BlockSpec(a.shape, lambda i: (0,) * a.ndim)
    tab = pl.BlockSpec((tm, LANES), lambda i: (i % tps, 0))
    sds = lambda w, dt: jax.ShapeDtypeStruct((T, w), dt)
    return pl.pallas_call(
        functools.partial(_in_proj_kernel, tiles_per_seq=tps),
        grid=(T // tm,),
        in_specs=[row(D_MODEL),
                  pl.BlockSpec((SUBLANES, D_MODEL), lambda i: (jnp.maximum(i * r8 - 1, 0), 0)),
                  pl.BlockSpec((SUBLANES, D_MODEL),
                               lambda i: (jnp.minimum((i + 1) * r8, T // SUBLANES - 1), 0)),
                  full(g), full(w_bf), full(hcw), full(hcb), full(scw),
                  tab, tab, full(qg), full(kg), full(j_bf)],
        out_specs=[pl.BlockSpec((None, ATT_Q_HEADS, tm // tq, LANES, tq),
                                lambda i: (i // tps, 0, i % tps, 0, 0)),
                   pl.BlockSpec((None, tm, LANES), lambda i: (i // tps, i % tps, 0)),
                   pl.BlockSpec((None, None, ATT_KV_HEADS * VT_ROWS, tm),
                                lambda i: (i // tps, (i % tps) // per_kb, 0, (i % tps) % per_kb)),
                   row(RET_W * 4), row(N_BRANCH * D_MODEL),
                   row(HY_WIDTH), row(HY_WIDTH), row(SC_WIDTH)],
        out_shape=[jax.ShapeDtypeStruct((B, ATT_Q_HEADS, L // tq, LANES, tq), BF16),
                   jax.ShapeDtypeStruct((B, L, LANES), BF16),
                   jax.ShapeDtypeStruct((B, L // kb, ATT_KV_HEADS * VT_ROWS, kb), BF16),
                   sds(RET_W * 4, BF16), sds(N_BRANCH * D_MODEL, BF16),
                   sds(HY_WIDTH, F32), sds(HY_WIDTH, F32), sds(SC_WIDTH, BF16)],
        compiler_params=_cparams(("parallel",), 48),
        name="in_proj",
    )(x2, x2, x2, g, w_bf, hcw, hcb, scw, cos_t, sin_t, qg, kg, j_bf)


def _attn_prep_tile(qkv, c_ref, s_ref, qg_ref, kg_ref, j_ref, qt_ref, k_ref, vt_ref):
    c = c_ref[...]
    s = s_ref[...]
    j_bf = j_ref[...]
    tm = qkv.shape[0]
    tq = qt_ref.shape[-1]
    zeros = jnp.zeros((HEAD_DIM, tm), BF16)
    for blk in range(2):
        q = qkv[:, LANES * blk:LANES * (blk + 1)]
        qn = q * lax.rsqrt(_head_mean_sq(q, j_bf) + NORM_EPS) * qg_ref[...]
        qt = (_rope(qn, c, s) * Q_SCALE).astype(BF16).T
        lo, hi = qt[:HEAD_DIM], qt[HEAD_DIM:]
        if blk == 0:
            heads = (jnp.concatenate([lo, zeros], axis=0), jnp.concatenate([hi, zeros], axis=0))
        else:
            heads = (jnp.concatenate([zeros, lo], axis=0), jnp.concatenate([zeros, hi], axis=0))
        for hp, qh in enumerate(heads):
            for t in range(tm // tq):
                qt_ref[2 * blk + hp, t] = qh[:, t * tq:(t + 1) * tq]
    k = qkv[:, A_K_OFF:A_K_OFF + LANES]
    kn = k * lax.rsqrt(_head_mean_sq(k, j_bf) + NORM_EPS) * kg_ref[...]
    k_ref[...] = _rope(kn, c, s).astype(BF16)
    v = qkv[:, A_V_OFF:A_V_OFF + LANES].astype(BF16)
    ones = jnp.ones((VT_ROWS - HEAD_DIM, tm), BF16)
    vt = v.T
    vt_ref[...] = jnp.concatenate([vt[:HEAD_DIM], ones, vt[HEAD_DIM:], ones], axis=0)


def _flash_kernel(qt_ref, k_ref, vt_ref, o_ref, sa_ref, ma_ref, sb_ref, mb_ref, *,
                  ngrp, nsub, ntile):
    tq = ATT_TQ
    kb = vt_ref.shape[-1]
    group = ATT_Q_HEADS // ATT_KV_HEADS

    def produce(slot, qi, g, t, h):
        s_ref, mx_ref = slot
        st = pl.multiple_of((g * nsub + t) * kb, kb)
        s = jnp.dot(k_ref[pl.ds(st, kb), :], qt_ref[h, qi],
                    preferred_element_type=F32)
        s_ref[t * ATT_Q_HEADS + h] = s
        mx_ref[t * ATT_Q_HEADS + h] = jnp.max(s, axis=0, keepdims=True)

    def consume(slot, g, t, h, state):
        s_ref, mx_ref = slot
        m, acc = state
        m_new = jnp.maximum(m, mx_ref[t * ATT_Q_HEADS + h])
        alpha = jnp.exp2(m - m_new)
        p = jnp.exp2((s_ref[t * ATT_Q_HEADS + h] - m_new).astype(BF16))
        j = h // group
        vt = vt_ref[g * nsub + t, VT_ROWS * j:VT_ROWS * (j + 1), :]
        return m_new, alpha * acc + jnp.dot(vt, p, preferred_element_type=F32)

    def step(slot_in, g_in, slot_out, qi_out, g_out, carry):
        carry = list(carry)
        for t in range(nsub):
            for h in range(ATT_Q_HEADS):
                if slot_out is not None:
                    produce(slot_out, qi_out, g_out, t, h)
                if slot_in is not None:
                    carry[h] = consume(slot_in, g_in, t, h, carry[h])
        return tuple(carry)

    def fresh():
        return tuple((jnp.full((1, tq), -jnp.inf, F32), jnp.zeros((VT_ROWS, tq), F32))
                     for _ in range(ATT_Q_HEADS))

    def finalize(qi, carry):
        outs = [acc[:HEAD_DIM] / acc[HEAD_DIM:HEAD_DIM + 1] for _, acc in carry]
        rows = pl.ds(pl.multiple_of(qi * tq, tq), tq)
        o_ref[rows, :] = jnp.concatenate(outs, axis=0).T.astype(o_ref.dtype)

    slot_a = (sa_ref, ma_ref)
    slot_b = (sb_ref, mb_ref)

    if ngrp == 1:
        def tile(qi, _):
            step(None, None, slot_a, qi, 0, ())
            finalize(qi, step(slot_a, 0, None, None, None, fresh()))
            return 0
    else:
        step(None, None, slot_a, 0, 0, ())

        def tile(qi, _):
            def body(i, carry):
                g = 2 * i
                carry = step(slot_a, g, slot_b, qi, g + 1, carry)
                return step(slot_b, g + 1, slot_a, qi, g + 2, carry)

            carry = lax.fori_loop(0, ngrp // 2 - 1, body, fresh())
            carry = step(slot_a, ngrp - 2, slot_b, qi, ngrp - 1, carry)
            carry = step(slot_b, ngrp - 1, slot_a, jnp.minimum(qi + 1, ntile - 1), 0, carry)
            finalize(qi, carry)
            return 0

    lax.fori_loop(0, ntile, tile, 0)


def _flash(qt, k, vt):
    B, _, nqt, _, tq = qt.shape
    L = nqt * tq
    ntile = math.gcd(nqt, 8)
    kb = vt.shape[-1]
    nkb = L // kb
    nsub = 2 if nkb % 4 == 0 and nkb >= 8 else 1
    ngrp = L // (kb * nsub)
    assert ngrp == 1 or ngrp % 2 == 0
    s_scratch = pltpu.VMEM((nsub * ATT_Q_HEADS, kb, tq), F32)
    m_scratch = pltpu.VMEM((nsub * ATT_Q_HEADS, 1, tq), F32)
    return pl.pallas_call(
        functools.partial(_flash_kernel, ngrp=ngrp, nsub=nsub, ntile=ntile),
        grid=(B, nqt // ntile),
        in_specs=[pl.BlockSpec((None, ATT_Q_HEADS, ntile, LANES, tq),
                               lambda b, i: (b, 0, i, 0, 0)),
                  pl.BlockSpec((None, L, LANES), lambda b, i: (b, 0, 0)),
                  pl.BlockSpec((None, nkb, ATT_KV_HEADS * VT_ROWS, kb),
                               lambda b, i: (b, 0, 0, 0))],
        out_specs=pl.BlockSpec((None, ntile * tq, ATT_Q_W), lambda b, i: (b, i, 0)),
        out_shape=jax.ShapeDtypeStruct((B, L, ATT_Q_W), BF16),
        scratch_shapes=[s_scratch, m_scratch, s_scratch, m_scratch],
        compiler_params=_cparams(("parallel", "arbitrary"), 48),
        name="flash_attn",
    )(qt, k, vt)


def _hy_filter_kernel(z_ref, w1_ref, b1_ref, w2_ref, b2_ref, w3_ref, fr_ref, dl_ref,
                      k_ref, ss_ref):
    step = pl.program_id(0)
    tr = z_ref.shape[0]
    H = HY_FILTER_HIDDEN
    z = z_ref[...]
    h = jnp.sin(fr_ref[0:1, :] * (_mm3(z, w1_ref[...]) + b1_ref[...]))
    h = jnp.sin(fr_ref[1:2, :] * (_mm3(h, w2_ref[...]) + b2_ref[...]))
    h3 = _mm3(h, w3_ref[...])
    fwd = h3[:, :HY_WIDTH] * jnp.exp(-z[:, 0:1] * dl_ref[...])
    bwd = h3[:, HY_WIDTH:] * jnp.exp(-z[:, H:H + 1] * dl_ref[...])
    row = step * tr + lax.broadcasted_iota(jnp.int32, (tr, 1), 0)
    bwd = jnp.where(row == 0, 0.0, bwd)
    k_ref[0] = fwd
    k_ref[1] = bwd

    @pl.when(step == 0)
    def _():
        ss_ref[...] = jnp.zeros_like(ss_ref)

    ss_ref[...] += jnp.sum(fwd * fwd + bwd * bwd, axis=0, keepdims=True)


def _block_diag2(a, b):
    za = jnp.zeros((a.shape[0], b.shape[1]), a.dtype)
    zb = jnp.zeros((b.shape[0], a.shape[1]), a.dtype)
    return jnp.concatenate([jnp.concatenate([a, za], axis=1),
                            jnp.concatenate([zb, b], axis=1)], axis=0)


def _hy_filter_raw(z2, w1p, b1, w2, b2, w3, freq, deltas, L):
    tr = min(512, L)
    H2 = 2 * HY_FILTER_HIDDEN
    two = lambda v: jnp.tile(v, (1, 2))
    args = (z2, _block_diag2(w1p, w1p), two(b1), _block_diag2(w2, w2), two(b2),
            _block_diag2(w3[:, :HY_WIDTH], w3[:, HY_WIDTH:]), two(freq), deltas)
    full = lambda a: pl.BlockSpec(a.shape, lambda i: (0, 0))
    k2, sumsq = pl.pallas_call(
        _hy_filter_kernel,
        grid=(L // tr,),
        in_specs=[pl.BlockSpec((tr, H2), lambda i: (i, 0))] + [full(a) for a in args[1:]],
        out_specs=[pl.BlockSpec((2, tr, HY_WIDTH), lambda i: (0, i, 0)),
                   pl.BlockSpec((1, HY_WIDTH), lambda i: (0, 0))],
        out_shape=[jax.ShapeDtypeStruct((2, L, HY_WIDTH), F32),
                   jax.ShapeDtypeStruct((1, HY_WIDTH), F32)],
        compiler_params=_cparams(("arbitrary",), 32),
        name="hyena_filter",
    )(*args)
    return k2.reshape(2 * L, HY_WIDTH), sumsq


def _fft_dims(L):
    n = 2 * L
    n2 = 128 if n >= 32 * 128 else 64
    return n // n2, n2


def _fft_consts(n1, n2, r):
    n = n1 * n2
    a1 = 2.0 * np.pi * np.outer(np.arange(n1), np.arange(n1)) / n1
    c1, s1 = np.cos(a1), np.sin(a1)
    a2 = 2.0 * np.pi * np.outer(np.arange(n2), np.arange(n2)) / n2
    c2, s2 = np.cos(a2), np.sin(a2)
    f1 = np.block([[c1[:, :r], s1[:, :r]], [-s1[:, :r], c1[:, :r]]])
    f1_full = np.concatenate([c1, -s1], axis=0)
    m2 = np.block([[c2, s2], [-s2, c2]])
    m2i = np.block([[c2, -s2], [s2, c2]])
    fi = np.block([[c1[:r, :], -s1[:r, :]], [s1[:r, :], c1[:r, :]]])
    aw = 2.0 * np.pi * np.arange(n1) / n
    off = np.repeat(np.arange(FFT_CH), LANES)[None, :] * aw[:, None]
    step = np.broadcast_to((FFT_CH * aw)[:, None], (n1, LANES))
    f32 = lambda a: jnp.asarray(np.asarray(a, np.float32))
    return dict(f1=_np_split(f1), f1_full=_np_split(f1_full), m2=_np_split(m2),
                m2i=_np_split(m2i), fi=_np_split(fi),
                tw=(f32(np.cos(off)), f32(-np.sin(off)), f32(np.cos(step)), f32(-np.sin(step))))


def _cmul(a_re, a_im, b_re, b_im):
    return a_re * b_re - a_im * b_im, a_re * b_im + a_im * b_re


def _lane_tile(x, reps):
    return jnp.concatenate([x] * reps, axis=1)


def _lane_part(x, j):
    return x[:, LANES * j:LANES * (j + 1)]


def _chunk_twiddle(t_re, t_im, tw_refs):
    d_re, d_im, s_re, s_im = (ref[...] for ref in tw_refs)
    cur = _cmul(_lane_tile(t_re, FFT_CH), _lane_tile(t_im, FFT_CH), d_re, d_im)
    return cur, _cmul(t_re, t_im, s_re, s_im)


def _twiddle_init(n1):
    return jnp.ones((n1, LANES), F32), jnp.zeros((n1, LANES), F32)


def _fft_stage1(load_rows, a_ref, f_hi, f_lo, tw_refs, n1, n2, passes):
    def body(ci, carry):
        i0s, tws = [], []
        for u in range(FFT_SETS):
            tw, carry = _chunk_twiddle(*carry, tw_refs)
            tws.append(tw)
            i0s.append((ci * FFT_SETS + u) * FFT_CH)
        prods = [_mm_const(f_hi, f_lo, jnp.concatenate(
            [load_rows(i0 + j) for j in range(FFT_CH)], axis=1), passes)
            for i0 in i0s]
        for i0, (t_re, t_im), a in zip(i0s, tws, prods):
            o_re, o_im = _cmul(a[:n1], a[n1:], t_re, t_im)
            for j in range(FFT_CH):
                base = pl.multiple_of((i0 + j) * 2 * n1, 2 * n1)
                a_ref[pl.ds(base, n1), :] = _lane_part(o_re, j)
                a_ref[pl.ds(base + n1, n1), :] = _lane_part(o_im, j)
        return carry

    lax.fori_loop(0, n2 // (FFT_CH * FFT_SETS), body, _twiddle_init(n1))


def _load_spectrum_rows(a_ref, k1, n1, n2):
    ld = lambda off: a_ref[pl.ds(off, n2, stride=2 * n1), :]
    return jnp.concatenate(
        [jnp.concatenate([ld(k1 + j) for j in range(FFT_CH)], axis=1),
         jnp.concatenate([ld(n1 + k1 + j) for j in range(FFT_CH)], axis=1)], axis=0)


def _filter_fft_kernel(k_ref, ss_ref, f_hi_ref, f_lo_ref, m_hi_ref, m_lo_ref,
                       dre_ref, dim_ref, sre_ref, sim_ref, kr_ref, ki_ref, a_ref, *, n1, n2):
    c = pl.program_id(1)

    @pl.when(c == 0)
    def _():
        _fft_stage1(lambda i: k_ref[pl.ds(i, n1, stride=n2), :], a_ref,
                    f_hi_ref[...], f_lo_ref[...], (dre_ref, dim_ref, sre_ref, sim_ref),
                    n1, n2, FFT_PASSES_FILTER)

    scale = _lane_tile(lax.rsqrt(ss_ref[...] + NORM_EPS) * (1.0 / (n1 * n2)), FFT_CH)
    xin = _load_spectrum_rows(a_ref, c * FFT_CH, n1, n2)
    x = _mm_const(m_hi_ref[...], m_lo_ref[...], xin, FFT_PASSES_FILTER)
    kr_ref[...] = x[:n2] * scale
    ki_ref[...] = x[n2:] * scale


def _filter_fft(k2raw, sumsq, cst, n1, n2):
    n = n1 * n2
    nw = HY_WIDTH // LANES
    consts = [*cst["f1_full"], *cst["m2"], *cst["tw"]]
    full = lambda a: pl.BlockSpec(a.shape, lambda w, c: (0,) * a.ndim)
    out_spec = pl.BlockSpec((None, None, n2, FFT_CH * LANES), lambda w, c: (w, c, 0, 0))
    out_sh = jax.ShapeDtypeStruct((nw, n1 // FFT_CH, n2, FFT_CH * LANES), F32)
    return pl.pallas_call(
        functools.partial(_filter_fft_kernel, n1=n1, n2=n2),
        grid=(nw, n1 // FFT_CH),
        in_specs=[pl.BlockSpec((n, LANES), lambda w, c: (0, w)),
                  pl.BlockSpec((1, LANES), lambda w, c: (0, w))] + [full(a) for a in consts],
        out_specs=[out_spec, out_spec],
        out_shape=[out_sh, out_sh],
        scratch_shapes=[pltpu.VMEM((2 * n, LANES), F32)],
        compiler_params=_cparams(("parallel", "arbitrary"), 48),
        name="hyena_filter_fft",
    )(k2raw, sumsq, *consts)


def _hy_conv_kernel(z_ref, x0_ref, hb_ref, kr_ref, ki_ref, f_hi_ref, f_lo_ref,
                    m_hi_ref, m_lo_ref, mi_hi_ref, mi_lo_ref, fi_hi_ref, fi_lo_ref,
                    dre_ref, dim_ref, sre_ref, sim_ref, o_ref, a_ref, *, n1, n2):
    c = pl.program_id(2)
    r = n1 // 2
    tw_refs = (dre_ref, dim_ref, sre_ref, sim_ref)

    def load_pair(i):
        seq = pl.ds(i, r, stride=n2)
        return jnp.concatenate([z_ref[0, seq, :], z_ref[1, seq, :]], axis=0)

    @pl.when(c == 0)
    def _():
        _fft_stage1(load_pair, a_ref, f_hi_ref[...], f_lo_ref[...], tw_refs, n1, n2,
                    FFT_PASSES_DATA)

    k1s = [(c * FFT_SETS + u) * FFT_CH for u in range(FFT_SETS)]
    xs = [_mm_const(m_hi_ref[...], m_lo_ref[...], _load_spectrum_rows(a_ref, k1, n1, n2),
                    FFT_PASSES_DATA) for k1 in k1s]
    ys = [jnp.concatenate(_cmul(x[:n2], x[n2:], kr_ref[u], ki_ref[u]), axis=0)
          for u, x in enumerate(xs)]
    bqs = [_mm_const(mi_hi_ref[...], mi_lo_ref[...], y, FFT_PASSES_DATA) for y in ys]
    for k1, bq in zip(k1s, bqs):
        for j in range(FFT_CH):
            a_ref[pl.ds(k1 + j, n2, stride=2 * n1), :] = _lane_part(bq[:n2], j)
            a_ref[pl.ds(n1 + k1 + j, n2, stride=2 * n1), :] = _lane_part(bq[n2:], j)

    @pl.when(c == pl.num_programs(2) - 1)
    def _():
        hb = hb_ref[...]

        def body(ci, carry):
            i0s, bts = [], []
            for u in range(FFT_SETS):
                (t_re, t_im), carry = _chunk_twiddle(*carry, tw_refs)
                i0 = (ci * FFT_SETS + u) * FFT_CH
                rows = [pl.multiple_of((i0 + j) * 2 * n1, 2 * n1) for j in range(FFT_CH)]
                br = jnp.concatenate([a_ref[pl.ds(b0, n1), :] for b0 in rows], axis=1)
                bi = jnp.concatenate([a_ref[pl.ds(b0 + n1, n1), :] for b0 in rows], axis=1)
                bts.append(jnp.concatenate([br * t_re + bi * t_im, bi * t_re - br * t_im],
                                           axis=0))
                i0s.append(i0)
            ys = [_mm_const(fi_hi_ref[...], fi_lo_ref[...], bt, FFT_PASSES_DATA) for bt in bts]
            for i0, y in zip(i0s, ys):
                for j in range(FFT_CH):
                    seq = pl.ds(i0 + j, r, stride=n2)
                    for row in range(2):
                        conv = _lane_part(y[row * r:(row + 1) * r], j)
                        o_ref[row, seq, :] = (x0_ref[row, seq, :]
                                              * (conv + z_ref[row, seq, :] * hb)
                                              ).astype(o_ref.dtype)
            return carry

        lax.fori_loop(0, n2 // (FFT_CH * FFT_SETS), body, _twiddle_init(n1))


def _hy_conv(z, x0u, hbias, kf_re, kf_im, cst, n1, n2):
    B, L, _ = z.shape
    assert B % 2 == 0
    nw = HY_WIDTH // LANES
    seq = pl.BlockSpec((2, L, LANES), lambda b, w, c: (b, 0, w), pipeline_mode=pl.Buffered(1))
    seq_z = pl.BlockSpec((2, L, LANES), lambda b, w, c: (b, 0, w))
    kf_spec = pl.BlockSpec((None, FFT_SETS, n2, FFT_CH * LANES), lambda b, w, c: (w, c, 0, 0))
    full = lambda a: pl.BlockSpec(a.shape, lambda b, w, c: (0,) * a.ndim)
    consts = [*cst["f1"], *cst["m2"], *cst["m2i"], *cst["fi"], *cst["tw"]]
    return pl.pallas_call(
        functools.partial(_hy_conv_kernel, n1=n1, n2=n2),
        grid=(B // 2, nw, n1 // (FFT_CH * FFT_SETS)),
        in_specs=[seq_z, seq, pl.BlockSpec((1, LANES), lambda b, w, c: (0, w)), kf_spec, kf_spec]
                 + [full(a) for a in consts],
        out_specs=seq,
        out_shape=jax.ShapeDtypeStruct((B, L, HY_WIDTH), F32),
        scratch_shapes=[pltpu.VMEM((2 * n1 * n2, LANES), F32)],
        compiler_params=_cparams(("parallel", "parallel", "arbitrary"), 58),
        name="hyena_conv",
    )(z, x0u, hbias, kf_re, kf_im, *consts)


T_D0, T_D1, T_WQF, T_WQB, T_WKF, T_WKB, T_GCF, T_GCB, T_BD = range(9)


def _ret_tables_kernel(rl_ref, rh_ref, t_ref):
    C = RET_CHUNK
    i = lax.broadcasted_iota(jnp.int32, (C, LANES), 0).astype(F32)
    jn = lax.broadcasted_iota(jnp.int32, (C, LANES), 1).astype(F32)
    log_g = lambda e: jnp.log1p(-jnp.exp2(-e))
    diff = i - jn
    for hp in range(2):
        lgf = log_g(rh_ref[hp, 0:1, :])
        lgb = log_g(rh_ref[hp, 1:2, :])
        fwd = jnp.exp(jnp.where(diff >= 0, diff, 0.0) * lgf)
        bwd = jnp.exp(jnp.where(diff < 0, -diff, 0.0) * lgb)
        t_ref[T_D0 + hp] = jnp.where(diff >= 0, fwd, bwd)
    lf = log_g(rl_ref[0:1, :])
    lb = log_g(rl_ref[1:2, :])
    t_ref[T_WQF] = jnp.exp((i + 1.0) * lf)
    t_ref[T_WQB] = jnp.exp((C - i) * lb)
    t_ref[T_WKF] = jnp.exp((C - 1.0 - i) * lf)
    t_ref[T_WKB] = jnp.exp(i * lb)
    bd = jnp.where((i < HEAD_DIM) == (jn < HEAD_DIM), 1.0, 0.0)
    t_ref[T_GCF] = jnp.exp(C * lf) * bd
    t_ref[T_GCB] = jnp.exp(C * lb) * bd
    t_ref[T_BD] = bd


def _ret_tables(rde):
    per_lane = jnp.repeat(rde.reshape(2, 2, 2), HEAD_DIM, axis=2)
    per_lane = per_lane.transpose(1, 0, 2)
    per_head = jnp.broadcast_to(rde.reshape(2, 2, 2, 1), (2, 2, 2, LANES))
    per_head = per_head.transpose(1, 2, 0, 3)
    return pl.pallas_call(
        _ret_tables_kernel,
        grid=(2,),
        in_specs=[pl.BlockSpec((None, 2, LANES), lambda j: (j, 0, 0)),
                  pl.BlockSpec((None, 2, 2, LANES), lambda j: (j, 0, 0, 0))],
        out_specs=pl.BlockSpec((None, 9, RET_CHUNK, LANES), lambda j: (j, 0, 0, 0)),
        out_shape=jax.ShapeDtypeStruct((2, 9, RET_CHUNK, LANES), F32),
        compiler_params=_cparams(("parallel",), 32),
        name="retention_tables",
    )(per_lane, per_head)


def _ret_state_kernel(rk_ref, rv_ref, t_ref, st_ref, r_ref):
    C = RET_CHUNK
    nch = rk_ref.shape[0] // C

    @pl.when(pl.program_id(2) == 0)
    def _():
        r_ref[...] = jnp.zeros_like(r_ref)

    wkb = t_ref[T_WKB]
    gcb = t_ref[T_GCB]
    bd = t_ref[T_BD]
    kvs = []
    for n in range(nch):
        sl = slice(n * C, (n + 1) * C)
        kvs.append(lax.dot_general((rk_ref[sl, :].astype(F32) * wkb).astype(BF16), rv_ref[sl, :],
                                   (((0,), (0,)), ((), ())), preferred_element_type=F32))
    state = r_ref[...]
    for n in range(nch - 1, -1, -1):
        st_ref[n] = state
        state = gcb * state + bd * kvs[n]
    r_ref[...] = state


def _ret_main_kernel(rq_ref, rk_ref, rv_ref, rg_ref, t_ref, st_ref, j_ref, o_ref, sf_ref):
    C = RET_CHUNK
    nch = rq_ref.shape[0] // C

    @pl.when(pl.program_id(2) == 0)
    def _():
        sf_ref[...] = jnp.zeros_like(sf_ref)

    lane = lax.broadcasted_iota(jnp.int32, (1, LANES), 1)
    lo64 = lane < HEAD_DIM
    j_bf = j_ref[...]
    chunks = [slice(n * C, (n + 1) * C) for n in range(nch)]
    nt = (((1,), (1,)), ((), ()))
    tn = (((0,), (0,)), ((), ()))
    q, k, vb = [], [], []
    for sl in chunks:
        q.append(rq_ref[sl, :])
        k.append(rk_ref[sl, :])
        vb.append(rv_ref[sl, :])
    scores = []
    for n in range(nch):
        qb = q[n]
        kb = k[n]
        zero = jnp.zeros_like(qb)
        scores.append([lax.dot_general(jnp.where(lo64, qb, zero), kb, nt,
                                       preferred_element_type=F32),
                       lax.dot_general(jnp.where(lo64, zero, qb), kb, nt,
                                       preferred_element_type=F32)])
    kvs = [lax.dot_general((k[n] * t_ref[T_WKF]).astype(BF16), vb[n], tn,
                           preferred_element_type=F32) for n in range(nch)]
    cross_b = [jnp.dot((q[n] * t_ref[T_WQB]).astype(BF16), st_ref[n].astype(BF16),
                       preferred_element_type=F32) for n in range(nch)]
    intra = []
    for n in range(nch):
        parts = [jnp.dot((scores[n][hp] * t_ref[T_D0 + hp]).astype(BF16), vb[n],
                         preferred_element_type=F32) for hp in range(2)]
        intra.append(jnp.where(lo64, parts[0], parts[1]))
    state = sf_ref[...]
    states = []
    for n in range(nch):
        states.append(state.astype(BF16))
        state = t_ref[T_GCF] * state + t_ref[T_BD] * kvs[n]
    sf_ref[...] = state
    outs = [intra[n] + cross_b[n]
            + jnp.dot((q[n] * t_ref[T_WQF]).astype(BF16), states[n], preferred_element_type=F32)
            for n in range(nch)]
    mean_sq = [_head_mean_sq(o, j_bf) for o in outs]
    for n, sl in enumerate(chunks):
        ret = outs[n] * lax.rsqrt(mean_sq[n] + NORM_EPS)
        g = rg_ref[sl, :].astype(F32)
        o_ref[sl, :] = (ret * (g * _sigmoid(g))).astype(o_ref.dtype)


def _retention(p3, tables, j_bf):
    B, L, _ = p3.shape
    C = RET_CHUNK
    tb = min(4096, L)
    nblk = L // tb
    nch = tb // C
    nw = RET_W // LANES
    col = lambda piece: (lambda b, j, s: (b, s, nw * piece + j))
    colr = lambda piece: (lambda b, j, s: (b, nblk - 1 - s, nw * piece + j))
    seq = lambda fn: pl.BlockSpec((None, tb, LANES), fn)
    tab_spec = pl.BlockSpec((None, 9, C, LANES), lambda b, j, s: (j, 0, 0, 0))
    states = pl.pallas_call(
        _ret_state_kernel,
        grid=(B, nw, nblk),
        in_specs=[seq(colr(1)), seq(colr(2)), tab_spec],
        out_specs=pl.BlockSpec((None, None, nch, LANES, LANES),
                               lambda b, j, s: (b, j, nblk - 1 - s, 0, 0)),
        out_shape=jax.ShapeDtypeStruct((B, nw, L // C, LANES, LANES), F32),
        scratch_shapes=[pltpu.VMEM((LANES, LANES), F32)],
        compiler_params=_cparams(("parallel", "parallel", "arbitrary"), 32),
        name="retention_state",
    )(p3, p3, tables)
    return pl.pallas_call(
        _ret_main_kernel,
        grid=(B, nw, nblk),
        in_specs=[seq(col(0)), seq(col(1)), seq(col(2)), seq(col(3)),
                  tab_spec,
                  pl.BlockSpec((None, None, nch, LANES, LANES), lambda b, j, s: (b, j, s, 0, 0)),
                  pl.BlockSpec((LANES, LANES), lambda b, j, s: (0, 0))],
        out_specs=pl.BlockSpec((None, tb, LANES), lambda b, j, s: (b, s, j)),
        out_shape=jax.ShapeDtypeStruct((B, L, RET_W), BF16),
        scratch_shapes=[pltpu.VMEM((LANES, LANES), F32)],
        compiler_params=_cparams(("parallel", "parallel", "arbitrary"), 32),
        name="retention_main",
    )(p3, p3, p3, p3, tables, states, j_bf)


def _merge_kernel(oa_ref, ob_ref, oc_ref, od_ref, g0_ref, g1_ref, g2_ref, g3_ref,
                  wb_ref, wo_ref, x_ref, gn_ref, h_ref):
    tm = x_ref.shape[0]
    nparts = 2 if tm % (2 * HALO) == 0 else 1
    parts = [slice(i * tm // nparts, (i + 1) * tm // nparts) for i in range(nparts)]
    branches = ((oa_ref, g0_ref), (ob_ref, g1_ref), (oc_ref, g2_ref), (od_ref, g3_ref))
    proj = [[jnp.dot(br[rows, :].astype(BF16), wb_ref[n], preferred_element_type=F32)
             for n, (br, _) in enumerate(branches)] for rows in parts]
    merged = []
    for pi, rows in enumerate(parts):
        acc = None
        for n, (_, gr) in enumerate(branches):
            term = gr[rows, :].astype(F32) * proj[pi][n]
            acc = term if acc is None else acc + term
        merged.append(acc.astype(BF16))
    ys = [jnp.dot(m, wo_ref[...], preferred_element_type=F32) for m in merged]
    for rows, y in zip(parts, ys):
        ms = jnp.mean(y * y, axis=-1, keepdims=True)
        h_ref[rows, :] = x_ref[rows, :] + y * lax.rsqrt(ms + NORM_EPS) * gn_ref[...]


def _merge(oa, ob, oc, od, p2, wb_bf, wo_bf, x2, gn):
    T = x2.shape[0]
    tm = min(1024, T)
    br = pl.BlockSpec((tm, BRANCH_W), lambda i: (i, 0))
    gate = lambda n: pl.BlockSpec((tm, D_MODEL), lambda i, n=n: (i, n))
    row = pl.BlockSpec((tm, D_MODEL), lambda i: (i, 0))
    return pl.pallas_call(
        _merge_kernel,
        grid=(T // tm,),
        in_specs=[br, br, br, br, gate(0), gate(1), gate(2), gate(3),
                  pl.BlockSpec((N_BRANCH, BRANCH_W, D_MODEL), lambda i: (0, 0, 0)),
                  pl.BlockSpec((D_MODEL, D_MODEL), lambda i: (0, 0)),
                  row, pl.BlockSpec((1, D_MODEL), lambda i: (0, 0))],
        out_specs=row,
        out_shape=jax.ShapeDtypeStruct((T, D_MODEL), F32),
        compiler_params=_cparams(("parallel",), 48),
        name="merge_out_proj",
    )(oa, ob, oc, od, p2, p2, p2, p2, wb_bf, wo_bf, x2, gn)


def _ffn_kernel(h_ref, g2_ref, wg_ref, wu_ref, wd_ref, g3_ref, o_ref):
    tm = h_ref.shape[0]
    halves = [slice(0, tm // 2), slice(tm // 2, tm)] if tm % (2 * HALO) == 0 else [slice(0, tm)]
    hns = []
    for rows in halves:
        h = h_ref[rows, :]
        ms = jnp.mean(h * h, axis=-1, keepdims=True)
        hns.append((h * lax.rsqrt(ms + NORM_EPS) * g2_ref[...]).astype(BF16))
    fs = [None] * len(halves)
    for lo, hi in _col_chunks(D_FF, 2):
        gus = [(jnp.dot(hn, wg_ref[:, lo:hi], preferred_element_type=F32),
                jnp.dot(hn, wu_ref[:, lo:hi], preferred_element_type=F32)) for hn in hns]
        acts = [(g * _sigmoid(g) * u).astype(BF16) for g, u in gus]
        for i, a in enumerate(acts):
            part = jnp.dot(a, wd_ref[lo:hi, :], preferred_element_type=F32)
            fs[i] = part if fs[i] is None else fs[i] + part
    for rows, f in zip(halves, fs):
        ms = jnp.mean(f * f, axis=-1, keepdims=True)
        o_ref[rows, :] = h_ref[rows, :] + f * lax.rsqrt(ms + NORM_EPS) * g3_ref[...]


def _ffn(h2, g2, wg_bf, wu_bf, wd_bf, g3):
    T = h2.shape[0]
    tm = min(512, T)
    row = pl.BlockSpec((tm, D_MODEL), lambda i: (i, 0))
    vec = pl.BlockSpec((1, D_MODEL), lambda i: (0, 0))
    return pl.pallas_call(
        _ffn_kernel,
        grid=(T // tm,),
        in_specs=[row, vec,
                  pl.BlockSpec((D_MODEL, D_FF), lambda i: (0, 0)),
                  pl.BlockSpec((D_MODEL, D_FF), lambda i: (0, 0)),
                  pl.BlockSpec((D_FF, D_MODEL), lambda i: (0, 0)),
                  vec],
        out_specs=row,
        out_shape=jax.ShapeDtypeStruct((T, D_MODEL), F32),
        compiler_params=_cparams(("parallel",), 48),
        name="ffn",
    )(h2, g2, wg_bf, wu_bf, wd_bf, g3)


def _rope_tables(L):
    rows = L // GRID_W
    r = jnp.repeat(jnp.arange(rows, dtype=F32), GRID_W)
    c = jnp.tile(jnp.arange(GRID_W, dtype=F32), rows)
    inv = ROPE_BASE ** (-jnp.arange(ROPE_FREQS, dtype=F32) / ROPE_FREQS)
    ar = r[:, None] * inv
    ac = c[:, None] * inv
    cos64 = jnp.concatenate([jnp.cos(ar), jnp.cos(ar), jnp.cos(ac), jnp.cos(ac)], axis=1)
    sin64 = jnp.concatenate([-jnp.sin(ar), jnp.sin(ar), -jnp.sin(ac), jnp.sin(ac)], axis=1)
    return jnp.tile(cos64, (1, 2)), jnp.tile(sin64, (1, 2))


def _filter_features(L):
    t = np.linspace(0.0, 1.0, L)[:, None]
    f = np.linspace(1e-4, HY_BANDS - 1, HY_BANDS)
    ang = (2.0 * math.pi / L) * np.arange(L)[:, None] * f[None, :]
    z = np.concatenate([t, np.cos(ang), -np.sin(ang)], axis=-1)
    z = np.pad(z, ((0, 0), (0, HY_FILTER_HIDDEN - HY_EMB)))
    back = np.concatenate([[0], np.arange(L - 1, 0, -1)])
    return jnp.asarray(np.concatenate([z, z[back]], axis=1).astype(np.float32))


def _head_mean_matrix():
    i = np.arange(LANES)
    j = ((i[:, None] // HEAD_DIM) == (i[None, :] // HEAD_DIM)).astype(np.float32) / HEAD_DIM
    return jnp.asarray(j.astype(BF16))


def _halve_gate_columns(w_in_bf):
    scale = jnp.where(jnp.arange(IN_COLS) >= GATE_OFF, 0.5, 1.0).astype(BF16)
    return w_in_bf * scale


def _trunk(x, wts, fft_dims=None):
    B, L, _ = x.shape
    T = B * L
    n1, n2 = fft_dims or _fft_dims(L)
    cst = _fft_consts(n1, n2, n1 // 2)
    cos_t, sin_t = _rope_tables(L)
    z2 = _filter_features(L)
    deltas = jnp.abs(jnp.linspace(math.log(HY_TARGET) / HY_SLOW_DECAY,
                                  math.log(HY_TARGET) / HY_FAST_DECAY, HY_WIDTH, dtype=F32))[None, :]
    j_bf = _head_mean_matrix()
    depth = wts["w_in"].shape[0]
    x2 = x.reshape(T, D_MODEL)
    for l in range(depth):
        ng = wts["norm_gains"][l]
        w1p = jnp.pad(wts["hy_w1"][l], ((0, HY_FILTER_HIDDEN - HY_EMB), (0, 0)))
        k2raw, sumsq = _hy_filter_raw(z2, w1p, wts["hy_b1"][l][None, :], wts["hy_w2"][l],
                                      wts["hy_b2"][l][None, :], wts["hy_w3"][l],
                                      wts["hy_freq"][l], deltas, L)
        kf_re, kf_im = _filter_fft(k2raw, sumsq, cst, n1, n2)

        qg = jnp.tile(wts["qk_norm"][l, 0], 2)[None, :]
        kg = jnp.tile(wts["qk_norm"][l, 1], 2)[None, :]
        qt, k, vt, p_ret, p_gate, z, x0u, out_d = _in_proj(
            x2, ng[0][None, :], wts["w_in_bf"][l], wts["hy_conv_w"][l],
            wts["hy_conv_b"][l][None, :], wts["sc_conv_w"][l], cos_t, sin_t, qg, kg, j_bf, L)
        out_a = _flash(qt, k, vt)

        out_b = _hy_conv(z.reshape(B, L, HY_WIDTH), x0u.reshape(B, L, HY_WIDTH),
                         wts["hy_bias"][l][None, :], kf_re, kf_im, cst, n1, n2)

        tables = _ret_tables(wts["ret_decay_exp"][l])
        out_c = _retention(p_ret.reshape(B, L, 4 * RET_W), tables, j_bf)

        h2 = _merge(out_a.reshape(T, BRANCH_W), out_b.reshape(T, BRANCH_W),
                    out_c.reshape(T, BRANCH_W), out_d,
                    p_gate, wts["w_branch_bf"][l], wts["w_out_bf"][l], x2, ng[1][None, :])
        x2 = _ffn(h2, ng[2][None, :], wts["w_gate_bf"][l], wts["w_up_bf"][l],
                  wts["w_ffn_out_bf"][l], ng[3][None, :])
    return x2.reshape(B, L, D_MODEL)


def kernel(x_prompt, x_sample, norm_gains, w_in, qk_norm, hy_conv_w, hy_conv_b, hy_w1, hy_b1, hy_w2,
           hy_b2, hy_w3, hy_freq, hy_bias, ret_decay_exp, sc_conv_w, w_branch, w_out, w_ffn_in,
           w_ffn_out):
    wts = dict(norm_gains=norm_gains, w_in=w_in, qk_norm=qk_norm, hy_conv_w=hy_conv_w,
               hy_conv_b=hy_conv_b, hy_w1=hy_w1, hy_b1=hy_b1, hy_w2=hy_w2, hy_b2=hy_b2,
               hy_w3=hy_w3, hy_freq=hy_freq, hy_bias=hy_bias, ret_decay_exp=ret_decay_exp,
               sc_conv_w=sc_conv_w,
               w_in_bf=_halve_gate_columns(w_in.astype(BF16)), w_branch_bf=w_branch.astype(BF16),
               w_out_bf=w_out.astype(BF16), w_gate_bf=w_ffn_in[..., :D_FF].astype(BF16),
               w_up_bf=w_ffn_in[..., D_FF:].astype(BF16), w_ffn_out_bf=w_ffn_out.astype(BF16))
    return _trunk(x_prompt, wts), _trunk(x_sample, wts)
```
